```python
import functools
import jax, jax.numpy as jnp
from jax import lax
import numpy as np

D_MODEL = 1024
BATCH = 8
SEQ = 2048
DEPTH = 2

GRID_W = 64
CTX_LEN = 256
N_MIXERS = 2
CHUNK = 64
EPS = 1e-6
N_ADA = 6

M_HEADS = 4
M_DV = D_MODEL // M_HEADS
M_DK = M_DV // 2
M_CONV = 3
M_QK = M_HEADS * M_DK
M_V = M_HEADS * M_DV
M_PROJ = 2 * M_QK + 2 * M_V + 4 * M_HEADS

H_HEADS = 8
H_DK = 128
H_DV = D_MODEL // H_HEADS
H_K = H_HEADS * H_DK
H_V = H_HEADS * H_DV
H_PROJ = 3 * H_K + 2 * H_V

N_EXPERTS = 16
N_GROUPS = 4
E_PER_GROUP = N_EXPERTS // N_GROUPS
TOP_K = 2
D_EXPERT = 512

N_MLSTM = (DEPTH + 1) // 2
N_HGRN = DEPTH // 2

kernel_name = 'hybrid_mlstm_hgrn2_moe_dit'


def rmsnorm(x, g):
    xf = x.astype(jnp.float32)
    y = xf * lax.rsqrt(jnp.mean(xf * xf, axis=-1, keepdims=True) + EPS)
    return (y * g.astype(jnp.float32)).astype(x.dtype)


def to_heads(t, n_heads):
    b, l, _ = t.shape
    return t.reshape(b, l, n_heads, -1).transpose(0, 2, 1, 3)


def head_rmsnorm(o, g, n_heads):
    b, _, l, _ = o.shape
    y = o * lax.rsqrt(jnp.mean(o * o, axis=-1, keepdims=True) + EPS)
    y = y * g.astype(jnp.float32).reshape(n_heads, 1, -1)
    return y.transpose(0, 2, 1, 3).reshape(b, l, -1)


def centred_depthwise_conv(x, w, b):
    width, ch = w.shape
    y = lax.conv_general_dilated(x, w[:, None, :].astype(x.dtype), window_strides=(1,),
                                 padding=[(width // 2, width // 2)],
                                 dimension_numbers=('NWC', 'WIO', 'NWC'), feature_group_count=ch)
    return y + b.astype(x.dtype)


def raster_to_column(t):
    b, l, d = t.shape
    rows = l // GRID_W
    return t.reshape(b, rows, GRID_W, d).transpose(0, 2, 1, 3).reshape(b, l, d)


def column_to_raster(t):
    b, l, d = t.shape
    rows = l // GRID_W
    return t.reshape(b, GRID_W, rows, d).transpose(0, 2, 1, 3).reshape(b, l, d)


def to_chunks(t):
    b, h, l = t.shape[:3]
    return jnp.moveaxis(t.reshape(b, h, l // CHUNK, CHUNK, *t.shape[3:]), 2, 0)


def from_chunks(t):
    t = jnp.moveaxis(t, 0, 2)
    return t.reshape(t.shape[0], t.shape[1], -1, *t.shape[4:])


def mlstm_chunk_step(carry, xs, need_out):
    c_mem, n_mem, m_prev = carry
    q, k, v, ig, lf = xs
    b = jnp.cumsum(lf, axis=-1)
    b_last = b[..., -1]
    g = b_last[..., None] - b + ig
    m_new = jnp.maximum(b_last + m_prev, jnp.max(g, axis=-1))
    w_s = jnp.exp(g - m_new[..., None])
    carry_decay = jnp.exp(b_last + m_prev - m_new)
    c_new = carry_decay[..., None, None] * c_mem + jnp.einsum('bhs,bhsv,bhsk->bhvk', w_s, v, k)
    n_new = carry_decay[..., None] * n_mem + jnp.einsum('bhs,bhsk->bhk', w_s, k)
    new_carry = (c_new, n_new, m_new)
    if not need_out:
        return new_carry, None
    prefix = jnp.tril(jnp.ones((CHUNK, CHUNK), dtype=bool))
    log_d = jnp.where(prefix, b[..., :, None] - b[..., None, :] + ig[..., None, :], -jnp.inf)
    log_inter = b + m_prev[..., None]
    m_t = jnp.maximum(log_inter, jnp.max(log_d, axis=-1))
    w_ts = jnp.exp(log_d - m_t[..., None]) * jnp.einsum('bhtk,bhsk->bhts', q, k)
    inter = jnp.exp(log_inter - m_t)
    num = jnp.einsum('bhts,bhsv->bhtv', w_ts, v) + inter[..., None] * jnp.einsum('bhvk,bhtk->bhtv', c_mem, q)
    den = jnp.sum(w_ts, axis=-1) + inter * jnp.einsum('bhk,bhtk->bht', n_mem, q)
    h = num / jnp.maximum(jnp.abs(den), jnp.exp(-m_t))[..., None]
    return new_carry, h


def hgrn2_chunk_step(s_mem, xs, need_out):
    q, k, lf, i = xs
    b = jnp.cumsum(lf, axis=2)
    b_last = b[:, :, -1:, :]
    s_new = jnp.exp(b_last[:, :, 0, :])[..., None] * s_mem + jnp.einsum('bhsk,bhsv->bhkv', k * jnp.exp(b_last - b), i)
    if not need_out:
        return s_new, None
    prefix = jnp.tril(jnp.ones((CHUNK, CHUNK), dtype=bool))[:, :, None]
    decay = jnp.exp(jnp.where(prefix, b[:, :, :, None, :] - b[:, :, None, :, :], -jnp.inf))
    a = jnp.einsum('bhtk,bhsk,bhtsk->bhts', q, k, decay)
    o = jnp.einsum('bhts,bhsv->bhtv', a, i) + jnp.einsum('bhtk,bhkv->bhtv', q * jnp.exp(b), s_mem)
    return s_new, o


def directional_scan(step, init, ctx_xs, lat_xs, reverse, need_ctx_out):
    if reverse:
        ctx_xs = tuple(jnp.flip(t, axis=2) for t in ctx_xs)
        lat_xs = tuple(jnp.flip(t, axis=2) for t in lat_xs)
    ctx_state, ctx_out = lax.scan(functools.partial(step, need_out=need_ctx_out), init,
                                  tuple(to_chunks(t) for t in ctx_xs))
    _, lat_out = lax.scan(functools.partial(step, need_out=True), ctx_state,
                          tuple(to_chunks(t) for t in lat_xs))
    lat_out = from_chunks(lat_out)
    ctx_out = from_chunks(ctx_out) if need_ctx_out else None
    if reverse:
        lat_out = jnp.flip(lat_out, axis=2)
        ctx_out = jnp.flip(ctx_out, axis=2) if need_ctx_out else None
    return lat_out, ctx_out


def mlstm_mixer(h_lat, h_ctx, w_in, conv_w, conv_b, gate_b, head_g, w_out, need_ctx_out):
    f32 = jnp.float32

    def prepare(h):
        b, l, _ = h.shape
        qk, v, o, gates = jnp.split(h @ w_in, [2 * M_QK, 2 * M_QK + M_V, 2 * M_QK + 2 * M_V], axis=-1)
        qk = jax.nn.silu(centred_depthwise_conv(qk, conv_w, conv_b)).astype(f32)
        q = to_heads(qk[..., :M_QK], M_HEADS) * (M_DK ** -0.5)
        k = to_heads(qk[..., M_QK:], M_HEADS)
        v = to_heads(v.astype(f32), M_HEADS)
        gates = (gates.astype(f32) + gate_b.astype(f32)).reshape(b, l, 4, M_HEADS).transpose(2, 0, 3, 1)
        fwd = (q, k, v, gates[0], jax.nn.log_sigmoid(gates[1]))
        bwd = (q, k, v, gates[2], jax.nn.log_sigmoid(gates[3]))
        return (fwd, bwd), o

    lat_dirs, o_lat = prepare(h_lat)
    ctx_dirs, o_ctx = prepare(h_ctx)
    bsz = h_lat.shape[0]
    init = (jnp.zeros((bsz, M_HEADS, M_DV, M_DK), f32), jnp.zeros((bsz, M_HEADS, M_DK), f32),
            jnp.zeros((bsz, M_HEADS), f32))
    lat_f, ctx_f = directional_scan(mlstm_chunk_step, init, ctx_dirs[0], lat_dirs[0], False, need_ctx_out)
    lat_b, ctx_b = directional_scan(mlstm_chunk_step, init, ctx_dirs[1], lat_dirs[1], True, need_ctx_out)

    def read_out(hs, o):
        return (head_rmsnorm(hs, head_g, M_HEADS).astype(o.dtype) * jax.nn.sigmoid(o)) @ w_out

    y_lat = read_out(lat_f + lat_b, o_lat)
    y_ctx = read_out(ctx_f + ctx_b, o_ctx) if need_ctx_out else None
    return y_lat, y_ctx


def hgrn2_mixer(h_lat, h_ctx, w_in, lb, head_g, w_out, need_ctx_out):
    f32 = jnp.float32
    lb = lb.reshape(2, H_HEADS, 1, H_DK)
    log_lb = jnp.log(lb)
    log_1mlb = jnp.log1p(-lb)

    def prepare(h):
        q, z_fw, z_bw, i, g = jnp.split(h @ w_in, [H_K, 2 * H_K, 3 * H_K, 3 * H_K + H_V], axis=-1)
        q = to_heads(jax.nn.silu(q.astype(f32)), H_HEADS)
        i = to_heads(i.astype(f32), H_HEADS)
        dirs = []
        for d, z in enumerate((z_fw, z_bw)):
            z = to_heads(z.astype(f32), H_HEADS)
            log_f = jnp.logaddexp(log_lb[d], log_1mlb[d] + jax.nn.log_sigmoid(z))
            k = jnp.exp(log_1mlb[d]) * jax.nn.sigmoid(-z)
            dirs.append((q, k, log_f, i))
        return dirs, g

    lat_dirs, g_lat = prepare(h_lat)
    ctx_dirs, g_ctx = prepare(h_ctx)
    init = jnp.zeros((h_lat.shape[0], H_HEADS, H_DK, H_DV), f32)
    lat_f, ctx_f = directional_scan(hgrn2_chunk_step, init, ctx_dirs[0], lat_dirs[0], False, need_ctx_out)
    lat_b, ctx_b = directional_scan(hgrn2_chunk_step, init, ctx_dirs[1], lat_dirs[1], True, need_ctx_out)

    def read_out(os, g):
        return (head_rmsnorm(os, head_g, H_HEADS).astype(g.dtype) * jax.nn.silu(g)) @ w_out

    y_lat = read_out(lat_f + lat_b, g_lat)
    y_ctx = read_out(ctx_f + ctx_b, g_ctx) if need_ctx_out else None
    return y_lat, y_ctx


def moe_ffn(h, router_w, router_bias, w_gate, w_up, w_down):
    f32 = jnp.float32
    s = jax.nn.sigmoid((h @ router_w).astype(f32))
    sel = s + router_bias.astype(f32)
    group_score = jnp.sum(lax.top_k(sel.reshape(-1, N_GROUPS, E_PER_GROUP), TOP_K)[0], axis=-1)
    best = jnp.argmax(group_score, axis=-1)
    in_group = (jnp.arange(N_EXPERTS) // E_PER_GROUP)[None, :] == best[:, None]
    _, idx = lax.top_k(jnp.where(in_group, sel, -jnp.inf), TOP_K)
    w = jnp.take_along_axis(s, idx, axis=-1)
    w = w / jnp.sum(w, axis=-1, keepdims=True)
    combine = jnp.sum(jax.nn.one_hot(idx, N_EXPERTS, dtype=f32) * w[..., None], axis=1)
    out = jnp.zeros(h.shape, f32)
    for e in range(N_EXPERTS):
        y = (jax.nn.silu(h @ w_gate[e]) * (h @ w_up[e])) @ w_down[e]
        out = out + combine[:, e:e + 1] * y.astype(f32)
    return out.astype(h.dtype)


def setup_inputs(seed: int = 0) -> dict:
    key = jax.random.key(seed)
    ks = iter(jax.random.split(key, 32))
    f32 = jnp.float32

    def w(shape, fan_in, scale=1.0):
        return (scale * fan_in ** -0.5) * jax.random.normal(next(ks), shape, f32)

    def gain(shape):
        return 1.0 + 0.05 * jax.random.normal(next(ks), shape, f32)

    def small(shape, s=0.02):
        return s * jax.random.normal(next(ks), shape, f32)

    forget_base = jnp.linspace(3.0, 6.0, M_HEADS, dtype=f32)
    zeros_h = jnp.zeros((M_HEADS,), f32)
    gate_base = jnp.concatenate([zeros_h, forget_base, zeros_h, forget_base])
    return {
        'x': jax.random.normal(next(ks), (BATCH, SEQ, D_MODEL), f32),
        'c': jax.random.normal(next(ks), (BATCH, D_MODEL), f32),
        'ctx': jax.random.normal(next(ks), (BATCH, CTX_LEN, D_MODEL), f32),
        'c_ctx': jax.random.normal(next(ks), (D_MODEL,), f32),
        'ada_w': w((DEPTH, D_MODEL, N_ADA * D_MODEL), D_MODEL, 0.5),
        'ada_b': small((DEPTH, N_ADA * D_MODEL)),
        'norm_mix_g': gain((DEPTH, D_MODEL)),
        'norm_ffn_g': gain((DEPTH, D_MODEL)),
        'final_g': gain((D_MODEL,)),
        'm_w_in': w((N_MLSTM, D_MODEL, M_PROJ), D_MODEL),
        'm_conv_w': w((N_MLSTM, M_CONV, 2 * M_QK), M_CONV),
        'm_conv_b': small((N_MLSTM, 2 * M_QK)),
        'm_gate_b': gate_base + small((N_MLSTM, 4 * M_HEADS), 0.1),
        'm_head_g': gain((N_MLSTM, M_V)),
        'm_w_out': w((N_MLSTM, M_V, D_MODEL), M_V),
        'h_w_in': w((N_HGRN, D_MODEL, H_PROJ), D_MODEL),
        'h_lower_bounds': small((DEPTH, 2 * H_K), 0.1),
        'h_head_g': gain((N_HGRN, H_V)),
        'h_w_out': w((N_HGRN, H_V, D_MODEL), H_V),
        'router_w': w((D_MODEL, N_EXPERTS), D_MODEL),
        'router_bias': small((N_EXPERTS,), 0.01),
        'e_w_gate': w((DEPTH, N_EXPERTS, D_MODEL, D_EXPERT), D_MODEL),
        'e_w_up': w((DEPTH, N_EXPERTS, D_MODEL, D_EXPERT), D_MODEL),
        'e_w_down': w((DEPTH, N_EXPERTS, D_EXPERT, D_MODEL), D_EXPERT),
    }


def reference(x, c, ctx, c_ctx, ada_w, ada_b, norm_mix_g, norm_ffn_g, final_g,
              m_w_in, m_conv_w, m_conv_b, m_gate_b, m_head_g, m_w_out,
              h_w_in, h_lower_bounds, h_head_g, h_w_out,
              router_w, router_bias, e_w_gate, e_w_up, e_w_down):
    bsz, seq, d = x.shape
    lbs = jnp.cumsum(jax.nn.softmax(h_lower_bounds.astype(jnp.float32), axis=0), axis=0)
    lbs = lbs - lbs[0]
    s_lat, s_ctx = x, ctx
    for i in range(DEPTH):
        last = i == DEPTH - 1
        j = i // N_MIXERS
        mod = jax.nn.silu(c) @ ada_w[i] + ada_b[i]
        mod_c = jax.nn.silu(c_ctx) @ ada_w[i] + ada_b[i]
        sh1, sc1, g1, sh2, sc2, g2 = [t[:, None, :] for t in jnp.split(mod, N_ADA, axis=-1)]
        csh1, csc1, cg1, csh2, csc2, cg2 = jnp.split(mod_c, N_ADA, axis=-1)
        h_lat = rmsnorm(s_lat, norm_mix_g[i]) * (1 + sc1) + sh1
        h_ctx = rmsnorm(s_ctx, norm_mix_g[i]) * (1 + csc1) + csh1
        if i % N_MIXERS == 0:
            y_lat, y_ctx = mlstm_mixer(h_lat, h_ctx, m_w_in[j], m_conv_w[j], m_conv_b[j], m_gate_b[j],
                                       m_head_g[j], m_w_out[j], not last)
        else:
            y_lat, y_ctx = hgrn2_mixer(raster_to_column(h_lat), h_ctx, h_w_in[j], lbs[i],
                                       h_head_g[j], h_w_out[j], not last)
            y_lat = column_to_raster(y_lat)
        s_lat = s_lat + g1 * y_lat
        h2_lat = rmsnorm(s_lat, norm_ffn_g[i]) * (1 + sc2) + sh2
        if last:
            f_lat = moe_ffn(h2_lat.reshape(-1, d), router_w, router_bias, e_w_gate[i], e_w_up[i], e_w_down[i])
            s_lat = s_lat + g2 * f_lat.reshape(bsz, seq, d)
        else:
            s_ctx = s_ctx + cg1 * y_ctx
            h2_ctx = rmsnorm(s_ctx, norm_ffn_g[i]) * (1 + csc2) + csh2
            tokens = jnp.concatenate([h2_lat.reshape(-1, d), h2_ctx.reshape(-1, d)], axis=0)
            f = moe_ffn(tokens, router_w, router_bias, e_w_gate[i], e_w_up[i], e_w_down[i])
            n_lat = bsz * seq
            s_lat = s_lat + g2 * f[:n_lat].reshape(bsz, seq, d)
            s_ctx = s_ctx + cg2 * f[n_lat:].reshape(s_ctx.shape)
    return rmsnorm(s_lat, final_g)
```

```python
import functools

import numpy as np
import jax
import jax.numpy as jnp
from jax import lax
from jax.experimental import pallas as pl
from jax.experimental.pallas import tpu as pltpu

F32 = jnp.float32
BF16 = jnp.bfloat16

EPS = 1e-6
N_ADA = 6
GRID_W = 64
M_HEADS = 4
H_HEADS = 8
N_EXPERTS = 16
N_GROUPS = 4
E_PER_GROUP = N_EXPERTS // N_GROUPS
N_PAIRS = 6
N_CLASSES = N_GROUPS * N_PAIRS
PAIR_LO = (0, 0, 0, 1, 1, 2)
PAIR_HI = (1, 2, 3, 2, 3, 3)

LANES = 128
COL_TILE = 8
SCAN_CHUNK = 128
MOE_TILE = 256
VMEM_LIMIT = 56 * 1024 * 1024


def _cparams(sem):
    return pltpu.CompilerParams(dimension_semantics=sem, vmem_limit_bytes=VMEM_LIMIT)


def _dot(a, b):
    return jnp.dot(a, b, preferred_element_type=F32)


def _dot_nt(a, b):
    return lax.dot_general(a, b, (((1,), (1,)), ((), ())), preferred_element_type=F32)


def _dot_tn(a, b):
    return lax.dot_general(a, b, (((0,), (0,)), ((), ())), preferred_element_type=F32)


def _split_bf16(a):
    hi = a.astype(BF16)
    lo = (a - hi.astype(F32)).astype(BF16)
    return hi, lo


def _sigmoid(x):
    return 1.0 / (1.0 + jnp.exp(-x))


def _silu(x):
    return x * _sigmoid(x)


def _log_sigmoid(x):
    return jnp.minimum(x, 0.0) - jnp.log1p(jnp.exp(-jnp.abs(x)))


def _rms(x, g):
    return x * lax.rsqrt(jnp.mean(x * x, axis=-1, keepdims=True) + EPS) * g


def _ada_kernel(c_ref, w_ref, b_ref, o_ref):
    a = _silu(c_ref[...])
    a_hi, a_lo = _split_bf16(a)
    w_hi, w_lo = _split_bf16(w_ref[0])
    acc = _dot(a_hi, w_hi) + _dot(a_lo, w_hi) + _dot(a_hi, w_lo)
    o_ref[0] = acc + b_ref[0]


def _ada(cc, ada_w, ada_b):
    depth, d, n = ada_w.shape
    tn = 1024
    rows = cc.shape[0]
    return pl.pallas_call(
        _ada_kernel,
        grid=(depth, n // tn),
        in_specs=[
            pl.BlockSpec((rows, d), lambda l, j: (0, 0)),
            pl.BlockSpec((1, d, tn), lambda l, j: (l, 0, j)),
            pl.BlockSpec((1, 1, tn), lambda l, j: (l, 0, j)),
        ],
        out_specs=pl.BlockSpec((1, rows, tn), lambda l, j: (l, 0, j)),
        out_shape=jax.ShapeDtypeStruct((depth, rows, n), F32),
        compiler_params=_cparams(("arbitrary", "arbitrary")),
        name="ada_mod",
    )(cc, ada_w, ada_b.reshape(depth, 1, n))


def _mproj_kernel(n_lat_tiles, xl_ref, xc_ref, sc_ref, sh_ref, g_ref, wqk_ref, wv_ref, wo_ref, wg_ref,
                  qk_ref, v_ref, o_ref, gt_ref):
    j = pl.program_id(1)
    x = jnp.where(j < n_lat_tiles, xl_ref[0], xc_ref[0])
    h = (_rms(x, g_ref[...]) * (1.0 + sc_ref[0]) + sh_ref[0]).astype(BF16)
    qk_ref[0] = _dot(h, wqk_ref[...]).astype(BF16)
    v_ref[0] = _dot(h, wv_ref[...]).astype(BF16)
    o_ref[0] = _dot(h, wo_ref[...]).astype(BF16)
    gt_ref[0] = _dot(h, wg_ref[...])[:, :4 * M_HEADS]


def _mod_spec(d, layer, k, n_lat_tiles, rows):
    def imap(b, j):
        row = jnp.where(j < n_lat_tiles, b, rows - 1)
        return ((layer * rows + row) * N_ADA + k, 0, 0)
    return pl.BlockSpec((1, 1, d), imap)


def _mproj(x, ctx, modv, rows, norm_g, w_qk, w_v, w_o, w_g, tm):
    bsz, ll, d = x.shape
    lc = ctx.shape[1]
    nl, nc = ll // tm, lc // tm
    ltot = ll + lc
    wspec = lambda n: pl.BlockSpec((d, n), lambda b, j: (0, 0))
    tok = lambda n: pl.BlockSpec((1, tm, n), lambda b, j: (b, j, 0))
    return pl.pallas_call(
        functools.partial(_mproj_kernel, nl),
        grid=(bsz, nl + nc),
        in_specs=[
            pl.BlockSpec((1, tm, d), lambda b, j: (b, jnp.minimum(j, nl - 1), 0)),
            pl.BlockSpec((1, tm, d), lambda b, j: (b, jnp.maximum(j - nl, 0), 0)),
            _mod_spec(d, 0, 1, nl, rows), _mod_spec(d, 0, 0, nl, rows),
            pl.BlockSpec((1, d), lambda b, j: (0, 0)),
            wspec(w_qk.shape[1]), wspec(w_v.shape[1]), wspec(w_o.shape[1]), wspec(w_g.shape[1]),
        ],
        out_specs=[tok(w_qk.shape[1]), tok(w_v.shape[1]), tok(w_o.shape[1]), tok(4 * M_HEADS)],
        out_shape=[
            jax.ShapeDtypeStruct((bsz, ltot, w_qk.shape[1]), BF16),
            jax.ShapeDtypeStruct((bsz, ltot, w_v.shape[1]), BF16),
            jax.ShapeDtypeStruct((bsz, ltot, w_o.shape[1]), BF16),
            jax.ShapeDtypeStruct((bsz, ltot, 4 * M_HEADS), F32),
        ],
        compiler_params=_cparams(("arbitrary", "arbitrary")),
        name="mlstm_in_proj",
    )(x, ctx, modv, modv, norm_g, w_qk, w_v, w_o, w_g)


def _mlstm_kernel(t, ll, lc, q_ref, k_ref, v_ref, o_ref, gc_ref, gr_ref, cwq_ref, cwk_ref, cbq_ref, cbk_ref,
                  gbc_ref, gbr_ref, hg_ref, out_ref, qs, ks, hf, hb, c_s, n_s, m_s):
    ltot = ll + lc
    dk = q_ref.shape[2]
    row = lax.broadcasted_iota(jnp.int32, (ltot, 1), 0)
    first = (row == 0) | (row == ll)
    last = (row == ll - 1) | (row == ltot - 1)

    def conv(x_ref, w_ref, b_ref):
        x = x_ref[0].astype(F32)
        w = w_ref[...]
        xp = jnp.where(first, 0.0, pltpu.roll(x, 1, 0))
        xn = jnp.where(last, 0.0, pltpu.roll(x, ltot - 1, 0))
        return _silu(xp * w[0:1] + x * w[1:2] + xn * w[2:3] + b_ref[...])

    qs[...] = (conv(q_ref, cwq_ref, cbq_ref) * (dk ** -0.5)).astype(BF16)
    ks[...] = conv(k_ref, cwk_ref, cbk_ref).astype(BF16)

    c_s[...] = jnp.zeros(c_s.shape, F32)
    n_s[...] = jnp.zeros(n_s.shape, F32)
    m_s[...] = jnp.zeros(m_s.shape, F32)

    ti = lax.broadcasted_iota(jnp.int32, (t, t), 0)
    si = lax.broadcasted_iota(jnp.int32, (t, t), 1)

    def chunk(start, d, h_out):
        q = qs[pl.ds(start, t), :]
        k = ks[pl.ds(start, t), :]
        v = v_ref[0, pl.ds(start, t), :]
        gc = gc_ref[0, 0, pl.ds(start, t), :] + gbc_ref[0]
        gr = gr_ref[0, 0, :, pl.ds(start, t)] + gbr_ref[0]
        ig_c, lf_c = gc[:, 2 * d:2 * d + 1], _log_sigmoid(gc[:, 2 * d + 1:2 * d + 2])
        ig_r, lf_r = gr[2 * d:2 * d + 1, :], _log_sigmoid(gr[2 * d + 1:2 * d + 2, :])
        seen = (si <= ti) if d == 0 else (si >= ti)
        seen_t = (ti <= si) if d == 0 else (ti >= si)
        b_c = jnp.sum(jnp.where(seen, lf_r, 0.0), axis=1, keepdims=True)
        b_r = jnp.sum(jnp.where(seen_t, lf_c, 0.0), axis=0, keepdims=True)
        total = jnp.sum(lf_r, axis=1, keepdims=True)
        c_mem = c_s[d]
        n_mem = n_s[d]
        m_prev = m_s[d][0:1, 0:1]

        log_d = jnp.where(seen, b_c - b_r + ig_r, -jnp.inf)
        log_inter = b_c + m_prev
        m_t = jnp.maximum(log_inter, jnp.max(log_d, axis=1, keepdims=True))
        w_ts = jnp.exp(log_d - m_t) * _dot_nt(q, k)
        inter = jnp.exp(log_inter - m_t)
        num = _dot(w_ts.astype(BF16), v) + inter * _dot_nt(q, c_mem.astype(BF16))
        den = jnp.sum(w_ts, axis=1, keepdims=True) + inter * jnp.sum(q.astype(F32) * n_mem, axis=1, keepdims=True)
        h_out[pl.ds(start, t), :] = num / jnp.maximum(jnp.abs(den), jnp.exp(-m_t))

        g_c = total - b_c + ig_c
        g_r = total - b_r + ig_r
        m_new = jnp.maximum(total + m_prev, jnp.max(g_r, axis=1, keepdims=True))
        w_c = jnp.exp(g_c - m_new)
        decay = jnp.exp(total + m_prev - m_new)
        wv = (w_c * v.astype(F32)).astype(BF16)
        c_s[d] = decay * c_mem + _dot_tn(wv, k)
        n_s[d] = decay * n_mem + jnp.sum(w_c * k.astype(F32), axis=0, keepdims=True)
        m_s[d] = jnp.broadcast_to(m_new, m_s.shape[1:])

    ncc, ncl = lc // t, ll // t
    for i in range(ncc):
        chunk(ll + i * t, 0, hf)
        chunk(ll + (ncc - 1 - i) * t, 1, hb)

    def body(i, carry):
        chunk(pl.multiple_of(i * t, t), 0, hf)
        chunk(pl.multiple_of((ncl - 1 - i) * t, t), 1, hb)
        return carry

    lax.fori_loop(0, ncl, body, 0)

    def epilogue(i, carry):
        s = pl.multiple_of(i * t, t)
        hs = hf[pl.ds(s, t), :] + hb[pl.ds(s, t), :]
        y = _rms(hs, hg_ref[...])
        out_ref[0, pl.ds(s, t), :] = (y * _sigmoid(o_ref[0, pl.ds(s, t), :].astype(F32))).astype(BF16)
        return carry

    lax.fori_loop(0, ltot // t, epilogue, 0)


def _mlstm_scan(qk, v, o, gates, conv_w, conv_b, gate_b, head_g, ll, lc):
    bsz, ltot, _ = qk.shape
    nh = M_HEADS
    dk = qk.shape[2] // (2 * nh)
    dv = v.shape[2] // nh
    t = SCAN_CHUNK
    g4 = gates.reshape(bsz, ltot, 4, nh).transpose(0, 3, 1, 2)
    g4t = g4.transpose(0, 1, 3, 2)
    gb = gate_b.reshape(4, nh).T
    return pl.pallas_call(
        functools.partial(_mlstm_kernel, t, ll, lc),
        grid=(bsz, nh),
        in_specs=[
            pl.BlockSpec((1, ltot, dk), lambda b, h: (b, 0, h)),
            pl.BlockSpec((1, ltot, dk), lambda b, h: (b, 0, nh + h)),
            pl.BlockSpec((1, ltot, dv), lambda b, h: (b, 0, h)),
            pl.BlockSpec((1, ltot, dv), lambda b, h: (b, 0, h)),
            pl.BlockSpec((1, 1, ltot, 4), lambda b, h: (b, h, 0, 0)),
            pl.BlockSpec((1, 1, 4, ltot), lambda b, h: (b, h, 0, 0)),
            pl.BlockSpec((3, dk), lambda b, h: (0, h)),
            pl.BlockSpec((3, dk), lambda b, h: (0, nh + h)),
            pl.BlockSpec((1, dk), lambda b, h: (0, h)),
            pl.BlockSpec((1, dk), lambda b, h: (0, nh + h)),
            pl.BlockSpec((1, 1, 4), lambda b, h: (h, 0, 0)),
            pl.BlockSpec((1, 4, 1), lambda b, h: (h, 0, 0)),
            pl.BlockSpec((1, dv), lambda b, h: (0, h)),
        ],
        out_specs=pl.BlockSpec((1, ltot, dv), lambda b, h: (b, 0, h)),
        out_shape=jax.ShapeDtypeStruct((bsz, ltot, nh * dv), BF16),
        scratch_shapes=[
            pltpu.VMEM((ltot, dk), BF16), pltpu.VMEM((ltot, dk), BF16),
            pltpu.VMEM((ltot, dv), F32), pltpu.VMEM((ltot, dv), F32),
            pltpu.VMEM((2, dv, dk), F32), pltpu.VMEM((2, 1, dk), F32), pltpu.VMEM((2, 8, LANES), F32),
        ],
        compiler_params=_cparams(("arbitrary", "arbitrary")),
        name="mlstm_scan",
    )(qk, qk, v, o, g4, g4t, conv_w, conv_w, conv_b.reshape(1, -1), conv_b.reshape(1, -1),
      gb.reshape(nh, 1, 4), gb.reshape(nh, 4, 1), head_g.reshape(1, -1))


def _router_logits_t(h2, rhi_ref, rlo_ref):
    h_hi, h_lo = _split_bf16(h2)
    lg = _dot(h_hi, rhi_ref[...]) + _dot(h_lo, rhi_ref[...]) + _dot(h_hi, rlo_ref[...])
    return lg.T[:N_EXPERTS, :]


def _out0_kernel(n_lat_tiles, hg_ref, xl_ref, xc_ref, g1_ref, sc_ref, sh_ref, ng_ref, wo_ref, rhi_ref, rlo_ref,
                 s_ref, h2_ref, lg_ref):
    j = pl.program_id(1)
    x = jnp.where(j < n_lat_tiles, xl_ref[0], xc_ref[0])
    s = x + g1_ref[0] * _dot(hg_ref[0], wo_ref[...])
    s_ref[0] = s
    h2 = _rms(s, ng_ref[...]) * (1.0 + sc_ref[0]) + sh_ref[0]
    h2_ref[...] = h2
    lg_ref[...] = _router_logits_t(h2, rhi_ref, rlo_ref)


def _out0(hg, x, ctx, modv, rows, norm_g, w_out, r_hi, r_lo, tm):
    bsz, ll, d = x.shape
    lc = ctx.shape[1]
    nl, nc = ll // tm, lc // tm
    nt = nl + nc
    ltot = ll + lc
    return pl.pallas_call(
        functools.partial(_out0_kernel, nl),
        grid=(bsz, nt),
        in_specs=[
            pl.BlockSpec((1, tm, hg.shape[2]), lambda b, j: (b, j, 0)),
            pl.BlockSpec((1, tm, d), lambda b, j: (b, jnp.minimum(j, nl - 1), 0)),
            pl.BlockSpec((1, tm, d), lambda b, j: (b, jnp.maximum(j - nl, 0), 0)),
            _mod_spec(d, 0, 2, nl, rows), _mod_spec(d, 0, 4, nl, rows), _mod_spec(d, 0, 3, nl, rows),
            pl.BlockSpec((1, d), lambda b, j: (0, 0)),
            pl.BlockSpec(w_out.shape, lambda b, j: (0, 0)),
            pl.BlockSpec(r_hi.shape, lambda b, j: (0, 0)),
            pl.BlockSpec(r_lo.shape, lambda b, j: (0, 0)),
        ],
        out_specs=[
            pl.BlockSpec((1, tm, d), lambda b, j: (b, j, 0)),
            pl.BlockSpec((tm, d), lambda b, j: (b * nt + j, 0)),
            pl.BlockSpec((N_EXPERTS, tm), lambda b, j: (0, b * nt + j)),
        ],
        out_shape=[
            jax.ShapeDtypeStruct((bsz, ltot, d), F32),
            jax.ShapeDtypeStruct((bsz * ltot, d), F32),
            jax.ShapeDtypeStruct((N_EXPERTS, bsz * ltot), F32),
        ],
        compiler_params=_cparams(("arbitrary", "arbitrary")),
        name="mlstm_out_proj",
    )(hg, x, ctx, modv, modv, modv, norm_g, w_out, r_hi, r_lo)


def _route_kernel(lg_ref, bias_ref, cls_ref, w_ref):
    s = _sigmoid(lg_ref[...])
    sel = s + bias_ref[...]
    srow = [s[e:e + 1, :] for e in range(N_EXPERTS)]
    row = [sel[e:e + 1, :] for e in range(N_EXPERTS)]
    best = jnp.zeros(row[0].shape, jnp.int32)
    best_score = None
    for g in range(N_GROUPS):
        r = row[g * E_PER_GROUP:(g + 1) * E_PER_GROUP]
        score = None
        for lo, hi in zip(PAIR_LO, PAIR_HI):
            pair = r[lo] + r[hi]
            score = pair if score is None else jnp.maximum(score, pair)
        if g == 0:
            best_score = score
        else:
            better = score > best_score
            best = jnp.where(better, g, best)
            best_score = jnp.where(better, score, best_score)
    gs = [row[i] for i in range(E_PER_GROUP)]
    gw = [srow[i] for i in range(E_PER_GROUP)]
    for g in range(1, N_GROUPS):
        hit = best == g
        gs = [jnp.where(hit, row[g * E_PER_GROUP + i], gs[i]) for i in range(E_PER_GROUP)]
        gw = [jnp.where(hit, srow[g * E_PER_GROUP + i], gw[i]) for i in range(E_PER_GROUP)]
    keep = []
    for i in range(E_PER_GROUP):
        beaten = jnp.zeros(best.shape, jnp.int32)
        for j in range(E_PER_GROUP):
            if j == i:
                continue
            wins = (gs[j] > gs[i]) | ((gs[j] == gs[i]) & (j < i))
            beaten = beaten + wins.astype(jnp.int32)
        keep.append(beaten < 2)
    pair_id = jnp.zeros(best.shape, jnp.int32)
    w_lo = jnp.zeros(best.shape, F32)
    w_hi = jnp.zeros(best.shape, F32)
    for p, (lo, hi) in enumerate(zip(PAIR_LO, PAIR_HI)):
        hit = keep[lo] & keep[hi]
        pair_id = jnp.where(hit, p, pair_id)
        w_lo = jnp.where(hit, gw[lo], w_lo)
        w_hi = jnp.where(hit, gw[hi], w_hi)
    tot = w_lo + w_hi
    cls_ref[...] = best * N_PAIRS + pair_id
    w_ref[0:1, :] = w_lo / tot
    w_ref[1:2, :] = w_hi / tot


def _route(logits_t, router_bias):
    n = logits_t.shape[1]
    tn = next(cand for cand in (2048, 1024, 512, 256, 128) if n % cand == 0)
    return pl.pallas_call(
        _route_kernel,
        grid=(n // tn,),
        in_specs=[
            pl.BlockSpec((N_EXPERTS, tn), lambda i: (0, i)),
            pl.BlockSpec((N_EXPERTS, 1), lambda i: (0, 0)),
        ],
        out_specs=[pl.BlockSpec((1, tn), lambda i: (0, i)), pl.BlockSpec((2, tn), lambda i: (0, i))],
        out_shape=[jax.ShapeDtypeStruct((1, n), jnp.int32), jax.ShapeDtypeStruct((2, n), F32)],
        compiler_params=_cparams(("arbitrary",)),
        name="moe_route",
    )(logits_t, router_bias.reshape(N_EXPERTS, 1).astype(F32))


def _moe_kernel(tm, e0_ref, e1_ref, nv_ref, src_ref, dst_ref, w_ref, h2_hbm,
                wg0_ref, wu0_ref, wd0_ref, wg1_ref, wu1_ref, wd1_ref, f_hbm, xbuf, ybuf, gsem, ssem):
    del e0_ref, e1_ref
    tile = pl.program_id(0)
    nv = nv_ref[tile]

    @pl.when(nv > 0)
    def _():
        def gather(r, carry):
            tok = src_ref[0, 0, r]
            pltpu.make_async_copy(h2_hbm.at[pl.ds(tok, 1)], xbuf.at[pl.ds(r, 1)], gsem).start()
            return carry

        lax.fori_loop(0, tm, gather, 0)
        pltpu.make_async_copy(h2_hbm.at[pl.ds(0, tm)], xbuf, gsem).wait()

        x = xbuf[...].astype(BF16)
        w = w_ref[...]
        a0 = _silu(_dot(x, wg0_ref[0])) * _dot(x, wu0_ref[0]) * w[:, 0:1]
        a1 = _silu(_dot(x, wg1_ref[0])) * _dot(x, wu1_ref[0]) * w[:, 1:2]
        ybuf[...] = _dot(a0.astype(BF16), wd0_ref[0]) + _dot(a1.astype(BF16), wd1_ref[0])

        def scatter(r, carry):
            tok = dst_ref[0, 0, r]
            pltpu.make_async_copy(ybuf.at[pl.ds(r, 1)], f_hbm.at[pl.ds(tok, 1)], ssem).start()
            return carry

        lax.fori_loop(0, nv, scatter, 0)
        nv8 = pl.multiple_of((nv // 8) * 8, 8)

        @pl.when(nv8 > 0)
        def _():
            pltpu.make_async_copy(ybuf.at[pl.ds(0, nv8)], f_hbm.at[pl.ds(0, nv8)], ssem).wait()

        def drain(r, carry):
            pltpu.make_async_copy(ybuf.at[pl.ds(0, 1)], f_hbm.at[pl.ds(0, 1)], ssem).wait()
            return carry

        lax.fori_loop(0, nv - nv8, drain, 0)


def _moe(h2, cls, w2, dst_of_token, wg, wu, wd):
    n, d = h2.shape
    tm = MOE_TILE
    n_tiles = n // tm + N_CLASSES
    p_rows = n_tiles * tm
    cls = cls.reshape(n)
    order = jnp.argsort(cls, stable=True).astype(jnp.int32)
    cls_sorted = cls[order]
    counts = jnp.sum(cls[:, None] == jnp.arange(N_CLASSES, dtype=jnp.int32)[None, :], axis=0).astype(jnp.int32)
    tiles_per = (counts + tm - 1) // tm
    tile_end = jnp.cumsum(tiles_per)
    tile_start = tile_end - tiles_per
    first_sorted = jnp.cumsum(counts) - counts
    rank = jnp.arange(n, dtype=jnp.int32) - first_sorted[cls_sorted]
    slot = tile_start[cls_sorted] * tm + rank
    src = jnp.zeros((p_rows,), jnp.int32).at[slot].set(order)
    dst = jnp.zeros((p_rows,), jnp.int32).at[slot].set(dst_of_token[order])
    w_rows = jnp.zeros((p_rows, 2), F32).at[slot].set(w2.T[order])
    tile_id = jnp.arange(n_tiles, dtype=jnp.int32)
    n_used = tile_end[-1]
    tile_cls = jnp.searchsorted(tile_end, jnp.minimum(tile_id, n_used - 1), side="right").astype(jnp.int32)
    tile_cls = jnp.minimum(tile_cls, N_CLASSES - 1)
    group, pair = tile_cls // N_PAIRS, tile_cls % N_PAIRS
    e0 = group * E_PER_GROUP + jnp.asarray(PAIR_LO, jnp.int32)[pair]
    e1 = group * E_PER_GROUP + jnp.asarray(PAIR_HI, jnp.int32)[pair]
    valid = jnp.clip(counts[tile_cls] - (tile_id - tile_start[tile_cls]) * tm, 0, tm)
    valid = jnp.where(tile_id < n_used, valid, 0).astype(jnp.int32)

    de = wg.shape[2]
    idx = pl.BlockSpec((1, 1, tm), lambda i, e0, e1, nv: (i, 0, 0), memory_space=pltpu.SMEM)
    up0 = pl.BlockSpec((1, d, de), lambda i, e0, e1, nv: (e0[i], 0, 0))
    up1 = pl.BlockSpec((1, d, de), lambda i, e0, e1, nv: (e1[i], 0, 0))
    dn0 = pl.BlockSpec((1, de, d), lambda i, e0, e1, nv: (e0[i], 0, 0))
    dn1 = pl.BlockSpec((1, de, d), lambda i, e0, e1, nv: (e1[i], 0, 0))
    return pl.pallas_call(
        functools.partial(_moe_kernel, tm),
        grid_spec=pltpu.PrefetchScalarGridSpec(
            num_scalar_prefetch=3,
            grid=(n_tiles,),
            in_specs=[
                idx, idx,
                pl.BlockSpec((tm, 2), lambda i, e0, e1, nv: (i, 0)),
                pl.BlockSpec(memory_space=pl.ANY),
                up0, up0, dn0, up1, up1, dn1,
            ],
            out_specs=pl.BlockSpec(memory_space=pl.ANY),
            scratch_shapes=[
                pltpu.VMEM((tm, d), F32), pltpu.VMEM((tm, d), F32),
                pltpu.SemaphoreType.DMA(()), pltpu.SemaphoreType.DMA(()),
            ],
        ),
        out_shape=jax.ShapeDtypeStruct((n, d), F32),
        compiler_params=_cparams(("arbitrary",)),
        name="moe_experts",
    )(e0, e1, valid, src.reshape(n_tiles, 1, tm), dst.reshape(n_tiles, 1, tm), w_rows, h2,
      wg, wu, wd, wg, wu, wd)


def _hproj_kernel(n_lat_tiles, rows, sl_ref, fl_ref, sc_ref_, fc_ref, g2_ref, sc_ref, sh_ref, ng_ref, lb_ref,
                  wq_ref, wzf_ref, wzb_ref, wi_ref, wg_ref,
                  s2_ref, q_ref, kf_ref, kb_ref, lff_ref, lfb_ref, i_ref, g_ref, xt):
    j = pl.program_id(1)
    d = xt.shape[1]
    g2 = g2_ref[0]

    @pl.when(j < n_lat_tiles)
    def _():
        for c in range(COL_TILE):
            s = sl_ref[0, :, c * d:(c + 1) * d] + g2 * fl_ref[0, :, c * d:(c + 1) * d]
            s2_ref[0, :, c * d:(c + 1) * d] = s
            xt[c * rows:(c + 1) * rows, :] = s

    @pl.when(j >= n_lat_tiles)
    def _():
        xt[...] = sc_ref_[0] + g2 * fc_ref[0]

    h = (_rms(xt[...], ng_ref[...]) * (1.0 + sc_ref[0]) + sh_ref[0]).astype(BF16)
    q_ref[0] = _silu(_dot(h, wq_ref[...])).astype(BF16)
    i_ref[0] = _dot(h, wi_ref[...]).astype(BF16)
    g_ref[0] = _dot(h, wg_ref[...]).astype(BF16)
    for dd, (wz_ref, k_ref, lf_ref) in enumerate(((wzf_ref, kf_ref, lff_ref), (wzb_ref, kb_ref, lfb_ref))):
        z = _dot(h, wz_ref[...])
        log_lb = lb_ref[2 * dd:2 * dd + 1, :]
        log_1mlb = lb_ref[2 * dd + 1:2 * dd + 2, :]
        ls = _log_sigmoid(z)
        a = log_1mlb + ls
        lf_ref[0] = jnp.maximum(log_lb, a) + jnp.log1p(jnp.exp(-jnp.abs(log_lb - a)))
        k_ref[0] = jnp.exp(a - z).astype(BF16)


def _hproj(s1, f0, modv, rows_mod, norm_g, lbtab, ws, ll, lc):
    bsz, ltot, d = s1.shape
    rows = ll // GRID_W
    tm = COL_TILE * rows
    nl, nc = GRID_W // COL_TILE, lc // tm
    s_cm = s1.reshape(bsz, ltot // GRID_W, GRID_W * d)
    f_tok = f0.reshape(bsz, ltot, d)
    f_cm = f0.reshape(bsz, ltot // GRID_W, GRID_W * d)
    cm = pl.BlockSpec((1, rows, COL_TILE * d), lambda b, j: (b, 0, jnp.minimum(j, nl - 1)))
    ctx = pl.BlockSpec((1, tm, d), lambda b, j: (b, ll // tm + jnp.maximum(j - nl, 0), 0))
    wspec = pl.BlockSpec((d, d), lambda b, j: (0, 0))
    tok = pl.BlockSpec((1, tm, d), lambda b, j: (b, j, 0))

    def g2_map(b, j):
        row = jnp.where(j < nl, b, rows_mod - 1)
        return ((0 * rows_mod + row) * N_ADA + 5, 0, 0)

    outs = pl.pallas_call(
        functools.partial(_hproj_kernel, nl, rows),
        grid=(bsz, nl + nc),
        in_specs=[
            cm, cm, ctx, ctx,
            pl.BlockSpec((1, 1, d), g2_map),
            _mod_spec(d, 1, 1, nl, rows_mod), _mod_spec(d, 1, 0, nl, rows_mod),
            pl.BlockSpec((1, d), lambda b, j: (0, 0)),
            pl.BlockSpec((4, d), lambda b, j: (0, 0)),
            wspec, wspec, wspec, wspec, wspec,
        ],
        out_specs=[cm, tok, tok, tok, tok, tok, tok, tok],
        out_shape=[jax.ShapeDtypeStruct((bsz, rows, GRID_W * d), F32)]
        + [jax.ShapeDtypeStruct((bsz, ltot, d), dt) for dt in (BF16, BF16, BF16, F32, F32, BF16, BF16)],
        scratch_shapes=[pltpu.VMEM((tm, d), F32)],
        compiler_params=_cparams(("arbitrary", "arbitrary")),
        name="hgrn_in_proj",
    )(s_cm, f_cm, s1, f_tok, modv, modv, modv, norm_g, lbtab, *ws)
    return outs


def _hgrn_kernel(t, ll, lc, q_ref, kf_ref, kb_ref, lff_ref, lfb_ref, i_ref, g_ref, hg_ref, out_ref,
                 of, ob, st_s):
    st_s[...] = jnp.zeros(st_s.shape, F32)
    row = lax.broadcasted_iota(jnp.int32, (t, 1), 0)
    ti = lax.broadcasted_iota(jnp.int32, (t, t), 0)
    si = lax.broadcasted_iota(jnp.int32, (t, t), 1)
    xor = ti ^ si
    levels = []
    m = 2
    while m <= t:
        levels.append(m)
        m *= 2

    def cumsum(lf, d):
        b = lf
        sh = 1
        while sh < t:
            if d == 0:
                b = b + jnp.where(row >= sh, pltpu.roll(b, sh, 0), 0.0)
            else:
                b = b + jnp.where(row < t - sh, pltpu.roll(b, t - sh, 0), 0.0)
            sh *= 2
        return b

    def block_ref(b, m, d):
        half = m // 2
        pos = half - 1 if d == 0 else half
        if m >= 8:
            dk = b.shape[1]
            b3 = b.reshape(t // m, m, dk)
            return jnp.broadcast_to(b3[:, pos:pos + 1, :], b3.shape).reshape(t, dk)
        r = row % m
        out = b
        for res in range(m):
            if res == pos:
                continue
            out = jnp.where(r == res, pltpu.roll(b, (res - pos) % t, 0), out)
        return out

    def chunk(start, d, need_out, o_out):
        q = q_ref[0, pl.ds(start, t), :]
        k = (kf_ref if d == 0 else kb_ref)[0, pl.ds(start, t), :].astype(F32)
        lf = (lff_ref if d == 0 else lfb_ref)[0, pl.ds(start, t), :]
        iv = i_ref[0, pl.ds(start, t), :]
        b = cumsum(lf, d)
        b_tot = b[t - 1:t, :] if d == 0 else b[0:1, :]
        st = st_s[d]
        if need_out:
            qf = q.astype(F32)
            a = _dot_nt(q, k.astype(BF16))
            for m in levels:
                e = jnp.exp(-jnp.abs(b - block_ref(b, m, d)))
                p = _dot_nt((qf * e).astype(BF16), (k * e).astype(BF16))
                a = jnp.where(xor < m // 2, a, p) if m > 2 else jnp.where(xor == 0, a, p)
            feeds = (si <= ti) if d == 0 else (si >= ti)
            a = jnp.where(feeds, a, 0.0)
            o = _dot(a.astype(BF16), iv) + _dot_nt((qf * jnp.exp(b)).astype(BF16), st.astype(BF16))
            o_out[pl.ds(start, t), :] = o
        kd = (k * jnp.exp(b_tot - b)).astype(BF16)
        st_s[d] = jnp.exp(b_tot) * st + _dot_tn(iv, kd)

    ncc, ncl = lc // t, ll // t
    for i in range(ncc):
        chunk(ll + i * t, 0, False, None)
        chunk(ll + (ncc - 1 - i) * t, 1, False, None)

    def body(i, carry):
        chunk(pl.multiple_of(i * t, t), 0, True, of)
        chunk(pl.multiple_of((ncl - 1 - i) * t, t), 1, True, ob)
        return carry

    lax.fori_loop(0, ncl, body, 0)

    def epilogue(i, carry):
        s = pl.multiple_of(i * t, t)
        y = _rms(of[pl.ds(s, t), :] + ob[pl.ds(s, t), :], hg_ref[...])
        out_ref[0, pl.ds(s, t), :] = (y * _silu(g_ref[0, pl.ds(s, t), :].astype(F32))).astype(BF16)
        return carry

    lax.fori_loop(0, ncl, epilogue, 0)


def _hgrn_scan(q, kf, kb, lff, lfb, iv, g, head_g, ll, lc):
    bsz, ltot, hk = q.shape
    nh = H_HEADS
    dk = hk // nh
    t = SCAN_CHUNK
    full = pl.BlockSpec((1, ltot, dk), lambda b, h: (b, 0, h))
    lat = pl.BlockSpec((1, ll, dk), lambda b, h: (b, 0, h))
    return pl.pallas_call(
        functools.partial(_hgrn_kernel, t, ll, lc),
        grid=(bsz, nh),
        in_specs=[full, full, full, full, full, full, lat, pl.BlockSpec((1, dk), lambda b, h: (0, h))],
        out_specs=lat,
        out_shape=jax.ShapeDtypeStruct((bsz, ll, hk), BF16),
        scratch_shapes=[pltpu.VMEM((ll, dk), F32), pltpu.VMEM((ll, dk), F32), pltpu.VMEM((2, dk, dk), F32)],
        compiler_params=_cparams(("arbitrary", "arbitrary")),
        name="hgrn_scan",
    )(q, kf, kb, lff, lfb, iv, g, head_g.reshape(1, -1))


def _out1_kernel(rows, hg_ref, s_ref, g1_ref, sc_ref, sh_ref, ng_ref, wo_ref, rhi_ref, rlo_ref,
                 s3_ref, h2_ref, lg_ref, st):
    d = st.shape[1]
    y = g1_ref[0] * _dot(hg_ref[0], wo_ref[...])
    for c in range(COL_TILE):
        s = s_ref[0, :, c * d:(c + 1) * d] + y[c * rows:(c + 1) * rows, :]
        s3_ref[0, :, c * d:(c + 1) * d] = s
        st[c * rows:(c + 1) * rows, :] = s
    h2 = _rms(st[...], ng_ref[...]) * (1.0 + sc_ref[0]) + sh_ref[0]
    h2_ref[...] = h2
    lg_ref[...] = _router_logits_t(h2, rhi_ref, rlo_ref)


def _out1(hg, s2_cm, modv, rows_mod, norm_g, w_out, r_hi, r_lo):
    bsz, ll, hv = hg.shape
    rows = s2_cm.shape[1]
    d = s2_cm.shape[2] // GRID_W
    tm = COL_TILE * rows
    nl = GRID_W // COL_TILE
    cm = pl.BlockSpec((1, rows, COL_TILE * d), lambda b, j: (b, 0, j))
    return pl.pallas_call(
        functools.partial(_out1_kernel, rows),
        grid=(bsz, nl),
        in_specs=[
            pl.BlockSpec((1, tm, hv), lambda b, j: (b, j, 0)),
            cm,
            _mod_spec(d, 1, 2, nl, rows_mod), _mod_spec(d, 1, 4, nl, rows_mod), _mod_spec(d, 1, 3, nl, rows_mod),
            pl.BlockSpec((1, d), lambda b, j: (0, 0)),
            pl.BlockSpec(w_out.shape, lambda b, j: (0, 0)),
            pl.BlockSpec(r_hi.shape, lambda b, j: (0, 0)),
            pl.BlockSpec(r_lo.shape, lambda b, j: (0, 0)),
        ],
        out_specs=[
            cm,
            pl.BlockSpec((tm, d), lambda b, j: (b * nl + j, 0)),
            pl.BlockSpec((N_EXPERTS, tm), lambda b, j: (0, b * nl + j)),
        ],
        out_shape=[
            jax.ShapeDtypeStruct(s2_cm.shape, F32),
            jax.ShapeDtypeStruct((bsz * ll, d), F32),
            jax.ShapeDtypeStruct((N_EXPERTS, bsz * ll), F32),
        ],
        scratch_shapes=[pltpu.VMEM((tm, d), F32)],
        compiler_params=_cparams(("arbitrary", "arbitrary")),
        name="hgrn_out_proj",
    )(hg, s2_cm, modv, modv, modv, norm_g, w_out, r_hi, r_lo)


def _final_kernel(s_ref, f_ref, g2_ref, fg_ref, o_ref):
    o_ref[0] = _rms(s_ref[0] + g2_ref[0] * f_ref[0], fg_ref[...])


def _final(s3, f1, modv, rows_mod, final_g, tm):
    bsz, ll, d = s3.shape
    nl = ll // tm
    tok = pl.BlockSpec((1, tm, d), lambda b, j: (b, j, 0))
    return pl.pallas_call(
        _final_kernel,
        grid=(bsz, nl),
        in_specs=[tok, tok, _mod_spec(d, 1, 5, nl, rows_mod), pl.BlockSpec((1, d), lambda b, j: (0, 0))],
        out_specs=tok,
        out_shape=jax.ShapeDtypeStruct((bsz, ll, d), F32),
        compiler_params=_cparams(("arbitrary", "arbitrary")),
        name="final_norm",
    )(s3, f1.reshape(bsz, ll, d), modv, final_g)


def kernel(x, c, ctx, c_ctx, ada_w, ada_b, norm_mix_g, norm_ffn_g, final_g, m_w_in, m_conv_w, m_conv_b, m_gate_b,
           m_head_g, m_w_out, h_w_in, h_lower_bounds, h_head_g, h_w_out, router_w, router_bias, e_w_gate,
           e_w_up, e_w_down):
    bsz, ll, d = x.shape
    lc = ctx.shape[1]
    ltot = ll + lc
    depth = ada_w.shape[0]
    assert depth == 2 and ll % GRID_W == 0 and lc % SCAN_CHUNK == 0 and ll % SCAN_CHUNK == 0
    rows = ll // GRID_W
    tm = COL_TILE * rows
    assert lc % tm == 0 and ltot % GRID_W == 0 and (bsz * ltot) % MOE_TILE == 0 and (bsz * ll) % MOE_TILE == 0

    rows_mod = 8 * ((bsz + 1 + 7) // 8)
    cc = jnp.zeros((rows_mod, d), F32).at[:bsz].set(c).at[rows_mod - 1].set(c_ctx)
    modv = _ada(cc, ada_w, ada_b).reshape(depth * rows_mod * N_ADA, 1, d)

    r_pad = jnp.zeros((d, LANES), F32).at[:, :N_EXPERTS].set(router_w)
    r_hi = r_pad.astype(BF16)
    r_lo = (r_pad - r_hi.astype(F32)).astype(BF16)
    wg_all, wu_all, wd_all = e_w_gate.astype(BF16), e_w_up.astype(BF16), e_w_down.astype(BF16)

    m_qk, m_v = m_conv_w.shape[2], m_head_g.shape[1]
    w_in = m_w_in[0].astype(BF16)
    w_gates = jnp.zeros((d, LANES), BF16).at[:, :4 * M_HEADS].set(w_in[:, m_qk + 2 * m_v:])
    qk, v, o, gates = _mproj(x, ctx, modv, rows_mod, norm_mix_g[0:1], w_in[:, :m_qk], w_in[:, m_qk:m_qk + m_v],
                             w_in[:, m_qk + m_v:m_qk + 2 * m_v], w_gates, tm)
    hg = _mlstm_scan(qk, v, o, gates, m_conv_w[0], m_conv_b[0], m_gate_b[0], m_head_g[0], ll, lc)
    s1, h2, lg = _out0(hg, x, ctx, modv, rows_mod, norm_ffn_g[0:1], m_w_out[0].astype(BF16), r_hi, r_lo, tm)
    cls, w2 = _route(lg, router_bias)
    n0 = bsz * ltot
    f0 = _moe(h2, cls, w2, jnp.arange(n0, dtype=jnp.int32), wg_all[0], wu_all[0], wd_all[0])

    lbs = jnp.cumsum(jax.nn.softmax(h_lower_bounds.astype(F32), axis=0), axis=0)
    lb = (lbs - lbs[0])[1].reshape(2, -1)
    lbtab = jnp.stack([jnp.log(lb[0]), jnp.log1p(-lb[0]), jnp.log(lb[1]), jnp.log1p(-lb[1])])
    hw = h_w_in[0].astype(BF16)
    hk = lb.shape[1]
    ws = (hw[:, :hk], hw[:, hk:2 * hk], hw[:, 2 * hk:3 * hk], hw[:, 3 * hk:3 * hk + d], hw[:, 3 * hk + d:])
    s2_cm, q, kf, kb, lff, lfb, iv, g = _hproj(s1, f0, modv, rows_mod, norm_mix_g[1:2], lbtab, ws, ll, lc)
    hg1 = _hgrn_scan(q, kf, kb, lff, lfb, iv, g, h_head_g[0], ll, lc)
    s3_cm, h2b, lgb = _out1(hg1, s2_cm, modv, rows_mod, norm_ffn_g[1:2], h_w_out[0].astype(BF16), r_hi, r_lo)
    clsb, w2b = _route(lgb, router_bias)
    n1 = bsz * ll
    tokid = jnp.arange(n1, dtype=jnp.int32)
    within = tokid % ll
    raster = (tokid // ll) * ll + (within % rows) * GRID_W + within // rows
    f1 = _moe(h2b, clsb, w2b, raster, wg_all[1], wu_all[1], wd_all[1])

    return _final(s3_cm.reshape(bsz, ll, d), f1, modv, rows_mod, final_g.reshape(1, d), tm)
```

```python
import functools

import numpy as np
import jax
import jax.numpy as jnp
from jax import lax
from jax.experimental import pallas as pl
from jax.experimental.pallas import tpu as pltpu

F32 = jnp.float32
BF16 = jnp.bfloat16

EPS = 1e-6
N_ADA = 6
GRID_W = 64
M_HEADS = 4
H_HEADS = 8
N_EXPERTS = 16
N_GROUPS = 4
E_PER_GROUP = N_EXPERTS // N_GROUPS
N_PAIRS = 6
N_CLASSES = N_GROUPS * N_PAIRS
PAIR_LO = (0, 0, 0, 1, 1, 2)
PAIR_HI = (1, 2, 3, 2, 3, 3)

LANES = 128
COL_TILE = 8
SCAN_CHUNK = 128
MOE_TILE = 256
MOE_DMA_GROUPS = 4
VMEM_LIMIT = 56 * 1024 * 1024


def _cparams(sem):
    return pltpu.CompilerParams(dimension_semantics=sem, vmem_limit_bytes=VMEM_LIMIT)


def _dot(a, b):
    return jnp.dot(a, b, preferred_element_type=F32)


def _dot_nt(a, b):
    return lax.dot_general(a, b, (((1,), (1,)), ((), ())), preferred_element_type=F32)


def _dot_tn(a, b):
    return lax.dot_general(a, b, (((0,), (0,)), ((), ())), preferred_element_type=F32)


def _split_bf16(a):
    hi = a.astype(BF16)
    lo = (a - hi.astype(F32)).astype(BF16)
    return hi, lo


def _sigmoid(x):
    return 1.0 / (1.0 + jnp.exp(-x))


def _silu(x):
    return x * _sigmoid(x)


def _log_sigmoid(x):
    return jnp.minimum(x, 0.0) - jnp.log1p(jnp.exp(-jnp.abs(x)))


def _rms(x, g):
    return x * lax.rsqrt(jnp.mean(x * x, axis=-1, keepdims=True) + EPS) * g


def _ada_kernel(c_ref, w_ref, b_ref, o_ref):
    a = _silu(c_ref[...])
    a_hi, a_lo = _split_bf16(a)
    w_hi, w_lo = _split_bf16(w_ref[0])
    acc = _dot(a_hi, w_hi) + _dot(a_lo, w_hi) + _dot(a_hi, w_lo)
    o_ref[0] = acc + b_ref[0]


def _ada(cc, ada_w, ada_b):
    depth, d, n = ada_w.shape
    tn = 1024
    rows = cc.shape[0]
    return pl.pallas_call(
        _ada_kernel,
        grid=(depth, n // tn),
        in_specs=[
            pl.BlockSpec((rows, d), lambda l, j: (0, 0)),
            pl.BlockSpec((1, d, tn), lambda l, j: (l, 0, j)),
            pl.BlockSpec((1, 1, tn), lambda l, j: (l, 0, j)),
        ],
        out_specs=pl.BlockSpec((1, rows, tn), lambda l, j: (l, 0, j)),
        out_shape=jax.ShapeDtypeStruct((depth, rows, n), F32),
        compiler_params=_cparams(("arbitrary", "arbitrary")),
        name="ada_mod",
    )(cc, ada_w, ada_b.reshape(depth, 1, n))


def _mproj_kernel(n_lat_tiles, xl_ref, xc_ref, sc_ref, sh_ref, g_ref, wqk_ref, wv_ref, wo_ref, wg_ref,
                  qk_ref, v_ref, o_ref, gt_ref):
    j = pl.program_id(1)
    x = jnp.where(j < n_lat_tiles, xl_ref[0], xc_ref[0])
    h = (_rms(x, g_ref[...]) * (1.0 + sc_ref[0]) + sh_ref[0]).astype(BF16)
    qk_ref[0] = _dot(h, wqk_ref[...]).astype(BF16)
    v_ref[0] = _dot(h, wv_ref[...]).astype(BF16)
    o_ref[0] = _dot(h, wo_ref[...]).astype(BF16)
    gt_ref[0] = _dot(h, wg_ref[...])[:, :4 * M_HEADS]


def _mod_spec(d, layer, k, n_lat_tiles, rows):
    def imap(b, j):
        row = jnp.where(j < n_lat_tiles, b, rows - 1)
        return ((layer * rows + row) * N_ADA + k, 0, 0)
    return pl.BlockSpec((1, 1, d), imap)


def _mproj(x, ctx, modv, rows, norm_g, w_qk, w_v, w_o, w_g, tm):
    bsz, ll, d = x.shape
    lc = ctx.shape[1]
    nl, nc = ll // tm, lc // tm
    ltot = ll + lc
    wspec = lambda n: pl.BlockSpec((d, n), lambda b, j: (0, 0))
    tok = lambda n: pl.BlockSpec((1, tm, n), lambda b, j: (b, j, 0))
    return pl.pallas_call(
        functools.partial(_mproj_kernel, nl),
        grid=(bsz, nl + nc),
        in_specs=[
            pl.BlockSpec((1, tm, d), lambda b, j: (b, jnp.minimum(j, nl - 1), 0)),
            pl.BlockSpec((1, tm, d), lambda b, j: (b, jnp.maximum(j - nl, 0), 0)),
            _mod_spec(d, 0, 1, nl, rows), _mod_spec(d, 0, 0, nl, rows),
            pl.BlockSpec((1, d), lambda b, j: (0, 0)),
            wspec(w_qk.shape[1]), wspec(w_v.shape[1]), wspec(w_o.shape[1]), wspec(w_g.shape[1]),
        ],
        out_specs=[tok(w_qk.shape[1]), tok(w_v.shape[1]), tok(w_o.shape[1]), tok(4 * M_HEADS)],
        out_shape=[
            jax.ShapeDtypeStruct((bsz, ltot, w_qk.shape[1]), BF16),
            jax.ShapeDtypeStruct((bsz, ltot, w_v.shape[1]), BF16),
            jax.ShapeDtypeStruct((bsz, ltot, w_o.shape[1]), BF16),
            jax.ShapeDtypeStruct((bsz, ltot, 4 * M_HEADS), F32),
        ],
        compiler_params=_cparams(("arbitrary", "arbitrary")),
        name="mlstm_in_proj",
    )(x, ctx, modv, modv, norm_g, w_qk, w_v, w_o, w_g)


def _mlstm_kernel(t, ll, lc, q_ref, k_ref, v_ref, o_ref, gc_ref, gr_ref, cwq_ref, cwk_ref, cbq_ref, cbk_ref,
                  gbc_ref, gbr_ref, hg_ref, out_ref, qs, ks, hf, hb, c_s, n_s, m_s):
    ltot = ll + lc
    dk = q_ref.shape[2]
    row = lax.broadcasted_iota(jnp.int32, (ltot, 1), 0)
    first = (row == 0) | (row == ll)
    last = (row == ll - 1) | (row == ltot - 1)

    def conv(x_ref, w_ref, b_ref):
        x = x_ref[0].astype(F32)
        w = w_ref[...]
        xp = jnp.where(first, 0.0, pltpu.roll(x, 1, 0))
        xn = jnp.where(last, 0.0, pltpu.roll(x, ltot - 1, 0))
        return _silu(xp * w[0:1] + x * w[1:2] + xn * w[2:3] + b_ref[...])

    qs[...] = (conv(q_ref, cwq_ref, cbq_ref) * (dk ** -0.5)).astype(BF16)
    ks[...] = conv(k_ref, cwk_ref, cbk_ref).astype(BF16)

    c_s[...] = jnp.zeros(c_s.shape, F32)
    n_s[...] = jnp.zeros(n_s.shape, F32)
    m_s[...] = jnp.zeros(m_s.shape, F32)

    ti = lax.broadcasted_iota(jnp.int32, (t, t), 0)
    si = lax.broadcasted_iota(jnp.int32, (t, t), 1)

    def chunk(start, d, h_out):
        q = qs[pl.ds(start, t), :]
        k = ks[pl.ds(start, t), :]
        v = v_ref[0, pl.ds(start, t), :]
        gc = gc_ref[0, 0, pl.ds(start, t), :] + gbc_ref[0]
        gr = gr_ref[0, 0, :, pl.ds(start, t)] + gbr_ref[0]
        ig_c, lf_c = gc[:, 2 * d:2 * d + 1], _log_sigmoid(gc[:, 2 * d + 1:2 * d + 2])
        ig_r, lf_r = gr[2 * d:2 * d + 1, :], _log_sigmoid(gr[2 * d + 1:2 * d + 2, :])
        seen = (si <= ti) if d == 0 else (si >= ti)
        seen_t = (ti <= si) if d == 0 else (ti >= si)
        b_c = jnp.sum(jnp.where(seen, lf_r, 0.0), axis=1, keepdims=True)
        b_r = jnp.sum(jnp.where(seen_t, lf_c, 0.0), axis=0, keepdims=True)
        total = jnp.sum(lf_r, axis=1, keepdims=True)
        c_mem = c_s[d]
        n_mem = n_s[d]
        m_prev = m_s[d][0:1, 0:1]

        log_d = jnp.where(seen, b_c - b_r + ig_r, -jnp.inf)
        log_inter = b_c + m_prev
        m_t = jnp.maximum(log_inter, jnp.max(log_d, axis=1, keepdims=True))
        w_ts = jnp.exp(log_d - m_t) * _dot_nt(q, k)
        inter = jnp.exp(log_inter - m_t)
        num = _dot(w_ts.astype(BF16), v) + inter * _dot_nt(q, c_mem.astype(BF16))
        den = jnp.sum(w_ts, axis=1, keepdims=True) + inter * jnp.sum(q.astype(F32) * n_mem, axis=1, keepdims=True)
        h_out[pl.ds(start, t), :] = num / jnp.maximum(jnp.abs(den), jnp.exp(-m_t))

        g_c = total - b_c + ig_c
        g_r = total - b_r + ig_r
        m_new = jnp.maximum(total + m_prev, jnp.max(g_r, axis=1, keepdims=True))
        w_c = jnp.exp(g_c - m_new)
        decay = jnp.exp(total + m_prev - m_new)
        wv = (w_c * v.astype(F32)).astype(BF16)
        c_s[d] = decay * c_mem + _dot_tn(wv, k)
        n_s[d] = decay * n_mem + jnp.sum(w_c * k.astype(F32), axis=0, keepdims=True)
        m_s[d] = jnp.broadcast_to(m_new, m_s.shape[1:])

    ncc, ncl = lc // t, ll // t
    for i in range(ncc):
        chunk(ll + i * t, 0, hf)
        chunk(ll + (ncc - 1 - i) * t, 1, hb)

    def body(i, carry):
        chunk(pl.multiple_of(i * t, t), 0, hf)
        chunk(pl.multiple_of((ncl - 1 - i) * t, t), 1, hb)
        return carry

    lax.fori_loop(0, ncl, body, 0)

    def epilogue(i, carry):
        s = pl.multiple_of(i * t, t)
        hs = hf[pl.ds(s, t), :] + hb[pl.ds(s, t), :]
        y = _rms(hs, hg_ref[...])
        out_ref[0, pl.ds(s, t), :] = (y * _sigmoid(o_ref[0, pl.ds(s, t), :].astype(F32))).astype(BF16)
        return carry

    lax.fori_loop(0, ltot // t, epilogue, 0)


def _mlstm_scan(qk, v, o, gates, conv_w, conv_b, gate_b, head_g, ll, lc):
    bsz, ltot, _ = qk.shape
    nh = M_HEADS
    dk = qk.shape[2] // (2 * nh)
    dv = v.shape[2] // nh
    t = SCAN_CHUNK
    g4 = gates.reshape(bsz, ltot, 4, nh).transpose(0, 3, 1, 2)
    g4t = g4.transpose(0, 1, 3, 2)
    gb = gate_b.reshape(4, nh).T
    return pl.pallas_call(
        functools.partial(_mlstm_kernel, t, ll, lc),
        grid=(bsz, nh),
        in_specs=[
            pl.BlockSpec((1, ltot, dk), lambda b, h: (b, 0, h)),
            pl.BlockSpec((1, ltot, dk), lambda b, h: (b, 0, nh + h)),
            pl.BlockSpec((1, ltot, dv), lambda b, h: (b, 0, h)),
            pl.BlockSpec((1, ltot, dv), lambda b, h: (b, 0, h)),
            pl.BlockSpec((1, 1, ltot, 4), lambda b, h: (b, h, 0, 0)),
            pl.BlockSpec((1, 1, 4, ltot), lambda b, h: (b, h, 0, 0)),
            pl.BlockSpec((3, dk), lambda b, h: (0, h)),
            pl.BlockSpec((3, dk), lambda b, h: (0, nh + h)),
            pl.BlockSpec((1, dk), lambda b, h: (0, h)),
            pl.BlockSpec((1, dk), lambda b, h: (0, nh + h)),
            pl.BlockSpec((1, 1, 4), lambda b, h: (h, 0, 0)),
            pl.BlockSpec((1, 4, 1), lambda b, h: (h, 0, 0)),
            pl.BlockSpec((1, dv), lambda b, h: (0, h)),
        ],
        out_specs=pl.BlockSpec((1, ltot, dv), lambda b, h: (b, 0, h)),
        out_shape=jax.ShapeDtypeStruct((bsz, ltot, nh * dv), BF16),
        scratch_shapes=[
            pltpu.VMEM((ltot, dk), BF16), pltpu.VMEM((ltot, dk), BF16),
            pltpu.VMEM((ltot, dv), F32), pltpu.VMEM((ltot, dv), F32),
            pltpu.VMEM((2, dv, dk), F32), pltpu.VMEM((2, 1, dk), F32), pltpu.VMEM((2, 8, LANES), F32),
        ],
        compiler_params=_cparams(("arbitrary", "arbitrary")),
        name="mlstm_scan",
    )(qk, qk, v, o, g4, g4t, conv_w, conv_w, conv_b.reshape(1, -1), conv_b.reshape(1, -1),
      gb.reshape(nh, 1, 4), gb.reshape(nh, 4, 1), head_g.reshape(1, -1))


def _router_logits_t(h2, rhi_ref, rlo_ref):
    h_hi, h_lo = _split_bf16(h2)
    lg = _dot(h_hi, rhi_ref[...]) + _dot(h_lo, rhi_ref[...]) + _dot(h_hi, rlo_ref[...])
    return lg.T[:N_EXPERTS, :]


def _out0_kernel(n_lat_tiles, hg_ref, xl_ref, xc_ref, g1_ref, sc_ref, sh_ref, ng_ref, wo_ref, rhi_ref, rlo_ref,
                 s_ref, h2_ref, lg_ref):
    j = pl.program_id(1)
    x = jnp.where(j < n_lat_tiles, xl_ref[0], xc_ref[0])
    s = x + g1_ref[0] * _dot(hg_ref[0], wo_ref[...])
    s_ref[0] = s
    h2 = _rms(s, ng_ref[...]) * (1.0 + sc_ref[0]) + sh_ref[0]
    h2_ref[...] = h2
    lg_ref[...] = _router_logits_t(h2, rhi_ref, rlo_ref)


def _out0(hg, x, ctx, modv, rows, norm_g, w_out, r_hi, r_lo, tm):
    bsz, ll, d = x.shape
    lc = ctx.shape[1]
    nl, nc = ll // tm, lc // tm
    nt = nl + nc
    ltot = ll + lc
    return pl.pallas_call(
        functools.partial(_out0_kernel, nl),
        grid=(bsz, nt),
        in_specs=[
            pl.BlockSpec((1, tm, hg.shape[2]), lambda b, j: (b, j, 0)),
            pl.BlockSpec((1, tm, d), lambda b, j: (b, jnp.minimum(j, nl - 1), 0)),
            pl.BlockSpec((1, tm, d), lambda b, j: (b, jnp.maximum(j - nl, 0), 0)),
            _mod_spec(d, 0, 2, nl, rows), _mod_spec(d, 0, 4, nl, rows), _mod_spec(d, 0, 3, nl, rows),
            pl.BlockSpec((1, d), lambda b, j: (0, 0)),
            pl.BlockSpec(w_out.shape, lambda b, j: (0, 0)),
            pl.BlockSpec(r_hi.shape, lambda b, j: (0, 0)),
            pl.BlockSpec(r_lo.shape, lambda b, j: (0, 0)),
        ],
        out_specs=[
            pl.BlockSpec((1, tm, d), lambda b, j: (b, j, 0)),
            pl.BlockSpec((tm, d), lambda b, j: (b * nt + j, 0)),
            pl.BlockSpec((N_EXPERTS, tm), lambda b, j: (0, b * nt + j)),
        ],
        out_shape=[
            jax.ShapeDtypeStruct((bsz, ltot, d), F32),
            jax.ShapeDtypeStruct((bsz * ltot, d), F32),
            jax.ShapeDtypeStruct((N_EXPERTS, bsz * ltot), F32),
        ],
        compiler_params=_cparams(("arbitrary", "arbitrary")),
        name="mlstm_out_proj",
    )(hg, x, ctx, modv, modv, modv, norm_g, w_out, r_hi, r_lo)


def _route_kernel(lg_ref, bias_ref, cls_ref, w_ref):
    s = _sigmoid(lg_ref[...])
    sel = s + bias_ref[...]
    srow = [s[e:e + 1, :] for e in range(N_EXPERTS)]
    row = [sel[e:e + 1, :] for e in range(N_EXPERTS)]
    best = jnp.zeros(row[0].shape, jnp.int32)
    best_score = None
    for g in range(N_GROUPS):
        r = row[g * E_PER_GROUP:(g + 1) * E_PER_GROUP]
        score = None
        for lo, hi in zip(PAIR_LO, PAIR_HI):
            pair = r[lo] + r[hi]
            score = pair if score is None else jnp.maximum(score, pair)
        if g == 0:
            best_score = score
        else:
            better = score > best_score
            best = jnp.where(better, g, best)
            best_score = jnp.where(better, score, best_score)
    gs = [row[i] for i in range(E_PER_GROUP)]
    gw = [srow[i] for i in range(E_PER_GROUP)]
    for g in range(1, N_GROUPS):
        hit = best == g
        gs = [jnp.where(hit, row[g * E_PER_GROUP + i], gs[i]) for i in range(E_PER_GROUP)]
        gw = [jnp.where(hit, srow[g * E_PER_GROUP + i], gw[i]) for i in range(E_PER_GROUP)]
    keep = []
    for i in range(E_PER_GROUP):
        beaten = jnp.zeros(best.shape, jnp.int32)
        for j in range(E_PER_GROUP):
            if j == i:
                continue
            wins = (gs[j] > gs[i]) | ((gs[j] == gs[i]) & (j < i))
            beaten = beaten + wins.astype(jnp.int32)
        keep.append(beaten < 2)
    pair_id = jnp.zeros(best.shape, jnp.int32)
    w_lo = jnp.zeros(best.shape, F32)
    w_hi = jnp.zeros(best.shape, F32)
    for p, (lo, hi) in enumerate(zip(PAIR_LO, PAIR_HI)):
        hit = keep[lo] & keep[hi]
        pair_id = jnp.where(hit, p, pair_id)
        w_lo = jnp.where(hit, gw[lo], w_lo)
        w_hi = jnp.where(hit, gw[hi], w_hi)
    tot = w_lo + w_hi
    cls_ref[...] = best * N_PAIRS + pair_id
    w_ref[0:1, :] = w_lo / tot
    w_ref[1:2, :] = w_hi / tot


def _route(logits_t, router_bias):
    n = logits_t.shape[1]
    tn = next(cand for cand in (2048, 1024, 512, 256, 128) if n % cand == 0)
    return pl.pallas_call(
        _route_kernel,
        grid=(n // tn,),
        in_specs=[
            pl.BlockSpec((N_EXPERTS, tn), lambda i: (0, i)),
            pl.BlockSpec((N_EXPERTS, 1), lambda i: (0, 0)),
        ],
        out_specs=[pl.BlockSpec((1, tn), lambda i: (0, i)), pl.BlockSpec((2, tn), lambda i: (0, i))],
        out_shape=[jax.ShapeDtypeStruct((1, n), jnp.int32), jax.ShapeDtypeStruct((2, n), F32)],
        compiler_params=_cparams(("arbitrary",)),
        name="moe_route",
    )(logits_t, router_bias.reshape(N_EXPERTS, 1).astype(F32))


def _moe_kernel(tm, spare_rows, nu_ref, e0_ref, e1_ref, src0_ref, srcn_ref, dstp_ref, w_ref, h2_hbm,
                wg0_ref, wu0_ref, wd0_ref, wg1_ref, wu1_ref, wd1_ref, f_hbm,
                xb0, xb1, yb0, yb1, xbf, gsem, ssem, zsem):
    del e0_ref, e1_ref
    t = pl.program_id(0)
    n_used = nu_ref[0]
    xbuf, ybuf = (xb0, xb1), (yb0, yb1)

    def gather_row(idx_ref, r, slot):
        tok = idx_ref[0, 0, r]
        pltpu.make_async_copy(h2_hbm.at[pl.ds(tok, 1)], xbuf[slot].at[pl.ds(r, 1)], gsem.at[slot]).start()

    def scatter_row(r, slot):
        tok = dstp_ref[0, 0, r]
        pltpu.make_async_copy(ybuf[slot].at[pl.ds(r, 1)], f_hbm.at[pl.ds(tok, 1)], ssem.at[slot]).start()

    def wait_gather(slot):
        pltpu.make_async_copy(h2_hbm.at[pl.ds(0, tm)], xbuf[slot], gsem.at[slot]).wait()

    def wait_scatter(slot):
        pltpu.make_async_copy(ybuf[slot], f_hbm.at[pl.ds(0, tm)], ssem.at[slot]).wait()

    @pl.when(t == 0)
    def _():
        yb1[...] = jnp.zeros(yb1.shape, F32)
        fills = [pltpu.make_async_copy(yb1, f_hbm.at[pl.ds(row, tm)], zsem) for row in spare_rows]
        for fill in fills:
            fill.start()
        for fill in fills:
            fill.wait()

        def first(r, carry):
            gather_row(src0_ref, r, 0)
            return carry

        lax.fori_loop(0, tm, first, 0)

    def step(cur):
        nxt = 1 - cur

        @pl.when((t >= 1) & (t <= n_used))
        def _():
            wait_scatter(cur)

        @pl.when(t < n_used)
        def _():
            wait_gather(cur)
            xbf[...] = xbuf[cur][...].astype(BF16)
            w = w_ref[...]
            group = tm // MOE_DMA_GROUPS

            def copies(g):
                for r in range(g * group, (g + 1) * group):
                    gather_row(srcn_ref, r, nxt)
                    scatter_row(r, nxt)

            copies(0)
            hg0 = _dot(xbf[...], wg0_ref[0])
            copies(1)
            hu0 = _dot(xbf[...], wu0_ref[0])
            copies(2)
            hg1 = _dot(xbf[...], wg1_ref[0])
            copies(3)
            hu1 = _dot(xbf[...], wu1_ref[0])
            a0 = (_silu(hg0) * hu0 * w[:, 0:1]).astype(BF16)
            a1 = (_silu(hg1) * hu1 * w[:, 1:2]).astype(BF16)
            ybuf[cur][...] = _dot(a0, wd0_ref[0]) + _dot(a1, wd1_ref[0])

        @pl.when(t == n_used)
        def _():
            wait_gather(cur)

            def last(r, carry):
                scatter_row(r, nxt)
                return carry

            lax.fori_loop(0, tm, last, 0)
            wait_scatter(nxt)

    @pl.when(t % 2 == 0)
    def _():
        step(0)

    @pl.when(t % 2 == 1)
    def _():
        step(1)


def _moe(h2, cls, w2, dst_of_token, dump_base, spare_rows, out_rows, wg, wu, wd):
    n, d = h2.shape
    tm = MOE_TILE
    n_tiles = n // tm + N_CLASSES
    p_rows = n_tiles * tm
    cls = cls.reshape(n)
    order = jnp.argsort(cls, stable=True).astype(jnp.int32)
    counts = jnp.sum(cls[:, None] == jnp.arange(N_CLASSES, dtype=jnp.int32)[None, :], axis=0).astype(jnp.int32)
    tiles_per = (counts + tm - 1) // tm
    tile_end = jnp.cumsum(tiles_per)
    tile_start = tile_end - tiles_per
    first_sorted = jnp.cumsum(counts) - counts
    n_used = tile_end[-1]
    tile_id = jnp.arange(n_tiles, dtype=jnp.int32)
    tile_cls = jnp.searchsorted(tile_end, jnp.minimum(tile_id, n_used - 1), side="right").astype(jnp.int32)
    tile_cls = jnp.minimum(tile_cls, N_CLASSES - 1)
    group, pair = tile_cls // N_PAIRS, tile_cls % N_PAIRS
    e0 = group * E_PER_GROUP + jnp.asarray(PAIR_LO, jnp.int32)[pair]
    e1 = group * E_PER_GROUP + jnp.asarray(PAIR_HI, jnp.int32)[pair]
    p = jnp.arange(p_rows, dtype=jnp.int32)
    p_cls = tile_cls[p // tm]
    rank = p - tile_start[p_cls] * tm
    valid = (rank < counts[p_cls]) & (p // tm < n_used)
    tok = order[jnp.clip(first_sorted[p_cls] + rank, 0, n - 1)]
    src = jnp.where(valid, tok, 0).reshape(n_tiles, 1, tm)
    dump = (dump_base + p % tm).astype(jnp.int32)
    dst = jnp.where(valid, dst_of_token[tok], dump).reshape(n_tiles, 1, tm)
    w_rows = jnp.where(valid[:, None], w2[:, tok].T, 0.0)
    src_next = jnp.concatenate([src[1:], jnp.zeros((1, 1, tm), jnp.int32)], axis=0)
    dst_prev = jnp.concatenate([dump[:tm].reshape(1, 1, tm), dst[:-1]], axis=0)

    de = wg.shape[2]
    idx = pl.BlockSpec((1, 1, tm), lambda i, nu, e0, e1: (i, 0, 0), memory_space=pltpu.SMEM)
    idx0 = pl.BlockSpec((1, 1, tm), lambda i, nu, e0, e1: (0, 0, 0), memory_space=pltpu.SMEM)
    up0 = pl.BlockSpec((1, d, de), lambda i, nu, e0, e1: (e0[i], 0, 0))
    up1 = pl.BlockSpec((1, d, de), lambda i, nu, e0, e1: (e1[i], 0, 0))
    dn0 = pl.BlockSpec((1, de, d), lambda i, nu, e0, e1: (e0[i], 0, 0))
    dn1 = pl.BlockSpec((1, de, d), lambda i, nu, e0, e1: (e1[i], 0, 0))
    return pl.pallas_call(
        functools.partial(_moe_kernel, tm, spare_rows),
        grid_spec=pltpu.PrefetchScalarGridSpec(
            num_scalar_prefetch=3,
            grid=(n_tiles,),
            in_specs=[
                idx0, idx, idx,
                pl.BlockSpec((tm, 2), lambda i, nu, e0, e1: (i, 0)),
                pl.BlockSpec(memory_space=pl.ANY),
                up0, up0, dn0, up1, up1, dn1,
            ],
            out_specs=pl.BlockSpec(memory_space=pl.ANY),
            scratch_shapes=[
                pltpu.VMEM((tm, d), F32), pltpu.VMEM((tm, d), F32),
                pltpu.VMEM((tm, d), F32), pltpu.VMEM((tm, d), F32),
                pltpu.VMEM((tm, d), BF16),
                pltpu.SemaphoreType.DMA((2,)), pltpu.SemaphoreType.DMA((2,)), pltpu.SemaphoreType.DMA(()),
            ],
        ),
        out_shape=jax.ShapeDtypeStruct((out_rows, d), F32),
        compiler_params=_cparams(("arbitrary",)),
        name="moe_experts",
    )(n_used.reshape(1).astype(jnp.int32), e0, e1, src, src_next, dst_prev, w_rows, h2,
      wg, wu, wd, wg, wu, wd)


def _hproj_kernel(n_lat_tiles, rows, sl_ref, fl_ref, sc_ref_, fc_ref, g2_ref, sc_ref, sh_ref, ng_ref, lb_ref,
                  wq_ref, wzf_ref, wzb_ref, wi_ref, wg_ref,
                  s2_ref, q_ref, kf_ref, kb_ref, lff_ref, lfb_ref, i_ref, g_ref, xt):
    j = pl.program_id(1)
    d = xt.shape[1]
    g2 = g2_ref[0]

    @pl.when(j < n_lat_tiles)
    def _():
        for c in range(COL_TILE):
            s = sl_ref[0, :, c, :] + g2 * fl_ref[0, :, c, :]
            s2_ref[0, :, c, :] = s
            xt[c * rows:(c + 1) * rows, :] = s

    @pl.when(j >= n_lat_tiles)
    def _():
        xt[...] = sc_ref_[0] + g2 * fc_ref[0]

    h = (_rms(xt[...], ng_ref[...]) * (1.0 + sc_ref[0]) + sh_ref[0]).astype(BF16)
    q_ref[0] = _silu(_dot(h, wq_ref[...])).astype(BF16)
    i_ref[0] = _dot(h, wi_ref[...]).astype(BF16)
    g_ref[0] = _dot(h, wg_ref[...]).astype(BF16)
    for dd, (wz_ref, k_ref, lf_ref) in enumerate(((wzf_ref, kf_ref, lff_ref), (wzb_ref, kb_ref, lfb_ref))):
        z = _dot(h, wz_ref[...])
        log_lb = lb_ref[2 * dd:2 * dd + 1, :]
        log_1mlb = lb_ref[2 * dd + 1:2 * dd + 2, :]
        ls = _log_sigmoid(z)
        a = log_1mlb + ls
        lf_ref[0] = jnp.maximum(log_lb, a) + jnp.log1p(jnp.exp(-jnp.abs(log_lb - a)))
        k_ref[0] = jnp.exp(a - z).astype(BF16)


def _hproj(s1, f0, modv, rows_mod, norm_g, lbtab, ws, ll, lc):
    bsz, ltot, d = s1.shape
    rows = ll // GRID_W
    tm = COL_TILE * rows
    nl, nc = GRID_W // COL_TILE, lc // tm
    s_cm = s1.reshape(bsz, ltot // GRID_W, GRID_W, d)
    f_tok = f0.reshape(bsz, -1, d)
    f_cm = f0.reshape(bsz, -1, GRID_W, d)
    cm = pl.BlockSpec((1, rows, COL_TILE, d), lambda b, j: (b, 0, jnp.minimum(j, nl - 1), 0))
    ctx = pl.BlockSpec((1, tm, d), lambda b, j: (b, ll // tm + jnp.maximum(j - nl, 0), 0))
    wspec = pl.BlockSpec((d, d), lambda b, j: (0, 0))
    tok = pl.BlockSpec((1, tm, d), lambda b, j: (b, j, 0))

    def g2_map(b, j):
        row = jnp.where(j < nl, b, rows_mod - 1)
        return ((0 * rows_mod + row) * N_ADA + 5, 0, 0)

    outs = pl.pallas_call(
        functools.partial(_hproj_kernel, nl, rows),
        grid=(bsz, nl + nc),
        in_specs=[
            cm, cm, ctx, ctx,
            pl.BlockSpec((1, 1, d), g2_map),
            _mod_spec(d, 1, 1, nl, rows_mod), _mod_spec(d, 1, 0, nl, rows_mod),
            pl.BlockSpec((1, d), lambda b, j: (0, 0)),
            pl.BlockSpec((4, d), lambda b, j: (0, 0)),
            wspec, wspec, wspec, wspec, wspec,
        ],
        out_specs=[cm, tok, tok, tok, tok, tok, tok, tok],
        out_shape=[jax.ShapeDtypeStruct((bsz, rows, GRID_W, d), F32)]
        + [jax.ShapeDtypeStruct((bsz, ltot, d), dt) for dt in (BF16, BF16, BF16, F32, F32, BF16, BF16)],
        scratch_shapes=[pltpu.VMEM((tm, d), F32)],
        compiler_params=_cparams(("arbitrary", "arbitrary")),
        name="hgrn_in_proj",
    )(s_cm, f_cm, s1, f_tok, modv, modv, modv, norm_g, lbtab, *ws)
    return outs


def _hgrn_kernel(t, ll, lc, q_ref, kf_ref, kb_ref, lff_ref, lfb_ref, i_ref, g_ref, hg_ref, out_ref,
                 of, ob, st_s):
    st_s[...] = jnp.zeros(st_s.shape, F32)
    row = lax.broadcasted_iota(jnp.int32, (t, 1), 0)
    ti = lax.broadcasted_iota(jnp.int32, (t, t), 0)
    si = lax.broadcasted_iota(jnp.int32, (t, t), 1)
    xor = ti ^ si
    levels = []
    m = 2
    while m <= t:
        levels.append(m)
        m *= 2

    def cumsum(lf, d):
        b = lf
        sh = 1
        while sh < t:
            if d == 0:
                b = b + jnp.where(row >= sh, pltpu.roll(b, sh, 0), 0.0)
            else:
                b = b + jnp.where(row < t - sh, pltpu.roll(b, t - sh, 0), 0.0)
            sh *= 2
        return b

    def block_ref(b, m, d):
        half = m // 2
        pos = half - 1 if d == 0 else half
        if m >= 8:
            dk = b.shape[1]
            b3 = b.reshape(t // m, m, dk)
            return jnp.broadcast_to(b3[:, pos:pos + 1, :], b3.shape).reshape(t, dk)
        r = row % m
        out = b
        for res in range(m):
            if res == pos:
                continue
            out = jnp.where(r == res, pltpu.roll(b, (res - pos) % t, 0), out)
        return out

    def chunk(start, d, need_out, o_out):
        q = q_ref[0, pl.ds(start, t), :]
        k = (kf_ref if d == 0 else kb_ref)[0, pl.ds(start, t), :].astype(F32)
        lf = (lff_ref if d == 0 else lfb_ref)[0, pl.ds(start, t), :]
        iv = i_ref[0, pl.ds(start, t), :]
        b = cumsum(lf, d)
        b_tot = b[t - 1:t, :] if d == 0 else b[0:1, :]
        st = st_s[d]
        if need_out:
            qf = q.astype(F32)
            a = _dot_nt(q, k.astype(BF16))
            for m in levels:
                e = jnp.exp(-jnp.abs(b - block_ref(b, m, d)))
                p = _dot_nt((qf * e).astype(BF16), (k * e).astype(BF16))
                a = jnp.where(xor < m // 2, a, p) if m > 2 else jnp.where(xor == 0, a, p)
            feeds = (si <= ti) if d == 0 else (si >= ti)
            a = jnp.where(feeds, a, 0.0)
            o = _dot(a.astype(BF16), iv) + _dot_nt((qf * jnp.exp(b)).astype(BF16), st.astype(BF16))
            o_out[pl.ds(start, t), :] = o
        kd = (k * jnp.exp(b_tot - b)).astype(BF16)
        st_s[d] = jnp.exp(b_tot) * st + _dot_tn(iv, kd)

    ncc, ncl = lc // t, ll // t
    for i in range(ncc):
        chunk(ll + i * t, 0, False, None)
        chunk(ll + (ncc - 1 - i) * t, 1, False, None)

    def body(i, carry):
        chunk(pl.multiple_of(i * t, t), 0, True, of)
        chunk(pl.multiple_of((ncl - 1 - i) * t, t), 1, True, ob)
        return carry

    lax.fori_loop(0, ncl, body, 0)

    def epilogue(i, carry):
        s = pl.multiple_of(i * t, t)
        y = _rms(of[pl.ds(s, t), :] + ob[pl.ds(s, t), :], hg_ref[...])
        out_ref[0, pl.ds(s, t), :] = (y * _silu(g_ref[0, pl.ds(s, t), :].astype(F32))).astype(BF16)
        return carry

    lax.fori_loop(0, ncl, epilogue, 0)


def _hgrn_scan(q, kf, kb, lff, lfb, iv, g, head_g, ll, lc):
    bsz, ltot, hk = q.shape
    nh = H_HEADS
    dk = hk // nh
    t = SCAN_CHUNK
    full = pl.BlockSpec((1, ltot, dk), lambda b, h: (b, 0, h))
    lat = pl.BlockSpec((1, ll, dk), lambda b, h: (b, 0, h))
    return pl.pallas_call(
        functools.partial(_hgrn_kernel, t, ll, lc),
        grid=(bsz, nh),
        in_specs=[full, full, full, full, full, full, lat, pl.BlockSpec((1, dk), lambda b, h: (0, h))],
        out_specs=lat,
        out_shape=jax.ShapeDtypeStruct((bsz, ll, hk), BF16),
        scratch_shapes=[pltpu.VMEM((ll, dk), F32), pltpu.VMEM((ll, dk), F32), pltpu.VMEM((2, dk, dk), F32)],
        compiler_params=_cparams(("arbitrary", "arbitrary")),
        name="hgrn_scan",
    )(q, kf, kb, lff, lfb, iv, g, head_g.reshape(1, -1))


def _out1_kernel(rows, hg_ref, s_ref, g1_ref, sc_ref, sh_ref, ng_ref, wo_ref, rhi_ref, rlo_ref,
                 s3_ref, h2_ref, lg_ref, st):
    d = st.shape[1]
    y = g1_ref[0] * _dot(hg_ref[0], wo_ref[...])
    for c in range(COL_TILE):
        s = s_ref[0, :, c, :] + y[c * rows:(c + 1) * rows, :]
        s3_ref[0, :, c, :] = s
        st[c * rows:(c + 1) * rows, :] = s
    h2 = _rms(st[...], ng_ref[...]) * (1.0 + sc_ref[0]) + sh_ref[0]
    h2_ref[...] = h2
    lg_ref[...] = _router_logits_t(h2, rhi_ref, rlo_ref)


def _out1(hg, s2_cm, modv, rows_mod, norm_g, w_out, r_hi, r_lo):
    bsz, ll, hv = hg.shape
    rows = s2_cm.shape[1]
    d = s2_cm.shape[3]
    tm = COL_TILE * rows
    nl = GRID_W // COL_TILE
    cm = pl.BlockSpec((1, rows, COL_TILE, d), lambda b, j: (b, 0, j, 0))
    return pl.pallas_call(
        functools.partial(_out1_kernel, rows),
        grid=(bsz, nl),
        in_specs=[
            pl.BlockSpec((1, tm, hv), lambda b, j: (b, j, 0)),
            cm,
            _mod_spec(d, 1, 2, nl, rows_mod), _mod_spec(d, 1, 4, nl, rows_mod), _mod_spec(d, 1, 3, nl, rows_mod),
            pl.BlockSpec((1, d), lambda b, j: (0, 0)),
            pl.BlockSpec(w_out.shape, lambda b, j: (0, 0)),
            pl.BlockSpec(r_hi.shape, lambda b, j: (0, 0)),
            pl.BlockSpec(r_lo.shape, lambda b, j: (0, 0)),
        ],
        out_specs=[
            cm,
            pl.BlockSpec((tm, d), lambda b, j: (b * nl + j, 0)),
            pl.BlockSpec((N_EXPERTS, tm), lambda b, j: (0, b * nl + j)),
        ],
        out_shape=[
            jax.ShapeDtypeStruct(s2_cm.shape, F32),
            jax.ShapeDtypeStruct((bsz * ll, d), F32),
            jax.ShapeDtypeStruct((N_EXPERTS, bsz * ll), F32),
        ],
        scratch_shapes=[pltpu.VMEM((tm, d), F32)],
        compiler_params=_cparams(("arbitrary", "arbitrary")),
        name="hgrn_out_proj",
    )(hg, s2_cm, modv, modv, modv, norm_g, w_out, r_hi, r_lo)


def _final_kernel(s_ref, f_ref, g2_ref, fg_ref, o_ref):
    o_ref[0] = _rms(s_ref[0] + g2_ref[0] * f_ref[0], fg_ref[...])


def _final(s3, f1, modv, rows_mod, final_g, tm):
    bsz, ll, d = s3.shape
    nl = ll // tm
    tok = pl.BlockSpec((1, tm, d), lambda b, j: (b, j, 0))
    return pl.pallas_call(
        _final_kernel,
        grid=(bsz, nl),
        in_specs=[tok, tok, _mod_spec(d, 1, 5, nl, rows_mod), pl.BlockSpec((1, d), lambda b, j: (0, 0))],
        out_specs=tok,
        out_shape=jax.ShapeDtypeStruct((bsz, ll, d), F32),
        compiler_params=_cparams(("arbitrary", "arbitrary")),
        name="final_norm",
    )(s3, f1.reshape(bsz, -1, d), modv, final_g)


def kernel(x, c, ctx, c_ctx, ada_w, ada_b, norm_mix_g, norm_ffn_g, final_g, m_w_in, m_conv_w, m_conv_b, m_gate_b,
           m_head_g, m_w_out, h_w_in, h_lower_bounds, h_head_g, h_w_out, router_w, router_bias, e_w_gate,
           e_w_up, e_w_down):
    bsz, ll, d = x.shape
    lc = ctx.shape[1]
    ltot = ll + lc
    depth = ada_w.shape[0]
    assert depth == 2 and ll % GRID_W == 0 and lc % SCAN_CHUNK == 0 and ll % SCAN_CHUNK == 0
    rows = ll // GRID_W
    tm = COL_TILE * rows
    assert lc % tm == 0 and ltot % GRID_W == 0 and (bsz * ltot) % MOE_TILE == 0 and (bsz * ll) % MOE_TILE == 0

    rows_mod = 8 * ((bsz + 1 + 7) // 8)
    cc = jnp.zeros((rows_mod, d), F32).at[:bsz].set(c).at[rows_mod - 1].set(c_ctx)
    modv = _ada(cc, ada_w, ada_b).reshape(depth * rows_mod * N_ADA, 1, d)

    r_pad = jnp.zeros((d, LANES), F32).at[:, :N_EXPERTS].set(router_w)
    r_hi = r_pad.astype(BF16)
    r_lo = (r_pad - r_hi.astype(F32)).astype(BF16)
    wg_all, wu_all, wd_all = e_w_gate.astype(BF16), e_w_up.astype(BF16), e_w_down.astype(BF16)

    m_qk, m_v = m_conv_w.shape[2], m_head_g.shape[1]
    w_in = m_w_in[0].astype(BF16)
    w_gates = jnp.zeros((d, LANES), BF16).at[:, :4 * M_HEADS].set(w_in[:, m_qk + 2 * m_v:])
    qk, v, o, gates = _mproj(x, ctx, modv, rows_mod, norm_mix_g[0:1], w_in[:, :m_qk], w_in[:, m_qk:m_qk + m_v],
                             w_in[:, m_qk + m_v:m_qk + 2 * m_v], w_gates, tm)
    hg = _mlstm_scan(qk, v, o, gates, m_conv_w[0], m_conv_b[0], m_gate_b[0], m_head_g[0], ll, lc)
    s1, h2, lg = _out0(hg, x, ctx, modv, rows_mod, norm_ffn_g[0:1], m_w_out[0].astype(BF16), r_hi, r_lo, tm)
    cls, w2 = _route(lg, router_bias)
    tok0 = jnp.arange(bsz * ltot, dtype=jnp.int32)
    dst0 = (tok0 // ltot) * (ltot + MOE_TILE) + tok0 % ltot
    spare0 = tuple(b * (ltot + MOE_TILE) + ltot for b in range(1, bsz))
    f0 = _moe(h2, cls, w2, dst0, ltot, spare0, bsz * (ltot + MOE_TILE), wg_all[0], wu_all[0], wd_all[0])

    lbs = jnp.cumsum(jax.nn.softmax(h_lower_bounds.astype(F32), axis=0), axis=0)
    lb = (lbs - lbs[0])[1].reshape(2, -1)
    lbtab = jnp.stack([jnp.log(lb[0]), jnp.log1p(-lb[0]), jnp.log(lb[1]), jnp.log1p(-lb[1])])
    hw = h_w_in[0].astype(BF16)
    hk = lb.shape[1]
    ws = (hw[:, :hk], hw[:, hk:2 * hk], hw[:, 2 * hk:3 * hk], hw[:, 3 * hk:3 * hk + d], hw[:, 3 * hk + d:])
    s2_cm, q, kf, kb, lff, lfb, iv, g = _hproj(s1, f0, modv, rows_mod, norm_mix_g[1:2], lbtab, ws, ll, lc)
    hg1 = _hgrn_scan(q, kf, kb, lff, lfb, iv, g, h_head_g[0], ll, lc)
    s3_cm, h2b, lgb = _out1(hg1, s2_cm, modv, rows_mod, norm_ffn_g[1:2], h_w_out[0].astype(BF16), r_hi, r_lo)
    clsb, w2b = _route(lgb, router_bias)
    n1 = bsz * ll
    tokid = jnp.arange(n1, dtype=jnp.int32)
    within = tokid % ll
    raster = (tokid // ll) * (ll + MOE_TILE) + (within % rows) * GRID_W + within // rows
    spare1 = tuple(b * (ll + MOE_TILE) + ll for b in range(1, bsz))
    f1 = _moe(h2b, clsb, w2b, raster, ll, spare1, bsz * (ll + MOE_TILE), wg_all[1], wu_all[1], wd_all[1])

    return _final(s3_cm.reshape(bsz, ll, d), f1, modv, rows_mod, final_g.reshape(1, d), tm)
```

```python
import functools

import numpy as np
import jax
import jax.numpy as jnp
from jax import lax
from jax.experimental import pallas as pl
from jax.experimental.pallas import tpu as pltpu

F32 = jnp.float32
BF16 = jnp.bfloat16

EPS = 1e-6
N_ADA = 6
GRID_W = 64
M_HEADS = 4
H_HEADS = 8
N_EXPERTS = 16
N_GROUPS = 4
E_PER_GROUP = N_EXPERTS // N_GROUPS
N_PAIRS = 6
N_CLASSES = N_GROUPS * N_PAIRS
PAIR_LO = (0, 0, 0, 1, 1, 2)
PAIR_HI = (1, 2, 3, 2, 3, 3)

LANES = 128
COL_TILE = 8
SCAN_CHUNK = 128
MOE_TILE = 256
VMEM_LIMIT = 56 * 1024 * 1024


def _cparams(sem):
    return pltpu.CompilerParams(dimension_semantics=sem, vmem_limit_bytes=VMEM_LIMIT)


def _dot(a, b):
    return jnp.dot(a, b, preferred_element_type=F32)


def _dot_nt(a, b):
    return lax.dot_general(a, b, (((1,), (1,)), ((), ())), preferred_element_type=F32)


def _dot_tn(a, b):
    return lax.dot_general(a, b, (((0,), (0,)), ((), ())), preferred_element_type=F32)


def _split_bf16(a):
    hi = a.astype(BF16)
    lo = (a - hi.astype(F32)).astype(BF16)
    return hi, lo


def _sigmoid(x):
    return 1.0 / (1.0 + jnp.exp(-x))


def _silu(x):
    return x * _sigmoid(x)


def _log_sigmoid(x):
    return jnp.minimum(x, 0.0) - jnp.log1p(jnp.exp(-jnp.abs(x)))


def _rms(x, g):
    return x * lax.rsqrt(jnp.mean(x * x, axis=-1, keepdims=True) + EPS) * g


def _ada_kernel(c_ref, w_ref, b_ref, o_ref):
    a = _silu(c_ref[...])
    a_hi, a_lo = _split_bf16(a)
    w_hi, w_lo = _split_bf16(w_ref[0])
    acc = _dot(a_hi, w_hi) + _dot(a_lo, w_hi) + _dot(a_hi, w_lo)
    o_ref[0] = acc + b_ref[0]


def _ada(cc, ada_w, ada_b):
    depth, d, n = ada_w.shape
    tn = 1024
    rows = cc.shape[0]
    return pl.pallas_call(
        _ada_kernel,
        grid=(depth, n // tn),
        in_specs=[
            pl.BlockSpec((rows, d), lambda l, j: (0, 0)),
            pl.BlockSpec((1, d, tn), lambda l, j: (l, 0, j)),
            pl.BlockSpec((1, 1, tn), lambda l, j: (l, 0, j)),
        ],
        out_specs=pl.BlockSpec((1, rows, tn), lambda l, j: (l, 0, j)),
        out_shape=jax.ShapeDtypeStruct((depth, rows, n), F32),
        compiler_params=_cparams(("arbitrary", "arbitrary")),
        name="ada_mod",
    )(cc, ada_w, ada_b.reshape(depth, 1, n))


def _mproj_kernel(n_lat_tiles, xl_ref, xc_ref, sc_ref, sh_ref, g_ref, wqk_ref, wv_ref, wo_ref, wg_ref,
                  qk_ref, v_ref, o_ref, gt_ref):
    j = pl.program_id(1)
    x = jnp.where(j < n_lat_tiles, xl_ref[0], xc_ref[0])
    h = (_rms(x, g_ref[...]) * (1.0 + sc_ref[0]) + sh_ref[0]).astype(BF16)
    qk_ref[0] = _dot(h, wqk_ref[...]).astype(BF16)
    v_ref[0] = _dot(h, wv_ref[...]).astype(BF16)
    o_ref[0] = _dot(h, wo_ref[...]).astype(BF16)
    gt_ref[0] = _dot(h, wg_ref[...])[:, :4 * M_HEADS]


def _mod_spec(d, layer, k, n_lat_tiles, rows):
    def imap(b, j):
        row = jnp.where(j < n_lat_tiles, b, rows - 1)
        return ((layer * rows + row) * N_ADA + k, 0, 0)
    return pl.BlockSpec((1, 1, d), imap)


def _mproj(x, ctx, modv, rows, norm_g, w_qk, w_v, w_o, w_g, tm):
    bsz, ll, d = x.shape
    lc = ctx.shape[1]
    nl, nc = ll // tm, lc // tm
    ltot = ll + lc
    wspec = lambda n: pl.BlockSpec((d, n), lambda b, j: (0, 0))
    tok = lambda n: pl.BlockSpec((1, tm, n), lambda b, j: (b, j, 0))
    return pl.pallas_call(
        functools.partial(_mproj_kernel, nl),
        grid=(bsz, nl + nc),
        in_specs=[
            pl.BlockSpec((1, tm, d), lambda b, j: (b, jnp.minimum(j, nl - 1), 0)),
            pl.BlockSpec((1, tm, d), lambda b, j: (b, jnp.maximum(j - nl, 0), 0)),
            _mod_spec(d, 0, 1, nl, rows), _mod_spec(d, 0, 0, nl, rows),
            pl.BlockSpec((1, d), lambda b, j: (0, 0)),
            wspec(w_qk.shape[1]), wspec(w_v.shape[1]), wspec(w_o.shape[1]), wspec(w_g.shape[1]),
        ],
        out_specs=[tok(w_qk.shape[1]), tok(w_v.shape[1]), tok(w_o.shape[1]), tok(4 * M_HEADS)],
        out_shape=[
            jax.ShapeDtypeStruct((bsz, ltot, w_qk.shape[1]), BF16),
            jax.ShapeDtypeStruct((bsz, ltot, w_v.shape[1]), BF16),
            jax.ShapeDtypeStruct((bsz, ltot, w_o.shape[1]), BF16),
            jax.ShapeDtypeStruct((bsz, ltot, 4 * M_HEADS), F32),
        ],
        compiler_params=_cparams(("arbitrary", "arbitrary")),
        name="mlstm_in_proj",
    )(x, ctx, modv, modv, norm_g, w_qk, w_v, w_o, w_g)


def _mlstm_kernel(t, ll, lc, q_ref, k_ref, v_ref, o_ref, gc_ref, gr_ref, cwq_ref, cwk_ref, cbq_ref, cbk_ref,
                  gbc_ref, gbr_ref, hg_ref, out_ref, qs, ks, hf, hb, c_s, n_s, m_s):
    ltot = ll + lc
    dk = q_ref.shape[2]
    row = lax.broadcasted_iota(jnp.int32, (ltot, 1), 0)
    first = (row == 0) | (row == ll)
    last = (row == ll - 1) | (row == ltot - 1)

    def conv(x_ref, w_ref, b_ref):
        x = x_ref[0].astype(F32)
        w = w_ref[...]
        xp = jnp.where(first, 0.0, pltpu.roll(x, 1, 0))
        xn = jnp.where(last, 0.0, pltpu.roll(x, ltot - 1, 0))
        return _silu(xp * w[0:1] + x * w[1:2] + xn * w[2:3] + b_ref[...])

    qs[...] = (conv(q_ref, cwq_ref, cbq_ref) * (dk ** -0.5)).astype(BF16)
    ks[...] = conv(k_ref, cwk_ref, cbk_ref).astype(BF16)

    c_s[...] = jnp.zeros(c_s.shape, F32)
    n_s[...] = jnp.zeros(n_s.shape, F32)
    m_s[...] = jnp.zeros(m_s.shape, F32)

    ti = lax.broadcasted_iota(jnp.int32, (t, t), 0)
    si = lax.broadcasted_iota(jnp.int32, (t, t), 1)

    def chunk(start, d, h_out):
        q = qs[pl.ds(start, t), :]
        k = ks[pl.ds(start, t), :]
        v = v_ref[0, pl.ds(start, t), :]
        gc = gc_ref[0, 0, pl.ds(start, t), :] + gbc_ref[0]
        gr = gr_ref[0, 0, :, pl.ds(start, t)] + gbr_ref[0]
        ig_c, lf_c = gc[:, 2 * d:2 * d + 1], _log_sigmoid(gc[:, 2 * d + 1:2 * d + 2])
        ig_r, lf_r = gr[2 * d:2 * d + 1, :], _log_sigmoid(gr[2 * d + 1:2 * d + 2, :])
        seen = (si <= ti) if d == 0 else (si >= ti)
        seen_t = (ti <= si) if d == 0 else (ti >= si)
        b_c = jnp.sum(jnp.where(seen, lf_r, 0.0), axis=1, keepdims=True)
        b_r = jnp.sum(jnp.where(seen_t, lf_c, 0.0), axis=0, keepdims=True)
        total = jnp.sum(lf_r, axis=1, keepdims=True)
        c_mem = c_s[d]
        n_mem = n_s[d]
        m_prev = m_s[d][0:1, 0:1]

        log_d = jnp.where(seen, b_c - b_r + ig_r, -jnp.inf)
        log_inter = b_c + m_prev
        m_t = jnp.maximum(log_inter, jnp.max(log_d, axis=1, keepdims=True))
        w_ts = jnp.exp(log_d - m_t) * _dot_nt(q, k)
        inter = jnp.exp(log_inter - m_t)
        num = _dot(w_ts.astype(BF16), v) + inter * _dot_nt(q, c_mem.astype(BF16))
        den = jnp.sum(w_ts, axis=1, keepdims=True) + inter * jnp.sum(q.astype(F32) * n_mem, axis=1, keepdims=True)
        h_out[pl.ds(start, t), :] = num / jnp.maximum(jnp.abs(den), jnp.exp(-m_t))

        g_c = total - b_c + ig_c
        g_r = total - b_r + ig_r
        m_new = jnp.maximum(total + m_prev, jnp.max(g_r, axis=1, keepdims=True))
        w_c = jnp.exp(g_c - m_new)
        decay = jnp.exp(total + m_prev - m_new)
        wv = (w_c * v.astype(F32)).astype(BF16)
        c_s[d] = decay * c_mem + _dot_tn(wv, k)
        n_s[d] = decay * n_mem + jnp.sum(w_c * k.astype(F32), axis=0, keepdims=True)
        m_s[d] = jnp.broadcast_to(m_new, m_s.shape[1:])

    ncc, ncl = lc // t, ll // t
    for i in range(ncc):
        chunk(ll + i * t, 0, hf)
        chunk(ll + (ncc - 1 - i) * t, 1, hb)

    def body(i, carry):
        chunk(pl.multiple_of(i * t, t), 0, hf)
        chunk(pl.multiple_of((ncl - 1 - i) * t, t), 1, hb)
        return carry

    lax.fori_loop(0, ncl, body, 0)

    def epilogue(i, carry):
        s = pl.multiple_of(i * t, t)
        hs = hf[pl.ds(s, t), :] + hb[pl.ds(s, t), :]
        y = _rms(hs, hg_ref[...])
        out_ref[0, pl.ds(s, t), :] = (y * _sigmoid(o_ref[0, pl.ds(s, t), :].astype(F32))).astype(BF16)
        return carry

    lax.fori_loop(0, ltot // t, epilogue, 0)


def _mlstm_scan(qk, v, o, gates, conv_w, conv_b, gate_b, head_g, ll, lc):
    bsz, ltot, _ = qk.shape
    nh = M_HEADS
    dk = qk.shape[2] // (2 * nh)
    dv = v.shape[2] // nh
    t = SCAN_CHUNK
    g4 = gates.reshape(bsz, ltot, 4, nh).transpose(0, 3, 1, 2)
    g4t = g4.transpose(0, 1, 3, 2)
    gb = gate_b.reshape(4, nh).T
    return pl.pallas_call(
        functools.partial(_mlstm_kernel, t, ll, lc),
        grid=(bsz, nh),
        in_specs=[
            pl.BlockSpec((1, ltot, dk), lambda b, h: (b, 0, h)),
            pl.BlockSpec((1, ltot, dk), lambda b, h: (b, 0, nh + h)),
            pl.BlockSpec((1, ltot, dv), lambda b, h: (b, 0, h)),
            pl.BlockSpec((1, ltot, dv), lambda b, h: (b, 0, h)),
            pl.BlockSpec((1, 1, ltot, 4), lambda b, h: (b, h, 0, 0)),
            pl.BlockSpec((1, 1, 4, ltot), lambda b, h: (b, h, 0, 0)),
            pl.BlockSpec((3, dk), lambda b, h: (0, h)),
            pl.BlockSpec((3, dk), lambda b, h: (0, nh + h)),
            pl.BlockSpec((1, dk), lambda b, h: (0, h)),
            pl.BlockSpec((1, dk), lambda b, h: (0, nh + h)),
            pl.BlockSpec((1, 1, 4), lambda b, h: (h, 0, 0)),
            pl.BlockSpec((1, 4, 1), lambda b, h: (h, 0, 0)),
            pl.BlockSpec((1, dv), lambda b, h: (0, h)),
        ],
        out_specs=pl.BlockSpec((1, ltot, dv), lambda b, h: (b, 0, h)),
        out_shape=jax.ShapeDtypeStruct((bsz, ltot, nh * dv), BF16),
        scratch_shapes=[
            pltpu.VMEM((ltot, dk), BF16), pltpu.VMEM((ltot, dk), BF16),
            pltpu.VMEM((ltot, dv), F32), pltpu.VMEM((ltot, dv), F32),
            pltpu.VMEM((2, dv, dk), F32), pltpu.VMEM((2, 1, dk), F32), pltpu.VMEM((2, 8, LANES), F32),
        ],
        compiler_params=_cparams(("arbitrary", "arbitrary")),
        name="mlstm_scan",
    )(qk, qk, v, o, g4, g4t, conv_w, conv_w, conv_b.reshape(1, -1), conv_b.reshape(1, -1),
      gb.reshape(nh, 1, 4), gb.reshape(nh, 4, 1), head_g.reshape(1, -1))


def _router_logits_t(h2, rhi_ref, rlo_ref):
    h_hi, h_lo = _split_bf16(h2)
    lg = _dot(h_hi, rhi_ref[...]) + _dot(h_lo, rhi_ref[...]) + _dot(h_hi, rlo_ref[...])
    return lg.T[:N_EXPERTS, :]


def _out0_kernel(n_lat_tiles, hg_ref, xl_ref, xc_ref, g1_ref, sc_ref, sh_ref, ng_ref, wo_ref, rhi_ref, rlo_ref,
                 s_ref, h2_ref, lg_ref):
    j = pl.program_id(1)
    x = jnp.where(j < n_lat_tiles, xl_ref[0], xc_ref[0])
    s = x + g1_ref[0] * _dot(hg_ref[0], wo_ref[...])
    s_ref[0] = s
    h2 = _rms(s, ng_ref[...]) * (1.0 + sc_ref[0]) + sh_ref[0]
    h2_ref[...] = h2
    lg_ref[...] = _router_logits_t(h2, rhi_ref, rlo_ref)


def _out0(hg, x, ctx, modv, rows, norm_g, w_out, r_hi, r_lo, tm):
    bsz, ll, d = x.shape
    lc = ctx.shape[1]
    nl, nc = ll // tm, lc // tm
    nt = nl + nc
    ltot = ll + lc
    return pl.pallas_call(
        functools.partial(_out0_kernel, nl),
        grid=(bsz, nt),
        in_specs=[
            pl.BlockSpec((1, tm, hg.shape[2]), lambda b, j: (b, j, 0)),
            pl.BlockSpec((1, tm, d), lambda b, j: (b, jnp.minimum(j, nl - 1), 0)),
            pl.BlockSpec((1, tm, d), lambda b, j: (b, jnp.maximum(j - nl, 0), 0)),
            _mod_spec(d, 0, 2, nl, rows), _mod_spec(d, 0, 4, nl, rows), _mod_spec(d, 0, 3, nl, rows),
            pl.BlockSpec((1, d), lambda b, j: (0, 0)),
            pl.BlockSpec(w_out.shape, lambda b, j: (0, 0)),
            pl.BlockSpec(r_hi.shape, lambda b, j: (0, 0)),
            pl.BlockSpec(r_lo.shape, lambda b, j: (0, 0)),
        ],
        out_specs=[
            pl.BlockSpec((1, tm, d), lambda b, j: (b, j, 0)),
            pl.BlockSpec((tm, d), lambda b, j: (b * nt + j, 0)),
            pl.BlockSpec((N_EXPERTS, tm), lambda b, j: (0, b * nt + j)),
        ],
        out_shape=[
            jax.ShapeDtypeStruct((bsz, ltot, d), F32),
            jax.ShapeDtypeStruct((bsz * ltot, d), F32),
            jax.ShapeDtypeStruct((N_EXPERTS, bsz * ltot), F32),
        ],
        compiler_params=_cparams(("arbitrary", "arbitrary")),
        name="mlstm_out_proj",
    )(hg, x, ctx, modv, modv, modv, norm_g, w_out, r_hi, r_lo)


def _route_kernel(lg_ref, bias_ref, cls_ref, w_ref):
    s = _sigmoid(lg_ref[...])
    sel = s + bias_ref[...]
    srow = [s[e:e + 1, :] for e in range(N_EXPERTS)]
    row = [sel[e:e + 1, :] for e in range(N_EXPERTS)]
    best = jnp.zeros(row[0].shape, jnp.int32)
    best_score = None
    for g in range(N_GROUPS):
        r = row[g * E_PER_GROUP:(g + 1) * E_PER_GROUP]
        score = None
        for lo, hi in zip(PAIR_LO, PAIR_HI):
            pair = r[lo] + r[hi]
            score = pair if score is None else jnp.maximum(score, pair)
        if g == 0:
            best_score = score
        else:
            better = score > best_score
            best = jnp.where(better, g, best)
            best_score = jnp.where(better, score, best_score)
    gs = [row[i] for i in range(E_PER_GROUP)]
    gw = [srow[i] for i in range(E_PER_GROUP)]
    for g in range(1, N_GROUPS):
        hit = best == g
        gs = [jnp.where(hit, row[g * E_PER_GROUP + i], gs[i]) for i in range(E_PER_GROUP)]
        gw = [jnp.where(hit, srow[g * E_PER_GROUP + i], gw[i]) for i in range(E_PER_GROUP)]
    keep = []
    for i in range(E_PER_GROUP):
        beaten = jnp.zeros(best.shape, jnp.int32)
        for j in range(E_PER_GROUP):
            if j == i:
                continue
            wins = (gs[j] > gs[i]) | ((gs[j] == gs[i]) & (j < i))
            beaten = beaten + wins.astype(jnp.int32)
        keep.append(beaten < 2)
    pair_id = jnp.zeros(best.shape, jnp.int32)
    w_lo = jnp.zeros(best.shape, F32)
    w_hi = jnp.zeros(best.shape, F32)
    for p, (lo, hi) in enumerate(zip(PAIR_LO, PAIR_HI)):
        hit = keep[lo] & keep[hi]
        pair_id = jnp.where(hit, p, pair_id)
        w_lo = jnp.where(hit, gw[lo], w_lo)
        w_hi = jnp.where(hit, gw[hi], w_hi)
    tot = w_lo + w_hi
    cls_ref[...] = best * N_PAIRS + pair_id
    w_ref[0:1, :] = w_lo / tot
    w_ref[1:2, :] = w_hi / tot


def _route(logits_t, router_bias):
    n = logits_t.shape[1]
    tn = next(cand for cand in (2048, 1024, 512, 256, 128) if n % cand == 0)
    return pl.pallas_call(
        _route_kernel,
        grid=(n // tn,),
        in_specs=[
            pl.BlockSpec((N_EXPERTS, tn), lambda i: (0, i)),
            pl.BlockSpec((N_EXPERTS, 1), lambda i: (0, 0)),
        ],
        out_specs=[pl.BlockSpec((1, tn), lambda i: (0, i)), pl.BlockSpec((2, tn), lambda i: (0, i))],
        out_shape=[jax.ShapeDtypeStruct((1, n), jnp.int32), jax.ShapeDtypeStruct((2, n), F32)],
        compiler_params=_cparams(("arbitrary",)),
        name="moe_route",
    )(logits_t, router_bias.reshape(N_EXPERTS, 1).astype(F32))


def _moe_kernel(tm, spare_rows, nu_ref, e0_ref, e1_ref, src0_ref, srcn_ref, dstp_ref, w_ref, h2_hbm,
                wg0_ref, wu0_ref, wd0_ref, wg1_ref, wu1_ref, wd1_ref, f_hbm,
                xb0, xb1, yb0, yb1, xbf, gsem, ssem, zsem, idle_sem):
    del e0_ref, e1_ref
    t = pl.program_id(0)
    n_used = nu_ref[0]
    xbuf, ybuf = (xb0, xb1), (yb0, yb1)

    def gather_row(idx_ref, r, slot):
        tok = idx_ref[0, 0, r]
        pltpu.make_async_copy(h2_hbm.at[pl.ds(tok, 1)], xbuf[slot].at[pl.ds(r, 1)], gsem.at[slot]).start()

    def scatter_row(r, slot):
        tok = dstp_ref[0, 0, r]
        pltpu.make_async_copy(ybuf[slot].at[pl.ds(r, 1)], f_hbm.at[pl.ds(tok, 1)], ssem.at[slot]).start()

    def wait_gather(slot):
        pltpu.make_async_copy(h2_hbm.at[pl.ds(0, tm)], xbuf[slot], gsem.at[slot]).wait()

    def wait_scatter(slot):
        pltpu.make_async_copy(ybuf[slot], f_hbm.at[pl.ds(0, tm)], ssem.at[slot]).wait()

    @pl.when(t == 0)
    def _():
        yb1[...] = jnp.zeros(yb1.shape, F32)
        fills = [pltpu.make_async_copy(yb1, f_hbm.at[pl.ds(row, tm)], zsem) for row in spare_rows]
        for fill in fills:
            fill.start()
        for fill in fills:
            fill.wait()

        def first(r, carry):
            gather_row(src0_ref, r, 0)
            return carry

        lax.fori_loop(0, tm, first, 0)

    def step(cur):
        nxt = 1 - cur

        @pl.when(t < n_used)
        def _():
            wait_gather(cur)
            xbf[...] = xbuf[cur][...].astype(BF16)
            w = w_ref[...]

            for r in range(tm):
                gather_row(srcn_ref, r, nxt)
            x = xbf[...]
            a0 = (_silu(_dot(x, wg0_ref[0])) * _dot(x, wu0_ref[0]) * w[:, 0:1]).astype(BF16)

            z = pl.semaphore_read(idle_sem)
            zero = lax.shift_right_logical(z.astype(jnp.uint32), jnp.uint32(32)).astype(jnp.int32)
            off = pl.multiple_of(zero * 16, 16)

            for r in range(tm):
                scatter_row(r, nxt)
            x = xbf[pl.ds(off, tm), :]
            a1 = (_silu(_dot(x, wg1_ref[0])) * _dot(x, wu1_ref[0]) * w[:, 1:2]).astype(BF16)
            y = _dot(a0, wd0_ref[0]) + _dot(a1, wd1_ref[0])

            @pl.when(t >= 1)
            def _():
                wait_scatter(cur)

            ybuf[cur][...] = y

        @pl.when(t == n_used)
        def _():
            wait_gather(cur)
            wait_scatter(cur)

            def last(r, carry):
                scatter_row(r, nxt)
                return carry

            lax.fori_loop(0, tm, last, 0)
            wait_scatter(nxt)

    @pl.when(t % 2 == 0)
    def _():
        step(0)

    @pl.when(t % 2 == 1)
    def _():
        step(1)


def _moe(h2, cls, w2, dst_of_token, dump_base, spare_rows, out_rows, wg, wu, wd):
    n, d = h2.shape
    tm = MOE_TILE
    n_tiles = n // tm + N_CLASSES
    p_rows = n_tiles * tm
    cls = cls.reshape(n)
    order = jnp.argsort(cls, stable=True).astype(jnp.int32)
    counts = jnp.sum(cls[:, None] == jnp.arange(N_CLASSES, dtype=jnp.int32)[None, :], axis=0).astype(jnp.int32)
    tiles_per = (counts + tm - 1) // tm
    tile_end = jnp.cumsum(tiles_per)
    tile_start = tile_end - tiles_per
    first_sorted = jnp.cumsum(counts) - counts
    n_used = tile_end[-1]
    tile_id = jnp.arange(n_tiles, dtype=jnp.int32)
    tile_cls = jnp.searchsorted(tile_end, jnp.minimum(tile_id, n_used - 1), side="right").astype(jnp.int32)
    tile_cls = jnp.minimum(tile_cls, N_CLASSES - 1)
    group, pair = tile_cls // N_PAIRS, tile_cls % N_PAIRS
    e0 = group * E_PER_GROUP + jnp.asarray(PAIR_LO, jnp.int32)[pair]
    e1 = group * E_PER_GROUP + jnp.asarray(PAIR_HI, jnp.int32)[pair]
    p = jnp.arange(p_rows, dtype=jnp.int32)
    p_cls = tile_cls[p // tm]
    rank = p - tile_start[p_cls] * tm
    valid = (rank < counts[p_cls]) & (p // tm < n_used)
    tok = order[jnp.clip(first_sorted[p_cls] + rank, 0, n - 1)]
    src = jnp.where(valid, tok, 0).reshape(n_tiles, 1, tm)
    dump = (dump_base + p % tm).astype(jnp.int32)
    dst = jnp.where(valid, dst_of_token[tok], dump).reshape(n_tiles, 1, tm)
    w_rows = jnp.where(valid[:, None], w2[:, tok].T, 0.0)
    src_next = jnp.concatenate([src[1:], jnp.zeros((1, 1, tm), jnp.int32)], axis=0)
    dst_prev = jnp.concatenate([dump[:tm].reshape(1, 1, tm), dst[:-1]], axis=0)

    de = wg.shape[2]
    idx = pl.BlockSpec((1, 1, tm), lambda i, nu, e0, e1: (i, 0, 0), memory_space=pltpu.SMEM)
    idx0 = pl.BlockSpec((1, 1, tm), lambda i, nu, e0, e1: (0, 0, 0), memory_space=pltpu.SMEM)
    up0 = pl.BlockSpec((1, d, de), lambda i, nu, e0, e1: (e0[i], 0, 0))
    up1 = pl.BlockSpec((1, d, de), lambda i, nu, e0, e1: (e1[i], 0, 0))
    dn0 = pl.BlockSpec((1, de, d), lambda i, nu, e0, e1: (e0[i], 0, 0))
    dn1 = pl.BlockSpec((1, de, d), lambda i, nu, e0, e1: (e1[i], 0, 0))
    return pl.pallas_call(
        functools.partial(_moe_kernel, tm, spare_rows),
        grid_spec=pltpu.PrefetchScalarGridSpec(
            num_scalar_prefetch=3,
            grid=(n_tiles,),
            in_specs=[
                idx0, idx, idx,
                pl.BlockSpec((tm, 2), lambda i, nu, e0, e1: (i, 0)),
                pl.BlockSpec(memory_space=pl.ANY),
                up0, up0, dn0, up1, up1, dn1,
            ],
            out_specs=pl.BlockSpec(memory_space=pl.ANY),
            scratch_shapes=[
                pltpu.VMEM((tm, d), F32), pltpu.VMEM((tm, d), F32),
                pltpu.VMEM((tm, d), F32), pltpu.VMEM((tm, d), F32),
                pltpu.VMEM((tm, d), BF16),
                pltpu.SemaphoreType.DMA((2,)), pltpu.SemaphoreType.DMA((2,)), pltpu.SemaphoreType.DMA(()),
                pltpu.SemaphoreType.REGULAR(()),
            ],
        ),
        out_shape=jax.ShapeDtypeStruct((out_rows, d), F32),
        compiler_params=_cparams(("arbitrary",)),
        name="moe_experts",
    )(n_used.reshape(1).astype(jnp.int32), e0, e1, src, src_next, dst_prev, w_rows, h2,
      wg, wu, wd, wg, wu, wd)


def _hproj_kernel(n_lat_tiles, rows, sl_ref, fl_ref, sc_ref_, fc_ref, g2_ref, sc_ref, sh_ref, ng_ref, lb_ref,
                  wq_ref, wzf_ref, wzb_ref, wi_ref, wg_ref,
                  s2_ref, q_ref, kf_ref, kb_ref, lff_ref, lfb_ref, i_ref, g_ref, xt):
    j = pl.program_id(1)
    d = xt.shape[1]
    g2 = g2_ref[0]

    @pl.when(j < n_lat_tiles)
    def _():
        for c in range(COL_TILE):
            s = sl_ref[0, :, c, :] + g2 * fl_ref[0, :, c, :]
            s2_ref[0, :, c, :] = s
            xt[c * rows:(c + 1) * rows, :] = s

    @pl.when(j >= n_lat_tiles)
    def _():
        xt[...] = sc_ref_[0] + g2 * fc_ref[0]

    h = (_rms(xt[...], ng_ref[...]) * (1.0 + sc_ref[0]) + sh_ref[0]).astype(BF16)
    q_ref[0] = _silu(_dot(h, wq_ref[...])).astype(BF16)
    i_ref[0] = _dot(h, wi_ref[...]).astype(BF16)
    g_ref[0] = _dot(h, wg_ref[...]).astype(BF16)
    for dd, (wz_ref, k_ref, lf_ref) in enumerate(((wzf_ref, kf_ref, lff_ref), (wzb_ref, kb_ref, lfb_ref))):
        z = _dot(h, wz_ref[...])
        log_lb = lb_ref[2 * dd:2 * dd + 1, :]
        log_1mlb = lb_ref[2 * dd + 1:2 * dd + 2, :]
        ls = _log_sigmoid(z)
        a = log_1mlb + ls
        lf_ref[0] = jnp.maximum(log_lb, a) + jnp.log1p(jnp.exp(-jnp.abs(log_lb - a)))
        k_ref[0] = jnp.exp(a - z).astype(BF16)


def _hproj(s1, f0, modv, rows_mod, norm_g, lbtab, ws, ll, lc):
    bsz, ltot, d = s1.shape
    rows = ll // GRID_W
    tm = COL_TILE * rows
    nl, nc = GRID_W // COL_TILE, lc // tm
    s_cm = s1.reshape(bsz, ltot // GRID_W, GRID_W, d)
    f_tok = f0.reshape(bsz, -1, d)
    f_cm = f0.reshape(bsz, -1, GRID_W, d)
    cm = pl.BlockSpec((1, rows, COL_TILE, d), lambda b, j: (b, 0, jnp.minimum(j, nl - 1), 0))
    ctx = pl.BlockSpec((1, tm, d), lambda b, j: (b, ll // tm + jnp.maximum(j - nl, 0), 0))
    wspec = pl.BlockSpec((d, d), lambda b, j: (0, 0))
    tok = pl.BlockSpec((1, tm, d), lambda b, j: (b, j, 0))

    def g2_map(b, j):
        row = jnp.where(j < nl, b, rows_mod - 1)
        return ((0 * rows_mod + row) * N_ADA + 5, 0, 0)

    outs = pl.pallas_call(
        functools.partial(_hproj_kernel, nl, rows),
        grid=(bsz, nl + nc),
        in_specs=[
            cm, cm, ctx, ctx,
            pl.BlockSpec((1, 1, d), g2_map),
            _mod_spec(d, 1, 1, nl, rows_mod), _mod_spec(d, 1, 0, nl, rows_mod),
            pl.BlockSpec((1, d), lambda b, j: (0, 0)),
            pl.BlockSpec((4, d), lambda b, j: (0, 0)),
            wspec, wspec, wspec, wspec, wspec,
        ],
        out_specs=[cm, tok, tok, tok, tok, tok, tok, tok],
        out_shape=[jax.ShapeDtypeStruct((bsz, rows, GRID_W, d), F32)]
        + [jax.ShapeDtypeStruct((bsz, ltot, d), dt) for dt in (BF16, BF16, BF16, F32, F32, BF16, BF16)],
        scratch_shapes=[pltpu.VMEM((tm, d), F32)],
        compiler_params=_cparams(("arbitrary", "arbitrary")),
        name="hgrn_in_proj",
    )(s_cm, f_cm, s1, f_tok, modv, modv, modv, norm_g, lbtab, *ws)
    return outs


def _hgrn_kernel(t, ll, lc, q_ref, kf_ref, kb_ref, lff_ref, lfb_ref, i_ref, g_ref, hg_ref, out_ref,
                 of, ob, st_s):
    st_s[...] = jnp.zeros(st_s.shape, F32)
    row = lax.broadcasted_iota(jnp.int32, (t, 1), 0)
    ti = lax.broadcasted_iota(jnp.int32, (t, t), 0)
    si = lax.broadcasted_iota(jnp.int32, (t, t), 1)
    xor = ti ^ si
    levels = []
    m = 2
    while m <= t:
        levels.append(m)
        m *= 2

    def cumsum(lf, d):
        b = lf
        sh = 1
        while sh < t:
            if d == 0:
                b = b + jnp.where(row >= sh, pltpu.roll(b, sh, 0), 0.0)
            else:
                b = b + jnp.where(row < t - sh, pltpu.roll(b, t - sh, 0), 0.0)
            sh *= 2
        return b

    def block_ref(b, m, d):
        half = m // 2
        pos = half - 1 if d == 0 else half
        if m >= 8:
            dk = b.shape[1]
            b3 = b.reshape(t // m, m, dk)
            return jnp.broadcast_to(b3[:, pos:pos + 1, :], b3.shape).reshape(t, dk)
        r = row % m
        out = b
        for res in range(m):
            if res == pos:
                continue
            out = jnp.where(r == res, pltpu.roll(b, (res - pos) % t, 0), out)
        return out

    def chunk(start, d, need_out, o_out):
        q = q_ref[0, pl.ds(start, t), :]
        k = (kf_ref if d == 0 else kb_ref)[0, pl.ds(start, t), :].astype(F32)
        lf = (lff_ref if d == 0 else lfb_ref)[0, pl.ds(start, t), :]
        iv = i_ref[0, pl.ds(start, t), :]
        b = cumsum(lf, d)
        b_tot = b[t - 1:t, :] if d == 0 else b[0:1, :]
        st = st_s[d]
        if need_out:
            qf = q.astype(F32)
            a = _dot_nt(q, k.astype(BF16))
            for m in levels:
                e = jnp.exp(-jnp.abs(b - block_ref(b, m, d)))
                p = _dot_nt((qf * e).astype(BF16), (k * e).astype(BF16))
                a = jnp.where(xor < m // 2, a, p) if m > 2 else jnp.where(xor == 0, a, p)
            feeds = (si <= ti) if d == 0 else (si >= ti)
            a = jnp.where(feeds, a, 0.0)
            o = _dot(a.astype(BF16), iv) + _dot_nt((qf * jnp.exp(b)).astype(BF16), st.astype(BF16))
            o_out[pl.ds(start, t), :] = o
        kd = (k * jnp.exp(b_tot - b)).astype(BF16)
        st_s[d] = jnp.exp(b_tot) * st + _dot_tn(iv, kd)

    ncc, ncl = lc // t, ll // t
    for i in range(ncc):
        chunk(ll + i * t, 0, False, None)
        chunk(ll + (ncc - 1 - i) * t, 1, False, None)

    def body(i, carry):
        chunk(pl.multiple_of(i * t, t), 0, True, of)
        chunk(pl.multiple_of((ncl - 1 - i) * t, t), 1, True, ob)
        return carry

    lax.fori_loop(0, ncl, body, 0)

    def epilogue(i, carry):
        s = pl.multiple_of(i * t, t)
        y = _rms(of[pl.ds(s, t), :] + ob[pl.ds(s, t), :], hg_ref[...])
        out_ref[0, pl.ds(s, t), :] = (y * _silu(g_ref[0, pl.ds(s, t), :].astype(F32))).astype(BF16)
        return carry

    lax.fori_loop(0, ncl, epilogue, 0)


def _hgrn_scan(q, kf, kb, lff, lfb, iv, g, head_g, ll, lc):
    bsz, ltot, hk = q.shape
    nh = H_HEADS
    dk = hk // nh
    t = SCAN_CHUNK
    full = pl.BlockSpec((1, ltot, dk), lambda b, h: (b, 0, h))
    lat = pl.BlockSpec((1, ll, dk), lambda b, h: (b, 0, h))
    return pl.pallas_call(
        functools.partial(_hgrn_kernel, t, ll, lc),
        grid=(bsz, nh),
        in_specs=[full, full, full, full, full, full, lat, pl.BlockSpec((1, dk), lambda b, h: (0, h))],
        out_specs=lat,
        out_shape=jax.ShapeDtypeStruct((bsz, ll, hk), BF16),
        scratch_shapes=[pltpu.VMEM((ll, dk), F32), pltpu.VMEM((ll, dk), F32), pltpu.VMEM((2, dk, dk), F32)],
        compiler_params=_cparams(("arbitrary", "arbitrary")),
        name="hgrn_scan",
    )(q, kf, kb, lff, lfb, iv, g, head_g.reshape(1, -1))


def _out1_kernel(rows, hg_ref, s_ref, g1_ref, sc_ref, sh_ref, ng_ref, wo_ref, rhi_ref, rlo_ref,
                 s3_ref, h2_ref, lg_ref, st):
    d = st.shape[1]
    y = g1_ref[0] * _dot(hg_ref[0], wo_ref[...])
    for c in range(COL_TILE):
        s = s_ref[0, :, c, :] + y[c * rows:(c + 1) * rows, :]
        s3_ref[0, :, c, :] = s
        st[c * rows:(c + 1) * rows, :] = s
    h2 = _rms(st[...], ng_ref[...]) * (1.0 + sc_ref[0]) + sh_ref[0]
    h2_ref[...] = h2
    lg_ref[...] = _router_logits_t(h2, rhi_ref, rlo_ref)


def _out1(hg, s2_cm, modv, rows_mod, norm_g, w_out, r_hi, r_lo):
    bsz, ll, hv = hg.shape
    rows = s2_cm.shape[1]
    d = s2_cm.shape[3]
    tm = COL_TILE * rows
    nl = GRID_W // COL_TILE
    cm = pl.BlockSpec((1, rows, COL_TILE, d), lambda b, j: (b, 0, j, 0))
    return pl.pallas_call(
        functools.partial(_out1_kernel, rows),
        grid=(bsz, nl),
        in_specs=[
            pl.BlockSpec((1, tm, hv), lambda b, j: (b, j, 0)),
            cm,
            _mod_spec(d, 1, 2, nl, rows_mod), _mod_spec(d, 1, 4, nl, rows_mod), _mod_spec(d, 1, 3, nl, rows_mod),
            pl.BlockSpec((1, d), lambda b, j: (0, 0)),
            pl.BlockSpec(w_out.shape, lambda b, j: (0, 0)),
            pl.BlockSpec(r_hi.shape, lambda b, j: (0, 0)),
            pl.BlockSpec(r_lo.shape, lambda b, j: (0, 0)),
        ],
        out_specs=[
            cm,
            pl.BlockSpec((tm, d), lambda b, j: (b * nl + j, 0)),
            pl.BlockSpec((N_EXPERTS, tm), lambda b, j: (0, b * nl + j)),
        ],
        out_shape=[
            jax.ShapeDtypeStruct(s2_cm.shape, F32),
            jax.ShapeDtypeStruct((bsz * ll, d), F32),
            jax.ShapeDtypeStruct((N_EXPERTS, bsz * ll), F32),
        ],
        scratch_shapes=[pltpu.VMEM((tm, d), F32)],
        compiler_params=_cparams(("arbitrary", "arbitrary")),
        name="hgrn_out_proj",
    )(hg, s2_cm, modv, modv, modv, norm_g, w_out, r_hi, r_lo)


def _final_kernel(s_ref, f_ref, g2_ref, fg_ref, o_ref):
    o_ref[0] = _rms(s_ref[0] + g2_ref[0] * f_ref[0], fg_ref[...])


def _final(s3, f1, modv, rows_mod, final_g, tm):
    bsz, ll, d = s3.shape
    nl = ll // tm
    tok = pl.BlockSpec((1, tm, d), lambda b, j: (b, j, 0))
    return pl.pallas_call(
        _final_kernel,
        grid=(bsz, nl),
        in_specs=[tok, tok, _mod_spec(d, 1, 5, nl, rows_mod), pl.BlockSpec((1, d), lambda b, j: (0, 0))],
        out_specs=tok,
        out_shape=jax.ShapeDtypeStruct((bsz, ll, d), F32),
        compiler_params=_cparams(("arbitrary", "arbitrary")),
        name="final_norm",
    )(s3, f1.reshape(bsz, -1, d), modv, final_g)


def kernel(x, c, ctx, c_ctx, ada_w, ada_b, norm_mix_g, norm_ffn_g, final_g, m_w_in, m_conv_w, m_conv_b, m_gate_b,
           m_head_g, m_w_out, h_w_in, h_lower_bounds, h_head_g, h_w_out, router_w, router_bias, e_w_gate,
           e_w_up, e_w_down):
    bsz, ll, d = x.shape
    lc = ctx.shape[1]
    ltot = ll + lc
    depth = ada_w.shape[0]
    assert depth == 2 and ll % GRID_W == 0 and lc % SCAN_CHUNK == 0 and ll % SCAN_CHUNK == 0
    rows = ll // GRID_W
    tm = COL_TILE * rows
    assert lc % tm == 0 and ltot % GRID_W == 0 and (bsz * ltot) % MOE_TILE == 0 and (bsz * ll) % MOE_TILE == 0

    rows_mod = 8 * ((bsz + 1 + 7) // 8)
    cc = jnp.zeros((rows_mod, d), F32).at[:bsz].set(c).at[rows_mod - 1].set(c_ctx)
    modv = _ada(cc, ada_w, ada_b).reshape(depth * rows_mod * N_ADA, 1, d)

    r_pad = jnp.zeros((d, LANES), F32).at[:, :N_EXPERTS].set(router_w)
    r_hi = r_pad.astype(BF16)
    r_lo = (r_pad - r_hi.astype(F32)).astype(BF16)
    wg_all, wu_all, wd_all = e_w_gate.astype(BF16), e_w_up.astype(BF16), e_w_down.astype(BF16)

    m_qk, m_v = m_conv_w.shape[2], m_head_g.shape[1]
    w_in = m_w_in[0].astype(BF16)
    w_gates = jnp.zeros((d, LANES), BF16).at[:, :4 * M_HEADS].set(w_in[:, m_qk + 2 * m_v:])
    qk, v, o, gates = _mproj(x, ctx, modv, rows_mod, norm_mix_g[0:1], w_in[:, :m_qk], w_in[:, m_qk:m_qk + m_v],
                             w_in[:, m_qk + m_v:m_qk + 2 * m_v], w_gates, tm)
    hg = _mlstm_scan(qk, v, o, gates, m_conv_w[0], m_conv_b[0], m_gate_b[0], m_head_g[0], ll, lc)
    s1, h2, lg = _out0(hg, x, ctx, modv, rows_mod, norm_ffn_g[0:1], m_w_out[0].astype(BF16), r_hi, r_lo, tm)
    cls, w2 = _route(lg, router_bias)
    tok0 = jnp.arange(bsz * ltot, dtype=jnp.int32)
    dst0 = (tok0 // ltot) * (ltot + MOE_TILE) + tok0 % ltot
    spare0 = tuple(b * (ltot + MOE_TILE) + ltot for b in range(1, bsz))
    f0 = _moe(h2, cls, w2, dst0, ltot, spare0, bsz * (ltot + MOE_TILE), wg_all[0], wu_all[0], wd_all[0])

    lbs = jnp.cumsum(jax.nn.softmax(h_lower_bounds.astype(F32), axis=0), axis=0)
    lb = (lbs - lbs[0])[1].reshape(2, -1)
    lbtab = jnp.stack([jnp.log(lb[0]), jnp.log1p(-lb[0]), jnp.log(lb[1]), jnp.log1p(-lb[1])])
    hw = h_w_in[0].astype(BF16)
    hk = lb.shape[1]
    ws = (hw[:, :hk], hw[:, hk:2 * hk], hw[:, 2 * hk:3 * hk], hw[:, 3 * hk:3 * hk + d], hw[:, 3 * hk + d:])
    s2_cm, q, kf, kb, lff, lfb, iv, g = _hproj(s1, f0, modv, rows_mod, norm_mix_g[1:2], lbtab, ws, ll, lc)
    hg1 = _hgrn_scan(q, kf, kb, lff, lfb, iv, g, h_head_g[0], ll, lc)
    s3_cm, h2b, lgb = _out1(hg1, s2_cm, modv, rows_mod, norm_ffn_g[1:2], h_w_out[0].astype(BF16), r_hi, r_lo)
    clsb, w2b = _route(lgb, router_bias)
    n1 = bsz * ll
    tokid = jnp.arange(n1, dtype=jnp.int32)
    within = tokid % ll
    raster = (tokid // ll) * (ll + MOE_TILE) + (within % rows) * GRID_W + within // rows
    spare1 = tuple(b * (ll + MOE_TILE) + ll for b in range(1, bsz))
    f1 = _moe(h2b, clsb, w2b, raster, ll, spare1, bsz * (ll + MOE_TILE), wg_all[1], wu_all[1], wd_all[1])

    return _final(s3_cm.reshape(bsz, ll, d), f1, modv, rows_mod, final_g.reshape(1, d), tm)
```

```python
import functools

import numpy as np
import jax
import jax.numpy as jnp
from jax import lax
from jax.experimental import pallas as pl
from jax.experimental.pallas import tpu as pltpu

F32 = jnp.float32
BF16 = jnp.bfloat16

EPS = 1e-6
N_ADA = 6
GRID_W = 64
M_HEADS = 4
H_HEADS = 8
N_EXPERTS = 16
N_GROUPS = 4
E_PER_GROUP = N_EXPERTS // N_GROUPS
N_PAIRS = 6
N_CLASSES = N_GROUPS * N_PAIRS
PAIR_LO = (0, 0, 0, 1, 1, 2)
PAIR_HI = (1, 2, 3, 2, 3, 3)

LANES = 128
COL_TILE = 8
SCAN_CHUNK = 128
MOE_TILE = 256
VMEM_LIMIT = 56 * 1024 * 1024


def _cparams(sem):
    return pltpu.CompilerParams(dimension_semantics=sem, vmem_limit_bytes=VMEM_LIMIT)


def _dot(a, b):
    return jnp.dot(a, b, preferred_element_type=F32)


def _dot_nt(a, b):
    return lax.dot_general(a, b, (((1,), (1,)), ((), ())), preferred_element_type=F32)


def _dot_tn(a, b):
    return lax.dot_general(a, b, (((0,), (0,)), ((), ())), preferred_element_type=F32)


def _split_bf16(a):
    hi = a.astype(BF16)
    lo = (a - hi.astype(F32)).astype(BF16)
    return hi, lo


def _sigmoid(x):
    return 1.0 / (1.0 + jnp.exp(-x))


def _silu(x):
    return x * _sigmoid(x)


def _log_sigmoid(x):
    return jnp.minimum(x, 0.0) - jnp.log1p(jnp.exp(-jnp.abs(x)))


def _rms(x, g):
    return x * lax.rsqrt(jnp.mean(x * x, axis=-1, keepdims=True) + EPS) * g


def _ada_kernel(c_ref, w_ref, b_ref, o_ref):
    a = _silu(c_ref[...])
    a_hi, a_lo = _split_bf16(a)
    w_hi, w_lo = _split_bf16(w_ref[0])
    acc = _dot(a_hi, w_hi) + _dot(a_lo, w_hi) + _dot(a_hi, w_lo)
    o_ref[0] = acc + b_ref[0]


def _ada(cc, ada_w, ada_b):
    depth, d, n = ada_w.shape
    tn = 1024
    rows = cc.shape[0]
    return pl.pallas_call(
        _ada_kernel,
        grid=(depth, n // tn),
        in_specs=[
            pl.BlockSpec((rows, d), lambda l, j: (0, 0)),
            pl.BlockSpec((1, d, tn), lambda l, j: (l, 0, j)),
            pl.BlockSpec((1, 1, tn), lambda l, j: (l, 0, j)),
        ],
        out_specs=pl.BlockSpec((1, rows, tn), lambda l, j: (l, 0, j)),
        out_shape=jax.ShapeDtypeStruct((depth, rows, n), F32),
        compiler_params=_cparams(("arbitrary", "arbitrary")),
        name="ada_mod",
    )(cc, ada_w, ada_b.reshape(depth, 1, n))


def _mproj_kernel(n_lat_tiles, xl_ref, xc_ref, sc_ref, sh_ref, g_ref, wqk_ref, wv_ref, wo_ref, wg_ref,
                  qk_ref, v_ref, o_ref, gt_ref):
    j = pl.program_id(1)
    x = jnp.where(j < n_lat_tiles, xl_ref[0], xc_ref[0])
    h = (_rms(x, g_ref[...]) * (1.0 + sc_ref[0]) + sh_ref[0]).astype(BF16)
    qk_ref[0] = _dot(h, wqk_ref[...]).astype(BF16)
    v_ref[0] = _dot(h, wv_ref[...]).astype(BF16)
    o_ref[0] = _dot(h, wo_ref[...]).astype(BF16)
    gt_ref[0] = _dot(h, wg_ref[...])[:, :4 * M_HEADS]


def _mod_spec(d, layer, k, n_lat_tiles, rows):
    def imap(b, j):
        row = jnp.where(j < n_lat_tiles, b, rows - 1)
        return ((layer * rows + row) * N_ADA + k, 0, 0)
    return pl.BlockSpec((1, 1, d), imap)


def _mproj(x, ctx, modv, rows, norm_g, w_qk, w_v, w_o, w_g, tm):
    bsz, ll, d = x.shape
    lc = ctx.shape[1]
    nl, nc = ll // tm, lc // tm
    ltot = ll + lc
    wspec = lambda n: pl.BlockSpec((d, n), lambda b, j: (0, 0))
    tok = lambda n: pl.BlockSpec((1, tm, n), lambda b, j: (b, j, 0))
    return pl.pallas_call(
        functools.partial(_mproj_kernel, nl),
        grid=(bsz, nl + nc),
        in_specs=[
            pl.BlockSpec((1, tm, d), lambda b, j: (b, jnp.minimum(j, nl - 1), 0)),
            pl.BlockSpec((1, tm, d), lambda b, j: (b, jnp.maximum(j - nl, 0), 0)),
            _mod_spec(d, 0, 1, nl, rows), _mod_spec(d, 0, 0, nl, rows),
            pl.BlockSpec((1, d), lambda b, j: (0, 0)),
            wspec(w_qk.shape[1]), wspec(w_v.shape[1]), wspec(w_o.shape[1]), wspec(w_g.shape[1]),
        ],
        out_specs=[tok(w_qk.shape[1]), tok(w_v.shape[1]), tok(w_o.shape[1]), tok(4 * M_HEADS)],
        out_shape=[
            jax.ShapeDtypeStruct((bsz, ltot, w_qk.shape[1]), BF16),
            jax.ShapeDtypeStruct((bsz, ltot, w_v.shape[1]), BF16),
            jax.ShapeDtypeStruct((bsz, ltot, w_o.shape[1]), BF16),
            jax.ShapeDtypeStruct((bsz, ltot, 4 * M_HEADS), F32),
        ],
        compiler_params=_cparams(("arbitrary", "arbitrary")),
        name="mlstm_in_proj",
    )(x, ctx, modv, modv, norm_g, w_qk, w_v, w_o, w_g)


def _mlstm_kernel(t, ll, lc, q_ref, k_ref, v_ref, o_ref, gc_ref, gr_ref, cwq_ref, cwk_ref, cbq_ref, cbk_ref,
                  gbc_ref, gbr_ref, hg_ref, out_ref, qs, ks, hf, hb, c_s, n_s, m_s):
    ltot = ll + lc
    dk = q_ref.shape[2]
    row = lax.broadcasted_iota(jnp.int32, (ltot, 1), 0)
    first = (row == 0) | (row == ll)
    last = (row == ll - 1) | (row == ltot - 1)

    def conv(x_ref, w_ref, b_ref):
        x = x_ref[0].astype(F32)
        w = w_ref[...]
        xp = jnp.where(first, 0.0, pltpu.roll(x, 1, 0))
        xn = jnp.where(last, 0.0, pltpu.roll(x, ltot - 1, 0))
        return _silu(xp * w[0:1] + x * w[1:2] + xn * w[2:3] + b_ref[...])

    qs[...] = (conv(q_ref, cwq_ref, cbq_ref) * (dk ** -0.5)).astype(BF16)
    ks[...] = conv(k_ref, cwk_ref, cbk_ref).astype(BF16)

    c_s[...] = jnp.zeros(c_s.shape, F32)
    n_s[...] = jnp.zeros(n_s.shape, F32)
    m_s[...] = jnp.zeros(m_s.shape, F32)

    ti = lax.broadcasted_iota(jnp.int32, (t, t), 0)
    si = lax.broadcasted_iota(jnp.int32, (t, t), 1)

    def chunk(start, d, h_out):
        q = qs[pl.ds(start, t), :]
        k = ks[pl.ds(start, t), :]
        v = v_ref[0, pl.ds(start, t), :]
        gc = gc_ref[0, 0, pl.ds(start, t), :] + gbc_ref[0]
        gr = gr_ref[0, 0, :, pl.ds(start, t)] + gbr_ref[0]
        ig_c, lf_c = gc[:, 2 * d:2 * d + 1], _log_sigmoid(gc[:, 2 * d + 1:2 * d + 2])
        ig_r, lf_r = gr[2 * d:2 * d + 1, :], _log_sigmoid(gr[2 * d + 1:2 * d + 2, :])
        seen = (si <= ti) if d == 0 else (si >= ti)
        seen_t = (ti <= si) if d == 0 else (ti >= si)
        b_c = jnp.sum(jnp.where(seen, lf_r, 0.0), axis=1, keepdims=True)
        b_r = jnp.sum(jnp.where(seen_t, lf_c, 0.0), axis=0, keepdims=True)
        total = jnp.sum(lf_r, axis=1, keepdims=True)
        c_mem = c_s[d]
        n_mem = n_s[d]
        m_prev = m_s[d][0:1, 0:1]

        log_d = jnp.where(seen, b_c - b_r + ig_r, -jnp.inf)
        log_inter = b_c + m_prev
        m_t = jnp.maximum(log_inter, jnp.max(log_d, axis=1, keepdims=True))
        w_ts = jnp.exp(log_d - m_t) * _dot_nt(q, k)
        inter = jnp.exp(log_inter - m_t)
        num = _dot(w_ts.astype(BF16), v) + inter * _dot_nt(q, c_mem.astype(BF16))
        den = jnp.sum(w_ts, axis=1, keepdims=True) + inter * jnp.sum(q.astype(F32) * n_mem, axis=1, keepdims=True)
        h_out[pl.ds(start, t), :] = num / jnp.maximum(jnp.abs(den), jnp.exp(-m_t))

        g_c = total - b_c + ig_c
        g_r = total - b_r + ig_r
        m_new = jnp.maximum(total + m_prev, jnp.max(g_r, axis=1, keepdims=True))
        w_c = jnp.exp(g_c - m_new)
        decay = jnp.exp(total + m_prev - m_new)
        wv = (w_c * v.astype(F32)).astype(BF16)
        c_s[d] = decay * c_mem + _dot_tn(wv, k)
        n_s[d] = decay * n_mem + jnp.sum(w_c * k.astype(F32), axis=0, keepdims=True)
        m_s[d] = jnp.broadcast_to(m_new, m_s.shape[1:])

    ncc, ncl = lc // t, ll // t
    for i in range(ncc):
        chunk(ll + i * t, 0, hf)
        chunk(ll + (ncc - 1 - i) * t, 1, hb)

    def body(i, carry):
        chunk(pl.multiple_of(i * t, t), 0, hf)
        chunk(pl.multiple_of((ncl - 1 - i) * t, t), 1, hb)
        return carry

    lax.fori_loop(0, ncl, body, 0)

    def epilogue(i, carry):
        s = pl.multiple_of(i * t, t)
        hs = hf[pl.ds(s, t), :] + hb[pl.ds(s, t), :]
        y = _rms(hs, hg_ref[...])
        out_ref[0, pl.ds(s, t), :] = (y * _sigmoid(o_ref[0, pl.ds(s, t), :].astype(F32))).astype(BF16)
        return carry

    lax.fori_loop(0, ltot // t, epilogue, 0)


def _mlstm_scan(qk, v, o, gates, conv_w, conv_b, gate_b, head_g, ll, lc):
    bsz, ltot, _ = qk.shape
    nh = M_HEADS
    dk = qk.shape[2] // (2 * nh)
    dv = v.shape[2] // nh
    t = SCAN_CHUNK
    g4 = gates.reshape(bsz, ltot, 4, nh).transpose(0, 3, 1, 2)
    g4t = g4.transpose(0, 1, 3, 2)
    gb = gate_b.reshape(4, nh).T
    return pl.pallas_call(
        functools.partial(_mlstm_kernel, t, ll, lc),
        grid=(bsz, nh),
        in_specs=[
            pl.BlockSpec((1, ltot, dk), lambda b, h: (b, 0, h)),
            pl.BlockSpec((1, ltot, dk), lambda b, h: (b, 0, nh + h)),
            pl.BlockSpec((1, ltot, dv), lambda b, h: (b, 0, h)),
            pl.BlockSpec((1, ltot, dv), lambda b, h: (b, 0, h)),
            pl.BlockSpec((1, 1, ltot, 4), lambda b, h: (b, h, 0, 0)),
            pl.BlockSpec((1, 1, 4, ltot), lambda b, h: (b, h, 0, 0)),
            pl.BlockSpec((3, dk), lambda b, h: (0, h)),
            pl.BlockSpec((3, dk), lambda b, h: (0, nh + h)),
            pl.BlockSpec((1, dk), lambda b, h: (0, h)),
            pl.BlockSpec((1, dk), lambda b, h: (0, nh + h)),
            pl.BlockSpec((1, 1, 4), lambda b, h: (h, 0, 0)),
            pl.BlockSpec((1, 4, 1), lambda b, h: (h, 0, 0)),
            pl.BlockSpec((1, dv), lambda b, h: (0, h)),
        ],
        out_specs=pl.BlockSpec((1, ltot, dv), lambda b, h: (b, 0, h)),
        out_shape=jax.ShapeDtypeStruct((bsz, ltot, nh * dv), BF16),
        scratch_shapes=[
            pltpu.VMEM((ltot, dk), BF16), pltpu.VMEM((ltot, dk), BF16),
            pltpu.VMEM((ltot, dv), F32), pltpu.VMEM((ltot, dv), F32),
            pltpu.VMEM((2, dv, dk), F32), pltpu.VMEM((2, 1, dk), F32), pltpu.VMEM((2, 8, LANES), F32),
        ],
        compiler_params=_cparams(("arbitrary", "arbitrary")),
        name="mlstm_scan",
    )(qk, qk, v, o, g4, g4t, conv_w, conv_w, conv_b.reshape(1, -1), conv_b.reshape(1, -1),
      gb.reshape(nh, 1, 4), gb.reshape(nh, 4, 1), head_g.reshape(1, -1))


def _router_logits_t(h2, rhi_ref, rlo_ref):
    h_hi, h_lo = _split_bf16(h2)
    lg = _dot(h_hi, rhi_ref[...]) + _dot(h_lo, rhi_ref[...]) + _dot(h_hi, rlo_ref[...])
    return lg.T[:N_EXPERTS, :]


def _out0_kernel(n_lat_tiles, hg_ref, xl_ref, xc_ref, g1_ref, sc_ref, sh_ref, ng_ref, wo_ref, rhi_ref, rlo_ref,
                 s_ref, h2_ref, lg_ref):
    j = pl.program_id(1)
    x = jnp.where(j < n_lat_tiles, xl_ref[0], xc_ref[0])
    s = x + g1_ref[0] * _dot(hg_ref[0], wo_ref[...])
    s_ref[0] = s
    h2 = _rms(s, ng_ref[...]) * (1.0 + sc_ref[0]) + sh_ref[0]
    h2_ref[:, :h2.shape[1]] = h2
    h2_ref[:, h2.shape[1]:] = jnp.zeros((h2.shape[0], LANES), F32)
    lg_ref[...] = _router_logits_t(h2, rhi_ref, rlo_ref)


def _out0(hg, x, ctx, modv, rows, norm_g, w_out, r_hi, r_lo, tm):
    bsz, ll, d = x.shape
    lc = ctx.shape[1]
    nl, nc = ll // tm, lc // tm
    nt = nl + nc
    ltot = ll + lc
    return pl.pallas_call(
        functools.partial(_out0_kernel, nl),
        grid=(bsz, nt),
        in_specs=[
            pl.BlockSpec((1, tm, hg.shape[2]), lambda b, j: (b, j, 0)),
            pl.BlockSpec((1, tm, d), lambda b, j: (b, jnp.minimum(j, nl - 1), 0)),
            pl.BlockSpec((1, tm, d), lambda b, j: (b, jnp.maximum(j - nl, 0), 0)),
            _mod_spec(d, 0, 2, nl, rows), _mod_spec(d, 0, 4, nl, rows), _mod_spec(d, 0, 3, nl, rows),
            pl.BlockSpec((1, d), lambda b, j: (0, 0)),
            pl.BlockSpec(w_out.shape, lambda b, j: (0, 0)),
            pl.BlockSpec(r_hi.shape, lambda b, j: (0, 0)),
            pl.BlockSpec(r_lo.shape, lambda b, j: (0, 0)),
        ],
        out_specs=[
            pl.BlockSpec((1, tm, d), lambda b, j: (b, j, 0)),
            pl.BlockSpec((tm, d + LANES), lambda b, j: (b * nt + j, 0)),
            pl.BlockSpec((N_EXPERTS, tm), lambda b, j: (0, b * nt + j)),
        ],
        out_shape=[
            jax.ShapeDtypeStruct((bsz, ltot, d), F32),
            jax.ShapeDtypeStruct((bsz * ltot, d + LANES), F32),
            jax.ShapeDtypeStruct((N_EXPERTS, bsz * ltot), F32),
        ],
        compiler_params=_cparams(("arbitrary", "arbitrary")),
        name="mlstm_out_proj",
    )(hg, x, ctx, modv, modv, modv, norm_g, w_out, r_hi, r_lo)


def _route_kernel(lg_ref, bias_ref, h2x_hbm, cls_ref, w_ref):
    del h2x_hbm
    s = _sigmoid(lg_ref[...])
    sel = s + bias_ref[...]
    srow = [s[e:e + 1, :] for e in range(N_EXPERTS)]
    row = [sel[e:e + 1, :] for e in range(N_EXPERTS)]
    best = jnp.zeros(row[0].shape, jnp.int32)
    best_score = None
    for g in range(N_GROUPS):
        r = row[g * E_PER_GROUP:(g + 1) * E_PER_GROUP]
        score = None
        for lo, hi in zip(PAIR_LO, PAIR_HI):
            pair = r[lo] + r[hi]
            score = pair if score is None else jnp.maximum(score, pair)
        if g == 0:
            best_score = score
        else:
            better = score > best_score
            best = jnp.where(better, g, best)
            best_score = jnp.where(better, score, best_score)
    gs = [row[i] for i in range(E_PER_GROUP)]
    gw = [srow[i] for i in range(E_PER_GROUP)]
    for g in range(1, N_GROUPS):
        hit = best == g
        gs = [jnp.where(hit, row[g * E_PER_GROUP + i], gs[i]) for i in range(E_PER_GROUP)]
        gw = [jnp.where(hit, srow[g * E_PER_GROUP + i], gw[i]) for i in range(E_PER_GROUP)]
    keep = []
    for i in range(E_PER_GROUP):
        beaten = jnp.zeros(best.shape, jnp.int32)
        for j in range(E_PER_GROUP):
            if j == i:
                continue
            wins = (gs[j] > gs[i]) | ((gs[j] == gs[i]) & (j < i))
            beaten = beaten + wins.astype(jnp.int32)
        keep.append(beaten < 2)
    pair_id = jnp.zeros(best.shape, jnp.int32)
    w_lo = jnp.zeros(best.shape, F32)
    w_hi = jnp.zeros(best.shape, F32)
    for p, (lo, hi) in enumerate(zip(PAIR_LO, PAIR_HI)):
        hit = keep[lo] & keep[hi]
        pair_id = jnp.where(hit, p, pair_id)
        w_lo = jnp.where(hit, gw[lo], w_lo)
        w_hi = jnp.where(hit, gw[hi], w_hi)
    tot = w_lo + w_hi
    cls_ref[...] = best * N_PAIRS + pair_id
    lanes_t = jnp.concatenate([w_lo / tot, w_hi / tot, jnp.zeros((LANES - 2, w_lo.shape[1]), F32)], axis=0)
    w_ref[...] = lanes_t.T


def _route(logits_t, router_bias, h2x):
    n = logits_t.shape[1]
    d = h2x.shape[1] - LANES
    tn = next(cand for cand in (2048, 1024, 512, 256, 128) if n % cand == 0)
    return pl.pallas_call(
        _route_kernel,
        grid=(n // tn,),
        in_specs=[
            pl.BlockSpec((N_EXPERTS, tn), lambda i: (0, i)),
            pl.BlockSpec((N_EXPERTS, 1), lambda i: (0, 0)),
            pl.BlockSpec(memory_space=pl.ANY),
        ],
        out_specs=[pl.BlockSpec((1, tn), lambda i: (0, i)), pl.BlockSpec((tn, LANES), lambda i: (i, d // LANES))],
        out_shape=[jax.ShapeDtypeStruct((1, n), jnp.int32), jax.ShapeDtypeStruct(h2x.shape, F32)],
        input_output_aliases={2: 1},
        compiler_params=_cparams(("arbitrary",)),
        name="moe_route",
    )(logits_t, router_bias.reshape(N_EXPERTS, 1).astype(F32), h2x)


def _moe_kernel(tm, dump_base, spare_rows, nu_ref, e0_ref, e1_ref, base_ref, nv_ref, tok_ref, dst_ref, h2_hbm,
                wg0_ref, wu0_ref, wd0_ref, wg1_ref, wu1_ref, wd1_ref, f_hbm,
                xb0, xb1, yb0, yb1, xbf, gsem, ssem, zsem, idle_sem):
    del e0_ref, e1_ref
    t = pl.program_id(0)
    n_used = nu_ref[0]
    d = xbf.shape[1]
    xbuf, ybuf = (xb0, xb1), (yb0, yb1)

    def gather_row(base, r, slot):
        tok = tok_ref[base + r]
        pltpu.make_async_copy(h2_hbm.at[pl.ds(tok, 1)], xbuf[slot].at[pl.ds(r, 1)], gsem.at[slot]).start()

    def scatter_row(base, nv, r, slot):
        row = jnp.where(r < nv, dst_ref[base + r], dump_base + r)
        pltpu.make_async_copy(ybuf[slot].at[pl.ds(r, 1)], f_hbm.at[pl.ds(row, 1)], ssem.at[slot]).start(priority=1)

    def wait_gather(slot):
        pltpu.make_async_copy(h2_hbm.at[pl.ds(0, tm)], xbuf[slot], gsem.at[slot]).wait()

    def wait_scatter(slot):
        pltpu.make_async_copy(ybuf[slot], f_hbm.at[pl.ds(0, tm)], ssem.at[slot]).wait()

    @pl.when(t == 0)
    def _():
        yb1[...] = jnp.zeros(yb1.shape, F32)
        fills = [pltpu.make_async_copy(yb1, f_hbm.at[pl.ds(row, tm)], zsem) for row in spare_rows]
        for fill in fills:
            fill.start()
        for fill in fills:
            fill.wait()
        base0 = base_ref[1]

        def first(r, carry):
            gather_row(base0, r, 0)
            return carry

        lax.fori_loop(0, tm, first, 0)

    def step(cur):
        nxt = 1 - cur
        base_next = base_ref[t + 2]
        base_prev = base_ref[t]
        nv_prev = nv_ref[t]

        @pl.when(t < n_used)
        def _():
            wait_gather(cur)
            xbf[...] = xbuf[cur][:, :d].astype(BF16)
            w = xbuf[cur][:, d:]

            for r in range(tm):
                gather_row(base_next, r, nxt)
            x = xbf[...]
            a0 = (_silu(_dot(x, wg0_ref[0])) * _dot(x, wu0_ref[0]) * w[:, 0:1]).astype(BF16)

            z = pl.semaphore_read(idle_sem)
            zero = lax.shift_right_logical(z.astype(jnp.uint32), jnp.uint32(32)).astype(jnp.int32)
            off = pl.multiple_of(zero * 16, 16)

            for r in range(tm):
                scatter_row(base_prev, nv_prev, r, nxt)
            x = xbf[pl.ds(off, tm), :]
            a1 = (_silu(_dot(x, wg1_ref[0])) * _dot(x, wu1_ref[0]) * w[:, 1:2]).astype(BF16)
            y = _dot(a0, wd0_ref[0]) + _dot(a1, wd1_ref[0])

            @pl.when(t >= 1)
            def _():
                wait_scatter(cur)

            ybuf[cur][...] = y

        @pl.when(t == n_used)
        def _():
            wait_gather(cur)
            wait_scatter(cur)

            def last(r, carry):
                scatter_row(base_prev, nv_prev, r, nxt)
                return carry

            lax.fori_loop(0, tm, last, 0)
            wait_scatter(nxt)

    @pl.when(t % 2 == 0)
    def _():
        step(0)

    @pl.when(t % 2 == 1)
    def _():
        step(1)


def _moe(h2x, cls, dst_of_token, dump_base, spare_rows, out_rows, wg, wu, wd):
    n = h2x.shape[0]
    d = h2x.shape[1] - LANES
    tm = MOE_TILE
    n_tiles = n // tm + N_CLASSES
    cls = cls.reshape(n)
    _, tok_sorted, dst_sorted = lax.sort((cls, jnp.arange(n, dtype=jnp.int32), dst_of_token), num_keys=1)
    tail = jnp.zeros((tm,), jnp.int32)
    tok_sorted = jnp.concatenate([tok_sorted, tail])
    dst_sorted = jnp.concatenate([dst_sorted, tail])
    counts = jnp.sum(cls[:, None] == jnp.arange(N_CLASSES, dtype=jnp.int32)[None, :], axis=0).astype(jnp.int32)
    tiles_per = (counts + tm - 1) // tm
    tile_end = jnp.cumsum(tiles_per)
    tile_start = tile_end - tiles_per
    first_sorted = jnp.cumsum(counts) - counts
    n_used = tile_end[-1]
    tile_id = jnp.arange(n_tiles, dtype=jnp.int32)
    tile_cls = jnp.searchsorted(tile_end, jnp.minimum(tile_id, n_used - 1), side="right").astype(jnp.int32)
    tile_cls = jnp.minimum(tile_cls, N_CLASSES - 1)
    group, pair = tile_cls // N_PAIRS, tile_cls % N_PAIRS
    e0 = group * E_PER_GROUP + jnp.asarray(PAIR_LO, jnp.int32)[pair]
    e1 = group * E_PER_GROUP + jnp.asarray(PAIR_HI, jnp.int32)[pair]
    in_class = (tile_id - tile_start[tile_cls]) * tm
    used = tile_id < n_used
    base = jnp.where(used, first_sorted[tile_cls] + in_class, 0)
    nv = jnp.where(used, jnp.clip(counts[tile_cls] - in_class, 0, tm), 0)
    guard = jnp.zeros((1,), jnp.int32)
    base = jnp.concatenate([guard, base, guard]).astype(jnp.int32)
    nv = jnp.concatenate([guard, nv, guard]).astype(jnp.int32)

    de = wg.shape[2]
    smem = pl.BlockSpec(memory_space=pltpu.SMEM)
    up0 = pl.BlockSpec((1, d, de), lambda i, nu, e0, e1, base, nv: (e0[i], 0, 0))
    up1 = pl.BlockSpec((1, d, de), lambda i, nu, e0, e1, base, nv: (e1[i], 0, 0))
    dn0 = pl.BlockSpec((1, de, d), lambda i, nu, e0, e1, base, nv: (e0[i], 0, 0))
    dn1 = pl.BlockSpec((1, de, d), lambda i, nu, e0, e1, base, nv: (e1[i], 0, 0))
    return pl.pallas_call(
        functools.partial(_moe_kernel, tm, dump_base, spare_rows),
        grid_spec=pltpu.PrefetchScalarGridSpec(
            num_scalar_prefetch=5,
            grid=(n_tiles,),
            in_specs=[
                smem, smem,
                pl.BlockSpec(memory_space=pl.ANY),
                up0, up0, dn0, up1, up1, dn1,
            ],
            out_specs=pl.BlockSpec(memory_space=pl.ANY),
            scratch_shapes=[
                pltpu.VMEM((tm, d + LANES), F32), pltpu.VMEM((tm, d + LANES), F32),
                pltpu.VMEM((tm, d), F32), pltpu.VMEM((tm, d), F32),
                pltpu.VMEM((tm, d), BF16),
                pltpu.SemaphoreType.DMA((2,)), pltpu.SemaphoreType.DMA((2,)), pltpu.SemaphoreType.DMA(()),
                pltpu.SemaphoreType.REGULAR(()),
            ],
        ),
        out_shape=jax.ShapeDtypeStruct((out_rows, d), F32),
        compiler_params=_cparams(("arbitrary",)),
        name="moe_experts",
    )(n_used.reshape(1).astype(jnp.int32), e0, e1, base, nv, tok_sorted, dst_sorted, h2x,
      wg, wu, wd, wg, wu, wd)


def _hproj_kernel(n_lat_tiles, rows, sl_ref, fl_ref, sc_ref_, fc_ref, g2_ref, sc_ref, sh_ref, ng_ref, lb_ref,
                  wq_ref, wzf_ref, wzb_ref, wi_ref, wg_ref,
                  s2_ref, q_ref, kf_ref, kb_ref, lff_ref, lfb_ref, i_ref, g_ref, xt):
    j = pl.program_id(1)
    d = xt.shape[1]
    g2 = g2_ref[0]

    @pl.when(j < n_lat_tiles)
    def _():
        for c in range(COL_TILE):
            s = sl_ref[0, :, c, :] + g2 * fl_ref[0, :, c, :]
            s2_ref[0, :, c, :] = s
            xt[c * rows:(c + 1) * rows, :] = s

    @pl.when(j >= n_lat_tiles)
    def _():
        xt[...] = sc_ref_[0] + g2 * fc_ref[0]

    h = (_rms(xt[...], ng_ref[...]) * (1.0 + sc_ref[0]) + sh_ref[0]).astype(BF16)
    q_ref[0] = _silu(_dot(h, wq_ref[...])).astype(BF16)
    i_ref[0] = _dot(h, wi_ref[...]).astype(BF16)
    g_ref[0] = _dot(h, wg_ref[...]).astype(BF16)
    for dd, (wz_ref, k_ref, lf_ref) in enumerate(((wzf_ref, kf_ref, lff_ref), (wzb_ref, kb_ref, lfb_ref))):
        z = _dot(h, wz_ref[...])
        log_lb = lb_ref[2 * dd:2 * dd + 1, :]
        log_1mlb = lb_ref[2 * dd + 1:2 * dd + 2, :]
        ls = _log_sigmoid(z)
        a = log_1mlb + ls
        lf_ref[0] = jnp.maximum(log_lb, a) + jnp.log1p(jnp.exp(-jnp.abs(log_lb - a)))
        k_ref[0] = jnp.exp(a - z).astype(BF16)


def _hproj(s1, f0, modv, rows_mod, norm_g, lbtab, ws, ll, lc):
    bsz, ltot, d = s1.shape
    rows = ll // GRID_W
    tm = COL_TILE * rows
    nl, nc = GRID_W // COL_TILE, lc // tm
    s_cm = s1.reshape(bsz, ltot // GRID_W, GRID_W, d)
    f_tok = f0.reshape(bsz, -1, d)
    f_cm = f0.reshape(bsz, -1, GRID_W, d)
    cm = pl.BlockSpec((1, rows, COL_TILE, d), lambda b, j: (b, 0, jnp.minimum(j, nl - 1), 0))
    ctx = pl.BlockSpec((1, tm, d), lambda b, j: (b, ll // tm + jnp.maximum(j - nl, 0), 0))
    wspec = pl.BlockSpec((d, d), lambda b, j: (0, 0))
    tok = pl.BlockSpec((1, tm, d), lambda b, j: (b, j, 0))

    def g2_map(b, j):
        row = jnp.where(j < nl, b, rows_mod - 1)
        return ((0 * rows_mod + row) * N_ADA + 5, 0, 0)

    outs = pl.pallas_call(
        functools.partial(_hproj_kernel, nl, rows),
        grid=(bsz, nl + nc),
        in_specs=[
            cm, cm, ctx, ctx,
            pl.BlockSpec((1, 1, d), g2_map),
            _mod_spec(d, 1, 1, nl, rows_mod), _mod_spec(d, 1, 0, nl, rows_mod),
            pl.BlockSpec((1, d), lambda b, j: (0, 0)),
            pl.BlockSpec((4, d), lambda b, j: (0, 0)),
            wspec, wspec, wspec, wspec, wspec,
        ],
        out_specs=[cm, tok, tok, tok, tok, tok, tok, tok],
        out_shape=[jax.ShapeDtypeStruct((bsz, rows, GRID_W, d), F32)]
        + [jax.ShapeDtypeStruct((bsz, ltot, d), dt) for dt in (BF16, BF16, BF16, F32, F32, BF16, BF16)],
        scratch_shapes=[pltpu.VMEM((tm, d), F32)],
        compiler_params=_cparams(("arbitrary", "arbitrary")),
        name="hgrn_in_proj",
    )(s_cm, f_cm, s1, f_tok, modv, modv, modv, norm_g, lbtab, *ws)
    return outs


def _hgrn_kernel(t, ll, lc, q_ref, kf_ref, kb_ref, lff_ref, lfb_ref, i_ref, g_ref, hg_ref, out_ref,
                 of, ob, st_s):
    st_s[...] = jnp.zeros(st_s.shape, F32)
    row = lax.broadcasted_iota(jnp.int32, (t, 1), 0)
    ti = lax.broadcasted_iota(jnp.int32, (t, t), 0)
    si = lax.broadcasted_iota(jnp.int32, (t, t), 1)
    xor = ti ^ si
    levels = []
    m = 2
    while m <= t:
        levels.append(m)
        m *= 2

    def cumsum(lf, d):
        b = lf
        sh = 1
        while sh < t:
            if d == 0:
                b = b + jnp.where(row >= sh, pltpu.roll(b, sh, 0), 0.0)
            else:
                b = b + jnp.where(row < t - sh, pltpu.roll(b, t - sh, 0), 0.0)
            sh *= 2
        return b

    def block_ref(b, m, d):
        half = m // 2
        pos = half - 1 if d == 0 else half
        if m >= 8:
            dk = b.shape[1]
            b3 = b.reshape(t // m, m, dk)
            return jnp.broadcast_to(b3[:, pos:pos + 1, :], b3.shape).reshape(t, dk)
        r = row % m
        out = b
        for res in range(m):
            if res == pos:
                continue
            out = jnp.where(r == res, pltpu.roll(b, (res - pos) % t, 0), out)
        return out

    def chunk(start, d, need_out, o_out):
        q = q_ref[0, pl.ds(start, t), :]
        k = (kf_ref if d == 0 else kb_ref)[0, pl.ds(start, t), :].astype(F32)
        lf = (lff_ref if d == 0 else lfb_ref)[0, pl.ds(start, t), :]
        iv = i_ref[0, pl.ds(start, t), :]
        b = cumsum(lf, d)
        b_tot = b[t - 1:t, :] if d == 0 else b[0:1, :]
        st = st_s[d]
        if need_out:
            qf = q.astype(F32)
            a = _dot_nt(q, k.astype(BF16))
            for m in levels:
                e = jnp.exp(-jnp.abs(b - block_ref(b, m, d)))
                p = _dot_nt((qf * e).astype(BF16), (k * e).astype(BF16))
                a = jnp.where(xor < m // 2, a, p) if m > 2 else jnp.where(xor == 0, a, p)
            feeds = (si <= ti) if d == 0 else (si >= ti)
            a = jnp.where(feeds, a, 0.0)
            o = _dot(a.astype(BF16), iv) + _dot_nt((qf * jnp.exp(b)).astype(BF16), st.astype(BF16))
            o_out[pl.ds(start, t), :] = o
        kd = (k * jnp.exp(b_tot - b)).astype(BF16)
        st_s[d] = jnp.exp(b_tot) * st + _dot_tn(iv, kd)

    ncc, ncl = lc // t, ll // t
    for i in range(ncc):
        chunk(ll + i * t, 0, False, None)
        chunk(ll + (ncc - 1 - i) * t, 1, False, None)

    def body(i, carry):
        chunk(pl.multiple_of(i * t, t), 0, True, of)
        chunk(pl.multiple_of((ncl - 1 - i) * t, t), 1, True, ob)
        return carry

    lax.fori_loop(0, ncl, body, 0)

    def epilogue(i, carry):
        s = pl.multiple_of(i * t, t)
        y = _rms(of[pl.ds(s, t), :] + ob[pl.ds(s, t), :], hg_ref[...])
        out_ref[0, pl.ds(s, t), :] = (y * _silu(g_ref[0, pl.ds(s, t), :].astype(F32))).astype(BF16)
        return carry

    lax.fori_loop(0, ncl, epilogue, 0)


def _hgrn_scan(q, kf, kb, lff, lfb, iv, g, head_g, ll, lc):
    bsz, ltot, hk = q.shape
    nh = H_HEADS
    dk = hk // nh
    t = SCAN_CHUNK
    full = pl.BlockSpec((1, ltot, dk), lambda b, h: (b, 0, h))
    lat = pl.BlockSpec((1, ll, dk), lambda b, h: (b, 0, h))
    return pl.pallas_call(
        functools.partial(_hgrn_kernel, t, ll, lc),
        grid=(bsz, nh),
        in_specs=[full, full, full, full, full, full, lat, pl.BlockSpec((1, dk), lambda b, h: (0, h))],
        out_specs=lat,
        out_shape=jax.ShapeDtypeStruct((bsz, ll, hk), BF16),
        scratch_shapes=[pltpu.VMEM((ll, dk), F32), pltpu.VMEM((ll, dk), F32), pltpu.VMEM((2, dk, dk), F32)],
        compiler_params=_cparams(("arbitrary", "arbitrary")),
        name="hgrn_scan",
    )(q, kf, kb, lff, lfb, iv, g, head_g.reshape(1, -1))


def _out1_kernel(rows, hg_ref, s_ref, g1_ref, sc_ref, sh_ref, ng_ref, wo_ref, rhi_ref, rlo_ref,
                 s3_ref, h2_ref, lg_ref, st):
    d = st.shape[1]
    y = g1_ref[0] * _dot(hg_ref[0], wo_ref[...])
    for c in range(COL_TILE):
        s = s_ref[0, :, c, :] + y[c * rows:(c + 1) * rows, :]
        s3_ref[0, :, c, :] = s
        st[c * rows:(c + 1) * rows, :] = s
    h2 = _rms(st[...], ng_ref[...]) * (1.0 + sc_ref[0]) + sh_ref[0]
    h2_ref[:, :h2.shape[1]] = h2
    h2_ref[:, h2.shape[1]:] = jnp.zeros((h2.shape[0], LANES), F32)
    lg_ref[...] = _router_logits_t(h2, rhi_ref, rlo_ref)


def _out1(hg, s2_cm, modv, rows_mod, norm_g, w_out, r_hi, r_lo):
    bsz, ll, hv = hg.shape
    rows = s2_cm.shape[1]
    d = s2_cm.shape[3]
    tm = COL_TILE * rows
    nl = GRID_W // COL_TILE
    cm = pl.BlockSpec((1, rows, COL_TILE, d), lambda b, j: (b, 0, j, 0))
    return pl.pallas_call(
        functools.partial(_out1_kernel, rows),
        grid=(bsz, nl),
        in_specs=[
            pl.BlockSpec((1, tm, hv), lambda b, j: (b, j, 0)),
            cm,
            _mod_spec(d, 1, 2, nl, rows_mod), _mod_spec(d, 1, 4, nl, rows_mod), _mod_spec(d, 1, 3, nl, rows_mod),
            pl.BlockSpec((1, d), lambda b, j: (0, 0)),
            pl.BlockSpec(w_out.shape, lambda b, j: (0, 0)),
            pl.BlockSpec(r_hi.shape, lambda b, j: (0, 0)),
            pl.BlockSpec(r_lo.shape, lambda b, j: (0, 0)),
        ],
        out_specs=[
            cm,
            pl.BlockSpec((tm, d + LANES), lambda b, j: (b * nl + j, 0)),
            pl.BlockSpec((N_EXPERTS, tm), lambda b, j: (0, b * nl + j)),
        ],
        out_shape=[
            jax.ShapeDtypeStruct(s2_cm.shape, F32),
            jax.ShapeDtypeStruct((bsz * ll, d + LANES), F32),
            jax.ShapeDtypeStruct((N_EXPERTS, bsz * ll), F32),
        ],
        scratch_shapes=[pltpu.VMEM((tm, d), F32)],
        compiler_params=_cparams(("arbitrary", "arbitrary")),
        name="hgrn_out_proj",
    )(hg, s2_cm, modv, modv, modv, norm_g, w_out, r_hi, r_lo)


def _final_kernel(s_ref, f_ref, g2_ref, fg_ref, o_ref):
    o_ref[0] = _rms(s_ref[0] + g2_ref[0] * f_ref[0], fg_ref[...])


def _final(s3, f1, modv, rows_mod, final_g, tm):
    bsz, ll, d = s3.shape
    nl = ll // tm
    tok = pl.BlockSpec((1, tm, d), lambda b, j: (b, j, 0))
    return pl.pallas_call(
        _final_kernel,
        grid=(bsz, nl),
        in_specs=[tok, tok, _mod_spec(d, 1, 5, nl, rows_mod), pl.BlockSpec((1, d), lambda b, j: (0, 0))],
        out_specs=tok,
        out_shape=jax.ShapeDtypeStruct((bsz, ll, d), F32),
        compiler_params=_cparams(("arbitrary", "arbitrary")),
        name="final_norm",
    )(s3, f1.reshape(bsz, -1, d), modv, final_g)


def kernel(x, c, ctx, c_ctx, ada_w, ada_b, norm_mix_g, norm_ffn_g, final_g, m_w_in, m_conv_w, m_conv_b, m_gate_b,
           m_head_g, m_w_out, h_w_in, h_lower_bounds, h_head_g, h_w_out, router_w, router_bias, e_w_gate,
           e_w_up, e_w_down):
    bsz, ll, d = x.shape
    lc = ctx.shape[1]
    ltot = ll + lc
    depth = ada_w.shape[0]
    assert depth == 2 and ll % GRID_W == 0 and lc % SCAN_CHUNK == 0 and ll % SCAN_CHUNK == 0
    rows = ll // GRID_W
    tm = COL_TILE * rows
    assert lc % tm == 0 and ltot % GRID_W == 0 and (bsz * ltot) % MOE_TILE == 0 and (bsz * ll) % MOE_TILE == 0

    rows_mod = 8 * ((bsz + 1 + 7) // 8)
    cc = jnp.zeros((rows_mod, d), F32).at[:bsz].set(c).at[rows_mod - 1].set(c_ctx)
    modv = _ada(cc, ada_w, ada_b).reshape(depth * rows_mod * N_ADA, 1, d)

    r_pad = jnp.zeros((d, LANES), F32).at[:, :N_EXPERTS].set(router_w)
    r_hi = r_pad.astype(BF16)
    r_lo = (r_pad - r_hi.astype(F32)).astype(BF16)
    wg_all, wu_all, wd_all = e_w_gate.astype(BF16), e_w_up.astype(BF16), e_w_down.astype(BF16)

    m_qk, m_v = m_conv_w.shape[2], m_head_g.shape[1]
    w_in = m_w_in[0].astype(BF16)
    w_gates = jnp.zeros((d, LANES), BF16).at[:, :4 * M_HEADS].set(w_in[:, m_qk + 2 * m_v:])
    qk, v, o, gates = _mproj(x, ctx, modv, rows_mod, norm_mix_g[0:1], w_in[:, :m_qk], w_in[:, m_qk:m_qk + m_v],
                             w_in[:, m_qk + m_v:m_qk + 2 * m_v], w_gates, tm)
    hg = _mlstm_scan(qk, v, o, gates, m_conv_w[0], m_conv_b[0], m_gate_b[0], m_head_g[0], ll, lc)
    s1, h2, lg = _out0(hg, x, ctx, modv, rows_mod, norm_ffn_g[0:1], m_w_out[0].astype(BF16), r_hi, r_lo, tm)
    cls, h2 = _route(lg, router_bias, h2)
    tok0 = jnp.arange(bsz * ltot, dtype=jnp.int32)
    dst0 = (tok0 // ltot) * (ltot + MOE_TILE) + tok0 % ltot
    spare0 = tuple(b * (ltot + MOE_TILE) + ltot for b in range(1, bsz))
    f0 = _moe(h2, cls, dst0, ltot, spare0, bsz * (ltot + MOE_TILE), wg_all[0], wu_all[0], wd_all[0])

    lbs = jnp.cumsum(jax.nn.softmax(h_lower_bounds.astype(F32), axis=0), axis=0)
    lb = (lbs - lbs[0])[1].reshape(2, -1)
    lbtab = jnp.stack([jnp.log(lb[0]), jnp.log1p(-lb[0]), jnp.log(lb[1]), jnp.log1p(-lb[1])])
    hw = h_w_in[0].astype(BF16)
    hk = lb.shape[1]
    ws = (hw[:, :hk], hw[:, hk:2 * hk], hw[:, 2 * hk:3 * hk], hw[:, 3 * hk:3 * hk + d], hw[:, 3 * hk + d:])
    s2_cm, q, kf, kb, lff, lfb, iv, g = _hproj(s1, f0, modv, rows_mod, norm_mix_g[1:2], lbtab, ws, ll, lc)
    hg1 = _hgrn_scan(q, kf, kb, lff, lfb, iv, g, h_head_g[0], ll, lc)
    s3_cm, h2b, lgb = _out1(hg1, s2_cm, modv, rows_mod, norm_ffn_g[1:2], h_w_out[0].astype(BF16), r_hi, r_lo)
    clsb, h2b = _route(lgb, router_bias, h2b)
    n1 = bsz * ll
    tokid = jnp.arange(n1, dtype=jnp.int32)
    within = tokid % ll
    raster = (tokid // ll) * (ll + MOE_TILE) + (within % rows) * GRID_W + within // rows
    spare1 = tuple(b * (ll + MOE_TILE) + ll for b in range(1, bsz))
    f1 = _moe(h2b, clsb, raster, ll, spare1, bsz * (ll + MOE_TILE), wg_all[1], wu_all[1], wd_all[1])

    return _final(s3_cm.reshape(bsz, ll, d), f1, modv, rows_mod, final_g.reshape(1, d), tm)
```

```python
import functools

import numpy as np
import jax
import jax.numpy as jnp
from jax import lax
from jax.experimental import pallas as pl
from jax.experimental.pallas import tpu as pltpu

F32 = jnp.float32
BF16 = jnp.bfloat16

EPS = 1e-6
LOG2_E = 1.4426950408889634
N_ADA = 6
GRID_W = 64
M_HEADS = 4
H_HEADS = 8
N_EXPERTS = 16
N_GROUPS = 4
E_PER_GROUP = N_EXPERTS // N_GROUPS
N_PAIRS = 6
N_CLASSES = N_GROUPS * N_PAIRS
PAIR_LO = (0, 0, 0, 1, 1, 2)
PAIR_HI = (1, 2, 3, 2, 3, 3)

LANES = 128
COL_TILE = 8
SCAN_CHUNK = 128
MOE_TILE = 256
VMEM_LIMIT = 56 * 1024 * 1024


def _cparams(sem):
    return pltpu.CompilerParams(dimension_semantics=sem, vmem_limit_bytes=VMEM_LIMIT)


def _dot(a, b):
    return jnp.dot(a, b, preferred_element_type=F32)


def _dot_nt(a, b):
    return lax.dot_general(a, b, (((1,), (1,)), ((), ())), preferred_element_type=F32)


def _dot_tn(a, b):
    return lax.dot_general(a, b, (((0,), (0,)), ((), ())), preferred_element_type=F32)


def _split_bf16(a):
    hi = a.astype(BF16)
    lo = (a - hi.astype(F32)).astype(BF16)
    return hi, lo


def _sigmoid(x):
    return 1.0 / (1.0 + jnp.exp(-x))


def _silu(x):
    return x * _sigmoid(x)


def _log_sigmoid(x):
    return jnp.minimum(x, 0.0) - jnp.log1p(jnp.exp(-jnp.abs(x)))


def _rms(x, g):
    return x * lax.rsqrt(jnp.mean(x * x, axis=-1, keepdims=True) + EPS) * g


def _ada_kernel(c_ref, w_ref, b_ref, o_ref):
    a = _silu(c_ref[...])
    a_hi, a_lo = _split_bf16(a)
    w_hi, w_lo = _split_bf16(w_ref[0])
    acc = _dot(a_hi, w_hi) + _dot(a_lo, w_hi) + _dot(a_hi, w_lo)
    o_ref[0] = acc + b_ref[0]


def _ada(cc, ada_w, ada_b):
    depth, d, n = ada_w.shape
    tn = 1024
    rows = cc.shape[0]
    return pl.pallas_call(
        _ada_kernel,
        grid=(depth, n // tn),
        in_specs=[
            pl.BlockSpec((rows, d), lambda l, j: (0, 0)),
            pl.BlockSpec((1, d, tn), lambda l, j: (l, 0, j)),
            pl.BlockSpec((1, 1, tn), lambda l, j: (l, 0, j)),
        ],
        out_specs=pl.BlockSpec((1, rows, tn), lambda l, j: (l, 0, j)),
        out_shape=jax.ShapeDtypeStruct((depth, rows, n), F32),
        compiler_params=_cparams(("arbitrary", "arbitrary")),
        name="ada_mod",
    )(cc, ada_w, ada_b.reshape(depth, 1, n))


def _mproj_kernel(n_lat_tiles, xl_ref, xc_ref, sc_ref, sh_ref, g_ref, wqk_ref, wv_ref, wo_ref, wg_ref,
                  qk_ref, v_ref, o_ref, gt_ref):
    j = pl.program_id(1)
    x = jnp.where(j < n_lat_tiles, xl_ref[0], xc_ref[0])
    h = (_rms(x, g_ref[...]) * (1.0 + sc_ref[0]) + sh_ref[0]).astype(BF16)
    qk_ref[0] = _dot(h, wqk_ref[...]).astype(BF16)
    v_ref[0] = _dot(h, wv_ref[...]).astype(BF16)
    o_ref[0] = _dot(h, wo_ref[...]).astype(BF16)
    gt_ref[0] = _dot(h, wg_ref[...])[:, :4 * M_HEADS]


def _mod_spec(d, layer, k, n_lat_tiles, rows):
    def imap(b, j):
        row = jnp.where(j < n_lat_tiles, b, rows - 1)
        return ((layer * rows + row) * N_ADA + k, 0, 0)
    return pl.BlockSpec((1, 1, d), imap)


def _mproj(x, ctx, modv, rows, norm_g, w_qk, w_v, w_o, w_g, tm):
    bsz, ll, d = x.shape
    lc = ctx.shape[1]
    nl, nc = ll // tm, lc // tm
    ltot = ll + lc
    wspec = lambda n: pl.BlockSpec((d, n), lambda b, j: (0, 0))
    tok = lambda n: pl.BlockSpec((1, tm, n), lambda b, j: (b, j, 0))
    return pl.pallas_call(
        functools.partial(_mproj_kernel, nl),
        grid=(bsz, nl + nc),
        in_specs=[
            pl.BlockSpec((1, tm, d), lambda b, j: (b, jnp.minimum(j, nl - 1), 0)),
            pl.BlockSpec((1, tm, d), lambda b, j: (b, jnp.maximum(j - nl, 0), 0)),
            _mod_spec(d, 0, 1, nl, rows), _mod_spec(d, 0, 0, nl, rows),
            pl.BlockSpec((1, d), lambda b, j: (0, 0)),
            wspec(w_qk.shape[1]), wspec(w_v.shape[1]), wspec(w_o.shape[1]), wspec(w_g.shape[1]),
        ],
        out_specs=[tok(w_qk.shape[1]), tok(w_v.shape[1]), tok(w_o.shape[1]), tok(4 * M_HEADS)],
        out_shape=[
            jax.ShapeDtypeStruct((bsz, ltot, w_qk.shape[1]), BF16),
            jax.ShapeDtypeStruct((bsz, ltot, w_v.shape[1]), BF16),
            jax.ShapeDtypeStruct((bsz, ltot, w_o.shape[1]), BF16),
            jax.ShapeDtypeStruct((bsz, ltot, 4 * M_HEADS), F32),
        ],
        compiler_params=_cparams(("arbitrary", "arbitrary")),
        name="mlstm_in_proj",
    )(x, ctx, modv, modv, norm_g, w_qk, w_v, w_o, w_g)


def _mlstm_kernel(t, ll, lc, q_ref, k_ref, v_ref, o_ref, gc_ref, gr_ref, cwq_ref, cwk_ref, cbq_ref, cbk_ref,
                  gbc_ref, gbr_ref, hg_ref, out_ref, qs, ks, hf, hb, c_s, n_s, m_s):
    ltot = ll + lc
    dk = q_ref.shape[2]
    row = lax.broadcasted_iota(jnp.int32, (ltot, 1), 0)
    first = (row == 0) | (row == ll)
    last = (row == ll - 1) | (row == ltot - 1)

    def conv(x_ref, w_ref, b_ref):
        x = x_ref[0].astype(F32)
        w = w_ref[...]
        xp = jnp.where(first, 0.0, pltpu.roll(x, 1, 0))
        xn = jnp.where(last, 0.0, pltpu.roll(x, ltot - 1, 0))
        return _silu(xp * w[0:1] + x * w[1:2] + xn * w[2:3] + b_ref[...])

    qs[...] = (conv(q_ref, cwq_ref, cbq_ref) * (dk ** -0.5)).astype(BF16)
    ks[...] = conv(k_ref, cwk_ref, cbk_ref).astype(BF16)

    c_s[...] = jnp.zeros(c_s.shape, F32)
    n_s[...] = jnp.zeros(n_s.shape, F32)
    m_s[...] = jnp.zeros(m_s.shape, F32)

    ti = lax.broadcasted_iota(jnp.int32, (t, t), 0)
    si = lax.broadcasted_iota(jnp.int32, (t, t), 1)

    def chunk(start, d, h_out):
        q = qs[pl.ds(start, t), :]
        k = ks[pl.ds(start, t), :]
        v = v_ref[0, pl.ds(start, t), :]
        gc = gc_ref[0, 0, pl.ds(start, t), :] + gbc_ref[0]
        gr = gr_ref[0, 0, :, pl.ds(start, t)] + gbr_ref[0]
        ig_c, lf_c = gc[:, 2 * d:2 * d + 1], _log_sigmoid(gc[:, 2 * d + 1:2 * d + 2])
        ig_r, lf_r = gr[2 * d:2 * d + 1, :], _log_sigmoid(gr[2 * d + 1:2 * d + 2, :])
        seen = (si <= ti) if d == 0 else (si >= ti)
        seen_t = (ti <= si) if d == 0 else (ti >= si)
        b_c = jnp.sum(jnp.where(seen, lf_r, 0.0), axis=1, keepdims=True)
        b_r = jnp.sum(jnp.where(seen_t, lf_c, 0.0), axis=0, keepdims=True)
        total = jnp.sum(lf_r, axis=1, keepdims=True)
        c_mem = c_s[d]
        n_mem = n_s[d]
        m_prev = m_s[d][0:1, 0:1]

        log_d = jnp.where(seen, b_c - b_r + ig_r, -jnp.inf)
        log_inter = b_c + m_prev
        m_t = jnp.maximum(log_inter, jnp.max(log_d, axis=1, keepdims=True))
        w_ts = jnp.exp(log_d - m_t) * _dot_nt(q, k)
        inter = jnp.exp(log_inter - m_t)
        num = _dot(w_ts.astype(BF16), v) + inter * _dot_nt(q, c_mem.astype(BF16))
        den = jnp.sum(w_ts, axis=1, keepdims=True) + inter * jnp.sum(q.astype(F32) * n_mem, axis=1, keepdims=True)
        h_out[pl.ds(start, t), :] = num / jnp.maximum(jnp.abs(den), jnp.exp(-m_t))

        g_c = total - b_c + ig_c
        g_r = total - b_r + ig_r
        m_new = jnp.maximum(total + m_prev, jnp.max(g_r, axis=1, keepdims=True))
        w_c = jnp.exp(g_c - m_new)
        decay = jnp.exp(total + m_prev - m_new)
        wv = (w_c * v.astype(F32)).astype(BF16)
        c_s[d] = decay * c_mem + _dot_tn(wv, k)
        n_s[d] = decay * n_mem + jnp.sum(w_c * k.astype(F32), axis=0, keepdims=True)
        m_s[d] = jnp.broadcast_to(m_new, m_s.shape[1:])

    ncc, ncl = lc // t, ll // t
    for i in range(ncc):
        chunk(ll + i * t, 0, hf)
        chunk(ll + (ncc - 1 - i) * t, 1, hb)

    def body(i, carry):
        chunk(pl.multiple_of(i * t, t), 0, hf)
        chunk(pl.multiple_of((ncl - 1 - i) * t, t), 1, hb)
        return carry

    lax.fori_loop(0, ncl, body, 0)

    def epilogue(i, carry):
        s = pl.multiple_of(i * t, t)
        hs = hf[pl.ds(s, t), :] + hb[pl.ds(s, t), :]
        y = _rms(hs, hg_ref[...])
        out_ref[0, pl.ds(s, t), :] = (y * _sigmoid(o_ref[0, pl.ds(s, t), :].astype(F32))).astype(BF16)
        return carry

    lax.fori_loop(0, ltot // t, epilogue, 0)


def _mlstm_scan(qk, v, o, gates, conv_w, conv_b, gate_b, head_g, ll, lc):
    bsz, ltot, _ = qk.shape
    nh = M_HEADS
    dk = qk.shape[2] // (2 * nh)
    dv = v.shape[2] // nh
    t = SCAN_CHUNK
    g4 = gates.reshape(bsz, ltot, 4, nh).transpose(0, 3, 1, 2)
    g4t = g4.transpose(0, 1, 3, 2)
    gb = gate_b.reshape(4, nh).T
    return pl.pallas_call(
        functools.partial(_mlstm_kernel, t, ll, lc),
        grid=(bsz, nh),
        in_specs=[
            pl.BlockSpec((1, ltot, dk), lambda b, h: (b, 0, h)),
            pl.BlockSpec((1, ltot, dk), lambda b, h: (b, 0, nh + h)),
            pl.BlockSpec((1, ltot, dv), lambda b, h: (b, 0, h)),
            pl.BlockSpec((1, ltot, dv), lambda b, h: (b, 0, h)),
            pl.BlockSpec((1, 1, ltot, 4), lambda b, h: (b, h, 0, 0)),
            pl.BlockSpec((1, 1, 4, ltot), lambda b, h: (b, h, 0, 0)),
            pl.BlockSpec((3, dk), lambda b, h: (0, h)),
            pl.BlockSpec((3, dk), lambda b, h: (0, nh + h)),
            pl.BlockSpec((1, dk), lambda b, h: (0, h)),
            pl.BlockSpec((1, dk), lambda b, h: (0, nh + h)),
            pl.BlockSpec((1, 1, 4), lambda b, h: (h, 0, 0)),
            pl.BlockSpec((1, 4, 1), lambda b, h: (h, 0, 0)),
            pl.BlockSpec((1, dv), lambda b, h: (0, h)),
        ],
        out_specs=pl.BlockSpec((1, ltot, dv), lambda b, h: (b, 0, h)),
        out_shape=jax.ShapeDtypeStruct((bsz, ltot, nh * dv), BF16),
        scratch_shapes=[
            pltpu.VMEM((ltot, dk), BF16), pltpu.VMEM((ltot, dk), BF16),
            pltpu.VMEM((ltot, dv), F32), pltpu.VMEM((ltot, dv), F32),
            pltpu.VMEM((2, dv, dk), F32), pltpu.VMEM((2, 1, dk), F32), pltpu.VMEM((2, 8, LANES), F32),
        ],
        compiler_params=_cparams(("arbitrary", "arbitrary")),
        name="mlstm_scan",
    )(qk, qk, v, o, g4, g4t, conv_w, conv_w, conv_b.reshape(1, -1), conv_b.reshape(1, -1),
      gb.reshape(nh, 1, 4), gb.reshape(nh, 4, 1), head_g.reshape(1, -1))


def _router_logits_t(h2, rhi_ref, rlo_ref):
    h_hi, h_lo = _split_bf16(h2)
    lg = _dot(h_hi, rhi_ref[...]) + _dot(h_lo, rhi_ref[...]) + _dot(h_hi, rlo_ref[...])
    return lg.T[:N_EXPERTS, :]


def _out0_kernel(n_lat_tiles, hg_ref, xl_ref, xc_ref, g1_ref, sc_ref, sh_ref, ng_ref, wo_ref, rhi_ref, rlo_ref,
                 s_ref, h2_ref, lg_ref):
    j = pl.program_id(1)
    x = jnp.where(j < n_lat_tiles, xl_ref[0], xc_ref[0])
    s = x + g1_ref[0] * _dot(hg_ref[0], wo_ref[...])
    s_ref[0] = s
    h2 = _rms(s, ng_ref[...]) * (1.0 + sc_ref[0]) + sh_ref[0]
    h2_ref[:, :h2.shape[1]] = h2
    h2_ref[:, h2.shape[1]:] = jnp.zeros((h2.shape[0], LANES), F32)
    lg_ref[...] = _router_logits_t(h2, rhi_ref, rlo_ref)


def _out0(hg, x, ctx, modv, rows, norm_g, w_out, r_hi, r_lo, tm):
    bsz, ll, d = x.shape
    lc = ctx.shape[1]
    nl, nc = ll // tm, lc // tm
    nt = nl + nc
    ltot = ll + lc
    return pl.pallas_call(
        functools.partial(_out0_kernel, nl),
        grid=(bsz, nt),
        in_specs=[
            pl.BlockSpec((1, tm, hg.shape[2]), lambda b, j: (b, j, 0)),
            pl.BlockSpec((1, tm, d), lambda b, j: (b, jnp.minimum(j, nl - 1), 0)),
            pl.BlockSpec((1, tm, d), lambda b, j: (b, jnp.maximum(j - nl, 0), 0)),
            _mod_spec(d, 0, 2, nl, rows), _mod_spec(d, 0, 4, nl, rows), _mod_spec(d, 0, 3, nl, rows),
            pl.BlockSpec((1, d), lambda b, j: (0, 0)),
            pl.BlockSpec(w_out.shape, lambda b, j: (0, 0)),
            pl.BlockSpec(r_hi.shape, lambda b, j: (0, 0)),
            pl.BlockSpec(r_lo.shape, lambda b, j: (0, 0)),
        ],
        out_specs=[
            pl.BlockSpec((1, tm, d), lambda b, j: (b, j, 0)),
            pl.BlockSpec((tm, d + LANES), lambda b, j: (b * nt + j, 0)),
            pl.BlockSpec((N_EXPERTS, tm), lambda b, j: (0, b * nt + j)),
        ],
        out_shape=[
            jax.ShapeDtypeStruct((bsz, ltot, d), F32),
            jax.ShapeDtypeStruct((bsz * ltot, d + LANES), F32),
            jax.ShapeDtypeStruct((N_EXPERTS, bsz * ltot), F32),
        ],
        compiler_params=_cparams(("arbitrary", "arbitrary")),
        name="mlstm_out_proj",
    )(hg, x, ctx, modv, modv, modv, norm_g, w_out, r_hi, r_lo)


def _route_kernel(lg_ref, bias_ref, h2x_hbm, cls_ref, w_ref):
    del h2x_hbm
    s = _sigmoid(lg_ref[...])
    sel = s + bias_ref[...]
    srow = [s[e:e + 1, :] for e in range(N_EXPERTS)]
    row = [sel[e:e + 1, :] for e in range(N_EXPERTS)]
    best = jnp.zeros(row[0].shape, jnp.int32)
    best_score = None
    for g in range(N_GROUPS):
        r = row[g * E_PER_GROUP:(g + 1) * E_PER_GROUP]
        score = None
        for lo, hi in zip(PAIR_LO, PAIR_HI):
            pair = r[lo] + r[hi]
            score = pair if score is None else jnp.maximum(score, pair)
        if g == 0:
            best_score = score
        else:
            better = score > best_score
            best = jnp.where(better, g, best)
            best_score = jnp.where(better, score, best_score)
    gs = [row[i] for i in range(E_PER_GROUP)]
    gw = [srow[i] for i in range(E_PER_GROUP)]
    for g in range(1, N_GROUPS):
        hit = best == g
        gs = [jnp.where(hit, row[g * E_PER_GROUP + i], gs[i]) for i in range(E_PER_GROUP)]
        gw = [jnp.where(hit, srow[g * E_PER_GROUP + i], gw[i]) for i in range(E_PER_GROUP)]
    keep = []
    for i in range(E_PER_GROUP):
        beaten = jnp.zeros(best.shape, jnp.int32)
        for j in range(E_PER_GROUP):
            if j == i:
                continue
            wins = (gs[j] > gs[i]) | ((gs[j] == gs[i]) & (j < i))
            beaten = beaten + wins.astype(jnp.int32)
        keep.append(beaten < 2)
    pair_id = jnp.zeros(best.shape, jnp.int32)
    w_lo = jnp.zeros(best.shape, F32)
    w_hi = jnp.zeros(best.shape, F32)
    for p, (lo, hi) in enumerate(zip(PAIR_LO, PAIR_HI)):
        hit = keep[lo] & keep[hi]
        pair_id = jnp.where(hit, p, pair_id)
        w_lo = jnp.where(hit, gw[lo], w_lo)
        w_hi = jnp.where(hit, gw[hi], w_hi)
    tot = w_lo + w_hi
    cls_ref[...] = best * N_PAIRS + pair_id
    lanes_t = jnp.concatenate([w_lo / tot, w_hi / tot, jnp.zeros((LANES - 2, w_lo.shape[1]), F32)], axis=0)
    w_ref[...] = lanes_t.T


def _route(logits_t, router_bias, h2x):
    n = logits_t.shape[1]
    d = h2x.shape[1] - LANES
    tn = next(cand for cand in (2048, 1024, 512, 256, 128) if n % cand == 0)
    return pl.pallas_call(
        _route_kernel,
        grid=(n // tn,),
        in_specs=[
            pl.BlockSpec((N_EXPERTS, tn), lambda i: (0, i)),
            pl.BlockSpec((N_EXPERTS, 1), lambda i: (0, 0)),
            pl.BlockSpec(memory_space=pl.ANY),
        ],
        out_specs=[pl.BlockSpec((1, tn), lambda i: (0, i)), pl.BlockSpec((tn, LANES), lambda i: (i, d // LANES))],
        out_shape=[jax.ShapeDtypeStruct((1, n), jnp.int32), jax.ShapeDtypeStruct(h2x.shape, F32)],
        input_output_aliases={2: 1},
        compiler_params=_cparams(("arbitrary",)),
        name="moe_route",
    )(logits_t, router_bias.reshape(N_EXPERTS, 1).astype(F32), h2x)


def _moe_kernel(tm, dump_base, spare_rows, nu_ref, e0_ref, e1_ref, base_ref, nv_ref, tok_ref, dst_ref, h2_hbm,
                wg0_ref, wu0_ref, wd0_ref, wg1_ref, wu1_ref, wd1_ref, f_hbm,
                xb0, xb1, yb0, yb1, xbf, gsem, ssem, zsem, idle_sem):
    del e0_ref, e1_ref
    t = pl.program_id(0)
    n_used = nu_ref[0]
    d = xbf.shape[1]
    xbuf, ybuf = (xb0, xb1), (yb0, yb1)

    def gather_row(base, r, slot):
        tok = tok_ref[base + r]
        pltpu.make_async_copy(h2_hbm.at[pl.ds(tok, 1)], xbuf[slot].at[pl.ds(r, 1)], gsem.at[slot]).start()

    def scatter_row(base, nv, r, slot):
        row = jnp.where(r < nv, dst_ref[base + r], dump_base + r)
        pltpu.make_async_copy(ybuf[slot].at[pl.ds(r, 1)], f_hbm.at[pl.ds(row, 1)], ssem.at[slot]).start(priority=1)

    def wait_gather(slot):
        pltpu.make_async_copy(h2_hbm.at[pl.ds(0, tm)], xbuf[slot], gsem.at[slot]).wait()

    def wait_scatter(slot):
        pltpu.make_async_copy(ybuf[slot], f_hbm.at[pl.ds(0, tm)], ssem.at[slot]).wait()

    @pl.when(t == 0)
    def _():
        yb1[...] = jnp.zeros(yb1.shape, F32)
        fills = [pltpu.make_async_copy(yb1, f_hbm.at[pl.ds(row, tm)], zsem) for row in spare_rows]
        for fill in fills:
            fill.start()
        for fill in fills:
            fill.wait()
        base0 = base_ref[1]

        def first(r, carry):
            gather_row(base0, r, 0)
            return carry

        lax.fori_loop(0, tm, first, 0)

    def step(cur):
        nxt = 1 - cur
        base_next = base_ref[t + 2]
        base_prev = base_ref[t]
        nv_prev = nv_ref[t]

        @pl.when(t < n_used)
        def _():
            wait_gather(cur)
            xbf[...] = xbuf[cur][:, :d].astype(BF16)
            w = xbuf[cur][:, d:]

            for r in range(tm):
                gather_row(base_next, r, nxt)
            x = xbf[...]
            a0 = (_silu(_dot(x, wg0_ref[0])) * _dot(x, wu0_ref[0]) * w[:, 0:1]).astype(BF16)

            z = pl.semaphore_read(idle_sem)
            zero = lax.shift_right_logical(z.astype(jnp.uint32), jnp.uint32(32)).astype(jnp.int32)
            off = pl.multiple_of(zero * 16, 16)

            for r in range(tm):
                scatter_row(base_prev, nv_prev, r, nxt)
            x = xbf[pl.ds(off, tm), :]
            a1 = (_silu(_dot(x, wg1_ref[0])) * _dot(x, wu1_ref[0]) * w[:, 1:2]).astype(BF16)
            y = _dot(a0, wd0_ref[0]) + _dot(a1, wd1_ref[0])

            @pl.when(t >= 1)
            def _():
                wait_scatter(cur)

            ybuf[cur][...] = y

        @pl.when(t == n_used)
        def _():
            wait_gather(cur)
            wait_scatter(cur)

            def last(r, carry):
                scatter_row(base_prev, nv_prev, r, nxt)
                return carry

            lax.fori_loop(0, tm, last, 0)
            wait_scatter(nxt)

    @pl.when(t % 2 == 0)
    def _():
        step(0)

    @pl.when(t % 2 == 1)
    def _():
        step(1)


def _moe(h2x, cls, dst_of_token, dump_base, spare_rows, out_rows, wg, wu, wd):
    n = h2x.shape[0]
    d = h2x.shape[1] - LANES
    tm = MOE_TILE
    n_tiles = n // tm + N_CLASSES
    cls = cls.reshape(n)
    _, tok_sorted, dst_sorted = lax.sort((cls, jnp.arange(n, dtype=jnp.int32), dst_of_token), num_keys=1)
    tail = jnp.zeros((tm,), jnp.int32)
    tok_sorted = jnp.concatenate([tok_sorted, tail])
    dst_sorted = jnp.concatenate([dst_sorted, tail])
    counts = jnp.sum(cls[:, None] == jnp.arange(N_CLASSES, dtype=jnp.int32)[None, :], axis=0).astype(jnp.int32)
    tiles_per = (counts + tm - 1) // tm
    tile_end = jnp.cumsum(tiles_per)
    tile_start = tile_end - tiles_per
    first_sorted = jnp.cumsum(counts) - counts
    n_used = tile_end[-1]
    tile_id = jnp.arange(n_tiles, dtype=jnp.int32)
    tile_cls = jnp.searchsorted(tile_end, jnp.minimum(tile_id, n_used - 1), side="right").astype(jnp.int32)
    tile_cls = jnp.minimum(tile_cls, N_CLASSES - 1)
    group, pair = tile_cls // N_PAIRS, tile_cls % N_PAIRS
    e0 = group * E_PER_GROUP + jnp.asarray(PAIR_LO, jnp.int32)[pair]
    e1 = group * E_PER_GROUP + jnp.asarray(PAIR_HI, jnp.int32)[pair]
    in_class = (tile_id - tile_start[tile_cls]) * tm
    used = tile_id < n_used
    base = jnp.where(used, first_sorted[tile_cls] + in_class, 0)
    nv = jnp.where(used, jnp.clip(counts[tile_cls] - in_class, 0, tm), 0)
    guard = jnp.zeros((1,), jnp.int32)
    base = jnp.concatenate([guard, base, guard]).astype(jnp.int32)
    nv = jnp.concatenate([guard, nv, guard]).astype(jnp.int32)

    de = wg.shape[2]
    smem = pl.BlockSpec(memory_space=pltpu.SMEM)
    up0 = pl.BlockSpec((1, d, de), lambda i, nu, e0, e1, base, nv: (e0[i], 0, 0))
    up1 = pl.BlockSpec((1, d, de), lambda i, nu, e0, e1, base, nv: (e1[i], 0, 0))
    dn0 = pl.BlockSpec((1, de, d), lambda i, nu, e0, e1, base, nv: (e0[i], 0, 0))
    dn1 = pl.BlockSpec((1, de, d), lambda i, nu, e0, e1, base, nv: (e1[i], 0, 0))
    return pl.pallas_call(
        functools.partial(_moe_kernel, tm, dump_base, spare_rows),
        grid_spec=pltpu.PrefetchScalarGridSpec(
            num_scalar_prefetch=5,
            grid=(n_tiles,),
            in_specs=[
                smem, smem,
                pl.BlockSpec(memory_space=pl.ANY),
                up0, up0, dn0, up1, up1, dn1,
            ],
            out_specs=pl.BlockSpec(memory_space=pl.ANY),
            scratch_shapes=[
                pltpu.VMEM((tm, d + LANES), F32), pltpu.VMEM((tm, d + LANES), F32),
                pltpu.VMEM((tm, d), F32), pltpu.VMEM((tm, d), F32),
                pltpu.VMEM((tm, d), BF16),
                pltpu.SemaphoreType.DMA((2,)), pltpu.SemaphoreType.DMA((2,)), pltpu.SemaphoreType.DMA(()),
                pltpu.SemaphoreType.REGULAR(()),
            ],
        ),
        out_shape=jax.ShapeDtypeStruct((out_rows, d), F32),
        compiler_params=_cparams(("arbitrary",)),
        name="moe_experts",
    )(n_used.reshape(1).astype(jnp.int32), e0, e1, base, nv, tok_sorted, dst_sorted, h2x,
      wg, wu, wd, wg, wu, wd)


def _hproj_kernel(n_lat_tiles, rows, sl_ref, fl_ref, sc_ref_, fc_ref, g2_ref, sc_ref, sh_ref, ng_ref, lb_ref,
                  wq_ref, wzf_ref, wzb_ref, wi_ref, wg_ref,
                  s2_ref, q_ref, kf_ref, kb_ref, lff_ref, lfb_ref, i_ref, g_ref, xt):
    j = pl.program_id(1)
    d = xt.shape[1]
    g2 = g2_ref[0]

    @pl.when(j < n_lat_tiles)
    def _():
        for c in range(COL_TILE):
            s = sl_ref[0, :, c, :] + g2 * fl_ref[0, :, c, :]
            s2_ref[0, :, c, :] = s
            xt[c * rows:(c + 1) * rows, :] = s

    @pl.when(j >= n_lat_tiles)
    def _():
        xt[...] = sc_ref_[0] + g2 * fc_ref[0]

    h = (_rms(xt[...], ng_ref[...]) * (1.0 + sc_ref[0]) + sh_ref[0]).astype(BF16)
    q_ref[0] = _silu(_dot(h, wq_ref[...])).astype(BF16)
    i_ref[0] = _dot(h, wi_ref[...]).astype(BF16)
    g_ref[0] = _dot(h, wg_ref[...]).astype(BF16)
    for dd, (wz_ref, k_ref, lf_ref) in enumerate(((wzf_ref, kf_ref, lff_ref), (wzb_ref, kb_ref, lfb_ref))):
        z = _dot(h, wz_ref[...])
        log_lb = lb_ref[2 * dd:2 * dd + 1, :]
        log_1mlb = lb_ref[2 * dd + 1:2 * dd + 2, :]
        ls = _log_sigmoid(z)
        a = log_1mlb + ls
        lf_ref[0] = jnp.maximum(log_lb, a) + jnp.log1p(jnp.exp(-jnp.abs(log_lb - a)))
        k_ref[0] = jnp.exp(a - z).astype(BF16)


def _hproj(s1, f0, modv, rows_mod, norm_g, lbtab, ws, ll, lc):
    bsz, ltot, d = s1.shape
    rows = ll // GRID_W
    tm = COL_TILE * rows
    nl, nc = GRID_W // COL_TILE, lc // tm
    s_cm = s1.reshape(bsz, ltot // GRID_W, GRID_W, d)
    f_tok = f0.reshape(bsz, -1, d)
    f_cm = f0.reshape(bsz, -1, GRID_W, d)
    cm = pl.BlockSpec((1, rows, COL_TILE, d), lambda b, j: (b, 0, jnp.minimum(j, nl - 1), 0))
    ctx = pl.BlockSpec((1, tm, d), lambda b, j: (b, ll // tm + jnp.maximum(j - nl, 0), 0))
    wspec = pl.BlockSpec((d, d), lambda b, j: (0, 0))
    tok = pl.BlockSpec((1, tm, d), lambda b, j: (b, j, 0))

    def g2_map(b, j):
        row = jnp.where(j < nl, b, rows_mod - 1)
        return ((0 * rows_mod + row) * N_ADA + 5, 0, 0)

    outs = pl.pallas_call(
        functools.partial(_hproj_kernel, nl, rows),
        grid=(bsz, nl + nc),
        in_specs=[
            cm, cm, ctx, ctx,
            pl.BlockSpec((1, 1, d), g2_map),
            _mod_spec(d, 1, 1, nl, rows_mod), _mod_spec(d, 1, 0, nl, rows_mod),
            pl.BlockSpec((1, d), lambda b, j: (0, 0)),
            pl.BlockSpec((4, d), lambda b, j: (0, 0)),
            wspec, wspec, wspec, wspec, wspec,
        ],
        out_specs=[cm, tok, tok, tok, tok, tok, tok, tok],
        out_shape=[jax.ShapeDtypeStruct((bsz, rows, GRID_W, d), F32)]
        + [jax.ShapeDtypeStruct((bsz, ltot, d), dt) for dt in (BF16, BF16, BF16, F32, F32, BF16, BF16)],
        scratch_shapes=[pltpu.VMEM((tm, d), F32)],
        compiler_params=_cparams(("arbitrary", "arbitrary")),
        name="hgrn_in_proj",
    )(s_cm, f_cm, s1, f_tok, modv, modv, modv, norm_g, lbtab, *ws)
    return outs


def _hgrn_kernel(t, ll, lc, q_ref, kf_ref, kb_ref, lff_ref, lfb_ref, i_ref, g_ref, hg_ref, out_ref,
                 of, ob, inc_s, dec_s, qd_s):
    levels = []
    m = 2
    while m <= t:
        levels.append(m)
        m *= 2
    ti = lax.broadcasted_iota(jnp.int32, (t, t), 0)
    si = lax.broadcasted_iota(jnp.int32, (t, t), 1)
    xor = ti ^ si
    level = jnp.zeros((t, t), jnp.int32)
    for m in levels:
        level = level + (xor >= m // 2).astype(jnp.int32)
    feeds = ((si <= ti).astype(BF16), (si >= ti).astype(BF16))

    row = lax.broadcasted_iota(jnp.int32, (t, 1), 0)
    feeds_twice = tuple(jnp.concatenate([f, f], axis=1) for f in feeds)

    def block_ref(b, m, d):
        half = m // 2
        pos = half - 1 if d == 0 else half
        if m >= 8:
            dk = b.shape[1]
            b3 = b.reshape(t // m, m, dk)
            return jnp.broadcast_to(b3[:, pos:pos + 1, :], b3.shape).reshape(t, dk)
        r = row % m
        out = b
        for res in range(m):
            if res != pos:
                out = jnp.where(r == res, pltpu.roll(b, (res - pos) % t, 0), out)
        return out

    def neg_abs(x):
        sign = jnp.uint32(0x80000000)
        return lax.bitcast_convert_type(lax.bitcast_convert_type(x, jnp.uint32) | sign, F32)

    o_dir = (of, ob)

    def prepare(jobs, need_out):
        work = []
        for ci, d in jobs:
            rows = pl.ds(ci * t if isinstance(ci, int) else pl.multiple_of(ci * t, t), t)
            lf = (lff_ref if d == 0 else lfb_ref)[0, rows, :] * LOG2_E
            hi = lf.astype(BF16)
            lo = (lf - hi.astype(F32)).astype(BF16)
            b = _dot(feeds_twice[d], jnp.concatenate([hi, lo], axis=0))
            work.append((ci, d, rows, b))
        for ci, d, rows, b in work:
            b_tot = b[t - 1:t, :] if d == 0 else b[0:1, :]
            k = (kf_ref if d == 0 else kb_ref)[0, rows, :]
            inc_s[d, ci] = _dot_tn(i_ref[0, rows, :], k * jnp.exp2(b_tot - b).astype(BF16))
            dec_s[d, ci] = jnp.exp2(b_tot)
            if need_out:
                qd_s[d, rows, :] = q_ref[0, rows, :] * jnp.exp2(b).astype(BF16)
        if not need_out:
            return
        acc = []
        for ci, d, rows, b in work:
            acc.append(_dot_nt(q_ref[0, rows, :], (kf_ref if d == 0 else kb_ref)[0, rows, :]))
        for idx, m in enumerate(levels):
            for n, (ci, d, rows, b) in enumerate(work):
                e = jnp.exp2(neg_abs(b - block_ref(b, m, d))).astype(BF16)
                p = _dot_nt(q_ref[0, rows, :] * e, (kf_ref if d == 0 else kb_ref)[0, rows, :] * e)
                acc[n] = jnp.where(level == idx + 1, p, acc[n])
        for n, (ci, d, rows, b) in enumerate(work):
            o_dir[d][rows, :] = _dot(acc[n].astype(BF16) * feeds[d], i_ref[0, rows, :])

    ncc, ncl = lc // t, ll // t
    prepare([(ncl + i, d) for i in range(ncc) for d in range(2)], False)

    def body(i, carry):
        prepare([(2 * i + j, d) for j in range(2) for d in range(2)], True)
        return carry

    lax.fori_loop(0, ncl // 2, body, 0)

    for d in range(2):
        order = list(range(ncl, ncl + ncc)) + list(range(ncl))
        if d == 1:
            order = list(range(ncl + ncc - 1, ncl - 1, -1)) + list(range(ncl - 1, -1, -1))
        st = jnp.zeros(inc_s.shape[2:], F32)
        for ci in order:
            if ci < ncl:
                rows = pl.ds(ci * t, t)
                o_dir[d][rows, :] = o_dir[d][rows, :] + _dot_nt(qd_s[d, rows, :], st.astype(BF16))
            st = dec_s[d, ci] * st + inc_s[d, ci]

    def epilogue(i, carry):
        s = pl.multiple_of(i * t, t)
        y = _rms(of[pl.ds(s, t), :] + ob[pl.ds(s, t), :], hg_ref[...])
        out_ref[0, pl.ds(s, t), :] = (y * _silu(g_ref[0, pl.ds(s, t), :].astype(F32))).astype(BF16)
        return carry

    lax.fori_loop(0, ncl, epilogue, 0)


def _hgrn_scan(q, kf, kb, lff, lfb, iv, g, head_g, ll, lc):
    bsz, ltot, hk = q.shape
    nh = H_HEADS
    dk = hk // nh
    t = SCAN_CHUNK
    full = pl.BlockSpec((1, ltot, dk), lambda b, h: (b, 0, h))
    lat = pl.BlockSpec((1, ll, dk), lambda b, h: (b, 0, h))
    return pl.pallas_call(
        functools.partial(_hgrn_kernel, t, ll, lc),
        grid=(bsz, nh),
        in_specs=[full, full, full, full, full, full, lat, pl.BlockSpec((1, dk), lambda b, h: (0, h))],
        out_specs=lat,
        out_shape=jax.ShapeDtypeStruct((bsz, ll, hk), BF16),
        scratch_shapes=[pltpu.VMEM((ll, dk), F32), pltpu.VMEM((ll, dk), F32),
                        pltpu.VMEM((2, ltot // t, dk, dk), F32), pltpu.VMEM((2, ltot // t, 1, dk), F32),
                        pltpu.VMEM((2, ll, dk), BF16)],
        compiler_params=_cparams(("arbitrary", "arbitrary")),
        name="hgrn_scan",
    )(q, kf, kb, lff, lfb, iv, g, head_g.reshape(1, -1))


def _out1_kernel(rows, hg_ref, s_ref, g1_ref, sc_ref, sh_ref, ng_ref, wo_ref, rhi_ref, rlo_ref,
                 s3_ref, h2_ref, lg_ref, st):
    d = st.shape[1]
    y = g1_ref[0] * _dot(hg_ref[0], wo_ref[...])
    for c in range(COL_TILE):
        s = s_ref[0, :, c, :] + y[c * rows:(c + 1) * rows, :]
        s3_ref[0, :, c, :] = s
        st[c * rows:(c + 1) * rows, :] = s
    h2 = _rms(st[...], ng_ref[...]) * (1.0 + sc_ref[0]) + sh_ref[0]
    h2_ref[:, :h2.shape[1]] = h2
    h2_ref[:, h2.shape[1]:] = jnp.zeros((h2.shape[0], LANES), F32)
    lg_ref[...] = _router_logits_t(h2, rhi_ref, rlo_ref)


def _out1(hg, s2_cm, modv, rows_mod, norm_g, w_out, r_hi, r_lo):
    bsz, ll, hv = hg.shape
    rows = s2_cm.shape[1]
    d = s2_cm.shape[3]
    tm = COL_TILE * rows
    nl = GRID_W // COL_TILE
    cm = pl.BlockSpec((1, rows, COL_TILE, d), lambda b, j: (b, 0, j, 0))
    return pl.pallas_call(
        functools.partial(_out1_kernel, rows),
        grid=(bsz, nl),
        in_specs=[
            pl.BlockSpec((1, tm, hv), lambda b, j: (b, j, 0)),
            cm,
            _mod_spec(d, 1, 2, nl, rows_mod), _mod_spec(d, 1, 4, nl, rows_mod), _mod_spec(d, 1, 3, nl, rows_mod),
            pl.BlockSpec((1, d), lambda b, j: (0, 0)),
            pl.BlockSpec(w_out.shape, lambda b, j: (0, 0)),
            pl.BlockSpec(r_hi.shape, lambda b, j: (0, 0)),
            pl.BlockSpec(r_lo.shape, lambda b, j: (0, 0)),
        ],
        out_specs=[
            cm,
            pl.BlockSpec((tm, d + LANES), lambda b, j: (b * nl + j, 0)),
            pl.BlockSpec((N_EXPERTS, tm), lambda b, j: (0, b * nl + j)),
        ],
        out_shape=[
            jax.ShapeDtypeStruct(s2_cm.shape, F32),
            jax.ShapeDtypeStruct((bsz * ll, d + LANES), F32),
            jax.ShapeDtypeStruct((N_EXPERTS, bsz * ll), F32),
        ],
        scratch_shapes=[pltpu.VMEM((tm, d), F32)],
        compiler_params=_cparams(("arbitrary", "arbitrary")),
        name="hgrn_out_proj",
    )(hg, s2_cm, modv, modv, modv, norm_g, w_out, r_hi, r_lo)


def _final_kernel(s_ref, f_ref, g2_ref, fg_ref, o_ref):
    o_ref[0] = _rms(s_ref[0] + g2_ref[0] * f_ref[0], fg_ref[...])


def _final(s3, f1, modv, rows_mod, final_g, tm):
    bsz, ll, d = s3.shape
    nl = ll // tm
    tok = pl.BlockSpec((1, tm, d), lambda b, j: (b, j, 0))
    return pl.pallas_call(
        _final_kernel,
        grid=(bsz, nl),
        in_specs=[tok, tok, _mod_spec(d, 1, 5, nl, rows_mod), pl.BlockSpec((1, d), lambda b, j: (0, 0))],
        out_specs=tok,
        out_shape=jax.ShapeDtypeStruct((bsz, ll, d), F32),
        compiler_params=_cparams(("arbitrary", "arbitrary")),
        name="final_norm",
    )(s3, f1.reshape(bsz, -1, d), modv, final_g)


def kernel(x, c, ctx, c_ctx, ada_w, ada_b, norm_mix_g, norm_ffn_g, final_g, m_w_in, m_conv_w, m_conv_b, m_gate_b,
           m_head_g, m_w_out, h_w_in, h_lower_bounds, h_head_g, h_w_out, router_w, router_bias, e_w_gate,
           e_w_up, e_w_down):
    bsz, ll, d = x.shape
    lc = ctx.shape[1]
    ltot = ll + lc
    depth = ada_w.shape[0]
    assert depth == 2 and ll % GRID_W == 0 and lc % SCAN_CHUNK == 0 and ll % SCAN_CHUNK == 0
    rows = ll // GRID_W
    tm = COL_TILE * rows
    assert lc % tm == 0 and ltot % GRID_W == 0 and (bsz * ltot) % MOE_TILE == 0 and (bsz * ll) % MOE_TILE == 0

    rows_mod = 8 * ((bsz + 1 + 7) // 8)
    cc = jnp.zeros((rows_mod, d), F32).at[:bsz].set(c).at[rows_mod - 1].set(c_ctx)
    modv = _ada(cc, ada_w, ada_b).reshape(depth * rows_mod * N_ADA, 1, d)

    r_pad = jnp.zeros((d, LANES), F32).at[:, :N_EXPERTS].set(router_w)
    r_hi = r_pad.astype(BF16)
    r_lo = (r_pad - r_hi.astype(F32)).astype(BF16)
    wg_all, wu_all, wd_all = e_w_gate.astype(BF16), e_w_up.astype(BF16), e_w_down.astype(BF16)

    m_qk, m_v = m_conv_w.shape[2], m_head_g.shape[1]
    w_in = m_w_in[0].astype(BF16)
    w_gates = jnp.zeros((d, LANES), BF16).at[:, :4 * M_HEADS].set(w_in[:, m_qk + 2 * m_v:])
    qk, v, o, gates = _mproj(x, ctx, modv, rows_mod, norm_mix_g[0:1], w_in[:, :m_qk], w_in[:, m_qk:m_qk + m_v],
                             w_in[:, m_qk + m_v:m_qk + 2 * m_v], w_gates, tm)
    hg = _mlstm_scan(qk, v, o, gates, m_conv_w[0], m_conv_b[0], m_gate_b[0], m_head_g[0], ll, lc)
    s1, h2, lg = _out0(hg, x, ctx, modv, rows_mod, norm_ffn_g[0:1], m_w_out[0].astype(BF16), r_hi, r_lo, tm)
    cls, h2 = _route(lg, router_bias, h2)
    tok0 = jnp.arange(bsz * ltot, dtype=jnp.int32)
    dst0 = (tok0 // ltot) * (ltot + MOE_TILE) + tok0 % ltot
    spare0 = tuple(b * (ltot + MOE_TILE) + ltot for b in range(1, bsz))
    f0 = _moe(h2, cls, dst0, ltot, spare0, bsz * (ltot + MOE_TILE), wg_all[0], wu_all[0], wd_all[0])

    lbs = jnp.cumsum(jax.nn.softmax(h_lower_bounds.astype(F32), axis=0), axis=0)
    lb = (lbs - lbs[0])[1].reshape(2, -1)
    lbtab = jnp.stack([jnp.log(lb[0]), jnp.log1p(-lb[0]), jnp.log(lb[1]), jnp.log1p(-lb[1])])
    hw = h_w_in[0].astype(BF16)
    hk = lb.shape[1]
    ws = (hw[:, :hk], hw[:, hk:2 * hk], hw[:, 2 * hk:3 * hk], hw[:, 3 * hk:3 * hk + d], hw[:, 3 * hk + d:])
    s2_cm, q, kf, kb, lff, lfb, iv, g = _hproj(s1, f0, modv, rows_mod, norm_mix_g[1:2], lbtab, ws, ll, lc)
    hg1 = _hgrn_scan(q, kf, kb, lff, lfb, iv, g, h_head_g[0], ll, lc)
    s3_cm, h2b, lgb = _out1(hg1, s2_cm, modv, rows_mod, norm_ffn_g[1:2], h_w_out[0].astype(BF16), r_hi, r_lo)
    clsb, h2b = _route(lgb, router_bias, h2b)
    n1 = bsz * ll
    tokid = jnp.arange(n1, dtype=jnp.int32)
    within = tokid % ll
    raster = (tokid // ll) * (ll + MOE_TILE) + (within % rows) * GRID_W + within // rows
    spare1 = tuple(b * (ll + MOE_TILE) + ll for b in range(1, bsz))
    f1 = _moe(h2b, clsb, raster, ll, spare1, bsz * (ll + MOE_TILE), wg_all[1], wu_all[1], wd_all[1])

    return _final(s3_cm.reshape(bsz, ll, d), f1, modv, rows_mod, final_g.reshape(1, d), tm)
```

```python
import functools

import numpy as np
import jax
import jax.numpy as jnp
from jax import lax
from jax.experimental import pallas as pl
from jax.experimental.pallas import tpu as pltpu

F32 = jnp.float32
BF16 = jnp.bfloat16

EPS = 1e-6
LOG2_E = 1.4426950408889634
N_ADA = 6
GRID_W = 64
M_HEADS = 4
H_HEADS = 8
N_EXPERTS = 16
N_GROUPS = 4
E_PER_GROUP = N_EXPERTS // N_GROUPS
N_PAIRS = 6
N_CLASSES = N_GROUPS * N_PAIRS
PAIR_LO = (0, 0, 0, 1, 1, 2)
PAIR_HI = (1, 2, 3, 2, 3, 3)

LANES = 128
COL_TILE = 8
SCAN_CHUNK = 128
MOE_TILE = 256
VMEM_LIMIT = 56 * 1024 * 1024


def _cparams(sem):
    return pltpu.CompilerParams(dimension_semantics=sem, vmem_limit_bytes=VMEM_LIMIT)


def _dot(a, b):
    return jnp.dot(a, b, preferred_element_type=F32)


def _dot_nt(a, b):
    return lax.dot_general(a, b, (((1,), (1,)), ((), ())), preferred_element_type=F32)


def _dot_tn(a, b):
    return lax.dot_general(a, b, (((0,), (0,)), ((), ())), preferred_element_type=F32)


def _split_bf16(a):
    hi = a.astype(BF16)
    lo = (a - hi.astype(F32)).astype(BF16)
    return hi, lo


def _sigmoid(x):
    return 1.0 / (1.0 + jnp.exp(-x))


def _silu(x):
    return x * _sigmoid(x)


def _log_sigmoid(x):
    return jnp.minimum(x, 0.0) - jnp.log1p(jnp.exp(-jnp.abs(x)))


def _rms(x, g):
    return x * lax.rsqrt(jnp.mean(x * x, axis=-1, keepdims=True) + EPS) * g


def _ada_kernel(c_ref, w_ref, b_ref, o_ref):
    a = _silu(c_ref[...])
    a_hi, a_lo = _split_bf16(a)
    w_hi, w_lo = _split_bf16(w_ref[0])
    acc = _dot(a_hi, w_hi) + _dot(a_lo, w_hi) + _dot(a_hi, w_lo)
    o_ref[0] = acc + b_ref[0]


def _ada(cc, ada_w, ada_b):
    depth, d, n = ada_w.shape
    tn = 1024
    rows = cc.shape[0]
    return pl.pallas_call(
        _ada_kernel,
        grid=(depth, n // tn),
        in_specs=[
            pl.BlockSpec((rows, d), lambda l, j: (0, 0)),
            pl.BlockSpec((1, d, tn), lambda l, j: (l, 0, j)),
            pl.BlockSpec((1, 1, tn), lambda l, j: (l, 0, j)),
        ],
        out_specs=pl.BlockSpec((1, rows, tn), lambda l, j: (l, 0, j)),
        out_shape=jax.ShapeDtypeStruct((depth, rows, n), F32),
        compiler_params=_cparams(("arbitrary", "arbitrary")),
        name="ada_mod",
    )(cc, ada_w, ada_b.reshape(depth, 1, n))


def _mproj_kernel(n_lat_tiles, xl_ref, xc_ref, sc_ref, sh_ref, g_ref, wqk_ref, wv_ref, wo_ref, wg_ref,
                  qk_ref, v_ref, o_ref, gt_ref):
    j = pl.program_id(1)
    x = jnp.where(j < n_lat_tiles, xl_ref[0], xc_ref[0])
    h = (_rms(x, g_ref[...]) * (1.0 + sc_ref[0]) + sh_ref[0]).astype(BF16)
    qk_ref[0] = _dot(h, wqk_ref[...]).astype(BF16)
    v_ref[0] = _dot(h, wv_ref[...]).astype(BF16)
    o_ref[0] = _dot(h, wo_ref[...]).astype(BF16)
    gt_ref[0] = _dot(h, wg_ref[...])[:, :4 * M_HEADS]


def _mod_spec(d, layer, k, n_lat_tiles, rows):
    def imap(b, j):
        row = jnp.where(j < n_lat_tiles, b, rows - 1)
        return ((layer * rows + row) * N_ADA + k, 0, 0)
    return pl.BlockSpec((1, 1, d), imap)


def _mproj(x, ctx, modv, rows, norm_g, w_qk, w_v, w_o, w_g, tm):
    bsz, ll, d = x.shape
    lc = ctx.shape[1]
    nl, nc = ll // tm, lc // tm
    ltot = ll + lc
    wspec = lambda n: pl.BlockSpec((d, n), lambda b, j: (0, 0))
    tok = lambda n: pl.BlockSpec((1, tm, n), lambda b, j: (b, j, 0))
    return pl.pallas_call(
        functools.partial(_mproj_kernel, nl),
        grid=(bsz, nl + nc),
        in_specs=[
            pl.BlockSpec((1, tm, d), lambda b, j: (b, jnp.minimum(j, nl - 1), 0)),
            pl.BlockSpec((1, tm, d), lambda b, j: (b, jnp.maximum(j - nl, 0), 0)),
            _mod_spec(d, 0, 1, nl, rows), _mod_spec(d, 0, 0, nl, rows),
            pl.BlockSpec((1, d), lambda b, j: (0, 0)),
            wspec(w_qk.shape[1]), wspec(w_v.shape[1]), wspec(w_o.shape[1]), wspec(w_g.shape[1]),
        ],
        out_specs=[tok(w_qk.shape[1]), tok(w_v.shape[1]), tok(w_o.shape[1]), tok(4 * M_HEADS)],
        out_shape=[
            jax.ShapeDtypeStruct((bsz, ltot, w_qk.shape[1]), BF16),
            jax.ShapeDtypeStruct((bsz, ltot, w_v.shape[1]), BF16),
            jax.ShapeDtypeStruct((bsz, ltot, w_o.shape[1]), BF16),
            jax.ShapeDtypeStruct((bsz, ltot, 4 * M_HEADS), F32),
        ],
        compiler_params=_cparams(("arbitrary", "arbitrary")),
        name="mlstm_in_proj",
    )(x, ctx, modv, modv, norm_g, w_qk, w_v, w_o, w_g)


def _mlstm_kernel(t, ll, lc, hp, q_ref, k_ref, v_ref, o_ref, gc_ref, gr_ref, cwq_ref, cwk_ref, cbq_ref, cbk_ref,
                  gbc_ref, gbr_ref, hg_ref, out_ref, qs, ks, hf, hb, c_s, m_s):
    ltot = ll + lc
    dk = q_ref.shape[2] // hp
    dv = v_ref.shape[2] // hp
    row = lax.broadcasted_iota(jnp.int32, (ltot, 1), 0)
    first = (row == 0) | (row == ll)
    last = (row == ll - 1) | (row == ltot - 1)

    def conv(x_ref, w_ref, b_ref):
        x = x_ref[0].astype(F32)
        w = w_ref[...]
        xp = jnp.where(first, 0.0, pltpu.roll(x, 1, 0))
        xn = jnp.where(last, 0.0, pltpu.roll(x, ltot - 1, 0))
        return _silu(xp * w[0:1] + x * w[1:2] + xn * w[2:3] + b_ref[...])

    qs[...] = (conv(q_ref, cwq_ref, cbq_ref) * (dk ** -0.5)).astype(BF16)
    ks[...] = conv(k_ref, cwk_ref, cbk_ref).astype(BF16)

    c_s[...] = jnp.zeros(c_s.shape, F32)
    m_s[...] = jnp.zeros(m_s.shape, F32)

    ti = lax.broadcasted_iota(jnp.int32, (t, t), 0)
    si = lax.broadcasted_iota(jnp.int32, (t, t), 1)

    h_dir = (hf, hb)
    ones = jnp.ones((t, LANES), BF16)

    def chunks(starts):
        jobs = [(j, d) for j in range(hp) for d in range(2)]
        st = {}
        for j, d in jobs:
            rows = pl.ds(starts[d], t)
            gc = gc_ref[0, j, rows, :] + gbc_ref[j]
            gr = gr_ref[0, j, :, rows] + gbr_ref[j]
            ig_c, lf_c = gc[:, 2 * d:2 * d + 1], _log_sigmoid(gc[:, 2 * d + 1:2 * d + 2])
            ig_r, lf_r = gr[2 * d:2 * d + 1, :], _log_sigmoid(gr[2 * d + 1:2 * d + 2, :])
            seen = (si <= ti) if d == 0 else (si >= ti)
            seen_t = (ti <= si) if d == 0 else (ti >= si)
            b_c = jnp.sum(jnp.where(seen, lf_r, 0.0), axis=1, keepdims=True)
            b_r = jnp.sum(jnp.where(seen_t, lf_c, 0.0), axis=0, keepdims=True)
            total = jnp.sum(lf_r, axis=1, keepdims=True)
            st[j, d] = dict(rows=rows, ig_c=ig_c, ig_r=ig_r, b_c=b_c, b_r=b_r, total=total, seen=seen,
                            m_prev=m_s[2 * j + d][0:1, 0:1])
        for j, d in jobs:
            s = st[j, d]
            s["q"] = qs[s["rows"], j * dk:(j + 1) * dk]
            s["k"] = ks[s["rows"], j * dk:(j + 1) * dk]
            s["v"] = jnp.concatenate([v_ref[0, s["rows"], j * dv:(j + 1) * dv], ones], axis=1)
            s["qk"] = _dot_nt(s["q"], s["k"])
            s["qc"] = _dot_nt(s["q"], c_s[2 * j + d].astype(BF16))
        for j, d in jobs:
            s = st[j, d]
            log_d = jnp.where(s["seen"], s["b_c"] - s["b_r"] + s["ig_r"], -jnp.inf)
            log_inter = s["b_c"] + s["m_prev"]
            m_t = jnp.maximum(log_inter, jnp.max(log_d, axis=1, keepdims=True))
            s["w_ts"] = jnp.exp(log_d - m_t) * s["qk"]
            s["inter"] = jnp.exp(log_inter - m_t)
            s["m_t"] = m_t
        for j, d in jobs:
            s = st[j, d]
            s["num"] = _dot(s["w_ts"].astype(BF16), s["v"]) + s["inter"] * s["qc"]
        for j, d in jobs:
            s = st[j, d]
            den = s["num"][:, dv:]
            scale = 1.0 / jnp.maximum(jnp.abs(den), jnp.exp(-s["m_t"]))
            h_dir[d][s["rows"], j * dv:(j + 1) * dv] = s["num"][:, :dv] * jnp.concatenate([scale] * (dv // LANES), axis=1)
        for j, d in jobs:
            s = st[j, d]
            g_c = s["total"] - s["b_c"] + s["ig_c"]
            g_r = s["total"] - s["b_r"] + s["ig_r"]
            m_new = jnp.maximum(s["total"] + s["m_prev"], jnp.max(g_r, axis=1, keepdims=True))
            w_c = jnp.exp(g_c - m_new)
            decay = jnp.exp(s["total"] + s["m_prev"] - m_new)
            wv = (w_c * s["v"].astype(F32)).astype(BF16)
            c_s[2 * j + d] = decay * c_s[2 * j + d] + _dot_tn(wv, s["k"])
            m_s[2 * j + d] = jnp.broadcast_to(m_new, m_s.shape[1:])

    ncc, ncl = lc // t, ll // t
    for i in range(ncc):
        chunks((ll + i * t, ll + (ncc - 1 - i) * t))

    def body(i, carry):
        chunks((pl.multiple_of(i * t, t), pl.multiple_of((ncl - 1 - i) * t, t)))
        return carry

    lax.fori_loop(0, ncl, body, 0)

    def epilogue(i, carry):
        s = pl.multiple_of(i * t, t)
        for j in range(hp):
            cols = slice(j * dv, (j + 1) * dv)
            y = _rms(hf[pl.ds(s, t), cols] + hb[pl.ds(s, t), cols], hg_ref[:, cols])
            out_ref[0, pl.ds(s, t), cols] = (y * _sigmoid(o_ref[0, pl.ds(s, t), cols].astype(F32))).astype(BF16)
        return carry

    lax.fori_loop(0, ltot // t, epilogue, 0)


def _mlstm_scan(qk, v, o, gates, conv_w, conv_b, gate_b, head_g, ll, lc):
    bsz, ltot, _ = qk.shape
    nh = M_HEADS
    hp = 2
    dk = qk.shape[2] // (2 * nh)
    dv = v.shape[2] // nh
    t = SCAN_CHUNK
    g4 = gates.reshape(bsz, ltot, 4, nh).transpose(0, 3, 1, 2)
    g4t = g4.transpose(0, 1, 3, 2)
    gb = gate_b.reshape(4, nh).T
    return pl.pallas_call(
        functools.partial(_mlstm_kernel, t, ll, lc, hp),
        grid=(bsz, nh // hp),
        in_specs=[
            pl.BlockSpec((1, ltot, hp * dk), lambda b, h: (b, 0, h)),
            pl.BlockSpec((1, ltot, hp * dk), lambda b, h: (b, 0, nh // hp + h)),
            pl.BlockSpec((1, ltot, hp * dv), lambda b, h: (b, 0, h)),
            pl.BlockSpec((1, ltot, hp * dv), lambda b, h: (b, 0, h)),
            pl.BlockSpec((1, hp, ltot, 4), lambda b, h: (b, h, 0, 0)),
            pl.BlockSpec((1, hp, 4, ltot), lambda b, h: (b, h, 0, 0)),
            pl.BlockSpec((3, hp * dk), lambda b, h: (0, h)),
            pl.BlockSpec((3, hp * dk), lambda b, h: (0, nh // hp + h)),
            pl.BlockSpec((1, hp * dk), lambda b, h: (0, h)),
            pl.BlockSpec((1, hp * dk), lambda b, h: (0, nh // hp + h)),
            pl.BlockSpec((hp, 1, 4), lambda b, h: (h, 0, 0)),
            pl.BlockSpec((hp, 4, 1), lambda b, h: (h, 0, 0)),
            pl.BlockSpec((1, hp * dv), lambda b, h: (0, h)),
        ],
        out_specs=pl.BlockSpec((1, ltot, hp * dv), lambda b, h: (b, 0, h)),
        out_shape=jax.ShapeDtypeStruct((bsz, ltot, nh * dv), BF16),
        scratch_shapes=[
            pltpu.VMEM((ltot, hp * dk), BF16), pltpu.VMEM((ltot, hp * dk), BF16),
            pltpu.VMEM((ltot, hp * dv), F32), pltpu.VMEM((ltot, hp * dv), F32),
            pltpu.VMEM((2 * hp, dv + LANES, dk), F32), pltpu.VMEM((2 * hp, 8, LANES), F32),
        ],
        compiler_params=_cparams(("arbitrary", "arbitrary")),
        name="mlstm_scan",
    )(qk, qk, v, o, g4, g4t, conv_w, conv_w, conv_b.reshape(1, -1), conv_b.reshape(1, -1),
      gb.reshape(nh, 1, 4), gb.reshape(nh, 4, 1), head_g.reshape(1, -1))


def _router_logits_t(h2, rhi_ref, rlo_ref):
    h_hi, h_lo = _split_bf16(h2)
    lg = _dot(h_hi, rhi_ref[...]) + _dot(h_lo, rhi_ref[...]) + _dot(h_hi, rlo_ref[...])
    return lg.T[:N_EXPERTS, :]


def _out0_kernel(n_lat_tiles, hg_ref, xl_ref, xc_ref, g1_ref, sc_ref, sh_ref, ng_ref, wo_ref, rhi_ref, rlo_ref,
                 s_ref, h2_ref, lg_ref):
    j = pl.program_id(1)
    x = jnp.where(j < n_lat_tiles, xl_ref[0], xc_ref[0])
    s = x + g1_ref[0] * _dot(hg_ref[0], wo_ref[...])
    s_ref[0] = s
    h2 = _rms(s, ng_ref[...]) * (1.0 + sc_ref[0]) + sh_ref[0]
    h2_ref[:, :h2.shape[1]] = h2
    h2_ref[:, h2.shape[1]:] = jnp.zeros((h2.shape[0], LANES), F32)
    lg_ref[...] = _router_logits_t(h2, rhi_ref, rlo_ref)


def _out0(hg, x, ctx, modv, rows, norm_g, w_out, r_hi, r_lo, tm):
    bsz, ll, d = x.shape
    lc = ctx.shape[1]
    nl, nc = ll // tm, lc // tm
    nt = nl + nc
    ltot = ll + lc
    return pl.pallas_call(
        functools.partial(_out0_kernel, nl),
        grid=(bsz, nt),
        in_specs=[
            pl.BlockSpec((1, tm, hg.shape[2]), lambda b, j: (b, j, 0)),
            pl.BlockSpec((1, tm, d), lambda b, j: (b, jnp.minimum(j, nl - 1), 0)),
            pl.BlockSpec((1, tm, d), lambda b, j: (b, jnp.maximum(j - nl, 0), 0)),
            _mod_spec(d, 0, 2, nl, rows), _mod_spec(d, 0, 4, nl, rows), _mod_spec(d, 0, 3, nl, rows),
            pl.BlockSpec((1, d), lambda b, j: (0, 0)),
            pl.BlockSpec(w_out.shape, lambda b, j: (0, 0)),
            pl.BlockSpec(r_hi.shape, lambda b, j: (0, 0)),
            pl.BlockSpec(r_lo.shape, lambda b, j: (0, 0)),
        ],
        out_specs=[
            pl.BlockSpec((1, tm, d), lambda b, j: (b, j, 0)),
            pl.BlockSpec((tm, d + LANES), lambda b, j: (b * nt + j, 0)),
            pl.BlockSpec((N_EXPERTS, tm), lambda b, j: (0, b * nt + j)),
        ],
        out_shape=[
            jax.ShapeDtypeStruct((bsz, ltot, d), F32),
            jax.ShapeDtypeStruct((bsz * ltot, d + LANES), F32),
            jax.ShapeDtypeStruct((N_EXPERTS, bsz * ltot), F32),
        ],
        compiler_params=_cparams(("arbitrary", "arbitrary")),
        name="mlstm_out_proj",
    )(hg, x, ctx, modv, modv, modv, norm_g, w_out, r_hi, r_lo)


def _route_kernel(lg_ref, bias_ref, h2x_hbm, cls_ref, w_ref):
    del h2x_hbm
    s = _sigmoid(lg_ref[...])
    sel = s + bias_ref[...]
    srow = [s[e:e + 1, :] for e in range(N_EXPERTS)]
    row = [sel[e:e + 1, :] for e in range(N_EXPERTS)]
    best = jnp.zeros(row[0].shape, jnp.int32)
    best_score = None
    for g in range(N_GROUPS):
        r = row[g * E_PER_GROUP:(g + 1) * E_PER_GROUP]
        score = None
        for lo, hi in zip(PAIR_LO, PAIR_HI):
            pair = r[lo] + r[hi]
            score = pair if score is None else jnp.maximum(score, pair)
        if g == 0:
            best_score = score
        else:
            better = score > best_score
            best = jnp.where(better, g, best)
            best_score = jnp.where(better, score, best_score)
    gs = [row[i] for i in range(E_PER_GROUP)]
    gw = [srow[i] for i in range(E_PER_GROUP)]
    for g in range(1, N_GROUPS):
        hit = best == g
        gs = [jnp.where(hit, row[g * E_PER_GROUP + i], gs[i]) for i in range(E_PER_GROUP)]
        gw = [jnp.where(hit, srow[g * E_PER_GROUP + i], gw[i]) for i in range(E_PER_GROUP)]
    keep = []
    for i in range(E_PER_GROUP):
        beaten = jnp.zeros(best.shape, jnp.int32)
        for j in range(E_PER_GROUP):
            if j == i:
                continue
            wins = (gs[j] > gs[i]) | ((gs[j] == gs[i]) & (j < i))
            beaten = beaten + wins.astype(jnp.int32)
        keep.append(beaten < 2)
    pair_id = jnp.zeros(best.shape, jnp.int32)
    w_lo = jnp.zeros(best.shape, F32)
    w_hi = jnp.zeros(best.shape, F32)
    for p, (lo, hi) in enumerate(zip(PAIR_LO, PAIR_HI)):
        hit = keep[lo] & keep[hi]
        pair_id = jnp.where(hit, p, pair_id)
        w_lo = jnp.where(hit, gw[lo], w_lo)
        w_hi = jnp.where(hit, gw[hi], w_hi)
    tot = w_lo + w_hi
    cls_ref[...] = best * N_PAIRS + pair_id
    lanes_t = jnp.concatenate([w_lo / tot, w_hi / tot, jnp.zeros((LANES - 2, w_lo.shape[1]), F32)], axis=0)
    w_ref[...] = lanes_t.T


def _route(logits_t, router_bias, h2x):
    n = logits_t.shape[1]
    d = h2x.shape[1] - LANES
    tn = next(cand for cand in (2048, 1024, 512, 256, 128) if n % cand == 0)
    return pl.pallas_call(
        _route_kernel,
        grid=(n // tn,),
        in_specs=[
            pl.BlockSpec((N_EXPERTS, tn), lambda i: (0, i)),
            pl.BlockSpec((N_EXPERTS, 1), lambda i: (0, 0)),
            pl.BlockSpec(memory_space=pl.ANY),
        ],
        out_specs=[pl.BlockSpec((1, tn), lambda i: (0, i)), pl.BlockSpec((tn, LANES), lambda i: (i, d // LANES))],
        out_shape=[jax.ShapeDtypeStruct((1, n), jnp.int32), jax.ShapeDtypeStruct(h2x.shape, F32)],
        input_output_aliases={2: 1},
        compiler_params=_cparams(("arbitrary",)),
        name="moe_route",
    )(logits_t, router_bias.reshape(N_EXPERTS, 1).astype(F32), h2x)


def _moe_kernel(tm, dump_base, spare_rows, nu_ref, e0_ref, e1_ref, base_ref, nv_ref, tok_ref, dst_ref, h2_hbm,
                wg0_ref, wu0_ref, wd0_ref, wg1_ref, wu1_ref, wd1_ref, f_hbm,
                xb0, xb1, yb0, yb1, xbf, gsem, ssem, zsem, idle_sem):
    del e0_ref, e1_ref
    t = pl.program_id(0)
    n_used = nu_ref[0]
    d = xbf.shape[1]
    xbuf, ybuf = (xb0, xb1), (yb0, yb1)

    def gather_row(base, r, slot):
        tok = tok_ref[base + r]
        pltpu.make_async_copy(h2_hbm.at[pl.ds(tok, 1)], xbuf[slot].at[pl.ds(r, 1)], gsem.at[slot]).start()

    def scatter_row(base, nv, r, slot):
        row = jnp.where(r < nv, dst_ref[base + r], dump_base + r)
        pltpu.make_async_copy(ybuf[slot].at[pl.ds(r, 1)], f_hbm.at[pl.ds(row, 1)], ssem.at[slot]).start(priority=1)

    def wait_gather(slot):
        pltpu.make_async_copy(h2_hbm.at[pl.ds(0, tm)], xbuf[slot], gsem.at[slot]).wait()

    def wait_scatter(slot):
        pltpu.make_async_copy(ybuf[slot], f_hbm.at[pl.ds(0, tm)], ssem.at[slot]).wait()

    @pl.when(t == 0)
    def _():
        yb1[...] = jnp.zeros(yb1.shape, F32)
        fills = [pltpu.make_async_copy(yb1, f_hbm.at[pl.ds(row, tm)], zsem) for row in spare_rows]
        for fill in fills:
            fill.start()
        for fill in fills:
            fill.wait()
        base0 = base_ref[1]

        def first(r, carry):
            gather_row(base0, r, 0)
            return carry

        lax.fori_loop(0, tm, first, 0)

    def step(cur):
        nxt = 1 - cur
        base_next = base_ref[t + 2]
        base_prev = base_ref[t]
        nv_prev = nv_ref[t]

        @pl.when(t < n_used)
        def _():
            wait_gather(cur)
            xbf[...] = xbuf[cur][:, :d].astype(BF16)
            w = xbuf[cur][:, d:]

            for r in range(tm):
                gather_row(base_next, r, nxt)
            x = xbf[...]
            a0 = (_silu(_dot(x, wg0_ref[0])) * _dot(x, wu0_ref[0]) * w[:, 0:1]).astype(BF16)

            z = pl.semaphore_read(idle_sem)
            zero = lax.shift_right_logical(z.astype(jnp.uint32), jnp.uint32(32)).astype(jnp.int32)
            off = pl.multiple_of(zero * 16, 16)

            for r in range(tm):
                scatter_row(base_prev, nv_prev, r, nxt)
            x = xbf[pl.ds(off, tm), :]
            a1 = (_silu(_dot(x, wg1_ref[0])) * _dot(x, wu1_ref[0]) * w[:, 1:2]).astype(BF16)
            y = _dot(a0, wd0_ref[0]) + _dot(a1, wd1_ref[0])

            @pl.when(t >= 1)
            def _():
                wait_scatter(cur)

            ybuf[cur][...] = y

        @pl.when(t == n_used)
        def _():
            wait_gather(cur)
            wait_scatter(cur)

            def last(r, carry):
                scatter_row(base_prev, nv_prev, r, nxt)
                return carry

            lax.fori_loop(0, tm, last, 0)
            wait_scatter(nxt)

    @pl.when(t % 2 == 0)
    def _():
        step(0)

    @pl.when(t % 2 == 1)
    def _():
        step(1)


def _moe(h2x, cls, dst_of_token, dump_base, spare_rows, out_rows, wg, wu, wd):
    n = h2x.shape[0]
    d = h2x.shape[1] - LANES
    tm = MOE_TILE
    n_tiles = n // tm + N_CLASSES
    cls = cls.reshape(n)
    _, tok_sorted, dst_sorted = lax.sort((cls, jnp.arange(n, dtype=jnp.int32), dst_of_token), num_keys=1)
    tail = jnp.zeros((tm,), jnp.int32)
    tok_sorted = jnp.concatenate([tok_sorted, tail])
    dst_sorted = jnp.concatenate([dst_sorted, tail])
    counts = jnp.sum(cls[:, None] == jnp.arange(N_CLASSES, dtype=jnp.int32)[None, :], axis=0).astype(jnp.int32)
    tiles_per = (counts + tm - 1) // tm
    tile_end = jnp.cumsum(tiles_per)
    tile_start = tile_end - tiles_per
    first_sorted = jnp.cumsum(counts) - counts
    n_used = tile_end[-1]
    tile_id = jnp.arange(n_tiles, dtype=jnp.int32)
    tile_cls = jnp.searchsorted(tile_end, jnp.minimum(tile_id, n_used - 1), side="right").astype(jnp.int32)
    tile_cls = jnp.minimum(tile_cls, N_CLASSES - 1)
    group, pair = tile_cls // N_PAIRS, tile_cls % N_PAIRS
    e0 = group * E_PER_GROUP + jnp.asarray(PAIR_LO, jnp.int32)[pair]
    e1 = group * E_PER_GROUP + jnp.asarray(PAIR_HI, jnp.int32)[pair]
    in_class = (tile_id - tile_start[tile_cls]) * tm
    used = tile_id < n_used
    base = jnp.where(used, first_sorted[tile_cls] + in_class, 0)
    nv = jnp.where(used, jnp.clip(counts[tile_cls] - in_class, 0, tm), 0)
    guard = jnp.zeros((1,), jnp.int32)
    base = jnp.concatenate([guard, base, guard]).astype(jnp.int32)
    nv = jnp.concatenate([guard, nv, guard]).astype(jnp.int32)

    de = wg.shape[2]
    smem = pl.BlockSpec(memory_space=pltpu.SMEM)
    up0 = pl.BlockSpec((1, d, de), lambda i, nu, e0, e1, base, nv: (e0[i], 0, 0))
    up1 = pl.BlockSpec((1, d, de), lambda i, nu, e0, e1, base, nv: (e1[i], 0, 0))
    dn0 = pl.BlockSpec((1, de, d), lambda i, nu, e0, e1, base, nv: (e0[i], 0, 0))
    dn1 = pl.BlockSpec((1, de, d), lambda i, nu, e0, e1, base, nv: (e1[i], 0, 0))
    return pl.pallas_call(
        functools.partial(_moe_kernel, tm, dump_base, spare_rows),
        grid_spec=pltpu.PrefetchScalarGridSpec(
            num_scalar_prefetch=5,
            grid=(n_tiles,),
            in_specs=[
                smem, smem,
                pl.BlockSpec(memory_space=pl.ANY),
                up0, up0, dn0, up1, up1, dn1,
            ],
            out_specs=pl.BlockSpec(memory_space=pl.ANY),
            scratch_shapes=[
                pltpu.VMEM((tm, d + LANES), F32), pltpu.VMEM((tm, d + LANES), F32),
                pltpu.VMEM((tm, d), F32), pltpu.VMEM((tm, d), F32),
                pltpu.VMEM((tm, d), BF16),
                pltpu.SemaphoreType.DMA((2,)), pltpu.SemaphoreType.DMA((2,)), pltpu.SemaphoreType.DMA(()),
                pltpu.SemaphoreType.REGULAR(()),
            ],
        ),
        out_shape=jax.ShapeDtypeStruct((out_rows, d), F32),
        compiler_params=_cparams(("arbitrary",)),
        name="moe_experts",
    )(n_used.reshape(1).astype(jnp.int32), e0, e1, base, nv, tok_sorted, dst_sorted, h2x,
      wg, wu, wd, wg, wu, wd)


def _hproj_kernel(n_lat_tiles, rows, sl_ref, fl_ref, sc_ref_, fc_ref, g2_ref, sc_ref, sh_ref, ng_ref, lb_ref,
                  wq_ref, wzf_ref, wzb_ref, wi_ref, wg_ref,
                  s2_ref, q_ref, kf_ref, kb_ref, lff_ref, lfb_ref, i_ref, g_ref, xt):
    j = pl.program_id(1)
    d = xt.shape[1]
    g2 = g2_ref[0]

    @pl.when(j < n_lat_tiles)
    def _():
        for c in range(COL_TILE):
            s = sl_ref[0, :, c, :] + g2 * fl_ref[0, :, c, :]
            s2_ref[0, :, c, :] = s
            xt[c * rows:(c + 1) * rows, :] = s

    @pl.when(j >= n_lat_tiles)
    def _():
        xt[...] = sc_ref_[0] + g2 * fc_ref[0]

    h = (_rms(xt[...], ng_ref[...]) * (1.0 + sc_ref[0]) + sh_ref[0]).astype(BF16)
    q_ref[0] = _silu(_dot(h, wq_ref[...])).astype(BF16)
    i_ref[0] = _dot(h, wi_ref[...]).astype(BF16)
    g_ref[0] = _dot(h, wg_ref[...]).astype(BF16)
    for dd, (wz_ref, k_ref, lf_ref) in enumerate(((wzf_ref, kf_ref, lff_ref), (wzb_ref, kb_ref, lfb_ref))):
        z = _dot(h, wz_ref[...])
        log_lb = lb_ref[2 * dd:2 * dd + 1, :]
        log_1mlb = lb_ref[2 * dd + 1:2 * dd + 2, :]
        ls = _log_sigmoid(z)
        a = log_1mlb + ls
        lf_ref[0] = jnp.maximum(log_lb, a) + jnp.log1p(jnp.exp(-jnp.abs(log_lb - a)))
        k_ref[0] = jnp.exp(a - z).astype(BF16)


def _hproj(s1, f0, modv, rows_mod, norm_g, lbtab, ws, ll, lc):
    bsz, ltot, d = s1.shape
    rows = ll // GRID_W
    tm = COL_TILE * rows
    nl, nc = GRID_W // COL_TILE, lc // tm
    s_cm = s1.reshape(bsz, ltot // GRID_W, GRID_W, d)
    f_tok = f0.reshape(bsz, -1, d)
    f_cm = f0.reshape(bsz, -1, GRID_W, d)
    cm = pl.BlockSpec((1, rows, COL_TILE, d), lambda b, j: (b, 0, jnp.minimum(j, nl - 1), 0))
    ctx = pl.BlockSpec((1, tm, d), lambda b, j: (b, ll // tm + jnp.maximum(j - nl, 0), 0))
    wspec = pl.BlockSpec((d, d), lambda b, j: (0, 0))
    tok = pl.BlockSpec((1, tm, d), lambda b, j: (b, j, 0))

    def g2_map(b, j):
        row = jnp.where(j < nl, b, rows_mod - 1)
        return ((0 * rows_mod + row) * N_ADA + 5, 0, 0)

    outs = pl.pallas_call(
        functools.partial(_hproj_kernel, nl, rows),
        grid=(bsz, nl + nc),
        in_specs=[
            cm, cm, ctx, ctx,
            pl.BlockSpec((1, 1, d), g2_map),
            _mod_spec(d, 1, 1, nl, rows_mod), _mod_spec(d, 1, 0, nl, rows_mod),
            pl.BlockSpec((1, d), lambda b, j: (0, 0)),
            pl.BlockSpec((4, d), lambda b, j: (0, 0)),
            wspec, wspec, wspec, wspec, wspec,
        ],
        out_specs=[cm, tok, tok, tok, tok, tok, tok, tok],
        out_shape=[jax.ShapeDtypeStruct((bsz, rows, GRID_W, d), F32)]
        + [jax.ShapeDtypeStruct((bsz, ltot, d), dt) for dt in (BF16, BF16, BF16, F32, F32, BF16, BF16)],
        scratch_shapes=[pltpu.VMEM((tm, d), F32)],
        compiler_params=_cparams(("arbitrary", "arbitrary")),
        name="hgrn_in_proj",
    )(s_cm, f_cm, s1, f_tok, modv, modv, modv, norm_g, lbtab, *ws)
    return outs


def _hgrn_kernel(t, ll, lc, q_ref, kf_ref, kb_ref, lff_ref, lfb_ref, i_ref, g_ref, hg_ref, out_ref,
                 of, ob, inc_s, dec_s, qd_s):
    levels = []
    m = 2
    while m <= t:
        levels.append(m)
        m *= 2
    ti = lax.broadcasted_iota(jnp.int32, (t, t), 0)
    si = lax.broadcasted_iota(jnp.int32, (t, t), 1)
    xor = ti ^ si
    level = jnp.zeros((t, t), jnp.int32)
    for m in levels:
        level = level + (xor >= m // 2).astype(jnp.int32)
    feeds = ((si <= ti).astype(BF16), (si >= ti).astype(BF16))

    row = lax.broadcasted_iota(jnp.int32, (t, 1), 0)
    feeds_twice = tuple(jnp.concatenate([f, f], axis=1) for f in feeds)

    def block_ref(b, m, d):
        half = m // 2
        pos = half - 1 if d == 0 else half
        if m >= 8:
            dk = b.shape[1]
            b3 = b.reshape(t // m, m, dk)
            return jnp.broadcast_to(b3[:, pos:pos + 1, :], b3.shape).reshape(t, dk)
        r = row % m
        out = b
        for res in range(m):
            if res != pos:
                out = jnp.where(r == res, pltpu.roll(b, (res - pos) % t, 0), out)
        return out

    def neg_abs(x):
        sign = jnp.uint32(0x80000000)
        return lax.bitcast_convert_type(lax.bitcast_convert_type(x, jnp.uint32) | sign, F32)

    o_dir = (of, ob)

    def prepare(jobs, need_out):
        work = []
        for ci, d in jobs:
            rows = pl.ds(ci * t if isinstance(ci, int) else pl.multiple_of(ci * t, t), t)
            lf = (lff_ref if d == 0 else lfb_ref)[0, rows, :] * LOG2_E
            hi = lf.astype(BF16)
            lo = (lf - hi.astype(F32)).astype(BF16)
            b = _dot(feeds_twice[d], jnp.concatenate([hi, lo], axis=0))
            work.append((ci, d, rows, b))
        for ci, d, rows, b in work:
            b_tot = b[t - 1:t, :] if d == 0 else b[0:1, :]
            k = (kf_ref if d == 0 else kb_ref)[0, rows, :]
            inc_s[d, ci] = _dot_tn(i_ref[0, rows, :], k * jnp.exp2(b_tot - b).astype(BF16))
            dec_s[d, ci] = jnp.exp2(b_tot)
            if need_out:
                qd_s[d, rows, :] = q_ref[0, rows, :] * jnp.exp2(b).astype(BF16)
        if not need_out:
            return
        acc = []
        for ci, d, rows, b in work:
            acc.append(_dot_nt(q_ref[0, rows, :], (kf_ref if d == 0 else kb_ref)[0, rows, :]))
        for idx, m in enumerate(levels):
            for n, (ci, d, rows, b) in enumerate(work):
                e = jnp.exp2(neg_abs(b - block_ref(b, m, d))).astype(BF16)
                p = _dot_nt(q_ref[0, rows, :] * e, (kf_ref if d == 0 else kb_ref)[0, rows, :] * e)
                acc[n] = jnp.where(level == idx + 1, p, acc[n])
        for n, (ci, d, rows, b) in enumerate(work):
            o_dir[d][rows, :] = _dot(acc[n].astype(BF16) * feeds[d], i_ref[0, rows, :])

    ncc, ncl = lc // t, ll // t
    prepare([(ncl + i, d) for i in range(ncc) for d in range(2)], False)

    def body(i, carry):
        prepare([(2 * i + j, d) for j in range(2) for d in range(2)], True)
        return carry

    lax.fori_loop(0, ncl // 2, body, 0)

    for d in range(2):
        order = list(range(ncl, ncl + ncc)) + list(range(ncl))
        if d == 1:
            order = list(range(ncl + ncc - 1, ncl - 1, -1)) + list(range(ncl - 1, -1, -1))
        st = jnp.zeros(inc_s.shape[2:], F32)
        for ci in order:
            if ci < ncl:
                rows = pl.ds(ci * t, t)
                o_dir[d][rows, :] = o_dir[d][rows, :] + _dot_nt(qd_s[d, rows, :], st.astype(BF16))
            st = dec_s[d, ci] * st + inc_s[d, ci]

    def epilogue(i, carry):
        s = pl.multiple_of(i * t, t)
        y = _rms(of[pl.ds(s, t), :] + ob[pl.ds(s, t), :], hg_ref[...])
        out_ref[0, pl.ds(s, t), :] = (y * _silu(g_ref[0, pl.ds(s, t), :].astype(F32))).astype(BF16)
        return carry

    lax.fori_loop(0, ncl, epilogue, 0)


def _hgrn_scan(q, kf, kb, lff, lfb, iv, g, head_g, ll, lc):
    bsz, ltot, hk = q.shape
    nh = H_HEADS
    dk = hk // nh
    t = SCAN_CHUNK
    full = pl.BlockSpec((1, ltot, dk), lambda b, h: (b, 0, h))
    lat = pl.BlockSpec((1, ll, dk), lambda b, h: (b, 0, h))
    return pl.pallas_call(
        functools.partial(_hgrn_kernel, t, ll, lc),
        grid=(bsz, nh),
        in_specs=[full, full, full, full, full, full, lat, pl.BlockSpec((1, dk), lambda b, h: (0, h))],
        out_specs=lat,
        out_shape=jax.ShapeDtypeStruct((bsz, ll, hk), BF16),
        scratch_shapes=[pltpu.VMEM((ll, dk), F32), pltpu.VMEM((ll, dk), F32),
                        pltpu.VMEM((2, ltot // t, dk, dk), F32), pltpu.VMEM((2, ltot // t, 1, dk), F32),
                        pltpu.VMEM((2, ll, dk), BF16)],
        compiler_params=_cparams(("arbitrary", "arbitrary")),
        name="hgrn_scan",
    )(q, kf, kb, lff, lfb, iv, g, head_g.reshape(1, -1))


def _out1_kernel(rows, hg_ref, s_ref, g1_ref, sc_ref, sh_ref, ng_ref, wo_ref, rhi_ref, rlo_ref,
                 s3_ref, h2_ref, lg_ref, st):
    d = st.shape[1]
    y = g1_ref[0] * _dot(hg_ref[0], wo_ref[...])
    for c in range(COL_TILE):
        s = s_ref[0, :, c, :] + y[c * rows:(c + 1) * rows, :]
        s3_ref[0, :, c, :] = s
        st[c * rows:(c + 1) * rows, :] = s
    h2 = _rms(st[...], ng_ref[...]) * (1.0 + sc_ref[0]) + sh_ref[0]
    h2_ref[:, :h2.shape[1]] = h2
    h2_ref[:, h2.shape[1]:] = jnp.zeros((h2.shape[0], LANES), F32)
    lg_ref[...] = _router_logits_t(h2, rhi_ref, rlo_ref)


def _out1(hg, s2_cm, modv, rows_mod, norm_g, w_out, r_hi, r_lo):
    bsz, ll, hv = hg.shape
    rows = s2_cm.shape[1]
    d = s2_cm.shape[3]
    tm = COL_TILE * rows
    nl = GRID_W // COL_TILE
    cm = pl.BlockSpec((1, rows, COL_TILE, d), lambda b, j: (b, 0, j, 0))
    return pl.pallas_call(
        functools.partial(_out1_kernel, rows),
        grid=(bsz, nl),
        in_specs=[
            pl.BlockSpec((1, tm, hv), lambda b, j: (b, j, 0)),
            cm,
            _mod_spec(d, 1, 2, nl, rows_mod), _mod_spec(d, 1, 4, nl, rows_mod), _mod_spec(d, 1, 3, nl, rows_mod),
            pl.BlockSpec((1, d), lambda b, j: (0, 0)),
            pl.BlockSpec(w_out.shape, lambda b, j: (0, 0)),
            pl.BlockSpec(r_hi.shape, lambda b, j: (0, 0)),
            pl.BlockSpec(r_lo.shape, lambda b, j: (0, 0)),
        ],
        out_specs=[
            cm,
            pl.BlockSpec((tm, d + LANES), lambda b, j: (b * nl + j, 0)),
            pl.BlockSpec((N_EXPERTS, tm), lambda b, j: (0, b * nl + j)),
        ],
        out_shape=[
            jax.ShapeDtypeStruct(s2_cm.shape, F32),
            jax.ShapeDtypeStruct((bsz * ll, d + LANES), F32),
            jax.ShapeDtypeStruct((N_EXPERTS, bsz * ll), F32),
        ],
        scratch_shapes=[pltpu.VMEM((tm, d), F32)],
        compiler_params=_cparams(("arbitrary", "arbitrary")),
        name="hgrn_out_proj",
    )(hg, s2_cm, modv, modv, modv, norm_g, w_out, r_hi, r_lo)


def _final_kernel(s_ref, f_ref, g2_ref, fg_ref, o_ref):
    o_ref[0] = _rms(s_ref[0] + g2_ref[0] * f_ref[0], fg_ref[...])


def _final(s3, f1, modv, rows_mod, final_g, tm):
    bsz, ll, d = s3.shape
    nl = ll // tm
    tok = pl.BlockSpec((1, tm, d), lambda b, j: (b, j, 0))
    return pl.pallas_call(
        _final_kernel,
        grid=(bsz, nl),
        in_specs=[tok, tok, _mod_spec(d, 1, 5, nl, rows_mod), pl.BlockSpec((1, d), lambda b, j: (0, 0))],
        out_specs=tok,
        out_shape=jax.ShapeDtypeStruct((bsz, ll, d), F32),
        compiler_params=_cparams(("arbitrary", "arbitrary")),
        name="final_norm",
    )(s3, f1.reshape(bsz, -1, d), modv, final_g)


def kernel(x, c, ctx, c_ctx, ada_w, ada_b, norm_mix_g, norm_ffn_g, final_g, m_w_in, m_conv_w, m_conv_b, m_gate_b,
           m_head_g, m_w_out, h_w_in, h_lower_bounds, h_head_g, h_w_out, router_w, router_bias, e_w_gate,
           e_w_up, e_w_down):
    bsz, ll, d = x.shape
    lc = ctx.shape[1]
    ltot = ll + lc
    depth = ada_w.shape[0]
    assert depth == 2 and ll % GRID_W == 0 and lc % SCAN_CHUNK == 0 and ll % SCAN_CHUNK == 0
    rows = ll // GRID_W
    tm = COL_TILE * rows
    assert lc % tm == 0 and ltot % GRID_W == 0 and (bsz * ltot) % MOE_TILE == 0 and (bsz * ll) % MOE_TILE == 0

    rows_mod = 8 * ((bsz + 1 + 7) // 8)
    cc = jnp.zeros((rows_mod, d), F32).at[:bsz].set(c).at[rows_mod - 1].set(c_ctx)
    modv = _ada(cc, ada_w, ada_b).reshape(depth * rows_mod * N_ADA, 1, d)

    r_pad = jnp.zeros((d, LANES), F32).at[:, :N_EXPERTS].set(router_w)
    r_hi = r_pad.astype(BF16)
    r_lo = (r_pad - r_hi.astype(F32)).astype(BF16)
    wg_all, wu_all, wd_all = e_w_gate.astype(BF16), e_w_up.astype(BF16), e_w_down.astype(BF16)

    m_qk, m_v = m_conv_w.shape[2], m_head_g.shape[1]
    w_in = m_w_in[0].astype(BF16)
    w_gates = jnp.zeros((d, LANES), BF16).at[:, :4 * M_HEADS].set(w_in[:, m_qk + 2 * m_v:])
    qk, v, o, gates = _mproj(x, ctx, modv, rows_mod, norm_mix_g[0:1], w_in[:, :m_qk], w_in[:, m_qk:m_qk + m_v],
                             w_in[:, m_qk + m_v:m_qk + 2 * m_v], w_gates, tm)
    hg = _mlstm_scan(qk, v, o, gates, m_conv_w[0], m_conv_b[0], m_gate_b[0], m_head_g[0], ll, lc)
    s1, h2, lg = _out0(hg, x, ctx, modv, rows_mod, norm_ffn_g[0:1], m_w_out[0].astype(BF16), r_hi, r_lo, tm)
    cls, h2 = _route(lg, router_bias, h2)
    tok0 = jnp.arange(bsz * ltot, dtype=jnp.int32)
    dst0 = (tok0 // ltot) * (ltot + MOE_TILE) + tok0 % ltot
    spare0 = tuple(b * (ltot + MOE_TILE) + ltot for b in range(1, bsz))
    f0 = _moe(h2, cls, dst0, ltot, spare0, bsz * (ltot + MOE_TILE), wg_all[0], wu_all[0], wd_all[0])

    lbs = jnp.cumsum(jax.nn.softmax(h_lower_bounds.astype(F32), axis=0), axis=0)
    lb = (lbs - lbs[0])[1].reshape(2, -1)
    lbtab = jnp.stack([jnp.log(lb[0]), jnp.log1p(-lb[0]), jnp.log(lb[1]), jnp.log1p(-lb[1])])
    hw = h_w_in[0].astype(BF16)
    hk = lb.shape[1]
    ws = (hw[:, :hk], hw[:, hk:2 * hk], hw[:, 2 * hk:3 * hk], hw[:, 3 * hk:3 * hk + d], hw[:, 3 * hk + d:])
    s2_cm, q, kf, kb, lff, lfb, iv, g = _hproj(s1, f0, modv, rows_mod, norm_mix_g[1:2], lbtab, ws, ll, lc)
    hg1 = _hgrn_scan(q, kf, kb, lff, lfb, iv, g, h_head_g[0], ll, lc)
    s3_cm, h2b, lgb = _out1(hg1, s2_cm, modv, rows_mod, norm_ffn_g[1:2], h_w_out[0].astype(BF16), r_hi, r_lo)
    clsb, h2b = _route(lgb, router_bias, h2b)
    n1 = bsz * ll
    tokid = jnp.arange(n1, dtype=jnp.int32)
    within = tokid % ll
    raster = (tokid // ll) * (ll + MOE_TILE) + (within % rows) * GRID_W + within // rows
    spare1 = tuple(b * (ll + MOE_TILE) + ll for b in range(1, bsz))
    f1 = _moe(h2b, clsb, raster, ll, spare1, bsz * (ll + MOE_TILE), wg_all[1], wu_all[1], wd_all[1])

    return _final(s3_cm.reshape(bsz, ll, d), f1, modv, rows_mod, final_g.reshape(1, d), tm)
```

```python
import functools

import numpy as np
import jax
import jax.numpy as jnp
from jax import lax
from jax.experimental import pallas as pl
from jax.experimental.pallas import tpu as pltpu

F32 = jnp.float32
BF16 = jnp.bfloat16

EPS = 1e-6
LOG2_E = 1.4426950408889634
N_ADA = 6
GRID_W = 64
M_HEADS = 4
H_HEADS = 8
N_EXPERTS = 16
N_GROUPS = 4
E_PER_GROUP = N_EXPERTS // N_GROUPS
N_PAIRS = 6
N_CLASSES = N_GROUPS * N_PAIRS
PAIR_LO = (0, 0, 0, 1, 1, 2)
PAIR_HI = (1, 2, 3, 2, 3, 3)

LANES = 128
COL_TILE = 8
SCAN_CHUNK = 128
MOE_TILE = 256
VMEM_LIMIT = 56 * 1024 * 1024


def _cparams(sem):
    return pltpu.CompilerParams(dimension_semantics=sem, vmem_limit_bytes=VMEM_LIMIT)


def _dot(a, b):
    return jnp.dot(a, b, preferred_element_type=F32)


def _dot_nt(a, b):
    return lax.dot_general(a, b, (((1,), (1,)), ((), ())), preferred_element_type=F32)


def _dot_tn(a, b):
    return lax.dot_general(a, b, (((0,), (0,)), ((), ())), preferred_element_type=F32)


def _split_bf16(a):
    hi = a.astype(BF16)
    lo = (a - hi.astype(F32)).astype(BF16)
    return hi, lo


def _sigmoid(x):
    return 1.0 / (1.0 + jnp.exp(-x))


def _silu(x):
    return x * _sigmoid(x)


def _log_sigmoid(x):
    return jnp.minimum(x, 0.0) - jnp.log1p(jnp.exp(-jnp.abs(x)))


def _rms(x, g):
    return x * lax.rsqrt(jnp.mean(x * x, axis=-1, keepdims=True) + EPS) * g


def _ada_kernel(c_ref, w_ref, b_ref, o_ref):
    a = _silu(c_ref[...])
    a_hi, a_lo = _split_bf16(a)
    w_hi, w_lo = _split_bf16(w_ref[0])
    acc = _dot(a_hi, w_hi) + _dot(a_lo, w_hi) + _dot(a_hi, w_lo)
    o_ref[0] = acc + b_ref[0]


def _ada(cc, ada_w, ada_b):
    depth, d, n = ada_w.shape
    tn = 1024
    rows = cc.shape[0]
    return pl.pallas_call(
        _ada_kernel,
        grid=(depth, n // tn),
        in_specs=[
            pl.BlockSpec((rows, d), lambda l, j: (0, 0)),
            pl.BlockSpec((1, d, tn), lambda l, j: (l, 0, j)),
            pl.BlockSpec((1, 1, tn), lambda l, j: (l, 0, j)),
        ],
        out_specs=pl.BlockSpec((1, rows, tn), lambda l, j: (l, 0, j)),
        out_shape=jax.ShapeDtypeStruct((depth, rows, n), F32),
        compiler_params=_cparams(("arbitrary", "arbitrary")),
        name="ada_mod",
    )(cc, ada_w, ada_b.reshape(depth, 1, n))


def _mproj_kernel(n_lat_tiles, xl_ref, xc_ref, sc_ref, sh_ref, g_ref, wqk_ref, wv_ref, wo_ref, wg_ref,
                  qk_ref, v_ref, o_ref, gt_ref):
    j = pl.program_id(1)
    x = jnp.where(j < n_lat_tiles, xl_ref[0], xc_ref[0])
    h = (_rms(x, g_ref[...]) * (1.0 + sc_ref[0]) + sh_ref[0]).astype(BF16)
    qk_ref[0] = _dot(h, wqk_ref[...]).astype(BF16)
    v_ref[0] = _dot(h, wv_ref[...]).astype(BF16)
    o_ref[0] = _dot(h, wo_ref[...]).astype(BF16)
    gt_ref[0] = _dot(h, wg_ref[...])[:, :4 * M_HEADS]


def _mod_spec(d, layer, k, n_lat_tiles, rows):
    def imap(b, j):
        row = jnp.where(j < n_lat_tiles, b, rows - 1)
        return ((layer * rows + row) * N_ADA + k, 0, 0)
    return pl.BlockSpec((1, 1, d), imap)


def _mproj(x, ctx, modv, rows, norm_g, w_qk, w_v, w_o, w_g, tm):
    bsz, ll, d = x.shape
    lc = ctx.shape[1]
    nl, nc = ll // tm, lc // tm
    ltot = ll + lc
    wspec = lambda n: pl.BlockSpec((d, n), lambda b, j: (0, 0))
    tok = lambda n: pl.BlockSpec((1, tm, n), lambda b, j: (b, j, 0))
    return pl.pallas_call(
        functools.partial(_mproj_kernel, nl),
        grid=(bsz, nl + nc),
        in_specs=[
            pl.BlockSpec((1, tm, d), lambda b, j: (b, jnp.minimum(j, nl - 1), 0)),
            pl.BlockSpec((1, tm, d), lambda b, j: (b, jnp.maximum(j - nl, 0), 0)),
            _mod_spec(d, 0, 1, nl, rows), _mod_spec(d, 0, 0, nl, rows),
            pl.BlockSpec((1, d), lambda b, j: (0, 0)),
            wspec(w_qk.shape[1]), wspec(w_v.shape[1]), wspec(w_o.shape[1]), wspec(w_g.shape[1]),
        ],
        out_specs=[tok(w_qk.shape[1]), tok(w_v.shape[1]), tok(w_o.shape[1]), tok(4 * M_HEADS)],
        out_shape=[
            jax.ShapeDtypeStruct((bsz, ltot, w_qk.shape[1]), BF16),
            jax.ShapeDtypeStruct((bsz, ltot, w_v.shape[1]), BF16),
            jax.ShapeDtypeStruct((bsz, ltot, w_o.shape[1]), BF16),
            jax.ShapeDtypeStruct((bsz, ltot, 4 * M_HEADS), F32),
        ],
        compiler_params=_cparams(("arbitrary", "arbitrary")),
        name="mlstm_in_proj",
    )(x, ctx, modv, modv, norm_g, w_qk, w_v, w_o, w_g)


def _mlstm_kernel(t, ll, lc, hp, q_ref, k_ref, v_ref, o_ref, gc_ref, gr_ref, cwq_ref, cwk_ref, cbq_ref, cbk_ref,
                  gbc_ref, gbr_ref, hg_ref, out_ref, qs, ks, hf, hb, c_s, m_s):
    ltot = ll + lc
    dk = q_ref.shape[2] // hp
    dv = v_ref.shape[2] // hp
    row = lax.broadcasted_iota(jnp.int32, (ltot, 1), 0)
    first = (row == 0) | (row == ll)
    last = (row == ll - 1) | (row == ltot - 1)

    def conv(x_ref, w_ref, b_ref):
        x = x_ref[0].astype(F32)
        w = w_ref[...]
        xp = jnp.where(first, 0.0, pltpu.roll(x, 1, 0))
        xn = jnp.where(last, 0.0, pltpu.roll(x, ltot - 1, 0))
        return _silu(xp * w[0:1] + x * w[1:2] + xn * w[2:3] + b_ref[...])

    qs[...] = (conv(q_ref, cwq_ref, cbq_ref) * (dk ** -0.5)).astype(BF16)
    ks[...] = conv(k_ref, cwk_ref, cbk_ref).astype(BF16)

    c_s[...] = jnp.zeros(c_s.shape, F32)
    m_s[...] = jnp.zeros(m_s.shape, F32)

    ti = lax.broadcasted_iota(jnp.int32, (t, t), 0)
    si = lax.broadcasted_iota(jnp.int32, (t, t), 1)

    h_dir = (hf, hb)
    ones = jnp.ones((t, LANES), BF16)

    def chunks(starts):
        jobs = [(j, d) for j in range(hp) for d in range(2)]
        st = {}
        for j, d in jobs:
            rows = pl.ds(starts[d], t)
            gc = gc_ref[0, j, rows, :] + gbc_ref[j]
            gr = gr_ref[0, j, :, rows] + gbr_ref[j]
            ig_c, lf_c = gc[:, 2 * d:2 * d + 1], _log_sigmoid(gc[:, 2 * d + 1:2 * d + 2])
            ig_r, lf_r = gr[2 * d:2 * d + 1, :], _log_sigmoid(gr[2 * d + 1:2 * d + 2, :])
            seen = (si <= ti) if d == 0 else (si >= ti)
            seen_t = (ti <= si) if d == 0 else (ti >= si)
            b_c = jnp.sum(jnp.where(seen, lf_r, 0.0), axis=1, keepdims=True)
            b_r = jnp.sum(jnp.where(seen_t, lf_c, 0.0), axis=0, keepdims=True)
            total = jnp.sum(lf_r, axis=1, keepdims=True)
            st[j, d] = dict(rows=rows, ig_c=ig_c, ig_r=ig_r, b_c=b_c, b_r=b_r, total=total, seen=seen,
                            m_prev=m_s[2 * j + d][0:1, 0:1])
        for j, d in jobs:
            s = st[j, d]
            s["q"] = qs[s["rows"], j * dk:(j + 1) * dk]
            s["k"] = ks[s["rows"], j * dk:(j + 1) * dk]
            s["v"] = jnp.concatenate([v_ref[0, s["rows"], j * dv:(j + 1) * dv], ones], axis=1)
            s["qk"] = _dot_nt(s["q"], s["k"])
            s["qc"] = _dot_nt(s["q"], c_s[2 * j + d].astype(BF16))
        for j, d in jobs:
            s = st[j, d]
            log_d = jnp.where(s["seen"], s["b_c"] - s["b_r"] + s["ig_r"], -jnp.inf)
            log_inter = s["b_c"] + s["m_prev"]
            m_t = jnp.maximum(log_inter, jnp.max(log_d, axis=1, keepdims=True))
            s["w_ts"] = jnp.exp(log_d - m_t) * s["qk"]
            s["inter"] = jnp.exp(log_inter - m_t)
            s["m_t"] = m_t
        for j, d in jobs:
            s = st[j, d]
            s["num"] = _dot(s["w_ts"].astype(BF16), s["v"]) + s["inter"] * s["qc"]
        for j, d in jobs:
            s = st[j, d]
            den = s["num"][:, dv:]
            scale = 1.0 / jnp.maximum(jnp.abs(den), jnp.exp(-s["m_t"]))
            h_dir[d][s["rows"], j * dv:(j + 1) * dv] = s["num"][:, :dv] * jnp.concatenate([scale] * (dv // LANES), axis=1)
        for j, d in jobs:
            s = st[j, d]
            g_c = s["total"] - s["b_c"] + s["ig_c"]
            g_r = s["total"] - s["b_r"] + s["ig_r"]
            m_new = jnp.maximum(s["total"] + s["m_prev"], jnp.max(g_r, axis=1, keepdims=True))
            w_c = jnp.exp(g_c - m_new)
            decay = jnp.exp(s["total"] + s["m_prev"] - m_new)
            wv = (w_c * s["v"].astype(F32)).astype(BF16)
            c_s[2 * j + d] = decay * c_s[2 * j + d] + _dot_tn(wv, s["k"])
            m_s[2 * j + d] = jnp.broadcast_to(m_new, m_s.shape[1:])

    ncc, ncl = lc // t, ll // t
    for i in range(ncc):
        chunks((ll + i * t, ll + (ncc - 1 - i) * t))

    def body(i, carry):
        chunks((pl.multiple_of(i * t, t), pl.multiple_of((ncl - 1 - i) * t, t)))
        return carry

    lax.fori_loop(0, ncl, body, 0)

    def epilogue(i, carry):
        s = pl.multiple_of(i * t, t)
        for j in range(hp):
            cols = slice(j * dv, (j + 1) * dv)
            y = _rms(hf[pl.ds(s, t), cols] + hb[pl.ds(s, t), cols], hg_ref[:, cols])
            out_ref[0, pl.ds(s, t), cols] = (y * _sigmoid(o_ref[0, pl.ds(s, t), cols].astype(F32))).astype(BF16)
        return carry

    lax.fori_loop(0, ltot // t, epilogue, 0, unroll=2)


def _mlstm_scan(qk, v, o, gates, conv_w, conv_b, gate_b, head_g, ll, lc):
    bsz, ltot, _ = qk.shape
    nh = M_HEADS
    hp = 2
    dk = qk.shape[2] // (2 * nh)
    dv = v.shape[2] // nh
    t = SCAN_CHUNK
    g4 = gates.reshape(bsz, ltot, 4, nh).transpose(0, 3, 1, 2)
    g4t = g4.transpose(0, 1, 3, 2)
    gb = gate_b.reshape(4, nh).T
    return pl.pallas_call(
        functools.partial(_mlstm_kernel, t, ll, lc, hp),
        grid=(bsz, nh // hp),
        in_specs=[
            pl.BlockSpec((1, ltot, hp * dk), lambda b, h: (b, 0, h)),
            pl.BlockSpec((1, ltot, hp * dk), lambda b, h: (b, 0, nh // hp + h)),
            pl.BlockSpec((1, ltot, hp * dv), lambda b, h: (b, 0, h)),
            pl.BlockSpec((1, ltot, hp * dv), lambda b, h: (b, 0, h)),
            pl.BlockSpec((1, hp, ltot, 4), lambda b, h: (b, h, 0, 0)),
            pl.BlockSpec((1, hp, 4, ltot), lambda b, h: (b, h, 0, 0)),
            pl.BlockSpec((3, hp * dk), lambda b, h: (0, h)),
            pl.BlockSpec((3, hp * dk), lambda b, h: (0, nh // hp + h)),
            pl.BlockSpec((1, hp * dk), lambda b, h: (0, h)),
            pl.BlockSpec((1, hp * dk), lambda b, h: (0, nh // hp + h)),
            pl.BlockSpec((hp, 1, 4), lambda b, h: (h, 0, 0)),
            pl.BlockSpec((hp, 4, 1), lambda b, h: (h, 0, 0)),
            pl.BlockSpec((1, hp * dv), lambda b, h: (0, h)),
        ],
        out_specs=pl.BlockSpec((1, ltot, hp * dv), lambda b, h: (b, 0, h)),
        out_shape=jax.ShapeDtypeStruct((bsz, ltot, nh * dv), BF16),
        scratch_shapes=[
            pltpu.VMEM((ltot, hp * dk), BF16), pltpu.VMEM((ltot, hp * dk), BF16),
            pltpu.VMEM((ltot, hp * dv), F32), pltpu.VMEM((ltot, hp * dv), F32),
            pltpu.VMEM((2 * hp, dv + LANES, dk), F32), pltpu.VMEM((2 * hp, 8, LANES), F32),
        ],
        compiler_params=_cparams(("arbitrary", "arbitrary")),
        name="mlstm_scan",
    )(qk, qk, v, o, g4, g4t, conv_w, conv_w, conv_b.reshape(1, -1), conv_b.reshape(1, -1),
      gb.reshape(nh, 1, 4), gb.reshape(nh, 4, 1), head_g.reshape(1, -1))


def _router_logits_t(h2, rhi_ref, rlo_ref):
    h_hi, h_lo = _split_bf16(h2)
    lg = _dot(h_hi, rhi_ref[...]) + _dot(h_lo, rhi_ref[...]) + _dot(h_hi, rlo_ref[...])
    return lg.T[:N_EXPERTS, :]


def _out0_kernel(n_lat_tiles, hg_ref, xl_ref, xc_ref, g1_ref, sc_ref, sh_ref, ng_ref, wo_ref, rhi_ref, rlo_ref,
                 s_ref, h2_ref, lg_ref):
    j = pl.program_id(1)
    x = jnp.where(j < n_lat_tiles, xl_ref[0], xc_ref[0])
    s = x + g1_ref[0] * _dot(hg_ref[0], wo_ref[...])
    s_ref[0] = s
    h2 = _rms(s, ng_ref[...]) * (1.0 + sc_ref[0]) + sh_ref[0]
    h2_ref[:, :h2.shape[1]] = h2
    h2_ref[:, h2.shape[1]:] = jnp.zeros((h2.shape[0], LANES), F32)
    lg_ref[...] = _router_logits_t(h2, rhi_ref, rlo_ref)


def _out0(hg, x, ctx, modv, rows, norm_g, w_out, r_hi, r_lo, tm):
    bsz, ll, d = x.shape
    lc = ctx.shape[1]
    nl, nc = ll // tm, lc // tm
    nt = nl + nc
    ltot = ll + lc
    return pl.pallas_call(
        functools.partial(_out0_kernel, nl),
        grid=(bsz, nt),
        in_specs=[
            pl.BlockSpec((1, tm, hg.shape[2]), lambda b, j: (b, j, 0)),
            pl.BlockSpec((1, tm, d), lambda b, j: (b, jnp.minimum(j, nl - 1), 0)),
            pl.BlockSpec((1, tm, d), lambda b, j: (b, jnp.maximum(j - nl, 0), 0)),
            _mod_spec(d, 0, 2, nl, rows), _mod_spec(d, 0, 4, nl, rows), _mod_spec(d, 0, 3, nl, rows),
            pl.BlockSpec((1, d), lambda b, j: (0, 0)),
            pl.BlockSpec(w_out.shape, lambda b, j: (0, 0)),
            pl.BlockSpec(r_hi.shape, lambda b, j: (0, 0)),
            pl.BlockSpec(r_lo.shape, lambda b, j: (0, 0)),
        ],
        out_specs=[
            pl.BlockSpec((1, tm, d), lambda b, j: (b, j, 0)),
            pl.BlockSpec((tm, d + LANES), lambda b, j: (b * nt + j, 0)),
            pl.BlockSpec((N_EXPERTS, tm), lambda b, j: (0, b * nt + j)),
        ],
        out_shape=[
            jax.ShapeDtypeStruct((bsz, ltot, d), F32),
            jax.ShapeDtypeStruct((bsz * ltot, d + LANES), F32),
            jax.ShapeDtypeStruct((N_EXPERTS, bsz * ltot), F32),
        ],
        compiler_params=_cparams(("arbitrary", "arbitrary")),
        name="mlstm_out_proj",
    )(hg, x, ctx, modv, modv, modv, norm_g, w_out, r_hi, r_lo)


def _route_kernel(lg_ref, bias_ref, h2x_hbm, cls_ref, w_ref):
    del h2x_hbm
    s = _sigmoid(lg_ref[...])
    sel = s + bias_ref[...]
    srow = [s[e:e + 1, :] for e in range(N_EXPERTS)]
    row = [sel[e:e + 1, :] for e in range(N_EXPERTS)]
    best = jnp.zeros(row[0].shape, jnp.int32)
    best_score = None
    for g in range(N_GROUPS):
        r = row[g * E_PER_GROUP:(g + 1) * E_PER_GROUP]
        score = None
        for lo, hi in zip(PAIR_LO, PAIR_HI):
            pair = r[lo] + r[hi]
            score = pair if score is None else jnp.maximum(score, pair)
        if g == 0:
            best_score = score
        else:
            better = score > best_score
            best = jnp.where(better, g, best)
            best_score = jnp.where(better, score, best_score)
    gs = [row[i] for i in range(E_PER_GROUP)]
    gw = [srow[i] for i in range(E_PER_GROUP)]
    for g in range(1, N_GROUPS):
        hit = best == g
        gs = [jnp.where(hit, row[g * E_PER_GROUP + i], gs[i]) for i in range(E_PER_GROUP)]
        gw = [jnp.where(hit, srow[g * E_PER_GROUP + i], gw[i]) for i in range(E_PER_GROUP)]
    keep = []
    for i in range(E_PER_GROUP):
        beaten = jnp.zeros(best.shape, jnp.int32)
        for j in range(E_PER_GROUP):
            if j == i:
                continue
            wins = (gs[j] > gs[i]) | ((gs[j] == gs[i]) & (j < i))
            beaten = beaten + wins.astype(jnp.int32)
        keep.append(beaten < 2)
    pair_id = jnp.zeros(best.shape, jnp.int32)
    w_lo = jnp.zeros(best.shape, F32)
    w_hi = jnp.zeros(best.shape, F32)
    for p, (lo, hi) in enumerate(zip(PAIR_LO, PAIR_HI)):
        hit = keep[lo] & keep[hi]
        pair_id = jnp.where(hit, p, pair_id)
        w_lo = jnp.where(hit, gw[lo], w_lo)
        w_hi = jnp.where(hit, gw[hi], w_hi)
    tot = w_lo + w_hi
    cls_ref[...] = best * N_PAIRS + pair_id
    lanes_t = jnp.concatenate([w_lo / tot, w_hi / tot, jnp.zeros((LANES - 2, w_lo.shape[1]), F32)], axis=0)
    w_ref[...] = lanes_t.T


def _route(logits_t, router_bias, h2x):
    n = logits_t.shape[1]
    d = h2x.shape[1] - LANES
    tn = next(cand for cand in (2048, 1024, 512, 256, 128) if n % cand == 0)
    return pl.pallas_call(
        _route_kernel,
        grid=(n // tn,),
        in_specs=[
            pl.BlockSpec((N_EXPERTS, tn), lambda i: (0, i)),
            pl.BlockSpec((N_EXPERTS, 1), lambda i: (0, 0)),
            pl.BlockSpec(memory_space=pl.ANY),
        ],
        out_specs=[pl.BlockSpec((1, tn), lambda i: (0, i)), pl.BlockSpec((tn, LANES), lambda i: (i, d // LANES))],
        out_shape=[jax.ShapeDtypeStruct((1, n), jnp.int32), jax.ShapeDtypeStruct(h2x.shape, F32)],
        input_output_aliases={2: 1},
        compiler_params=_cparams(("arbitrary",)),
        name="moe_route",
    )(logits_t, router_bias.reshape(N_EXPERTS, 1).astype(F32), h2x)


def _moe_kernel(tm, dump_base, spare_rows, nu_ref, e0_ref, e1_ref, base_ref, nv_ref, tok_ref, dst_ref, h2_hbm,
                wg0_ref, wu0_ref, wd0_ref, wg1_ref, wu1_ref, wd1_ref, f_hbm,
                xb0, xb1, yb0, yb1, xbf, gsem, ssem, zsem, idle_sem):
    del e0_ref, e1_ref
    t = pl.program_id(0)
    n_used = nu_ref[0]
    d = xbf.shape[1]
    xbuf, ybuf = (xb0, xb1), (yb0, yb1)

    def gather_row(base, r, slot):
        tok = tok_ref[base + r]
        pltpu.make_async_copy(h2_hbm.at[pl.ds(tok, 1)], xbuf[slot].at[pl.ds(r, 1)], gsem.at[slot]).start()

    def scatter_row(base, nv, r, slot):
        row = jnp.where(r < nv, dst_ref[base + r], dump_base + r)
        pltpu.make_async_copy(ybuf[slot].at[pl.ds(r, 1)], f_hbm.at[pl.ds(row, 1)], ssem.at[slot]).start(priority=1)

    def wait_gather(slot):
        pltpu.make_async_copy(h2_hbm.at[pl.ds(0, tm)], xbuf[slot], gsem.at[slot]).wait()

    def wait_scatter(slot):
        pltpu.make_async_copy(ybuf[slot], f_hbm.at[pl.ds(0, tm)], ssem.at[slot]).wait()

    @pl.when(t == 0)
    def _():
        yb1[...] = jnp.zeros(yb1.shape, F32)
        fills = [pltpu.make_async_copy(yb1, f_hbm.at[pl.ds(row, tm)], zsem) for row in spare_rows]
        for fill in fills:
            fill.start()
        for fill in fills:
            fill.wait()
        base0 = base_ref[1]

        def first(r, carry):
            gather_row(base0, r, 0)
            return carry

        lax.fori_loop(0, tm, first, 0)

    def step(cur):
        nxt = 1 - cur
        base_next = base_ref[t + 2]
        base_prev = base_ref[t]
        nv_prev = nv_ref[t]

        @pl.when(t < n_used)
        def _():
            wait_gather(cur)
            xbf[...] = xbuf[cur][:, :d].astype(BF16)
            w = xbuf[cur][:, d:]

            for r in range(tm):
                gather_row(base_next, r, nxt)
            x = xbf[...]
            a0 = (_silu(_dot(x, wg0_ref[0, 0])) * _dot(x, wu0_ref[0, 0]) * w[:, 0:1]).astype(BF16)

            z = pl.semaphore_read(idle_sem)
            zero = lax.shift_right_logical(z.astype(jnp.uint32), jnp.uint32(32)).astype(jnp.int32)
            off = pl.multiple_of(zero * 16, 16)

            for r in range(tm):
                scatter_row(base_prev, nv_prev, r, nxt)
            x = xbf[pl.ds(off, tm), :]
            a1 = (_silu(_dot(x, wg1_ref[0, 0])) * _dot(x, wu1_ref[0, 0]) * w[:, 1:2]).astype(BF16)
            y = _dot(a0, wd0_ref[0, 0]) + _dot(a1, wd1_ref[0, 0])

            @pl.when(t >= 1)
            def _():
                wait_scatter(cur)

            ybuf[cur][...] = y

        @pl.when(t == n_used)
        def _():
            wait_gather(cur)
            wait_scatter(cur)

            def last(r, carry):
                scatter_row(base_prev, nv_prev, r, nxt)
                return carry

            lax.fori_loop(0, tm, last, 0)
            wait_scatter(nxt)

    @pl.when(t % 2 == 0)
    def _():
        step(0)

    @pl.when(t % 2 == 1)
    def _():
        step(1)


def _moe(h2x, cls, dst_of_token, dump_base, spare_rows, out_rows, layer, wg, wu, wd):
    n = h2x.shape[0]
    d = h2x.shape[1] - LANES
    tm = MOE_TILE
    n_tiles = n // tm + N_CLASSES
    cls = cls.reshape(n)
    _, tok_sorted, dst_sorted = lax.sort((cls, jnp.arange(n, dtype=jnp.int32), dst_of_token), num_keys=1)
    tail = jnp.zeros((tm,), jnp.int32)
    tok_sorted = jnp.concatenate([tok_sorted, tail])
    dst_sorted = jnp.concatenate([dst_sorted, tail])
    counts = jnp.sum(cls[:, None] == jnp.arange(N_CLASSES, dtype=jnp.int32)[None, :], axis=0).astype(jnp.int32)
    tiles_per = (counts + tm - 1) // tm
    tile_end = jnp.cumsum(tiles_per)
    tile_start = tile_end - tiles_per
    first_sorted = jnp.cumsum(counts) - counts
    n_used = tile_end[-1]
    tile_id = jnp.arange(n_tiles, dtype=jnp.int32)
    tile_cls = jnp.searchsorted(tile_end, jnp.minimum(tile_id, n_used - 1), side="right").astype(jnp.int32)
    tile_cls = jnp.minimum(tile_cls, N_CLASSES - 1)
    group, pair = tile_cls // N_PAIRS, tile_cls % N_PAIRS
    e0 = group * E_PER_GROUP + jnp.asarray(PAIR_LO, jnp.int32)[pair]
    e1 = group * E_PER_GROUP + jnp.asarray(PAIR_HI, jnp.int32)[pair]
    in_class = (tile_id - tile_start[tile_cls]) * tm
    used = tile_id < n_used
    base = jnp.where(used, first_sorted[tile_cls] + in_class, 0)
    nv = jnp.where(used, jnp.clip(counts[tile_cls] - in_class, 0, tm), 0)
    guard = jnp.zeros((1,), jnp.int32)
    base = jnp.concatenate([guard, base, guard]).astype(jnp.int32)
    nv = jnp.concatenate([guard, nv, guard]).astype(jnp.int32)

    de = wg.shape[3]
    smem = pl.BlockSpec(memory_space=pltpu.SMEM)
    up0 = pl.BlockSpec((1, 1, d, de), lambda i, nu, e0, e1, base, nv: (layer, e0[i], 0, 0))
    up1 = pl.BlockSpec((1, 1, d, de), lambda i, nu, e0, e1, base, nv: (layer, e1[i], 0, 0))
    dn0 = pl.BlockSpec((1, 1, de, d), lambda i, nu, e0, e1, base, nv: (layer, e0[i], 0, 0))
    dn1 = pl.BlockSpec((1, 1, de, d), lambda i, nu, e0, e1, base, nv: (layer, e1[i], 0, 0))
    return pl.pallas_call(
        functools.partial(_moe_kernel, tm, dump_base, spare_rows),
        grid_spec=pltpu.PrefetchScalarGridSpec(
            num_scalar_prefetch=5,
            grid=(n_tiles,),
            in_specs=[
                smem, smem,
                pl.BlockSpec(memory_space=pl.ANY),
                up0, up0, dn0, up1, up1, dn1,
            ],
            out_specs=pl.BlockSpec(memory_space=pl.ANY),
            scratch_shapes=[
                pltpu.VMEM((tm, d + LANES), F32), pltpu.VMEM((tm, d + LANES), F32),
                pltpu.VMEM((tm, d), F32), pltpu.VMEM((tm, d), F32),
                pltpu.VMEM((tm, d), BF16),
                pltpu.SemaphoreType.DMA((2,)), pltpu.SemaphoreType.DMA((2,)), pltpu.SemaphoreType.DMA(()),
                pltpu.SemaphoreType.REGULAR(()),
            ],
        ),
        out_shape=jax.ShapeDtypeStruct((out_rows, d), F32),
        compiler_params=_cparams(("arbitrary",)),
        name="moe_experts",
    )(n_used.reshape(1).astype(jnp.int32), e0, e1, base, nv, tok_sorted, dst_sorted, h2x,
      wg, wu, wd, wg, wu, wd)


def _hproj_kernel(n_lat_tiles, rows, sl_ref, fl_ref, sc_ref_, fc_ref, g2_ref, sc_ref, sh_ref, ng_ref, lb_ref,
                  wq_ref, wzf_ref, wzb_ref, wi_ref, wg_ref,
                  s2_ref, q_ref, kf_ref, kb_ref, lff_ref, lfb_ref, i_ref, g_ref, xt):
    j = pl.program_id(1)
    d = xt.shape[1]
    g2 = g2_ref[0]

    @pl.when(j < n_lat_tiles)
    def _():
        for c in range(COL_TILE):
            s = sl_ref[0, :, c, :] + g2 * fl_ref[0, :, c, :]
            s2_ref[0, :, c, :] = s
            xt[c * rows:(c + 1) * rows, :] = s

    @pl.when(j >= n_lat_tiles)
    def _():
        xt[...] = sc_ref_[0] + g2 * fc_ref[0]

    h = (_rms(xt[...], ng_ref[...]) * (1.0 + sc_ref[0]) + sh_ref[0]).astype(BF16)
    q_ref[0] = _silu(_dot(h, wq_ref[...])).astype(BF16)
    i_ref[0] = _dot(h, wi_ref[...]).astype(BF16)
    g_ref[0] = _dot(h, wg_ref[...]).astype(BF16)
    for dd, (wz_ref, k_ref, lf_ref) in enumerate(((wzf_ref, kf_ref, lff_ref), (wzb_ref, kb_ref, lfb_ref))):
        z = _dot(h, wz_ref[...])
        log_lb = lb_ref[2 * dd:2 * dd + 1, :]
        log_1mlb = lb_ref[2 * dd + 1:2 * dd + 2, :]
        ls = _log_sigmoid(z)
        a = log_1mlb + ls
        lf_ref[0] = jnp.maximum(log_lb, a) + jnp.log1p(jnp.exp(-jnp.abs(log_lb - a)))
        k_ref[0] = jnp.exp(a - z).astype(BF16)


def _hproj(s1, f0, modv, rows_mod, norm_g, lbtab, ws, ll, lc):
    bsz, ltot, d = s1.shape
    rows = ll // GRID_W
    tm = COL_TILE * rows
    nl, nc = GRID_W // COL_TILE, lc // tm
    s_cm = s1.reshape(bsz, ltot // GRID_W, GRID_W, d)
    f_tok = f0.reshape(bsz, -1, d)
    f_cm = f0.reshape(bsz, -1, GRID_W, d)
    cm = pl.BlockSpec((1, rows, COL_TILE, d), lambda b, j: (b, 0, jnp.minimum(j, nl - 1), 0))
    ctx = pl.BlockSpec((1, tm, d), lambda b, j: (b, ll // tm + jnp.maximum(j - nl, 0), 0))
    wspec = pl.BlockSpec((d, d), lambda b, j: (0, 0))
    tok = pl.BlockSpec((1, tm, d), lambda b, j: (b, j, 0))

    def g2_map(b, j):
        row = jnp.where(j < nl, b, rows_mod - 1)
        return ((0 * rows_mod + row) * N_ADA + 5, 0, 0)

    outs = pl.pallas_call(
        functools.partial(_hproj_kernel, nl, rows),
        grid=(bsz, nl + nc),
        in_specs=[
            cm, cm, ctx, ctx,
            pl.BlockSpec((1, 1, d), g2_map),
            _mod_spec(d, 1, 1, nl, rows_mod), _mod_spec(d, 1, 0, nl, rows_mod),
            pl.BlockSpec((1, d), lambda b, j: (0, 0)),
            pl.BlockSpec((4, d), lambda b, j: (0, 0)),
            wspec, wspec, wspec, wspec, wspec,
        ],
        out_specs=[cm, tok, tok, tok, tok, tok, tok, tok],
        out_shape=[jax.ShapeDtypeStruct((bsz, rows, GRID_W, d), F32)]
        + [jax.ShapeDtypeStruct((bsz, ltot, d), dt) for dt in (BF16, BF16, BF16, F32, F32, BF16, BF16)],
        scratch_shapes=[pltpu.VMEM((tm, d), F32)],
        compiler_params=_cparams(("arbitrary", "arbitrary")),
        name="hgrn_in_proj",
    )(s_cm, f_cm, s1, f_tok, modv, modv, modv, norm_g, lbtab, *ws)
    return outs


def _hgrn_kernel(t, ll, lc, q_ref, kf_ref, kb_ref, lff_ref, lfb_ref, i_ref, g_ref, hg_ref, out_ref,
                 of, ob, inc_s, dec_s, qd_s):
    levels = []
    m = 2
    while m <= t:
        levels.append(m)
        m *= 2
    ti = lax.broadcasted_iota(jnp.int32, (t, t), 0)
    si = lax.broadcasted_iota(jnp.int32, (t, t), 1)
    xor = ti ^ si
    level = jnp.zeros((t, t), jnp.int32)
    for m in levels:
        level = level + (xor >= m // 2).astype(jnp.int32)
    feeds = ((si <= ti).astype(BF16), (si >= ti).astype(BF16))

    row = lax.broadcasted_iota(jnp.int32, (t, 1), 0)
    feeds_twice = tuple(jnp.concatenate([f, f], axis=1) for f in feeds)

    def block_ref(b, m, d):
        half = m // 2
        pos = half - 1 if d == 0 else half
        if m >= 8:
            dk = b.shape[1]
            b3 = b.reshape(t // m, m, dk)
            return jnp.broadcast_to(b3[:, pos:pos + 1, :], b3.shape).reshape(t, dk)
        r = row % m
        out = b
        for res in range(m):
            if res != pos:
                out = jnp.where(r == res, pltpu.roll(b, (res - pos) % t, 0), out)
        return out

    def neg_abs(x):
        sign = jnp.uint32(0x80000000)
        return lax.bitcast_convert_type(lax.bitcast_convert_type(x, jnp.uint32) | sign, F32)

    o_dir = (of, ob)

    def prepare(jobs, need_out):
        work = []
        for ci, d in jobs:
            rows = pl.ds(ci * t if isinstance(ci, int) else pl.multiple_of(ci * t, t), t)
            lf = (lff_ref if d == 0 else lfb_ref)[0, rows, :] * LOG2_E
            hi = lf.astype(BF16)
            lo = (lf - hi.astype(F32)).astype(BF16)
            b = _dot(feeds_twice[d], jnp.concatenate([hi, lo], axis=0))
            work.append((ci, d, rows, b))
        for ci, d, rows, b in work:
            b_tot = b[t - 1:t, :] if d == 0 else b[0:1, :]
            k = (kf_ref if d == 0 else kb_ref)[0, rows, :]
            inc_s[d, ci] = _dot_tn(i_ref[0, rows, :], k * jnp.exp2(b_tot - b).astype(BF16))
            dec_s[d, ci] = jnp.exp2(b_tot)
            if need_out:
                qd_s[d, rows, :] = q_ref[0, rows, :] * jnp.exp2(b).astype(BF16)
        if not need_out:
            return
        acc = []
        for ci, d, rows, b in work:
            acc.append(_dot_nt(q_ref[0, rows, :], (kf_ref if d == 0 else kb_ref)[0, rows, :]))
        for idx, m in enumerate(levels):
            for n, (ci, d, rows, b) in enumerate(work):
                e = jnp.exp2(neg_abs(b - block_ref(b, m, d))).astype(BF16)
                p = _dot_nt(q_ref[0, rows, :] * e, (kf_ref if d == 0 else kb_ref)[0, rows, :] * e)
                acc[n] = jnp.where(level == idx + 1, p, acc[n])
        for n, (ci, d, rows, b) in enumerate(work):
            o_dir[d][rows, :] = _dot(acc[n].astype(BF16) * feeds[d], i_ref[0, rows, :])

    ncc, ncl = lc // t, ll // t
    prepare([(ncl + i, d) for i in range(ncc) for d in range(2)], False)

    group = 4 if ncl % 4 == 0 else 2

    def body(i, carry):
        prepare([(group * i + j, d) for j in range(group) for d in range(2)], True)
        return carry

    lax.fori_loop(0, ncl // group, body, 0)

    for d in range(2):
        order = list(range(ncl, ncl + ncc)) + list(range(ncl))
        if d == 1:
            order = list(range(ncl + ncc - 1, ncl - 1, -1)) + list(range(ncl - 1, -1, -1))
        st = jnp.zeros(inc_s.shape[2:], F32)
        for ci in order:
            if ci < ncl:
                rows = pl.ds(ci * t, t)
                o_dir[d][rows, :] = o_dir[d][rows, :] + _dot_nt(qd_s[d, rows, :], st.astype(BF16))
            st = dec_s[d, ci] * st + inc_s[d, ci]

    def epilogue(i, carry):
        s = pl.multiple_of(i * t, t)
        y = _rms(of[pl.ds(s, t), :] + ob[pl.ds(s, t), :], hg_ref[...])
        out_ref[0, pl.ds(s, t), :] = (y * _silu(g_ref[0, pl.ds(s, t), :].astype(F32))).astype(BF16)
        return carry

    lax.fori_loop(0, ncl, epilogue, 0, unroll=4)


def _hgrn_scan(q, kf, kb, lff, lfb, iv, g, head_g, ll, lc):
    bsz, ltot, hk = q.shape
    nh = H_HEADS
    dk = hk // nh
    t = SCAN_CHUNK
    full = pl.BlockSpec((1, ltot, dk), lambda b, h: (b, 0, h))
    lat = pl.BlockSpec((1, ll, dk), lambda b, h: (b, 0, h))
    return pl.pallas_call(
        functools.partial(_hgrn_kernel, t, ll, lc),
        grid=(bsz, nh),
        in_specs=[full, full, full, full, full, full, lat, pl.BlockSpec((1, dk), lambda b, h: (0, h))],
        out_specs=lat,
        out_shape=jax.ShapeDtypeStruct((bsz, ll, hk), BF16),
        scratch_shapes=[pltpu.VMEM((ll, dk), F32), pltpu.VMEM((ll, dk), F32),
                        pltpu.VMEM((2, ltot // t, dk, dk), F32), pltpu.VMEM((2, ltot // t, 1, dk), F32),
                        pltpu.VMEM((2, ll, dk), BF16)],
        compiler_params=_cparams(("arbitrary", "arbitrary")),
        name="hgrn_scan",
    )(q, kf, kb, lff, lfb, iv, g, head_g.reshape(1, -1))


def _out1_kernel(rows, hg_ref, s_ref, g1_ref, sc_ref, sh_ref, ng_ref, wo_ref, rhi_ref, rlo_ref,
                 s3_ref, h2_ref, lg_ref, st):
    d = st.shape[1]
    y = g1_ref[0] * _dot(hg_ref[0], wo_ref[...])
    for c in range(COL_TILE):
        s = s_ref[0, :, c, :] + y[c * rows:(c + 1) * rows, :]
        s3_ref[0, :, c, :] = s
        st[c * rows:(c + 1) * rows, :] = s
    h2 = _rms(st[...], ng_ref[...]) * (1.0 + sc_ref[0]) + sh_ref[0]
    h2_ref[:, :h2.shape[1]] = h2
    h2_ref[:, h2.shape[1]:] = jnp.zeros((h2.shape[0], LANES), F32)
    lg_ref[...] = _router_logits_t(h2, rhi_ref, rlo_ref)


def _out1(hg, s2_cm, modv, rows_mod, norm_g, w_out, r_hi, r_lo):
    bsz, ll, hv = hg.shape
    rows = s2_cm.shape[1]
    d = s2_cm.shape[3]
    tm = COL_TILE * rows
    nl = GRID_W // COL_TILE
    cm = pl.BlockSpec((1, rows, COL_TILE, d), lambda b, j: (b, 0, j, 0))
    return pl.pallas_call(
        functools.partial(_out1_kernel, rows),
        grid=(bsz, nl),
        in_specs=[
            pl.BlockSpec((1, tm, hv), lambda b, j: (b, j, 0)),
            cm,
            _mod_spec(d, 1, 2, nl, rows_mod), _mod_spec(d, 1, 4, nl, rows_mod), _mod_spec(d, 1, 3, nl, rows_mod),
            pl.BlockSpec((1, d), lambda b, j: (0, 0)),
            pl.BlockSpec(w_out.shape, lambda b, j: (0, 0)),
            pl.BlockSpec(r_hi.shape, lambda b, j: (0, 0)),
            pl.BlockSpec(r_lo.shape, lambda b, j: (0, 0)),
        ],
        out_specs=[
            cm,
            pl.BlockSpec((tm, d + LANES), lambda b, j: (b * nl + j, 0)),
            pl.BlockSpec((N_EXPERTS, tm), lambda b, j: (0, b * nl + j)),
        ],
        out_shape=[
            jax.ShapeDtypeStruct(s2_cm.shape, F32),
            jax.ShapeDtypeStruct((bsz * ll, d + LANES), F32),
            jax.ShapeDtypeStruct((N_EXPERTS, bsz * ll), F32),
        ],
        scratch_shapes=[pltpu.VMEM((tm, d), F32)],
        compiler_params=_cparams(("arbitrary", "arbitrary")),
        name="hgrn_out_proj",
    )(hg, s2_cm, modv, modv, modv, norm_g, w_out, r_hi, r_lo)


def _final_kernel(s_ref, f_ref, g2_ref, fg_ref, o_ref):
    o_ref[0] = _rms(s_ref[0] + g2_ref[0] * f_ref[0], fg_ref[...])


def _final(s3, f1, modv, rows_mod, final_g, tm):
    bsz, ll, d = s3.shape
    nl = ll // tm
    tok = pl.BlockSpec((1, tm, d), lambda b, j: (b, j, 0))
    return pl.pallas_call(
        _final_kernel,
        grid=(bsz, nl),
        in_specs=[tok, tok, _mod_spec(d, 1, 5, nl, rows_mod), pl.BlockSpec((1, d), lambda b, j: (0, 0))],
        out_specs=tok,
        out_shape=jax.ShapeDtypeStruct((bsz, ll, d), F32),
        compiler_params=_cparams(("arbitrary", "arbitrary")),
        name="final_norm",
    )(s3, f1.reshape(bsz, -1, d), modv, final_g)


def kernel(x, c, ctx, c_ctx, ada_w, ada_b, norm_mix_g, norm_ffn_g, final_g, m_w_in, m_conv_w, m_conv_b, m_gate_b,
           m_head_g, m_w_out, h_w_in, h_lower_bounds, h_head_g, h_w_out, router_w, router_bias, e_w_gate,
           e_w_up, e_w_down):
    bsz, ll, d = x.shape
    lc = ctx.shape[1]
    ltot = ll + lc
    depth = ada_w.shape[0]
    assert depth == 2 and ll % GRID_W == 0 and lc % SCAN_CHUNK == 0 and ll % SCAN_CHUNK == 0
    rows = ll // GRID_W
    tm = COL_TILE * rows
    assert lc % tm == 0 and ltot % GRID_W == 0 and (bsz * ltot) % MOE_TILE == 0 and (bsz * ll) % MOE_TILE == 0

    rows_mod = 8 * ((bsz + 1 + 7) // 8)
    cc = jnp.zeros((rows_mod, d), F32).at[:bsz].set(c).at[rows_mod - 1].set(c_ctx)
    modv = _ada(cc, ada_w, ada_b).reshape(depth * rows_mod * N_ADA, 1, d)

    r_pad = jnp.zeros((d, LANES), F32).at[:, :N_EXPERTS].set(router_w)
    r_hi = r_pad.astype(BF16)
    r_lo = (r_pad - r_hi.astype(F32)).astype(BF16)
    wg_all, wu_all, wd_all = e_w_gate.astype(BF16), e_w_up.astype(BF16), e_w_down.astype(BF16)

    m_qk, m_v = m_conv_w.shape[2], m_head_g.shape[1]
    w_in = m_w_in[0].astype(BF16)
    w_gates = jnp.zeros((d, LANES), BF16).at[:, :4 * M_HEADS].set(w_in[:, m_qk + 2 * m_v:])
    qk, v, o, gates = _mproj(x, ctx, modv, rows_mod, norm_mix_g[0:1], w_in[:, :m_qk], w_in[:, m_qk:m_qk + m_v],
                             w_in[:, m_qk + m_v:m_qk + 2 * m_v], w_gates, tm)
    hg = _mlstm_scan(qk, v, o, gates, m_conv_w[0], m_conv_b[0], m_gate_b[0], m_head_g[0], ll, lc)
    s1, h2, lg = _out0(hg, x, ctx, modv, rows_mod, norm_ffn_g[0:1], m_w_out[0].astype(BF16), r_hi, r_lo, tm)
    cls, h2 = _route(lg, router_bias, h2)
    tok0 = jnp.arange(bsz * ltot, dtype=jnp.int32)
    dst0 = (tok0 // ltot) * (ltot + MOE_TILE) + tok0 % ltot
    spare0 = tuple(b * (ltot + MOE_TILE) + ltot for b in range(1, bsz))
    f0 = _moe(h2, cls, dst0, ltot, spare0, bsz * (ltot + MOE_TILE), 0, wg_all, wu_all, wd_all)

    lbs = jnp.cumsum(jax.nn.softmax(h_lower_bounds.astype(F32), axis=0), axis=0)
    lb = (lbs - lbs[0])[1].reshape(2, -1)
    lbtab = jnp.stack([jnp.log(lb[0]), jnp.log1p(-lb[0]), jnp.log(lb[1]), jnp.log1p(-lb[1])])
    hw = h_w_in[0].astype(BF16)
    hk = lb.shape[1]
    ws = (hw[:, :hk], hw[:, hk:2 * hk], hw[:, 2 * hk:3 * hk], hw[:, 3 * hk:3 * hk + d], hw[:, 3 * hk + d:])
    s2_cm, q, kf, kb, lff, lfb, iv, g = _hproj(s1, f0, modv, rows_mod, norm_mix_g[1:2], lbtab, ws, ll, lc)
    hg1 = _hgrn_scan(q, kf, kb, lff, lfb, iv, g, h_head_g[0], ll, lc)
    s3_cm, h2b, lgb = _out1(hg1, s2_cm, modv, rows_mod, norm_ffn_g[1:2], h_w_out[0].astype(BF16), r_hi, r_lo)
    clsb, h2b = _route(lgb, router_bias, h2b)
    n1 = bsz * ll
    tokid = jnp.arange(n1, dtype=jnp.int32)
    within = tokid % ll
    raster = (tokid // ll) * (ll + MOE_TILE) + (within % rows) * GRID_W + within // rows
    spare1 = tuple(b * (ll + MOE_TILE) + ll for b in range(1, bsz))
    f1 = _moe(h2b, clsb, raster, ll, spare1, bsz * (ll + MOE_TILE), 1, wg_all, wu_all, wd_all)

    return _final(s3_cm.reshape(bsz, ll, d), f1, modv, rows_mod, final_g.reshape(1, d), tm)
```

```python
import functools

import numpy as np
import jax
import jax.numpy as jnp
from jax import lax
from jax.experimental import pallas as pl
from jax.experimental.pallas import tpu as pltpu

F32 = jnp.float32
BF16 = jnp.bfloat16

EPS = 1e-6
LOG2_E = 1.4426950408889634
N_ADA = 6
GRID_W = 64
M_HEADS = 4
H_HEADS = 8
N_EXPERTS = 16
N_GROUPS = 4
E_PER_GROUP = N_EXPERTS // N_GROUPS
N_PAIRS = 6
N_CLASSES = N_GROUPS * N_PAIRS
PAIR_LO = (0, 0, 0, 1, 1, 2)
PAIR_HI = (1, 2, 3, 2, 3, 3)

LANES = 128
COL_TILE = 8
SCAN_CHUNK = 128
MOE_TILE = 256
VMEM_LIMIT = 56 * 1024 * 1024


def _cparams(sem):
    return pltpu.CompilerParams(dimension_semantics=sem, vmem_limit_bytes=VMEM_LIMIT)


def _dot(a, b):
    return jnp.dot(a, b, preferred_element_type=F32)


def _dot_nt(a, b):
    return lax.dot_general(a, b, (((1,), (1,)), ((), ())), preferred_element_type=F32)


def _dot_tn(a, b):
    return lax.dot_general(a, b, (((0,), (0,)), ((), ())), preferred_element_type=F32)


def _split_bf16(a):
    hi = a.astype(BF16)
    lo = (a - hi.astype(F32)).astype(BF16)
    return hi, lo


def _sigmoid(x):
    return 1.0 / (1.0 + jnp.exp(-x))


def _silu(x):
    return x * _sigmoid(x)


def _log1p_exp_neg_abs(x):
    return jnp.log(1.0 + jnp.exp(-jnp.abs(x)))


def _log_sigmoid(x):
    return jnp.minimum(x, 0.0) - _log1p_exp_neg_abs(x)


def _rms(x, g):
    return x * lax.rsqrt(jnp.mean(x * x, axis=-1, keepdims=True) + EPS) * g


def _ada_kernel(c_ref, w_ref, b_ref, o_ref):
    a = _silu(c_ref[...])
    a_hi, a_lo = _split_bf16(a)
    w_hi, w_lo = _split_bf16(w_ref[0])
    acc = _dot(a_hi, w_hi) + _dot(a_lo, w_hi) + _dot(a_hi, w_lo)
    o_ref[0] = acc + b_ref[0]


def _ada(cc, ada_w, ada_b):
    depth, d, n = ada_w.shape
    tn = 1024
    rows = cc.shape[0]
    return pl.pallas_call(
        _ada_kernel,
        grid=(depth, n // tn),
        in_specs=[
            pl.BlockSpec((rows, d), lambda l, j: (0, 0)),
            pl.BlockSpec((1, d, tn), lambda l, j: (l, 0, j)),
            pl.BlockSpec((1, 1, tn), lambda l, j: (l, 0, j)),
        ],
        out_specs=pl.BlockSpec((1, rows, tn), lambda l, j: (l, 0, j)),
        out_shape=jax.ShapeDtypeStruct((depth, rows, n), F32),
        compiler_params=_cparams(("arbitrary", "arbitrary")),
        name="ada_mod",
    )(cc, ada_w, ada_b.reshape(depth, 1, n))


def _mproj_kernel(n_lat_tiles, xl_ref, xc_ref, sc_ref, sh_ref, g_ref, wqk_ref, wv_ref, wo_ref, wg_ref,
                  qk_ref, v_ref, o_ref, gt_ref):
    j = pl.program_id(1)
    x = jnp.where(j < n_lat_tiles, xl_ref[0], xc_ref[0])
    h = (_rms(x, g_ref[...]) * (1.0 + sc_ref[0]) + sh_ref[0]).astype(BF16)
    qk_ref[0] = _dot(h, wqk_ref[...]).astype(BF16)
    v_ref[0] = _dot(h, wv_ref[...]).astype(BF16)
    o_ref[0] = _dot(h, wo_ref[...]).astype(BF16)
    gt_ref[0] = _dot(h, wg_ref[...])[:, :4 * M_HEADS]


def _mod_spec(d, layer, k, n_lat_tiles, rows):
    def imap(b, j):
        row = jnp.where(j < n_lat_tiles, b, rows - 1)
        return ((layer * rows + row) * N_ADA + k, 0, 0)
    return pl.BlockSpec((1, 1, d), imap)


def _mproj(x, ctx, modv, rows, norm_g, w_qk, w_v, w_o, w_g, tm):
    bsz, ll, d = x.shape
    lc = ctx.shape[1]
    nl, nc = ll // tm, lc // tm
    ltot = ll + lc
    wspec = lambda n: pl.BlockSpec((d, n), lambda b, j: (0, 0))
    tok = lambda n: pl.BlockSpec((1, tm, n), lambda b, j: (b, j, 0))
    return pl.pallas_call(
        functools.partial(_mproj_kernel, nl),
        grid=(bsz, nl + nc),
        in_specs=[
            pl.BlockSpec((1, tm, d), lambda b, j: (b, jnp.minimum(j, nl - 1), 0)),
            pl.BlockSpec((1, tm, d), lambda b, j: (b, jnp.maximum(j - nl, 0), 0)),
            _mod_spec(d, 0, 1, nl, rows), _mod_spec(d, 0, 0, nl, rows),
            pl.BlockSpec((1, d), lambda b, j: (0, 0)),
            wspec(w_qk.shape[1]), wspec(w_v.shape[1]), wspec(w_o.shape[1]), wspec(w_g.shape[1]),
        ],
        out_specs=[tok(w_qk.shape[1]), tok(w_v.shape[1]), tok(w_o.shape[1]), tok(4 * M_HEADS)],
        out_shape=[
            jax.ShapeDtypeStruct((bsz, ltot, w_qk.shape[1]), BF16),
            jax.ShapeDtypeStruct((bsz, ltot, w_v.shape[1]), BF16),
            jax.ShapeDtypeStruct((bsz, ltot, w_o.shape[1]), BF16),
            jax.ShapeDtypeStruct((bsz, ltot, 4 * M_HEADS), F32),
        ],
        compiler_params=_cparams(("arbitrary", "arbitrary")),
        name="mlstm_in_proj",
    )(x, ctx, modv, modv, norm_g, w_qk, w_v, w_o, w_g)


def _mlstm_kernel(t, ll, lc, hp, q_ref, k_ref, v_ref, o_ref, gc_ref, gr_ref, cwq_ref, cwk_ref, cbq_ref, cbk_ref,
                  gbc_ref, gbr_ref, hg_ref, out_ref, qs, ks, hf, hb, c_s, m_s):
    ltot = ll + lc
    dk = q_ref.shape[2] // hp
    dv = v_ref.shape[2] // hp
    row = lax.broadcasted_iota(jnp.int32, (ltot, 1), 0)
    first = (row == 0) | (row == ll)
    last = (row == ll - 1) | (row == ltot - 1)

    def conv(x_ref, w_ref, b_ref):
        x = x_ref[0].astype(F32)
        w = w_ref[...]
        xp = jnp.where(first, 0.0, pltpu.roll(x, 1, 0))
        xn = jnp.where(last, 0.0, pltpu.roll(x, ltot - 1, 0))
        return _silu(xp * w[0:1] + x * w[1:2] + xn * w[2:3] + b_ref[...])

    qs[...] = (conv(q_ref, cwq_ref, cbq_ref) * (dk ** -0.5)).astype(BF16)
    ks[...] = conv(k_ref, cwk_ref, cbk_ref).astype(BF16)

    c_s[...] = jnp.zeros(c_s.shape, F32)
    m_s[...] = jnp.zeros(m_s.shape, F32)

    ti = lax.broadcasted_iota(jnp.int32, (t, t), 0)
    si = lax.broadcasted_iota(jnp.int32, (t, t), 1)

    h_dir = (hf, hb)
    ones = jnp.ones((t, LANES), BF16)

    def chunks(starts):
        jobs = [(j, d) for j in range(hp) for d in range(2)]
        st = {}
        for j, d in jobs:
            rows = pl.ds(starts[d], t)
            gc = gc_ref[0, j, rows, :] + gbc_ref[j]
            gr = gr_ref[0, j, :, rows] + gbr_ref[j]
            ig_c, lf_c = gc[:, 2 * d:2 * d + 1], _log_sigmoid(gc[:, 2 * d + 1:2 * d + 2])
            ig_r, lf_r = gr[2 * d:2 * d + 1, :], _log_sigmoid(gr[2 * d + 1:2 * d + 2, :])
            seen = (si <= ti) if d == 0 else (si >= ti)
            seen_t = (ti <= si) if d == 0 else (ti >= si)
            b_c = jnp.sum(jnp.where(seen, lf_r, 0.0), axis=1, keepdims=True)
            b_r = jnp.sum(jnp.where(seen_t, lf_c, 0.0), axis=0, keepdims=True)
            total = jnp.sum(lf_r, axis=1, keepdims=True)
            st[j, d] = dict(rows=rows, ig_c=ig_c, ig_r=ig_r, b_c=b_c, b_r=b_r, total=total, seen=seen,
                            m_prev=m_s[2 * j + d][0:1, 0:1])
        for j, d in jobs:
            s = st[j, d]
            s["q"] = qs[s["rows"], j * dk:(j + 1) * dk]
            s["k"] = ks[s["rows"], j * dk:(j + 1) * dk]
            s["v"] = jnp.concatenate([v_ref[0, s["rows"], j * dv:(j + 1) * dv], ones], axis=1)
            s["qk"] = _dot_nt(s["q"], s["k"])
            s["qc"] = _dot_nt(s["q"], c_s[2 * j + d].astype(BF16))
        for j, d in jobs:
            s = st[j, d]
            log_d = jnp.where(s["seen"], s["b_c"] - s["b_r"] + s["ig_r"], -jnp.inf)
            log_inter = s["b_c"] + s["m_prev"]
            m_t = jnp.maximum(log_inter, jnp.max(log_d, axis=1, keepdims=True))
            s["w_ts"] = jnp.exp(log_d - m_t) * s["qk"]
            s["inter"] = jnp.exp(log_inter - m_t)
            s["m_t"] = m_t
        for j, d in jobs:
            s = st[j, d]
            s["num"] = _dot(s["w_ts"].astype(BF16), s["v"]) + s["inter"] * s["qc"]
        for j, d in jobs:
            s = st[j, d]
            den = s["num"][:, dv:]
            scale = 1.0 / jnp.maximum(jnp.abs(den), jnp.exp(-s["m_t"]))
            h_dir[d][s["rows"], j * dv:(j + 1) * dv] = s["num"][:, :dv] * jnp.concatenate([scale] * (dv // LANES), axis=1)
        for j, d in jobs:
            s = st[j, d]
            g_c = s["total"] - s["b_c"] + s["ig_c"]
            g_r = s["total"] - s["b_r"] + s["ig_r"]
            m_new = jnp.maximum(s["total"] + s["m_prev"], jnp.max(g_r, axis=1, keepdims=True))
            w_c = jnp.exp(g_c - m_new)
            decay = jnp.exp(s["total"] + s["m_prev"] - m_new)
            wv = (w_c * s["v"].astype(F32)).astype(BF16)
            c_s[2 * j + d] = decay * c_s[2 * j + d] + _dot_tn(wv, s["k"])
            m_s[2 * j + d] = jnp.broadcast_to(m_new, m_s.shape[1:])

    ncc, ncl = lc // t, ll // t
    for i in range(ncc):
        chunks((ll + i * t, ll + (ncc - 1 - i) * t))

    def body(i, carry):
        chunks((pl.multiple_of(i * t, t), pl.multiple_of((ncl - 1 - i) * t, t)))
        return carry

    lax.fori_loop(0, ncl, body, 0)

    def epilogue(i, carry):
        s = pl.multiple_of(i * t, t)
        for j in range(hp):
            cols = slice(j * dv, (j + 1) * dv)
            y = _rms(hf[pl.ds(s, t), cols] + hb[pl.ds(s, t), cols], hg_ref[:, cols])
            out_ref[0, pl.ds(s, t), cols] = (y * _sigmoid(o_ref[0, pl.ds(s, t), cols].astype(F32))).astype(BF16)
        return carry

    lax.fori_loop(0, ltot // t, epilogue, 0, unroll=2)


def _mlstm_scan(qk, v, o, gates, conv_w, conv_b, gate_b, head_g, ll, lc):
    bsz, ltot, _ = qk.shape
    nh = M_HEADS
    hp = 2
    dk = qk.shape[2] // (2 * nh)
    dv = v.shape[2] // nh
    t = SCAN_CHUNK
    g4 = gates.reshape(bsz, ltot, 4, nh).transpose(0, 3, 1, 2)
    g4t = g4.transpose(0, 1, 3, 2)
    gb = gate_b.reshape(4, nh).T
    return pl.pallas_call(
        functools.partial(_mlstm_kernel, t, ll, lc, hp),
        grid=(bsz, nh // hp),
        in_specs=[
            pl.BlockSpec((1, ltot, hp * dk), lambda b, h: (b, 0, h)),
            pl.BlockSpec((1, ltot, hp * dk), lambda b, h: (b, 0, nh // hp + h)),
            pl.BlockSpec((1, ltot, hp * dv), lambda b, h: (b, 0, h)),
            pl.BlockSpec((1, ltot, hp * dv), lambda b, h: (b, 0, h)),
            pl.BlockSpec((1, hp, ltot, 4), lambda b, h: (b, h, 0, 0)),
            pl.BlockSpec((1, hp, 4, ltot), lambda b, h: (b, h, 0, 0)),
            pl.BlockSpec((3, hp * dk), lambda b, h: (0, h)),
            pl.BlockSpec((3, hp * dk), lambda b, h: (0, nh // hp + h)),
            pl.BlockSpec((1, hp * dk), lambda b, h: (0, h)),
            pl.BlockSpec((1, hp * dk), lambda b, h: (0, nh // hp + h)),
            pl.BlockSpec((hp, 1, 4), lambda b, h: (h, 0, 0)),
            pl.BlockSpec((hp, 4, 1), lambda b, h: (h, 0, 0)),
            pl.BlockSpec((1, hp * dv), lambda b, h: (0, h)),
        ],
        out_specs=pl.BlockSpec((1, ltot, hp * dv), lambda b, h: (b, 0, h)),
        out_shape=jax.ShapeDtypeStruct((bsz, ltot, nh * dv), BF16),
        scratch_shapes=[
            pltpu.VMEM((ltot, hp * dk), BF16), pltpu.VMEM((ltot, hp * dk), BF16),
            pltpu.VMEM((ltot, hp * dv), F32), pltpu.VMEM((ltot, hp * dv), F32),
            pltpu.VMEM((2 * hp, dv + LANES, dk), F32), pltpu.VMEM((2 * hp, 8, LANES), F32),
        ],
        compiler_params=_cparams(("arbitrary", "arbitrary")),
        name="mlstm_scan",
    )(qk, qk, v, o, g4, g4t, conv_w, conv_w, conv_b.reshape(1, -1), conv_b.reshape(1, -1),
      gb.reshape(nh, 1, 4), gb.reshape(nh, 4, 1), head_g.reshape(1, -1))


def _router_logits_t(h2, rhi_ref, rlo_ref):
    h_hi, h_lo = _split_bf16(h2)
    lg = _dot(h_hi, rhi_ref[...]) + _dot(h_lo, rhi_ref[...]) + _dot(h_hi, rlo_ref[...])
    return lg.T[:N_EXPERTS, :]


def _out0_kernel(n_lat_tiles, hg_ref, xl_ref, xc_ref, g1_ref, sc_ref, sh_ref, ng_ref, wo_ref, rhi_ref, rlo_ref,
                 s_ref, h2_ref, lg_ref):
    j = pl.program_id(1)
    x = jnp.where(j < n_lat_tiles, xl_ref[0], xc_ref[0])
    s = x + g1_ref[0] * _dot(hg_ref[0], wo_ref[...])
    s_ref[0] = s
    h2 = _rms(s, ng_ref[...]) * (1.0 + sc_ref[0]) + sh_ref[0]
    h2_ref[:, :h2.shape[1]] = h2
    h2_ref[:, h2.shape[1]:] = jnp.zeros((h2.shape[0], LANES), F32)
    lg_ref[...] = _router_logits_t(h2, rhi_ref, rlo_ref)


def _out0(hg, x, ctx, modv, rows, norm_g, w_out, r_hi, r_lo, tm):
    bsz, ll, d = x.shape
    lc = ctx.shape[1]
    nl, nc = ll // tm, lc // tm
    nt = nl + nc
    ltot = ll + lc
    return pl.pallas_call(
        functools.partial(_out0_kernel, nl),
        grid=(bsz, nt),
        in_specs=[
            pl.BlockSpec((1, tm, hg.shape[2]), lambda b, j: (b, j, 0)),
            pl.BlockSpec((1, tm, d), lambda b, j: (b, jnp.minimum(j, nl - 1), 0)),
            pl.BlockSpec((1, tm, d), lambda b, j: (b, jnp.maximum(j - nl, 0), 0)),
            _mod_spec(d, 0, 2, nl, rows), _mod_spec(d, 0, 4, nl, rows), _mod_spec(d, 0, 3, nl, rows),
            pl.BlockSpec((1, d), lambda b, j: (0, 0)),
            pl.BlockSpec(w_out.shape, lambda b, j: (0, 0)),
            pl.BlockSpec(r_hi.shape, lambda b, j: (0, 0)),
            pl.BlockSpec(r_lo.shape, lambda b, j: (0, 0)),
        ],
        out_specs=[
            pl.BlockSpec((1, tm, d), lambda b, j: (b, j, 0)),
            pl.BlockSpec((tm, d + LANES), lambda b, j: (b * nt + j, 0)),
            pl.BlockSpec((N_EXPERTS, tm), lambda b, j: (0, b * nt + j)),
        ],
        out_shape=[
            jax.ShapeDtypeStruct((bsz, ltot, d), F32),
            jax.ShapeDtypeStruct((bsz * ltot, d + LANES), F32),
            jax.ShapeDtypeStruct((N_EXPERTS, bsz * ltot), F32),
        ],
        compiler_params=_cparams(("arbitrary", "arbitrary")),
        name="mlstm_out_proj",
    )(hg, x, ctx, modv, modv, modv, norm_g, w_out, r_hi, r_lo)


def _route_kernel(lg_ref, bias_ref, h2x_hbm, cls_ref, w_ref):
    del h2x_hbm
    s = _sigmoid(lg_ref[...])
    sel = s + bias_ref[...]
    srow = [s[e:e + 1, :] for e in range(N_EXPERTS)]
    row = [sel[e:e + 1, :] for e in range(N_EXPERTS)]
    best = jnp.zeros(row[0].shape, jnp.int32)
    best_score = None
    for g in range(N_GROUPS):
        r = row[g * E_PER_GROUP:(g + 1) * E_PER_GROUP]
        score = None
        for lo, hi in zip(PAIR_LO, PAIR_HI):
            pair = r[lo] + r[hi]
            score = pair if score is None else jnp.maximum(score, pair)
        if g == 0:
            best_score = score
        else:
            better = score > best_score
            best = jnp.where(better, g, best)
            best_score = jnp.where(better, score, best_score)
    gs = [row[i] for i in range(E_PER_GROUP)]
    gw = [srow[i] for i in range(E_PER_GROUP)]
    for g in range(1, N_GROUPS):
        hit = best == g
        gs = [jnp.where(hit, row[g * E_PER_GROUP + i], gs[i]) for i in range(E_PER_GROUP)]
        gw = [jnp.where(hit, srow[g * E_PER_GROUP + i], gw[i]) for i in range(E_PER_GROUP)]
    keep = []
    for i in range(E_PER_GROUP):
        beaten = jnp.zeros(best.shape, jnp.int32)
        for j in range(E_PER_GROUP):
            if j == i:
                continue
            wins = (gs[j] > gs[i]) | ((gs[j] == gs[i]) & (j < i))
            beaten = beaten + wins.astype(jnp.int32)
        keep.append(beaten < 2)
    pair_id = jnp.zeros(best.shape, jnp.int32)
    w_lo = jnp.zeros(best.shape, F32)
    w_hi = jnp.zeros(best.shape, F32)
    for p, (lo, hi) in enumerate(zip(PAIR_LO, PAIR_HI)):
        hit = keep[lo] & keep[hi]
        pair_id = jnp.where(hit, p, pair_id)
        w_lo = jnp.where(hit, gw[lo], w_lo)
        w_hi = jnp.where(hit, gw[hi], w_hi)
    tot = w_lo + w_hi
    cls_ref[...] = best * N_PAIRS + pair_id
    lanes_t = jnp.concatenate([w_lo / tot, w_hi / tot, jnp.zeros((LANES - 2, w_lo.shape[1]), F32)], axis=0)
    w_ref[...] = lanes_t.T


def _route(logits_t, router_bias, h2x):
    n = logits_t.shape[1]
    d = h2x.shape[1] - LANES
    tn = next(cand for cand in (2048, 1024, 512, 256, 128) if n % cand == 0)
    return pl.pallas_call(
        _route_kernel,
        grid=(n // tn,),
        in_specs=[
            pl.BlockSpec((N_EXPERTS, tn), lambda i: (0, i)),
            pl.BlockSpec((N_EXPERTS, 1), lambda i: (0, 0)),
            pl.BlockSpec(memory_space=pl.ANY),
        ],
        out_specs=[pl.BlockSpec((1, tn), lambda i: (0, i)), pl.BlockSpec((tn, LANES), lambda i: (i, d // LANES))],
        out_shape=[jax.ShapeDtypeStruct((1, n), jnp.int32), jax.ShapeDtypeStruct(h2x.shape, F32)],
        input_output_aliases={2: 1},
        compiler_params=_cparams(("arbitrary",)),
        name="moe_route",
    )(logits_t, router_bias.reshape(N_EXPERTS, 1).astype(F32), h2x)


def _moe_kernel(tm, dump_base, spare_rows, nu_ref, e0_ref, e1_ref, base_ref, nv_ref, tok_ref, dst_ref, h2_hbm,
                wg0_ref, wu0_ref, wd0_ref, wg1_ref, wu1_ref, wd1_ref, f_hbm,
                xb0, xb1, yb0, yb1, xbf, gsem, ssem, zsem, idle_sem):
    del e0_ref, e1_ref
    t = pl.program_id(0)
    n_used = nu_ref[0]
    d = xbf.shape[1]
    xbuf, ybuf = (xb0, xb1), (yb0, yb1)

    def gather_row(base, r, slot):
        tok = tok_ref[base + r]
        pltpu.make_async_copy(h2_hbm.at[pl.ds(tok, 1)], xbuf[slot].at[pl.ds(r, 1)], gsem.at[slot]).start()

    def scatter_row(base, nv, r, slot):
        row = jnp.where(r < nv, dst_ref[base + r], dump_base + r)
        pltpu.make_async_copy(ybuf[slot].at[pl.ds(r, 1)], f_hbm.at[pl.ds(row, 1)], ssem.at[slot]).start(priority=1)

    def wait_gather(slot):
        pltpu.make_async_copy(h2_hbm.at[pl.ds(0, tm)], xbuf[slot], gsem.at[slot]).wait()

    def wait_scatter(slot):
        pltpu.make_async_copy(ybuf[slot], f_hbm.at[pl.ds(0, tm)], ssem.at[slot]).wait()

    @pl.when(t == 0)
    def _():
        yb1[...] = jnp.zeros(yb1.shape, F32)
        fills = [pltpu.make_async_copy(yb1, f_hbm.at[pl.ds(row, tm)], zsem) for row in spare_rows]
        for fill in fills:
            fill.start()
        for fill in fills:
            fill.wait()
        base0 = base_ref[1]

        def first(r, carry):
            gather_row(base0, r, 0)
            return carry

        lax.fori_loop(0, tm, first, 0)

    def step(cur):
        nxt = 1 - cur
        base_next = base_ref[t + 2]
        base_prev = base_ref[t]
        nv_prev = nv_ref[t]

        @pl.when(t < n_used)
        def _():
            wait_gather(cur)
            xbf[...] = xbuf[cur][:, :d].astype(BF16)
            w = xbuf[cur][:, d:]

            for r in range(tm):
                gather_row(base_next, r, nxt)
            x = xbf[...]
            a0 = (_silu(_dot(x, wg0_ref[0, 0])) * _dot(x, wu0_ref[0, 0]) * w[:, 0:1]).astype(BF16)

            z = pl.semaphore_read(idle_sem)
            zero = lax.shift_right_logical(z.astype(jnp.uint32), jnp.uint32(32)).astype(jnp.int32)
            off = pl.multiple_of(zero * 16, 16)

            for r in range(tm):
                scatter_row(base_prev, nv_prev, r, nxt)
            x = xbf[pl.ds(off, tm), :]
            a1 = (_silu(_dot(x, wg1_ref[0, 0])) * _dot(x, wu1_ref[0, 0]) * w[:, 1:2]).astype(BF16)
            y = _dot(a0, wd0_ref[0, 0]) + _dot(a1, wd1_ref[0, 0])

            @pl.when(t >= 1)
            def _():
                wait_scatter(cur)

            ybuf[cur][...] = y

        @pl.when(t == n_used)
        def _():
            wait_gather(cur)
            wait_scatter(cur)

            def last(r, carry):
                scatter_row(base_prev, nv_prev, r, nxt)
                return carry

            lax.fori_loop(0, tm, last, 0)
            wait_scatter(nxt)

    @pl.when(t % 2 == 0)
    def _():
        step(0)

    @pl.when(t % 2 == 1)
    def _():
        step(1)


def _moe(h2x, cls, dst_of_token, dump_base, spare_rows, out_rows, layer, wg, wu, wd):
    n = h2x.shape[0]
    d = h2x.shape[1] - LANES
    tm = MOE_TILE
    n_tiles = n // tm + N_CLASSES
    cls = cls.reshape(n)
    _, tok_sorted, dst_sorted = lax.sort((cls, jnp.arange(n, dtype=jnp.int32), dst_of_token), num_keys=1)
    tail = jnp.zeros((tm,), jnp.int32)
    tok_sorted = jnp.concatenate([tok_sorted, tail])
    dst_sorted = jnp.concatenate([dst_sorted, tail])
    counts = jnp.sum(cls[:, None] == jnp.arange(N_CLASSES, dtype=jnp.int32)[None, :], axis=0).astype(jnp.int32)
    tiles_per = (counts + tm - 1) // tm
    tile_end = jnp.cumsum(tiles_per)
    tile_start = tile_end - tiles_per
    first_sorted = jnp.cumsum(counts) - counts
    n_used = tile_end[-1]
    tile_id = jnp.arange(n_tiles, dtype=jnp.int32)
    tile_cls = jnp.searchsorted(tile_end, jnp.minimum(tile_id, n_used - 1), side="right").astype(jnp.int32)
    tile_cls = jnp.minimum(tile_cls, N_CLASSES - 1)
    group, pair = tile_cls // N_PAIRS, tile_cls % N_PAIRS
    e0 = group * E_PER_GROUP + jnp.asarray(PAIR_LO, jnp.int32)[pair]
    e1 = group * E_PER_GROUP + jnp.asarray(PAIR_HI, jnp.int32)[pair]
    in_class = (tile_id - tile_start[tile_cls]) * tm
    used = tile_id < n_used
    base = jnp.where(used, first_sorted[tile_cls] + in_class, 0)
    nv = jnp.where(used, jnp.clip(counts[tile_cls] - in_class, 0, tm), 0)
    guard = jnp.zeros((1,), jnp.int32)
    base = jnp.concatenate([guard, base, guard]).astype(jnp.int32)
    nv = jnp.concatenate([guard, nv, guard]).astype(jnp.int32)

    de = wg.shape[3]
    smem = pl.BlockSpec(memory_space=pltpu.SMEM)
    up0 = pl.BlockSpec((1, 1, d, de), lambda i, nu, e0, e1, base, nv: (layer, e0[i], 0, 0))
    up1 = pl.BlockSpec((1, 1, d, de), lambda i, nu, e0, e1, base, nv: (layer, e1[i], 0, 0))
    dn0 = pl.BlockSpec((1, 1, de, d), lambda i, nu, e0, e1, base, nv: (layer, e0[i], 0, 0))
    dn1 = pl.BlockSpec((1, 1, de, d), lambda i, nu, e0, e1, base, nv: (layer, e1[i], 0, 0))
    return pl.pallas_call(
        functools.partial(_moe_kernel, tm, dump_base, spare_rows),
        grid_spec=pltpu.PrefetchScalarGridSpec(
            num_scalar_prefetch=5,
            grid=(n_tiles,),
            in_specs=[
                smem, smem,
                pl.BlockSpec(memory_space=pl.ANY),
                up0, up0, dn0, up1, up1, dn1,
            ],
            out_specs=pl.BlockSpec(memory_space=pl.ANY),
            scratch_shapes=[
                pltpu.VMEM((tm, d + LANES), F32), pltpu.VMEM((tm, d + LANES), F32),
                pltpu.VMEM((tm, d), F32), pltpu.VMEM((tm, d), F32),
                pltpu.VMEM((tm, d), BF16),
                pltpu.SemaphoreType.DMA((2,)), pltpu.SemaphoreType.DMA((2,)), pltpu.SemaphoreType.DMA(()),
                pltpu.SemaphoreType.REGULAR(()),
            ],
        ),
        out_shape=jax.ShapeDtypeStruct((out_rows, d), F32),
        compiler_params=_cparams(("arbitrary",)),
        name="moe_experts",
    )(n_used.reshape(1).astype(jnp.int32), e0, e1, base, nv, tok_sorted, dst_sorted, h2x,
      wg, wu, wd, wg, wu, wd)


def _hproj_kernel(n_lat_tiles, rows, s_hbm, sc_ref_, f_ref, g2_ref, sc_ref, sh_ref, ng_ref, lb_ref,
                  wq_ref, wzf_ref, wzb_ref, wi_ref, wg_ref,
                  s2_ref, q_ref, kf_ref, kb_ref, lff_ref, lfb_ref, i_ref, g_ref, xt, sbuf, sem):
    b, j = pl.program_id(0), pl.program_id(1)
    n_lat = pl.num_programs(0) * n_lat_tiles
    g2 = g2_ref[0]

    def column_copies(tile, slot):
        bb, jj = tile // n_lat_tiles, tile % n_lat_tiles
        return [pltpu.make_async_copy(s_hbm.at[bb, pl.ds(0, rows), jj * COL_TILE + c, :],
                                      sbuf.at[slot, pl.ds(c * rows, rows), :], sem.at[slot])
                for c in range(COL_TILE)]

    @pl.when(j < n_lat_tiles)
    def _():
        tile = b * n_lat_tiles + j
        slot = tile % 2

        @pl.when(tile == 0)
        def _():
            for cp in column_copies(tile, slot):
                cp.start()

        @pl.when(tile + 1 < n_lat)
        def _():
            for cp in column_copies(tile + 1, 1 - slot):
                cp.start()

        for cp in column_copies(tile, slot):
            cp.wait()
        s = sbuf[slot] + g2 * f_ref[0]
        s2_ref[0] = s
        xt[...] = s

    @pl.when(j >= n_lat_tiles)
    def _():
        xt[...] = sc_ref_[0] + g2 * f_ref[0]

    h = (_rms(xt[...], ng_ref[...]) * (1.0 + sc_ref[0]) + sh_ref[0]).astype(BF16)
    n_chunks = 4
    width = wq_ref.shape[1] // n_chunks

    def gates(dd, k_ref, lf_ref, cols, z):
        log_lb = lb_ref[2 * dd:2 * dd + 1, cols]
        log_1mlb = lb_ref[2 * dd + 1:2 * dd + 2, cols]
        a = log_1mlb + _log_sigmoid(z)
        lf_ref[0, :, cols] = jnp.maximum(log_lb, a) + _log1p_exp_neg_abs(log_lb - a)
        k_ref[0, :, cols] = jnp.exp(a - z).astype(BF16)

    def store_q(cols, y):
        q_ref[0, :, cols] = _silu(y).astype(BF16)

    def store_plain(out_ref, cols, y):
        out_ref[0, :, cols] = y.astype(BF16)

    gates_f = functools.partial(gates, 0, kf_ref, lff_ref)
    gates_b = functools.partial(gates, 1, kb_ref, lfb_ref)
    plain_i = functools.partial(store_plain, i_ref)
    plain_g = functools.partial(store_plain, g_ref)
    work = []
    for c in range(n_chunks):
        work += [(wzf_ref, gates_f, c), (wi_ref, plain_i, c)]
    for c in range(n_chunks):
        work += [(wzb_ref, gates_b, c), (wg_ref, plain_g, c)]
    work += [(wq_ref, store_q, c) for c in range(n_chunks)]
    pending = None
    for w_ref, tail, c in work:
        cols = slice(c * width, (c + 1) * width)
        y = _dot(h, w_ref[:, cols])
        if pending is not None:
            pending()
        pending = functools.partial(tail, cols, y)
    pending()


def _hproj(s1, f0, modv, rows_mod, norm_g, lbtab, ws, ll, lc):
    bsz, ltot, d = s1.shape
    rows = ll // GRID_W
    tm = COL_TILE * rows
    nl, nc = GRID_W // COL_TILE, lc // tm
    s_grid = s1.reshape(bsz, ltot // GRID_W, GRID_W, d)
    f_tok = f0.reshape(bsz, -1, d)
    ctx = pl.BlockSpec((1, tm, d), lambda b, j: (b, ll // tm + jnp.maximum(j - nl, 0), 0))
    wspec = pl.BlockSpec((d, d), lambda b, j: (0, 0))
    tok = pl.BlockSpec((1, tm, d), lambda b, j: (b, j, 0))
    lat = pl.BlockSpec((1, tm, d), lambda b, j: (b, jnp.minimum(j, nl - 1), 0))

    def g2_map(b, j):
        row = jnp.where(j < nl, b, rows_mod - 1)
        return ((0 * rows_mod + row) * N_ADA + 5, 0, 0)

    outs = pl.pallas_call(
        functools.partial(_hproj_kernel, nl, rows),
        grid=(bsz, nl + nc),
        in_specs=[
            pl.BlockSpec(memory_space=pl.ANY), ctx, tok,
            pl.BlockSpec((1, 1, d), g2_map),
            _mod_spec(d, 1, 1, nl, rows_mod), _mod_spec(d, 1, 0, nl, rows_mod),
            pl.BlockSpec((1, d), lambda b, j: (0, 0)),
            pl.BlockSpec((4, d), lambda b, j: (0, 0)),
            wspec, wspec, wspec, wspec, wspec,
        ],
        out_specs=[lat, tok, tok, tok, tok, tok, tok, tok],
        out_shape=[jax.ShapeDtypeStruct((bsz, ll, d), F32)]
        + [jax.ShapeDtypeStruct((bsz, ltot, d), dt) for dt in (BF16, BF16, BF16, F32, F32, BF16, BF16)],
        scratch_shapes=[pltpu.VMEM((tm, d), F32), pltpu.VMEM((2, tm, d), F32), pltpu.SemaphoreType.DMA((2,))],
        compiler_params=_cparams(("arbitrary", "arbitrary")),
        name="hgrn_in_proj",
    )(s_grid, s1, f_tok, modv, modv, modv, norm_g, lbtab, *ws)
    return outs


def _hgrn_kernel(t, ll, lc, q_ref, kf_ref, kb_ref, lff_ref, lfb_ref, i_ref, g_ref, hg_ref, out_ref,
                 of, ob, inc_s, dec_s, qd_s):
    levels = []
    m = 2
    while m <= t:
        levels.append(m)
        m *= 2
    ti = lax.broadcasted_iota(jnp.int32, (t, t), 0)
    si = lax.broadcasted_iota(jnp.int32, (t, t), 1)
    xor = ti ^ si
    level = jnp.zeros((t, t), jnp.int32)
    for m in levels:
        level = level + (xor >= m // 2).astype(jnp.int32)
    feeds = ((si <= ti).astype(BF16), (si >= ti).astype(BF16))

    row = lax.broadcasted_iota(jnp.int32, (t, 1), 0)
    feeds_twice = tuple(jnp.concatenate([f, f], axis=1) for f in feeds)

    def block_ref(b, m, d):
        half = m // 2
        pos = half - 1 if d == 0 else half
        if m >= 8:
            dk = b.shape[1]
            b3 = b.reshape(t // m, m, dk)
            return jnp.broadcast_to(b3[:, pos:pos + 1, :], b3.shape).reshape(t, dk)
        r = row % m
        out = b
        for res in range(m):
            if res != pos:
                out = jnp.where(r == res, pltpu.roll(b, (res - pos) % t, 0), out)
        return out

    def neg_abs(x):
        sign = jnp.uint32(0x80000000)
        return lax.bitcast_convert_type(lax.bitcast_convert_type(x, jnp.uint32) | sign, F32)

    o_dir = (of, ob)

    def prepare(jobs, need_out):
        work = []
        for ci, d in jobs:
            rows = pl.ds(ci * t if isinstance(ci, int) else pl.multiple_of(ci * t, t), t)
            lf = (lff_ref if d == 0 else lfb_ref)[0, rows, :] * LOG2_E
            hi = lf.astype(BF16)
            lo = (lf - hi.astype(F32)).astype(BF16)
            b = _dot(feeds_twice[d], jnp.concatenate([hi, lo], axis=0))
            work.append((ci, d, rows, b))
        for ci, d, rows, b in work:
            b_tot = b[t - 1:t, :] if d == 0 else b[0:1, :]
            k = (kf_ref if d == 0 else kb_ref)[0, rows, :]
            inc_s[d, ci] = _dot_tn(i_ref[0, rows, :], k * jnp.exp2(b_tot - b).astype(BF16))
            dec_s[d, ci] = jnp.exp2(b_tot)
            if need_out:
                qd_s[d, rows, :] = q_ref[0, rows, :] * jnp.exp2(b).astype(BF16)
        if not need_out:
            return
        acc = []
        for ci, d, rows, b in work:
            acc.append(_dot_nt(q_ref[0, rows, :], (kf_ref if d == 0 else kb_ref)[0, rows, :]))
        for idx, m in enumerate(levels):
            for n, (ci, d, rows, b) in enumerate(work):
                e = jnp.exp2(neg_abs(b - block_ref(b, m, d))).astype(BF16)
                p = _dot_nt(q_ref[0, rows, :] * e, (kf_ref if d == 0 else kb_ref)[0, rows, :] * e)
                acc[n] = jnp.where(level == idx + 1, p, acc[n])
        for n, (ci, d, rows, b) in enumerate(work):
            o_dir[d][rows, :] = _dot(acc[n].astype(BF16) * feeds[d], i_ref[0, rows, :])

    ncc, ncl = lc // t, ll // t
    prepare([(ncl + i, d) for i in range(ncc) for d in range(2)], False)

    group = 4 if ncl % 4 == 0 else 2

    def body(i, carry):
        prepare([(group * i + j, d) for j in range(group) for d in range(2)], True)
        return carry

    lax.fori_loop(0, ncl // group, body, 0)

    for d in range(2):
        order = list(range(ncl, ncl + ncc)) + list(range(ncl))
        if d == 1:
            order = list(range(ncl + ncc - 1, ncl - 1, -1)) + list(range(ncl - 1, -1, -1))
        st = jnp.zeros(inc_s.shape[2:], F32)
        for ci in order:
            if ci < ncl:
                rows = pl.ds(ci * t, t)
                o_dir[d][rows, :] = o_dir[d][rows, :] + _dot_nt(qd_s[d, rows, :], st.astype(BF16))
            st = dec_s[d, ci] * st + inc_s[d, ci]

    def epilogue(i, carry):
        s = pl.multiple_of(i * t, t)
        y = _rms(of[pl.ds(s, t), :] + ob[pl.ds(s, t), :], hg_ref[...])
        out_ref[0, pl.ds(s, t), :] = (y * _silu(g_ref[0, pl.ds(s, t), :].astype(F32))).astype(BF16)
        return carry

    lax.fori_loop(0, ncl, epilogue, 0, unroll=4)


def _hgrn_scan(q, kf, kb, lff, lfb, iv, g, head_g, ll, lc):
    bsz, ltot, hk = q.shape
    nh = H_HEADS
    dk = hk // nh
    t = SCAN_CHUNK
    full = pl.BlockSpec((1, ltot, dk), lambda b, h: (b, 0, h))
    lat = pl.BlockSpec((1, ll, dk), lambda b, h: (b, 0, h))
    return pl.pallas_call(
        functools.partial(_hgrn_kernel, t, ll, lc),
        grid=(bsz, nh),
        in_specs=[full, full, full, full, full, full, lat, pl.BlockSpec((1, dk), lambda b, h: (0, h))],
        out_specs=lat,
        out_shape=jax.ShapeDtypeStruct((bsz, ll, hk), BF16),
        scratch_shapes=[pltpu.VMEM((ll, dk), F32), pltpu.VMEM((ll, dk), F32),
                        pltpu.VMEM((2, ltot // t, dk, dk), F32), pltpu.VMEM((2, ltot // t, 1, dk), F32),
                        pltpu.VMEM((2, ll, dk), BF16)],
        compiler_params=_cparams(("arbitrary", "arbitrary")),
        name="hgrn_scan",
    )(q, kf, kb, lff, lfb, iv, g, head_g.reshape(1, -1))


def _out1_kernel(hg_ref, s_ref, g1_ref, sc_ref, sh_ref, ng_ref, wo_ref, rhi_ref, rlo_ref,
                 s3_ref, h2_ref, lg_ref):
    s = s_ref[0] + g1_ref[0] * _dot(hg_ref[0], wo_ref[...])
    s3_ref[0] = s
    h2 = _rms(s, ng_ref[...]) * (1.0 + sc_ref[0]) + sh_ref[0]
    h2_ref[:, :h2.shape[1]] = h2
    h2_ref[:, h2.shape[1]:] = jnp.zeros((h2.shape[0], LANES), F32)
    lg_ref[...] = _router_logits_t(h2, rhi_ref, rlo_ref)


def _out1(hg, s2, modv, rows_mod, norm_g, w_out, r_hi, r_lo, tm):
    bsz, ll, hv = hg.shape
    d = s2.shape[2]
    nl = ll // tm
    cm = pl.BlockSpec((1, tm, d), lambda b, j: (b, j, 0))
    return pl.pallas_call(
        _out1_kernel,
        grid=(bsz, nl),
        in_specs=[
            pl.BlockSpec((1, tm, hv), lambda b, j: (b, j, 0)),
            cm,
            _mod_spec(d, 1, 2, nl, rows_mod), _mod_spec(d, 1, 4, nl, rows_mod), _mod_spec(d, 1, 3, nl, rows_mod),
            pl.BlockSpec((1, d), lambda b, j: (0, 0)),
            pl.BlockSpec(w_out.shape, lambda b, j: (0, 0)),
            pl.BlockSpec(r_hi.shape, lambda b, j: (0, 0)),
            pl.BlockSpec(r_lo.shape, lambda b, j: (0, 0)),
        ],
        out_specs=[
            cm,
            pl.BlockSpec((tm, d + LANES), lambda b, j: (b * nl + j, 0)),
            pl.BlockSpec((N_EXPERTS, tm), lambda b, j: (0, b * nl + j)),
        ],
        out_shape=[
            jax.ShapeDtypeStruct(s2.shape, F32),
            jax.ShapeDtypeStruct((bsz * ll, d + LANES), F32),
            jax.ShapeDtypeStruct((N_EXPERTS, bsz * ll), F32),
        ],
        compiler_params=_cparams(("arbitrary", "arbitrary")),
        name="hgrn_out_proj",
    )(hg, s2, modv, modv, modv, norm_g, w_out, r_hi, r_lo)


def _final_kernel(rows, s_ref, f_ref, g2_ref, fg_ref, o_hbm, obuf, sem):
    b, j = pl.program_id(0), pl.program_id(1)
    nl = pl.num_programs(1)
    tile = b * nl + j
    n_tiles = pl.num_programs(0) * nl
    slot = tile % 2

    def column_copies(t, sl):
        bb, jj = t // nl, t % nl
        return [pltpu.make_async_copy(obuf.at[sl, pl.ds(c * rows, rows), :],
                                      o_hbm.at[bb, pl.ds(0, rows), jj * COL_TILE + c, :], sem.at[sl])
                for c in range(COL_TILE)]

    @pl.when(tile >= 2)
    def _():
        for cp in column_copies(tile - 2, slot):
            cp.wait()

    obuf[slot] = _rms(s_ref[0] + g2_ref[0] * f_ref[0], fg_ref[...])
    for cp in column_copies(tile, slot):
        cp.start()

    @pl.when(tile == n_tiles - 1)
    def _():
        @pl.when(tile >= 1)
        def _():
            for cp in column_copies(tile - 1, 1 - slot):
                cp.wait()
        for cp in column_copies(tile, slot):
            cp.wait()


def _final(s3, f1, modv, rows_mod, final_g, tm):
    bsz, ll, d = s3.shape
    nl = ll // tm
    rows = ll // GRID_W
    tok = pl.BlockSpec((1, tm, d), lambda b, j: (b, j, 0))
    out = pl.pallas_call(
        functools.partial(_final_kernel, rows),
        grid=(bsz, nl),
        in_specs=[tok, tok, _mod_spec(d, 1, 5, nl, rows_mod), pl.BlockSpec((1, d), lambda b, j: (0, 0))],
        out_specs=pl.BlockSpec(memory_space=pl.ANY),
        out_shape=jax.ShapeDtypeStruct((bsz, rows, GRID_W, d), F32),
        scratch_shapes=[pltpu.VMEM((2, tm, d), F32), pltpu.SemaphoreType.DMA((2,))],
        compiler_params=_cparams(("arbitrary", "arbitrary")),
        name="final_norm",
    )(s3, f1.reshape(bsz, -1, d), modv, final_g)
    return out.reshape(bsz, ll, d)


def kernel(x, c, ctx, c_ctx, ada_w, ada_b, norm_mix_g, norm_ffn_g, final_g, m_w_in, m_conv_w, m_conv_b, m_gate_b,
           m_head_g, m_w_out, h_w_in, h_lower_bounds, h_head_g, h_w_out, router_w, router_bias, e_w_gate,
           e_w_up, e_w_down):
    bsz, ll, d = x.shape
    lc = ctx.shape[1]
    ltot = ll + lc
    depth = ada_w.shape[0]
    assert depth == 2 and ll % GRID_W == 0 and lc % SCAN_CHUNK == 0 and ll % SCAN_CHUNK == 0
    rows = ll // GRID_W
    tm = COL_TILE * rows
    assert lc % tm == 0 and ltot % GRID_W == 0 and (bsz * ltot) % MOE_TILE == 0 and (bsz * ll) % MOE_TILE == 0

    rows_mod = 8 * ((bsz + 1 + 7) // 8)
    cc = jnp.zeros((rows_mod, d), F32).at[:bsz].set(c).at[rows_mod - 1].set(c_ctx)
    modv = _ada(cc, ada_w, ada_b).reshape(depth * rows_mod * N_ADA, 1, d)

    r_pad = jnp.zeros((d, LANES), F32).at[:, :N_EXPERTS].set(router_w)
    r_hi = r_pad.astype(BF16)
    r_lo = (r_pad - r_hi.astype(F32)).astype(BF16)
    wg_all, wu_all, wd_all = e_w_gate.astype(BF16), e_w_up.astype(BF16), e_w_down.astype(BF16)

    m_qk, m_v = m_conv_w.shape[2], m_head_g.shape[1]
    w_in = m_w_in[0].astype(BF16)
    w_gates = jnp.zeros((d, LANES), BF16).at[:, :4 * M_HEADS].set(w_in[:, m_qk + 2 * m_v:])
    qk, v, o, gates = _mproj(x, ctx, modv, rows_mod, norm_mix_g[0:1], w_in[:, :m_qk], w_in[:, m_qk:m_qk + m_v],
                             w_in[:, m_qk + m_v:m_qk + 2 * m_v], w_gates, tm)
    hg = _mlstm_scan(qk, v, o, gates, m_conv_w[0], m_conv_b[0], m_gate_b[0], m_head_g[0], ll, lc)
    s1, h2, lg = _out0(hg, x, ctx, modv, rows_mod, norm_ffn_g[0:1], m_w_out[0].astype(BF16), r_hi, r_lo, tm)
    cls, h2 = _route(lg, router_bias, h2)
    tok0 = jnp.arange(bsz * ltot, dtype=jnp.int32)
    pos0 = tok0 % ltot
    pos0 = jnp.where(pos0 < ll, (pos0 % GRID_W) * rows + pos0 // GRID_W, pos0)
    dst0 = (tok0 // ltot) * (ltot + MOE_TILE) + pos0
    spare0 = tuple(b * (ltot + MOE_TILE) + ltot for b in range(1, bsz))
    f0 = _moe(h2, cls, dst0, ltot, spare0, bsz * (ltot + MOE_TILE), 0, wg_all, wu_all, wd_all)

    lbs = jnp.cumsum(jax.nn.softmax(h_lower_bounds.astype(F32), axis=0), axis=0)
    lb = (lbs - lbs[0])[1].reshape(2, -1)
    lbtab = jnp.stack([jnp.log(lb[0]), jnp.log1p(-lb[0]), jnp.log(lb[1]), jnp.log1p(-lb[1])])
    hw = h_w_in[0].astype(BF16)
    hk = lb.shape[1]
    ws = (hw[:, :hk], hw[:, hk:2 * hk], hw[:, 2 * hk:3 * hk], hw[:, 3 * hk:3 * hk + d], hw[:, 3 * hk + d:])
    s2, q, kf, kb, lff, lfb, iv, g = _hproj(s1, f0, modv, rows_mod, norm_mix_g[1:2], lbtab, ws, ll, lc)
    hg1 = _hgrn_scan(q, kf, kb, lff, lfb, iv, g, h_head_g[0], ll, lc)
    s3, h2b, lgb = _out1(hg1, s2, modv, rows_mod, norm_ffn_g[1:2], h_w_out[0].astype(BF16), r_hi, r_lo, tm)
    clsb, h2b = _route(lgb, router_bias, h2b)
    tok1 = jnp.arange(bsz * ll, dtype=jnp.int32)
    dst1 = (tok1 // ll) * (ll + MOE_TILE) + tok1 % ll
    spare1 = tuple(b * (ll + MOE_TILE) + ll for b in range(1, bsz))
    f1 = _moe(h2b, clsb, dst1, ll, spare1, bsz * (ll + MOE_TILE), 1, wg_all, wu_all, wd_all)

    return _final(s3, f1, modv, rows_mod, final_g.reshape(1, d), tm)
```

```python
import functools

import numpy as np
import jax
import jax.numpy as jnp
from jax import lax
from jax.experimental import pallas as pl
from jax.experimental.pallas import tpu as pltpu

F32 = jnp.float32
BF16 = jnp.bfloat16

EPS = 1e-6
LOG2_E = 1.4426950408889634
N_ADA = 6
GRID_W = 64
M_HEADS = 4
H_HEADS = 8
N_EXPERTS = 16
N_GROUPS = 4
E_PER_GROUP = N_EXPERTS // N_GROUPS
N_PAIRS = 6
N_CLASSES = N_GROUPS * N_PAIRS
PAIR_LO = (0, 0, 0, 1, 1, 2)
PAIR_HI = (1, 2, 3, 2, 3, 3)

LANES = 128
COL_TILE = 8
SCAN_CHUNK = 128
MOE_TILE = 256
VMEM_LIMIT = 56 * 1024 * 1024


def _cparams(sem):
    return pltpu.CompilerParams(dimension_semantics=sem, vmem_limit_bytes=VMEM_LIMIT)


def _dot(a, b):
    return jnp.dot(a, b, preferred_element_type=F32)


def _dot_nt(a, b):
    return lax.dot_general(a, b, (((1,), (1,)), ((), ())), preferred_element_type=F32)


def _dot_tn(a, b):
    return lax.dot_general(a, b, (((0,), (0,)), ((), ())), preferred_element_type=F32)


def _split_bf16(a):
    hi = a.astype(BF16)
    lo = (a - hi.astype(F32)).astype(BF16)
    return hi, lo


def _sigmoid(x):
    return 1.0 / (1.0 + jnp.exp(-x))


def _silu(x):
    return x * _sigmoid(x)


def _log1p_exp_neg_abs(x):
    return jnp.log(1.0 + jnp.exp(-jnp.abs(x)))


def _log_sigmoid(x):
    return jnp.minimum(x, 0.0) - _log1p_exp_neg_abs(x)


def _rms(x, g):
    return x * lax.rsqrt(jnp.mean(x * x, axis=-1, keepdims=True) + EPS) * g


def _ada_kernel(c_ref, w_ref, b_ref, o_ref):
    a = _silu(c_ref[...])
    a_hi, a_lo = _split_bf16(a)
    w_hi, w_lo = _split_bf16(w_ref[0])
    acc = _dot(a_hi, w_hi) + _dot(a_lo, w_hi) + _dot(a_hi, w_lo)
    o_ref[0] = acc + b_ref[0]


def _ada(cc, ada_w, ada_b):
    depth, d, n = ada_w.shape
    tn = 1024
    rows = cc.shape[0]
    return pl.pallas_call(
        _ada_kernel,
        grid=(depth, n // tn),
        in_specs=[
            pl.BlockSpec((rows, d), lambda l, j: (0, 0)),
            pl.BlockSpec((1, d, tn), lambda l, j: (l, 0, j)),
            pl.BlockSpec((1, 1, tn), lambda l, j: (l, 0, j)),
        ],
        out_specs=pl.BlockSpec((1, rows, tn), lambda l, j: (l, 0, j)),
        out_shape=jax.ShapeDtypeStruct((depth, rows, n), F32),
        compiler_params=_cparams(("arbitrary", "arbitrary")),
        name="ada_mod",
    )(cc, ada_w, ada_b.reshape(depth, 1, n))


def _mproj_kernel(n_lat_tiles, xl_ref, xc_ref, sc_ref, sh_ref, g_ref, wqk_ref, wv_ref, wo_ref, wg_ref,
                  qk_ref, v_ref, o_ref, gt_ref):
    j = pl.program_id(1)
    x = jnp.where(j < n_lat_tiles, xl_ref[0], xc_ref[0])
    h = (_rms(x, g_ref[...]) * (1.0 + sc_ref[0]) + sh_ref[0]).astype(BF16)
    qk_ref[0] = _dot(h, wqk_ref[...]).astype(BF16)
    v_ref[0] = _dot(h, wv_ref[...]).astype(BF16)
    o_ref[0] = _dot(h, wo_ref[...]).astype(BF16)
    gt_ref[0] = _dot(h, wg_ref[...])[:, :4 * M_HEADS]


def _mod_spec(d, layer, k, n_lat_tiles, rows):
    def imap(b, j):
        row = jnp.where(j < n_lat_tiles, b, rows - 1)
        return ((layer * rows + row) * N_ADA + k, 0, 0)
    return pl.BlockSpec((1, 1, d), imap)


def _mproj(x, ctx, modv, rows, norm_g, w_qk, w_v, w_o, w_g, tm):
    bsz, ll, d = x.shape
    lc = ctx.shape[1]
    nl, nc = ll // tm, lc // tm
    ltot = ll + lc
    wspec = lambda n: pl.BlockSpec((d, n), lambda b, j: (0, 0))
    tok = lambda n: pl.BlockSpec((1, tm, n), lambda b, j: (b, j, 0))
    return pl.pallas_call(
        functools.partial(_mproj_kernel, nl),
        grid=(bsz, nl + nc),
        in_specs=[
            pl.BlockSpec((1, tm, d), lambda b, j: (b, jnp.minimum(j, nl - 1), 0)),
            pl.BlockSpec((1, tm, d), lambda b, j: (b, jnp.maximum(j - nl, 0), 0)),
            _mod_spec(d, 0, 1, nl, rows), _mod_spec(d, 0, 0, nl, rows),
            pl.BlockSpec((1, d), lambda b, j: (0, 0)),
            wspec(w_qk.shape[1]), wspec(w_v.shape[1]), wspec(w_o.shape[1]), wspec(w_g.shape[1]),
        ],
        out_specs=[tok(w_qk.shape[1]), tok(w_v.shape[1]), tok(w_o.shape[1]), tok(4 * M_HEADS)],
        out_shape=[
            jax.ShapeDtypeStruct((bsz, ltot, w_qk.shape[1]), BF16),
            jax.ShapeDtypeStruct((bsz, ltot, w_v.shape[1]), BF16),
            jax.ShapeDtypeStruct((bsz, ltot, w_o.shape[1]), BF16),
            jax.ShapeDtypeStruct((bsz, ltot, 4 * M_HEADS), F32),
        ],
        compiler_params=_cparams(("arbitrary", "arbitrary")),
        name="mlstm_in_proj",
    )(x, ctx, modv, modv, norm_g, w_qk, w_v, w_o, w_g)


def _mlstm_kernel(t, ll, lc, hp, q_ref, k_ref, v_ref, o_ref, gc_ref, gr_ref, cwq_ref, cwk_ref, cbq_ref, cbk_ref,
                  gbc_ref, gbr_ref, hg_ref, out_ref, qs, ks, hf, hb, c_s, m_s):
    ltot = ll + lc
    dk = q_ref.shape[2] // hp
    dv = v_ref.shape[2] // hp
    row = lax.broadcasted_iota(jnp.int32, (ltot, 1), 0)
    first = (row == 0) | (row == ll)
    last = (row == ll - 1) | (row == ltot - 1)

    def conv(x_ref, w_ref, b_ref):
        x = x_ref[0].astype(F32)
        w = w_ref[...]
        xp = jnp.where(first, 0.0, pltpu.roll(x, 1, 0))
        xn = jnp.where(last, 0.0, pltpu.roll(x, ltot - 1, 0))
        return _silu(xp * w[0:1] + x * w[1:2] + xn * w[2:3] + b_ref[...])

    qs[...] = (conv(q_ref, cwq_ref, cbq_ref) * (dk ** -0.5)).astype(BF16)
    ks[...] = conv(k_ref, cwk_ref, cbk_ref).astype(BF16)

    c_s[...] = jnp.zeros(c_s.shape, F32)
    m_s[...] = jnp.zeros(m_s.shape, F32)

    ti = lax.broadcasted_iota(jnp.int32, (t, t), 0)
    si = lax.broadcasted_iota(jnp.int32, (t, t), 1)

    h_dir = (hf, hb)
    ones = jnp.ones((t, LANES), BF16)

    def chunks(starts):
        jobs = [(j, d) for j in range(hp) for d in range(2)]
        st = {}
        for j, d in jobs:
            rows = pl.ds(starts[d], t)
            gc = gc_ref[0, j, rows, :] + gbc_ref[j]
            gr = gr_ref[0, j, :, rows] + gbr_ref[j]
            ig_c, lf_c = gc[:, 2 * d:2 * d + 1], _log_sigmoid(gc[:, 2 * d + 1:2 * d + 2])
            ig_r, lf_r = gr[2 * d:2 * d + 1, :], _log_sigmoid(gr[2 * d + 1:2 * d + 2, :])
            seen = (si <= ti) if d == 0 else (si >= ti)
            seen_t = (ti <= si) if d == 0 else (ti >= si)
            b_c = jnp.sum(jnp.where(seen, lf_r, 0.0), axis=1, keepdims=True)
            b_r = jnp.sum(jnp.where(seen_t, lf_c, 0.0), axis=0, keepdims=True)
            total = jnp.sum(lf_r, axis=1, keepdims=True)
            st[j, d] = dict(rows=rows, ig_c=ig_c, ig_r=ig_r, b_c=b_c, b_r=b_r, total=total, seen=seen,
                            m_prev=m_s[2 * j + d][0:1, 0:1])
        for j, d in jobs:
            s = st[j, d]
            s["q"] = qs[s["rows"], j * dk:(j + 1) * dk]
            s["k"] = ks[s["rows"], j * dk:(j + 1) * dk]
            s["v"] = jnp.concatenate([v_ref[0, s["rows"], j * dv:(j + 1) * dv], ones], axis=1)
            s["qk"] = _dot_nt(s["q"], s["k"])
            s["qc"] = _dot(s["q"], c_s[2 * j + d].astype(BF16))
        for j, d in jobs:
            s = st[j, d]
            log_d = jnp.where(s["seen"], s["b_c"] - s["b_r"] + s["ig_r"], -jnp.inf)
            log_inter = s["b_c"] + s["m_prev"]
            m_t = jnp.maximum(log_inter, jnp.max(log_d, axis=1, keepdims=True))
            s["w_ts"] = jnp.exp(log_d - m_t) * s["qk"]
            s["inter"] = jnp.exp(log_inter - m_t)
            s["m_t"] = m_t
        for j, d in jobs:
            s = st[j, d]
            s["num"] = _dot(s["w_ts"].astype(BF16), s["v"]) + s["inter"] * s["qc"]
        for j, d in jobs:
            s = st[j, d]
            den = s["num"][:, dv:]
            scale = 1.0 / jnp.maximum(jnp.abs(den), jnp.exp(-s["m_t"]))
            h_dir[d][s["rows"], j * dv:(j + 1) * dv] = s["num"][:, :dv] * jnp.concatenate([scale] * (dv // LANES), axis=1)
        for j, d in jobs:
            s = st[j, d]
            g_c = s["total"] - s["b_c"] + s["ig_c"]
            g_r = s["total"] - s["b_r"] + s["ig_r"]
            m_new = jnp.maximum(s["total"] + s["m_prev"], jnp.max(g_r, axis=1, keepdims=True))
            w_c = jnp.exp(g_c - m_new)
            decay = jnp.exp(s["total"] + s["m_prev"] - m_new)
            wv = (w_c * s["v"].astype(F32)).astype(BF16)
            c_s[2 * j + d] = decay * c_s[2 * j + d] + _dot_tn(s["k"], wv)
            m_s[2 * j + d] = jnp.broadcast_to(m_new, m_s.shape[1:])

    ncc, ncl = lc // t, ll // t
    for i in range(ncc):
        chunks((ll + i * t, ll + (ncc - 1 - i) * t))

    def body(i, carry):
        chunks((pl.multiple_of(i * t, t), pl.multiple_of((ncl - 1 - i) * t, t)))
        return carry

    lax.fori_loop(0, ncl, body, 0)

    def epilogue(i, carry):
        s = pl.multiple_of(i * t, t)
        for j in range(hp):
            cols = slice(j * dv, (j + 1) * dv)
            y = _rms(hf[pl.ds(s, t), cols] + hb[pl.ds(s, t), cols], hg_ref[:, cols])
            out_ref[0, pl.ds(s, t), cols] = (y * _sigmoid(o_ref[0, pl.ds(s, t), cols].astype(F32))).astype(BF16)
        return carry

    lax.fori_loop(0, ltot // t, epilogue, 0, unroll=2)


def _mlstm_scan(qk, v, o, gates, conv_w, conv_b, gate_b, head_g, ll, lc):
    bsz, ltot, _ = qk.shape
    nh = M_HEADS
    hp = 2
    dk = qk.shape[2] // (2 * nh)
    dv = v.shape[2] // nh
    t = SCAN_CHUNK
    g4 = gates.reshape(bsz, ltot, 4, nh).transpose(0, 3, 1, 2)
    g4t = g4.transpose(0, 1, 3, 2)
    gb = gate_b.reshape(4, nh).T
    return pl.pallas_call(
        functools.partial(_mlstm_kernel, t, ll, lc, hp),
        grid=(bsz, nh // hp),
        in_specs=[
            pl.BlockSpec((1, ltot, hp * dk), lambda b, h: (b, 0, h)),
            pl.BlockSpec((1, ltot, hp * dk), lambda b, h: (b, 0, nh // hp + h)),
            pl.BlockSpec((1, ltot, hp * dv), lambda b, h: (b, 0, h)),
            pl.BlockSpec((1, ltot, hp * dv), lambda b, h: (b, 0, h)),
            pl.BlockSpec((1, hp, ltot, 4), lambda b, h: (b, h, 0, 0)),
            pl.BlockSpec((1, hp, 4, ltot), lambda b, h: (b, h, 0, 0)),
            pl.BlockSpec((3, hp * dk), lambda b, h: (0, h)),
            pl.BlockSpec((3, hp * dk), lambda b, h: (0, nh // hp + h)),
            pl.BlockSpec((1, hp * dk), lambda b, h: (0, h)),
            pl.BlockSpec((1, hp * dk), lambda b, h: (0, nh // hp + h)),
            pl.BlockSpec((hp, 1, 4), lambda b, h: (h, 0, 0)),
            pl.BlockSpec((hp, 4, 1), lambda b, h: (h, 0, 0)),
            pl.BlockSpec((1, hp * dv), lambda b, h: (0, h)),
        ],
        out_specs=pl.BlockSpec((1, ltot, hp * dv), lambda b, h: (b, 0, h)),
        out_shape=jax.ShapeDtypeStruct((bsz, ltot, nh * dv), BF16),
        scratch_shapes=[
            pltpu.VMEM((ltot, hp * dk), BF16), pltpu.VMEM((ltot, hp * dk), BF16),
            pltpu.VMEM((ltot, hp * dv), F32), pltpu.VMEM((ltot, hp * dv), F32),
            pltpu.VMEM((2 * hp, dk, dv + LANES), F32), pltpu.VMEM((2 * hp, 8, LANES), F32),
        ],
        compiler_params=_cparams(("arbitrary", "arbitrary")),
        name="mlstm_scan",
    )(qk, qk, v, o, g4, g4t, conv_w, conv_w, conv_b.reshape(1, -1), conv_b.reshape(1, -1),
      gb.reshape(nh, 1, 4), gb.reshape(nh, 4, 1), head_g.reshape(1, -1))


def _router_logits_t(h2, rhi_ref, rlo_ref):
    h_hi, h_lo = _split_bf16(h2)
    lg = _dot(h_hi, rhi_ref[...]) + _dot(h_lo, rhi_ref[...]) + _dot(h_hi, rlo_ref[...])
    return lg.T[:N_EXPERTS, :]


def _out0_kernel(n_lat_tiles, hg_ref, xl_ref, xc_ref, g1_ref, sc_ref, sh_ref, ng_ref, wo_ref, rhi_ref, rlo_ref,
                 s_ref, h2_ref, lg_ref):
    j = pl.program_id(1)
    x = jnp.where(j < n_lat_tiles, xl_ref[0], xc_ref[0])
    s = x + g1_ref[0] * _dot(hg_ref[0], wo_ref[...])
    s_ref[0] = s
    h2 = _rms(s, ng_ref[...]) * (1.0 + sc_ref[0]) + sh_ref[0]
    h2_ref[:, :h2.shape[1]] = h2
    h2_ref[:, h2.shape[1]:] = jnp.zeros((h2.shape[0], LANES), F32)
    lg_ref[...] = _router_logits_t(h2, rhi_ref, rlo_ref)


def _out0(hg, x, ctx, modv, rows, norm_g, w_out, r_hi, r_lo, tm):
    bsz, ll, d = x.shape
    lc = ctx.shape[1]
    nl, nc = ll // tm, lc // tm
    nt = nl + nc
    ltot = ll + lc
    return pl.pallas_call(
        functools.partial(_out0_kernel, nl),
        grid=(bsz, nt),
        in_specs=[
            pl.BlockSpec((1, tm, hg.shape[2]), lambda b, j: (b, j, 0)),
            pl.BlockSpec((1, tm, d), lambda b, j: (b, jnp.minimum(j, nl - 1), 0)),
            pl.BlockSpec((1, tm, d), lambda b, j: (b, jnp.maximum(j - nl, 0), 0)),
            _mod_spec(d, 0, 2, nl, rows), _mod_spec(d, 0, 4, nl, rows), _mod_spec(d, 0, 3, nl, rows),
            pl.BlockSpec((1, d), lambda b, j: (0, 0)),
            pl.BlockSpec(w_out.shape, lambda b, j: (0, 0)),
            pl.BlockSpec(r_hi.shape, lambda b, j: (0, 0)),
            pl.BlockSpec(r_lo.shape, lambda b, j: (0, 0)),
        ],
        out_specs=[
            pl.BlockSpec((1, tm, d), lambda b, j: (b, j, 0)),
            pl.BlockSpec((tm, d + LANES), lambda b, j: (b * nt + j, 0)),
            pl.BlockSpec((N_EXPERTS, tm), lambda b, j: (0, b * nt + j)),
        ],
        out_shape=[
            jax.ShapeDtypeStruct((bsz, ltot, d), F32),
            jax.ShapeDtypeStruct((bsz * ltot, d + LANES), F32),
            jax.ShapeDtypeStruct((N_EXPERTS, bsz * ltot), F32),
        ],
        compiler_params=_cparams(("arbitrary", "arbitrary")),
        name="mlstm_out_proj",
    )(hg, x, ctx, modv, modv, modv, norm_g, w_out, r_hi, r_lo)


def _route_kernel(lg_ref, bias_ref, h2x_hbm, cls_ref, w_ref):
    del h2x_hbm
    s = _sigmoid(lg_ref[...])
    sel = s + bias_ref[...]
    srow = [s[e:e + 1, :] for e in range(N_EXPERTS)]
    row = [sel[e:e + 1, :] for e in range(N_EXPERTS)]
    best = jnp.zeros(row[0].shape, jnp.int32)
    best_score = None
    for g in range(N_GROUPS):
        r = row[g * E_PER_GROUP:(g + 1) * E_PER_GROUP]
        score = None
        for lo, hi in zip(PAIR_LO, PAIR_HI):
            pair = r[lo] + r[hi]
            score = pair if score is None else jnp.maximum(score, pair)
        if g == 0:
            best_score = score
        else:
            better = score > best_score
            best = jnp.where(better, g, best)
            best_score = jnp.where(better, score, best_score)
    gs = [row[i] for i in range(E_PER_GROUP)]
    gw = [srow[i] for i in range(E_PER_GROUP)]
    for g in range(1, N_GROUPS):
        hit = best == g
        gs = [jnp.where(hit, row[g * E_PER_GROUP + i], gs[i]) for i in range(E_PER_GROUP)]
        gw = [jnp.where(hit, srow[g * E_PER_GROUP + i], gw[i]) for i in range(E_PER_GROUP)]
    keep = []
    for i in range(E_PER_GROUP):
        beaten = jnp.zeros(best.shape, jnp.int32)
        for j in range(E_PER_GROUP):
            if j == i:
                continue
            wins = (gs[j] > gs[i]) | ((gs[j] == gs[i]) & (j < i))
            beaten = beaten + wins.astype(jnp.int32)
        keep.append(beaten < 2)
    pair_id = jnp.zeros(best.shape, jnp.int32)
    w_lo = jnp.zeros(best.shape, F32)
    w_hi = jnp.zeros(best.shape, F32)
    for p, (lo, hi) in enumerate(zip(PAIR_LO, PAIR_HI)):
        hit = keep[lo] & keep[hi]
        pair_id = jnp.where(hit, p, pair_id)
        w_lo = jnp.where(hit, gw[lo], w_lo)
        w_hi = jnp.where(hit, gw[hi], w_hi)
    tot = w_lo + w_hi
    cls_ref[...] = best * N_PAIRS + pair_id
    lanes_t = jnp.concatenate([w_lo / tot, w_hi / tot, jnp.zeros((LANES - 2, w_lo.shape[1]), F32)], axis=0)
    w_ref[...] = lanes_t.T


def _route(logits_t, router_bias, h2x):
    n = logits_t.shape[1]
    d = h2x.shape[1] - LANES
    tn = next(cand for cand in (2048, 1024, 512, 256, 128) if n % cand == 0)
    return pl.pallas_call(
        _route_kernel,
        grid=(n // tn,),
        in_specs=[
            pl.BlockSpec((N_EXPERTS, tn), lambda i: (0, i)),
            pl.BlockSpec((N_EXPERTS, 1), lambda i: (0, 0)),
            pl.BlockSpec(memory_space=pl.ANY),
        ],
        out_specs=[pl.BlockSpec((1, tn), lambda i: (0, i)), pl.BlockSpec((tn, LANES), lambda i: (i, d // LANES))],
        out_shape=[jax.ShapeDtypeStruct((1, n), jnp.int32), jax.ShapeDtypeStruct(h2x.shape, F32)],
        input_output_aliases={2: 1},
        compiler_params=_cparams(("arbitrary",)),
        name="moe_route",
    )(logits_t, router_bias.reshape(N_EXPERTS, 1).astype(F32), h2x)


def _moe_kernel(tm, dump_base, spare_rows, nu_ref, e0_ref, e1_ref, base_ref, nv_ref, tok_ref, dst_ref, h2_hbm,
                wg0_ref, wu0_ref, wd0_ref, wg1_ref, wu1_ref, wd1_ref, f_hbm,
                xb0, xb1, yb0, yb1, xbf, gsem, ssem, zsem, idle_sem):
    del e0_ref, e1_ref
    t = pl.program_id(0)
    n_used = nu_ref[0]
    d = xbf.shape[1]
    xbuf, ybuf = (xb0, xb1), (yb0, yb1)

    def gather_row(base, r, slot):
        tok = tok_ref[base + r]
        pltpu.make_async_copy(h2_hbm.at[pl.ds(tok, 1)], xbuf[slot].at[pl.ds(r, 1)], gsem.at[slot]).start()

    def scatter_row(base, nv, r, slot):
        row = jnp.where(r < nv, dst_ref[base + r], dump_base + r)
        pltpu.make_async_copy(ybuf[slot].at[pl.ds(r, 1)], f_hbm.at[pl.ds(row, 1)], ssem.at[slot]).start(priority=1)

    def wait_gather(slot):
        pltpu.make_async_copy(h2_hbm.at[pl.ds(0, tm)], xbuf[slot], gsem.at[slot]).wait()

    def wait_scatter(slot):
        pltpu.make_async_copy(ybuf[slot], f_hbm.at[pl.ds(0, tm)], ssem.at[slot]).wait()

    @pl.when(t == 0)
    def _():
        yb1[...] = jnp.zeros(yb1.shape, F32)
        fills = [pltpu.make_async_copy(yb1, f_hbm.at[pl.ds(row, tm)], zsem) for row in spare_rows]
        for fill in fills:
            fill.start()
        for fill in fills:
            fill.wait()
        base0 = base_ref[1]

        def first(r, carry):
            gather_row(base0, r, 0)
            return carry

        lax.fori_loop(0, tm, first, 0)

    def step(cur):
        nxt = 1 - cur
        base_next = base_ref[t + 2]
        base_prev = base_ref[t]
        nv_prev = nv_ref[t]

        @pl.when(t < n_used)
        def _():
            wait_gather(cur)
            xbf[...] = xbuf[cur][:, :d].astype(BF16)
            w = xbuf[cur][:, d:]

            for r in range(tm):
                gather_row(base_next, r, nxt)
            x = xbf[...]
            a0 = (_silu(_dot(x, wg0_ref[0, 0])) * _dot(x, wu0_ref[0, 0]) * w[:, 0:1]).astype(BF16)

            z = pl.semaphore_read(idle_sem)
            zero = lax.shift_right_logical(z.astype(jnp.uint32), jnp.uint32(32)).astype(jnp.int32)
            off = pl.multiple_of(zero * 16, 16)

            for r in range(tm):
                scatter_row(base_prev, nv_prev, r, nxt)
            x = xbf[pl.ds(off, tm), :]
            a1 = (_silu(_dot(x, wg1_ref[0, 0])) * _dot(x, wu1_ref[0, 0]) * w[:, 1:2]).astype(BF16)
            y = _dot(a0, wd0_ref[0, 0]) + _dot(a1, wd1_ref[0, 0])

            @pl.when(t >= 1)
            def _():
                wait_scatter(cur)

            ybuf[cur][...] = y

        @pl.when(t == n_used)
        def _():
            wait_gather(cur)
            wait_scatter(cur)

            def last(r, carry):
                scatter_row(base_prev, nv_prev, r, nxt)
                return carry

            lax.fori_loop(0, tm, last, 0)
            wait_scatter(nxt)

    @pl.when(t % 2 == 0)
    def _():
        step(0)

    @pl.when(t % 2 == 1)
    def _():
        step(1)


def _moe(h2x, cls, dst_of_token, dump_base, spare_rows, out_rows, layer, wg, wu, wd):
    n = h2x.shape[0]
    d = h2x.shape[1] - LANES
    tm = MOE_TILE
    n_tiles = n // tm + N_CLASSES
    cls = cls.reshape(n)
    _, tok_sorted, dst_sorted = lax.sort((cls, jnp.arange(n, dtype=jnp.int32), dst_of_token), num_keys=1)
    tail = jnp.zeros((tm,), jnp.int32)
    tok_sorted = jnp.concatenate([tok_sorted, tail])
    dst_sorted = jnp.concatenate([dst_sorted, tail])
    counts = jnp.sum(cls[:, None] == jnp.arange(N_CLASSES, dtype=jnp.int32)[None, :], axis=0).astype(jnp.int32)
    tiles_per = (counts + tm - 1) // tm
    tile_end = jnp.cumsum(tiles_per)
    tile_start = tile_end - tiles_per
    first_sorted = jnp.cumsum(counts) - counts
    n_used = tile_end[-1]
    tile_id = jnp.arange(n_tiles, dtype=jnp.int32)
    tile_cls = jnp.searchsorted(tile_end, jnp.minimum(tile_id, n_used - 1), side="right").astype(jnp.int32)
    tile_cls = jnp.minimum(tile_cls, N_CLASSES - 1)
    group, pair = tile_cls // N_PAIRS, tile_cls % N_PAIRS
    e0 = group * E_PER_GROUP + jnp.asarray(PAIR_LO, jnp.int32)[pair]
    e1 = group * E_PER_GROUP + jnp.asarray(PAIR_HI, jnp.int32)[pair]
    in_class = (tile_id - tile_start[tile_cls]) * tm
    used = tile_id < n_used
    base = jnp.where(used, first_sorted[tile_cls] + in_class, 0)
    nv = jnp.where(used, jnp.clip(counts[tile_cls] - in_class, 0, tm), 0)
    guard = jnp.zeros((1,), jnp.int32)
    base = jnp.concatenate([guard, base, guard]).astype(jnp.int32)
    nv = jnp.concatenate([guard, nv, guard]).astype(jnp.int32)

    de = wg.shape[3]
    smem = pl.BlockSpec(memory_space=pltpu.SMEM)
    up0 = pl.BlockSpec((1, 1, d, de), lambda i, nu, e0, e1, base, nv: (layer, e0[i], 0, 0))
    up1 = pl.BlockSpec((1, 1, d, de), lambda i, nu, e0, e1, base, nv: (layer, e1[i], 0, 0))
    dn0 = pl.BlockSpec((1, 1, de, d), lambda i, nu, e0, e1, base, nv: (layer, e0[i], 0, 0))
    dn1 = pl.BlockSpec((1, 1, de, d), lambda i, nu, e0, e1, base, nv: (layer, e1[i], 0, 0))
    return pl.pallas_call(
        functools.partial(_moe_kernel, tm, dump_base, spare_rows),
        grid_spec=pltpu.PrefetchScalarGridSpec(
            num_scalar_prefetch=5,
            grid=(n_tiles,),
            in_specs=[
                smem, smem,
                pl.BlockSpec(memory_space=pl.ANY),
                up0, up0, dn0, up1, up1, dn1,
            ],
            out_specs=pl.BlockSpec(memory_space=pl.ANY),
            scratch_shapes=[
                pltpu.VMEM((tm, d + LANES), F32), pltpu.VMEM((tm, d + LANES), F32),
                pltpu.VMEM((tm, d), F32), pltpu.VMEM((tm, d), F32),
                pltpu.VMEM((tm, d), BF16),
                pltpu.SemaphoreType.DMA((2,)), pltpu.SemaphoreType.DMA((2,)), pltpu.SemaphoreType.DMA(()),
                pltpu.SemaphoreType.REGULAR(()),
            ],
        ),
        out_shape=jax.ShapeDtypeStruct((out_rows, d), F32),
        compiler_params=_cparams(("arbitrary",)),
        name="moe_experts",
    )(n_used.reshape(1).astype(jnp.int32), e0, e1, base, nv, tok_sorted, dst_sorted, h2x,
      wg, wu, wd, wg, wu, wd)


def _hproj_kernel(n_lat_tiles, rows, s_hbm, sc_ref_, f_ref, g2_ref, sc_ref, sh_ref, ng_ref, lb_ref,
                  wq_ref, wzf_ref, wzb_ref, wi_ref, wg_ref,
                  s2_ref, q_ref, kf_ref, kb_ref, lff_ref, lfb_ref, i_ref, g_ref, xt, sbuf, sem):
    b, j = pl.program_id(0), pl.program_id(1)
    n_lat = pl.num_programs(0) * n_lat_tiles
    g2 = g2_ref[0]

    def column_copies(tile, slot):
        bb, jj = tile // n_lat_tiles, tile % n_lat_tiles
        return [pltpu.make_async_copy(s_hbm.at[bb, pl.ds(0, rows), jj * COL_TILE + c, :],
                                      sbuf.at[slot, pl.ds(c * rows, rows), :], sem.at[slot])
                for c in range(COL_TILE)]

    @pl.when(j < n_lat_tiles)
    def _():
        tile = b * n_lat_tiles + j
        slot = tile % 2

        @pl.when(tile == 0)
        def _():
            for cp in column_copies(tile, slot):
                cp.start()

        @pl.when(tile + 1 < n_lat)
        def _():
            for cp in column_copies(tile + 1, 1 - slot):
                cp.start()

        for cp in column_copies(tile, slot):
            cp.wait()
        s = sbuf[slot] + g2 * f_ref[0]
        s2_ref[0] = s
        xt[...] = s

    @pl.when(j >= n_lat_tiles)
    def _():
        xt[...] = sc_ref_[0] + g2 * f_ref[0]

    h = (_rms(xt[...], ng_ref[...]) * (1.0 + sc_ref[0]) + sh_ref[0]).astype(BF16)
    n_chunks = 4
    width = wq_ref.shape[1] // n_chunks

    def gates(dd, k_ref, lf_ref, cols, z):
        log_lb = lb_ref[2 * dd:2 * dd + 1, cols]
        log_1mlb = lb_ref[2 * dd + 1:2 * dd + 2, cols]
        a = log_1mlb + _log_sigmoid(z)
        lf_ref[0, :, cols] = jnp.maximum(log_lb, a) + _log1p_exp_neg_abs(log_lb - a)
        k_ref[0, :, cols] = jnp.exp(a - z).astype(BF16)

    def store_q(cols, y):
        q_ref[0, :, cols] = _silu(y).astype(BF16)

    def store_plain(out_ref, cols, y):
        out_ref[0, :, cols] = y.astype(BF16)

    gates_f = functools.partial(gates, 0, kf_ref, lff_ref)
    gates_b = functools.partial(gates, 1, kb_ref, lfb_ref)
    plain_i = functools.partial(store_plain, i_ref)
    plain_g = functools.partial(store_plain, g_ref)
    work = []
    for c in range(n_chunks):
        work += [(wzf_ref, gates_f, c), (wi_ref, plain_i, c)]
    for c in range(n_chunks):
        work += [(wzb_ref, gates_b, c), (wg_ref, plain_g, c)]
    work += [(wq_ref, store_q, c) for c in range(n_chunks)]
    pending = None
    for w_ref, tail, c in work:
        cols = slice(c * width, (c + 1) * width)
        y = _dot(h, w_ref[:, cols])
        if pending is not None:
            pending()
        pending = functools.partial(tail, cols, y)
    pending()


def _hproj(s1, f0, modv, rows_mod, norm_g, lbtab, ws, ll, lc):
    bsz, ltot, d = s1.shape
    rows = ll // GRID_W
    tm = COL_TILE * rows
    nl, nc = GRID_W // COL_TILE, lc // tm
    s_grid = s1.reshape(bsz, ltot // GRID_W, GRID_W, d)
    f_tok = f0.reshape(bsz, -1, d)
    ctx = pl.BlockSpec((1, tm, d), lambda b, j: (b, ll // tm + jnp.maximum(j - nl, 0), 0))
    wspec = pl.BlockSpec((d, d), lambda b, j: (0, 0))
    tok = pl.BlockSpec((1, tm, d), lambda b, j: (b, j, 0))
    lat = pl.BlockSpec((1, tm, d), lambda b, j: (b, jnp.minimum(j, nl - 1), 0))

    def g2_map(b, j):
        row = jnp.where(j < nl, b, rows_mod - 1)
        return ((0 * rows_mod + row) * N_ADA + 5, 0, 0)

    outs = pl.pallas_call(
        functools.partial(_hproj_kernel, nl, rows),
        grid=(bsz, nl + nc),
        in_specs=[
            pl.BlockSpec(memory_space=pl.ANY), ctx, tok,
            pl.BlockSpec((1, 1, d), g2_map),
            _mod_spec(d, 1, 1, nl, rows_mod), _mod_spec(d, 1, 0, nl, rows_mod),
            pl.BlockSpec((1, d), lambda b, j: (0, 0)),
            pl.BlockSpec((4, d), lambda b, j: (0, 0)),
            wspec, wspec, wspec, wspec, wspec,
        ],
        out_specs=[lat, tok, tok, tok, tok, tok, tok, tok],
        out_shape=[jax.ShapeDtypeStruct((bsz, ll, d), F32)]
        + [jax.ShapeDtypeStruct((bsz, ltot, d), dt) for dt in (BF16, BF16, BF16, F32, F32, BF16, BF16)],
        scratch_shapes=[pltpu.VMEM((tm, d), F32), pltpu.VMEM((2, tm, d), F32), pltpu.SemaphoreType.DMA((2,))],
        compiler_params=_cparams(("arbitrary", "arbitrary")),
        name="hgrn_in_proj",
    )(s_grid, s1, f_tok, modv, modv, modv, norm_g, lbtab, *ws)
    return outs


def _hgrn_kernel(t, ll, lc, q_ref, kf_ref, kb_ref, lff_ref, lfb_ref, i_ref, g_ref, hg_ref, out_ref,
                 of, ob, inc_s, dec_s, qd_s):
    levels = []
    m = 2
    while m <= t:
        levels.append(m)
        m *= 2
    ti = lax.broadcasted_iota(jnp.int32, (t, t), 0)
    si = lax.broadcasted_iota(jnp.int32, (t, t), 1)
    xor = ti ^ si
    level = jnp.zeros((t, t), jnp.int32)
    for m in levels:
        level = level + (xor >= m // 2).astype(jnp.int32)
    feeds = ((si <= ti).astype(BF16), (si >= ti).astype(BF16))

    row = lax.broadcasted_iota(jnp.int32, (t, 1), 0)
    feeds_twice = tuple(jnp.concatenate([f, f], axis=1) for f in feeds)

    def block_ref(b, m, d):
        half = m // 2
        pos = half - 1 if d == 0 else half
        if m >= 8:
            dk = b.shape[1]
            b3 = b.reshape(t // m, m, dk)
            return jnp.broadcast_to(b3[:, pos:pos + 1, :], b3.shape).reshape(t, dk)
        r = row % m
        out = b
        for res in range(m):
            if res != pos:
                out = jnp.where(r == res, pltpu.roll(b, (res - pos) % t, 0), out)
        return out

    def neg_abs(x):
        sign = jnp.uint32(0x80000000)
        return lax.bitcast_convert_type(lax.bitcast_convert_type(x, jnp.uint32) | sign, F32)

    o_dir = (of, ob)

    def prepare(jobs, need_out):
        work = []
        for ci, d in jobs:
            rows = pl.ds(ci * t if isinstance(ci, int) else pl.multiple_of(ci * t, t), t)
            lf = (lff_ref if d == 0 else lfb_ref)[0, rows, :] * LOG2_E
            hi = lf.astype(BF16)
            lo = (lf - hi.astype(F32)).astype(BF16)
            b = _dot(feeds_twice[d], jnp.concatenate([hi, lo], axis=0))
            work.append((ci, d, rows, b))
        for ci, d, rows, b in work:
            b_tot = b[t - 1:t, :] if d == 0 else b[0:1, :]
            k = (kf_ref if d == 0 else kb_ref)[0, rows, :]
            inc_s[d, ci] = _dot_tn(i_ref[0, rows, :], k * jnp.exp2(b_tot - b).astype(BF16))
            dec_s[d, ci] = jnp.exp2(b_tot)
            if need_out:
                qd_s[d, rows, :] = q_ref[0, rows, :] * jnp.exp2(b).astype(BF16)
        if not need_out:
            return
        acc = []
        for ci, d, rows, b in work:
            acc.append(_dot_nt(q_ref[0, rows, :], (kf_ref if d == 0 else kb_ref)[0, rows, :]))
        for idx, m in enumerate(levels):
            for n, (ci, d, rows, b) in enumerate(work):
                e = jnp.exp2(neg_abs(b - block_ref(b, m, d))).astype(BF16)
                p = _dot_nt(q_ref[0, rows, :] * e, (kf_ref if d == 0 else kb_ref)[0, rows, :] * e)
                acc[n] = jnp.where(level == idx + 1, p, acc[n])
        for n, (ci, d, rows, b) in enumerate(work):
            o_dir[d][rows, :] = _dot(acc[n].astype(BF16) * feeds[d], i_ref[0, rows, :])

    ncc, ncl = lc // t, ll // t
    prepare([(ncl + i, d) for i in range(ncc) for d in range(2)], False)

    group = next(g for g in (8, 4, 2, 1) if ncl % g == 0)

    def body(i, carry):
        prepare([(group * i + j, d) for j in range(group) for d in range(2)], True)
        return carry

    lax.fori_loop(0, ncl // group, body, 0)

    for d in range(2):
        order = list(range(ncl, ncl + ncc)) + list(range(ncl))
        if d == 1:
            order = list(range(ncl + ncc - 1, ncl - 1, -1)) + list(range(ncl - 1, -1, -1))
        st = jnp.zeros(inc_s.shape[2:], F32)
        for ci in order:
            if ci < ncl:
                rows = pl.ds(ci * t, t)
                o_dir[d][rows, :] = o_dir[d][rows, :] + _dot_nt(qd_s[d, rows, :], st.astype(BF16))
            st = dec_s[d, ci] * st + inc_s[d, ci]

    def epilogue(i, carry):
        s = pl.multiple_of(i * t, t)
        y = _rms(of[pl.ds(s, t), :] + ob[pl.ds(s, t), :], hg_ref[...])
        out_ref[0, pl.ds(s, t), :] = (y * _silu(g_ref[0, pl.ds(s, t), :].astype(F32))).astype(BF16)
        return carry

    lax.fori_loop(0, ncl, epilogue, 0, unroll=4)


def _hgrn_scan(q, kf, kb, lff, lfb, iv, g, head_g, ll, lc):
    bsz, ltot, hk = q.shape
    nh = H_HEADS
    dk = hk // nh
    t = SCAN_CHUNK
    full = pl.BlockSpec((1, ltot, dk), lambda b, h: (b, 0, h))
    lat = pl.BlockSpec((1, ll, dk), lambda b, h: (b, 0, h))
    return pl.pallas_call(
        functools.partial(_hgrn_kernel, t, ll, lc),
        grid=(bsz, nh),
        in_specs=[full, full, full, full, full, full, lat, pl.BlockSpec((1, dk), lambda b, h: (0, h))],
        out_specs=lat,
        out_shape=jax.ShapeDtypeStruct((bsz, ll, hk), BF16),
        scratch_shapes=[pltpu.VMEM((ll, dk), F32), pltpu.VMEM((ll, dk), F32),
                        pltpu.VMEM((2, ltot // t, dk, dk), F32), pltpu.VMEM((2, ltot // t, 1, dk), F32),
                        pltpu.VMEM((2, ll, dk), BF16)],
        compiler_params=_cparams(("arbitrary", "arbitrary")),
        name="hgrn_scan",
    )(q, kf, kb, lff, lfb, iv, g, head_g.reshape(1, -1))


def _out1_kernel(hg_ref, s_ref, g1_ref, sc_ref, sh_ref, ng_ref, wo_ref, rhi_ref, rlo_ref,
                 s3_ref, h2_ref, lg_ref):
    s = s_ref[0] + g1_ref[0] * _dot(hg_ref[0], wo_ref[...])
    s3_ref[0] = s
    h2 = _rms(s, ng_ref[...]) * (1.0 + sc_ref[0]) + sh_ref[0]
    h2_ref[:, :h2.shape[1]] = h2
    h2_ref[:, h2.shape[1]:] = jnp.zeros((h2.shape[0], LANES), F32)
    lg_ref[...] = _router_logits_t(h2, rhi_ref, rlo_ref)


def _out1(hg, s2, modv, rows_mod, norm_g, w_out, r_hi, r_lo, tm):
    bsz, ll, hv = hg.shape
    d = s2.shape[2]
    nl = ll // tm
    cm = pl.BlockSpec((1, tm, d), lambda b, j: (b, j, 0))
    return pl.pallas_call(
        _out1_kernel,
        grid=(bsz, nl),
        in_specs=[
            pl.BlockSpec((1, tm, hv), lambda b, j: (b, j, 0)),
            cm,
            _mod_spec(d, 1, 2, nl, rows_mod), _mod_spec(d, 1, 4, nl, rows_mod), _mod_spec(d, 1, 3, nl, rows_mod),
            pl.BlockSpec((1, d), lambda b, j: (0, 0)),
            pl.BlockSpec(w_out.shape, lambda b, j: (0, 0)),
            pl.BlockSpec(r_hi.shape, lambda b, j: (0, 0)),
            pl.BlockSpec(r_lo.shape, lambda b, j: (0, 0)),
        ],
        out_specs=[
            cm,
            pl.BlockSpec((tm, d + LANES), lambda b, j: (b * nl + j, 0)),
            pl.BlockSpec((N_EXPERTS, tm), lambda b, j: (0, b * nl + j)),
        ],
        out_shape=[
            jax.ShapeDtypeStruct(s2.shape, F32),
            jax.ShapeDtypeStruct((bsz * ll, d + LANES), F32),
            jax.ShapeDtypeStruct((N_EXPERTS, bsz * ll), F32),
        ],
        compiler_params=_cparams(("arbitrary", "arbitrary")),
        name="hgrn_out_proj",
    )(hg, s2, modv, modv, modv, norm_g, w_out, r_hi, r_lo)


def _final_kernel(rows, s_ref, f_ref, g2_ref, fg_ref, o_hbm, obuf, sem):
    b, j = pl.program_id(0), pl.program_id(1)
    nl = pl.num_programs(1)
    tile = b * nl + j
    n_tiles = pl.num_programs(0) * nl
    slot = tile % 2

    def column_copies(t, sl):
        bb, jj = t // nl, t % nl
        return [pltpu.make_async_copy(obuf.at[sl, pl.ds(c * rows, rows), :],
                                      o_hbm.at[bb, pl.ds(0, rows), jj * COL_TILE + c, :], sem.at[sl])
                for c in range(COL_TILE)]

    @pl.when(tile >= 2)
    def _():
        for cp in column_copies(tile - 2, slot):
            cp.wait()

    obuf[slot] = _rms(s_ref[0] + g2_ref[0] * f_ref[0], fg_ref[...])
    for cp in column_copies(tile, slot):
        cp.start()

    @pl.when(tile == n_tiles - 1)
    def _():
        @pl.when(tile >= 1)
        def _():
            for cp in column_copies(tile - 1, 1 - slot):
                cp.wait()
        for cp in column_copies(tile, slot):
            cp.wait()


def _final(s3, f1, modv, rows_mod, final_g, tm):
    bsz, ll, d = s3.shape
    nl = ll // tm
    rows = ll // GRID_W
    tok = pl.BlockSpec((1, tm, d), lambda b, j: (b, j, 0))
    out = pl.pallas_call(
        functools.partial(_final_kernel, rows),
        grid=(bsz, nl),
        in_specs=[tok, tok, _mod_spec(d, 1, 5, nl, rows_mod), pl.BlockSpec((1, d), lambda b, j: (0, 0))],
        out_specs=pl.BlockSpec(memory_space=pl.ANY),
        out_shape=jax.ShapeDtypeStruct((bsz, rows, GRID_W, d), F32),
        scratch_shapes=[pltpu.VMEM((2, tm, d), F32), pltpu.SemaphoreType.DMA((2,))],
        compiler_params=_cparams(("arbitrary", "arbitrary")),
        name="final_norm",
    )(s3, f1.reshape(bsz, -1, d), modv, final_g)
    return out.reshape(bsz, ll, d)


def kernel(x, c, ctx, c_ctx, ada_w, ada_b, norm_mix_g, norm_ffn_g, final_g, m_w_in, m_conv_w, m_conv_b, m_gate_b,
           m_head_g, m_w_out, h_w_in, h_lower_bounds, h_head_g, h_w_out, router_w, router_bias, e_w_gate,
           e_w_up, e_w_down):
    bsz, ll, d = x.shape
    lc = ctx.shape[1]
    ltot = ll + lc
    depth = ada_w.shape[0]
    assert depth == 2 and ll % GRID_W == 0 and lc % SCAN_CHUNK == 0 and ll % SCAN_CHUNK == 0
    rows = ll // GRID_W
    tm = COL_TILE * rows
    assert lc % tm == 0 and ltot % GRID_W == 0 and (bsz * ltot) % MOE_TILE == 0 and (bsz * ll) % MOE_TILE == 0

    rows_mod = 8 * ((bsz + 1 + 7) // 8)
    cc = jnp.zeros((rows_mod, d), F32).at[:bsz].set(c).at[rows_mod - 1].set(c_ctx)
    modv = _ada(cc, ada_w, ada_b).reshape(depth * rows_mod * N_ADA, 1, d)

    r_pad = jnp.zeros((d, LANES), F32).at[:, :N_EXPERTS].set(router_w)
    r_hi = r_pad.astype(BF16)
    r_lo = (r_pad - r_hi.astype(F32)).astype(BF16)
    wg_all, wu_all, wd_all = e_w_gate.astype(BF16), e_w_up.astype(BF16), e_w_down.astype(BF16)

    m_qk, m_v = m_conv_w.shape[2], m_head_g.shape[1]
    w_in = m_w_in[0].astype(BF16)
    w_gates = jnp.zeros((d, LANES), BF16).at[:, :4 * M_HEADS].set(w_in[:, m_qk + 2 * m_v:])
    qk, v, o, gates = _mproj(x, ctx, modv, rows_mod, norm_mix_g[0:1], w_in[:, :m_qk], w_in[:, m_qk:m_qk + m_v],
                             w_in[:, m_qk + m_v:m_qk + 2 * m_v], w_gates, tm)
    hg = _mlstm_scan(qk, v, o, gates, m_conv_w[0], m_conv_b[0], m_gate_b[0], m_head_g[0], ll, lc)
    s1, h2, lg = _out0(hg, x, ctx, modv, rows_mod, norm_ffn_g[0:1], m_w_out[0].astype(BF16), r_hi, r_lo, tm)
    cls, h2 = _route(lg, router_bias, h2)
    tok0 = jnp.arange(bsz * ltot, dtype=jnp.int32)
    pos0 = tok0 % ltot
    pos0 = jnp.where(pos0 < ll, (pos0 % GRID_W) * rows + pos0 // GRID_W, pos0)
    dst0 = (tok0 // ltot) * (ltot + MOE_TILE) + pos0
    spare0 = tuple(b * (ltot + MOE_TILE) + ltot for b in range(1, bsz))
    f0 = _moe(h2, cls, dst0, ltot, spare0, bsz * (ltot + MOE_TILE), 0, wg_all, wu_all, wd_all)

    lbs = jnp.cumsum(jax.nn.softmax(h_lower_bounds.astype(F32), axis=0), axis=0)
    lb = (lbs - lbs[0])[1].reshape(2, -1)
    lbtab = jnp.stack([jnp.log(lb[0]), jnp.log1p(-lb[0]), jnp.log(lb[1]), jnp.log1p(-lb[1])])
    hw = h_w_in[0].astype(BF16)
    hk = lb.shape[1]
    ws = (hw[:, :hk], hw[:, hk:2 * hk], hw[:, 2 * hk:3 * hk], hw[:, 3 * hk:3 * hk + d], hw[:, 3 * hk + d:])
    s2, q, kf, kb, lff, lfb, iv, g = _hproj(s1, f0, modv, rows_mod, norm_mix_g[1:2], lbtab, ws, ll, lc)
    hg1 = _hgrn_scan(q, kf, kb, lff, lfb, iv, g, h_head_g[0], ll, lc)
    s3, h2b, lgb = _out1(hg1, s2, modv, rows_mod, norm_ffn_g[1:2], h_w_out[0].astype(BF16), r_hi, r_lo, tm)
    clsb, h2b = _route(lgb, router_bias, h2b)
    tok1 = jnp.arange(bsz * ll, dtype=jnp.int32)
    dst1 = (tok1 // ll) * (ll + MOE_TILE) + tok1 % ll
    spare1 = tuple(b * (ll + MOE_TILE) + ll for b in range(1, bsz))
    f1 = _moe(h2b, clsb, dst1, ll, spare1, bsz * (ll + MOE_TILE), 1, wg_all, wu_all, wd_all)

    return _final(s3, f1, modv, rows_mod, final_g.reshape(1, d), tm)
```

```python
import functools

import numpy as np
import jax
import jax.numpy as jnp
from jax import lax
from jax.experimental import pallas as pl
from jax.experimental.pallas import tpu as pltpu

F32 = jnp.float32
BF16 = jnp.bfloat16

EPS = 1e-6
LOG2_E = 1.4426950408889634
N_ADA = 6
GRID_W = 64
M_HEADS = 4
H_HEADS = 8
N_EXPERTS = 16
N_GROUPS = 4
E_PER_GROUP = N_EXPERTS // N_GROUPS
N_PAIRS = 6
N_CLASSES = N_GROUPS * N_PAIRS
PAIR_LO = (0, 0, 0, 1, 1, 2)
PAIR_HI = (1, 2, 3, 2, 3, 3)

LANES = 128
COL_TILE = 8
SCAN_CHUNK = 128
MOE_TILE = 256
VMEM_LIMIT = 56 * 1024 * 1024


def _cparams(sem):
    return pltpu.CompilerParams(dimension_semantics=sem, vmem_limit_bytes=VMEM_LIMIT)


def _dot(a, b):
    return jnp.dot(a, b, preferred_element_type=F32)


def _dot_nt(a, b):
    return lax.dot_general(a, b, (((1,), (1,)), ((), ())), preferred_element_type=F32)


def _dot_tn(a, b):
    return lax.dot_general(a, b, (((0,), (0,)), ((), ())), preferred_element_type=F32)


def _split_bf16(a):
    hi = a.astype(BF16)
    lo = (a - hi.astype(F32)).astype(BF16)
    return hi, lo


def _sigmoid(x):
    return 1.0 / (1.0 + jnp.exp(-x))


def _silu(x):
    return x * _sigmoid(x)


def _log1p_exp_neg_abs(x):
    return jnp.log(1.0 + jnp.exp(-jnp.abs(x)))


def _log_sigmoid(x):
    return jnp.minimum(x, 0.0) - _log1p_exp_neg_abs(x)


def _rms(x, g):
    return x * lax.rsqrt(jnp.mean(x * x, axis=-1, keepdims=True) + EPS) * g


def _ada_kernel(c_ref, w_ref, b_ref, o_ref):
    a = _silu(c_ref[...])
    a_hi, a_lo = _split_bf16(a)
    w_hi, w_lo = _split_bf16(w_ref[0])
    acc = _dot(a_hi, w_hi) + _dot(a_lo, w_hi) + _dot(a_hi, w_lo)
    o_ref[0] = acc + b_ref[0]


def _ada(cc, ada_w, ada_b):
    depth, d, n = ada_w.shape
    tn = 1024
    rows = cc.shape[0]
    return pl.pallas_call(
        _ada_kernel,
        grid=(depth, n // tn),
        in_specs=[
            pl.BlockSpec((rows, d), lambda l, j: (0, 0)),
            pl.BlockSpec((1, d, tn), lambda l, j: (l, 0, j)),
            pl.BlockSpec((1, 1, tn), lambda l, j: (l, 0, j)),
        ],
        out_specs=pl.BlockSpec((1, rows, tn), lambda l, j: (l, 0, j)),
        out_shape=jax.ShapeDtypeStruct((depth, rows, n), F32),
        compiler_params=_cparams(("arbitrary", "arbitrary")),
        name="ada_mod",
    )(cc, ada_w, ada_b.reshape(depth, 1, n))


def _mproj_kernel(n_lat_tiles, xl_ref, xc_ref, sc_ref, sh_ref, g_ref, wqk_ref, wv_ref, wo_ref, wg_ref,
                  qk_ref, v_ref, o_ref, gt_ref):
    j = pl.program_id(1)
    x = jnp.where(j < n_lat_tiles, xl_ref[0], xc_ref[0])
    h = (_rms(x, g_ref[...]) * (1.0 + sc_ref[0]) + sh_ref[0]).astype(BF16)
    qk_ref[0] = _dot(h, wqk_ref[...]).astype(BF16)
    v_ref[0] = _dot(h, wv_ref[...]).astype(BF16)
    o_ref[0] = _dot(h, wo_ref[...]).astype(BF16)
    gt_ref[0] = _dot(h, wg_ref[...])[:, :4 * M_HEADS]


def _mod_spec(d, layer, k, n_lat_tiles, rows):
    def imap(b, j):
        row = jnp.where(j < n_lat_tiles, b, rows - 1)
        return ((layer * rows + row) * N_ADA + k, 0, 0)
    return pl.BlockSpec((1, 1, d), imap)


def _mproj(x, ctx, modv, rows, norm_g, w_qk, w_v, w_o, w_g, tm):
    bsz, ll, d = x.shape
    lc = ctx.shape[1]
    nl, nc = ll // tm, lc // tm
    ltot = ll + lc
    wspec = lambda n: pl.BlockSpec((d, n), lambda b, j: (0, 0))
    tok = lambda n: pl.BlockSpec((1, tm, n), lambda b, j: (b, j, 0))
    return pl.pallas_call(
        functools.partial(_mproj_kernel, nl),
        grid=(bsz, nl + nc),
        in_specs=[
            pl.BlockSpec((1, tm, d), lambda b, j: (b, jnp.minimum(j, nl - 1), 0)),
            pl.BlockSpec((1, tm, d), lambda b, j: (b, jnp.maximum(j - nl, 0), 0)),
            _mod_spec(d, 0, 1, nl, rows), _mod_spec(d, 0, 0, nl, rows),
            pl.BlockSpec((1, d), lambda b, j: (0, 0)),
            wspec(w_qk.shape[1]), wspec(w_v.shape[1]), wspec(w_o.shape[1]), wspec(w_g.shape[1]),
        ],
        out_specs=[tok(w_qk.shape[1]), tok(w_v.shape[1]), tok(w_o.shape[1]), tok(4 * M_HEADS)],
        out_shape=[
            jax.ShapeDtypeStruct((bsz, ltot, w_qk.shape[1]), BF16),
            jax.ShapeDtypeStruct((bsz, ltot, w_v.shape[1]), BF16),
            jax.ShapeDtypeStruct((bsz, ltot, w_o.shape[1]), BF16),
            jax.ShapeDtypeStruct((bsz, ltot, 4 * M_HEADS), F32),
        ],
        compiler_params=_cparams(("arbitrary", "arbitrary")),
        name="mlstm_in_proj",
    )(x, ctx, modv, modv, norm_g, w_qk, w_v, w_o, w_g)


def _mlstm_kernel(t, ll, lc, hp, q_ref, k_ref, v_ref, o_ref, gc_ref, gr_ref, cwq_ref, cwk_ref, cbq_ref, cbk_ref,
                  gbc_ref, gbr_ref, hg_ref, out_ref, qs, ks, hb, c_s, m_s):
    ltot = ll + lc
    dk = q_ref.shape[2] // hp
    dv = v_ref.shape[2] // hp
    row = lax.broadcasted_iota(jnp.int32, (ltot, 1), 0)
    first = (row == 0) | (row == ll)
    last = (row == ll - 1) | (row == ltot - 1)

    def conv(x_ref, w_ref, b_ref, cols):
        x = x_ref[0, :, cols].astype(F32)
        w = w_ref[:, cols]
        xp = jnp.where(first, 0.0, pltpu.roll(x, 1, 0))
        xn = jnp.where(last, 0.0, pltpu.roll(x, ltot - 1, 0))
        return _silu(xp * w[0:1] + x * w[1:2] + xn * w[2:3] + b_ref[:, cols])

    for j in range(hp):
        cols = slice(j * dk, (j + 1) * dk)
        qs[:, cols] = (conv(q_ref, cwq_ref, cbq_ref, cols) * (dk ** -0.5)).astype(BF16)
        ks[:, cols] = conv(k_ref, cwk_ref, cbk_ref, cols).astype(BF16)

    c_s[...] = jnp.zeros(c_s.shape, F32)
    m_s[...] = jnp.zeros(m_s.shape, F32)

    ti = lax.broadcasted_iota(jnp.int32, (t, t), 0)
    si = lax.broadcasted_iota(jnp.int32, (t, t), 1)

    ones = jnp.ones((t, LANES), BF16)

    def chunks(starts):
        jobs = [(j, d) for j in range(hp) for d in range(2)]
        st = {}
        for j, d in jobs:
            rows = pl.ds(starts[d], t)
            gc = gc_ref[0, j, rows, :] + gbc_ref[j]
            gr = gr_ref[0, j, :, rows] + gbr_ref[j]
            ig_c, lf_c = gc[:, 2 * d:2 * d + 1], _log_sigmoid(gc[:, 2 * d + 1:2 * d + 2])
            ig_r, lf_r = gr[2 * d:2 * d + 1, :], _log_sigmoid(gr[2 * d + 1:2 * d + 2, :])
            seen = (si <= ti) if d == 0 else (si >= ti)
            seen_t = (ti <= si) if d == 0 else (ti >= si)
            b_c = jnp.sum(jnp.where(seen, lf_r, 0.0), axis=1, keepdims=True)
            b_r = jnp.sum(jnp.where(seen_t, lf_c, 0.0), axis=0, keepdims=True)
            total = jnp.sum(lf_r, axis=1, keepdims=True)
            st[j, d] = dict(rows=rows, ig_c=ig_c, ig_r=ig_r, b_c=b_c, b_r=b_r, total=total, seen=seen,
                            m_prev=m_s[2 * j + d][0:1, 0:1])
        for j, d in jobs:
            s = st[j, d]
            s["q"] = qs[s["rows"], j * dk:(j + 1) * dk]
            s["k"] = ks[s["rows"], j * dk:(j + 1) * dk]
            s["v"] = jnp.concatenate([v_ref[0, s["rows"], j * dv:(j + 1) * dv], ones], axis=1)
            s["qk"] = _dot_nt(s["q"], s["k"])
        for j, d in jobs:
            s = st[j, d]
            log_d = jnp.where(s["seen"], s["b_c"] - s["b_r"] + s["ig_r"], -jnp.inf)
            log_inter = s["b_c"] + s["m_prev"]
            m_t = jnp.maximum(log_inter, jnp.max(log_d, axis=1, keepdims=True))
            s["w_ts"] = jnp.exp(log_d - m_t) * s["qk"]
            s["inter"] = jnp.exp(log_inter - m_t)
            s["m_t"] = m_t
        for j, d in jobs:
            s = st[j, d]
            lhs = jnp.concatenate([s["w_ts"].astype(BF16), s["q"] * s["inter"].astype(BF16)], axis=1)
            rhs = jnp.concatenate([s["v"], c_s[2 * j + d].astype(BF16)], axis=0)
            s["num"] = _dot(lhs, rhs)
        for j, d in jobs:
            s = st[j, d]
            den = s["num"][:, dv:]
            scale = 1.0 / jnp.maximum(jnp.abs(den), jnp.exp(-s["m_t"]))
            h = s["num"][:, :dv] * jnp.concatenate([scale] * (dv // LANES), axis=1)
            if d == 0:
                out_ref[0, s["rows"], j * dv:(j + 1) * dv] = h.astype(BF16)
            else:
                hb[s["rows"], j * dv:(j + 1) * dv] = h.astype(BF16)
        for j, d in jobs:
            s = st[j, d]
            g_c = s["total"] - s["b_c"] + s["ig_c"]
            g_r = s["total"] - s["b_r"] + s["ig_r"]
            m_new = jnp.maximum(s["total"] + s["m_prev"], jnp.max(g_r, axis=1, keepdims=True))
            w_c = jnp.exp(g_c - m_new)
            decay = jnp.exp(s["total"] + s["m_prev"] - m_new)
            wv = s["v"] * w_c.astype(BF16)
            c_s[2 * j + d] = decay * c_s[2 * j + d] + _dot_tn(s["k"], wv)
            m_s[2 * j + d] = jnp.broadcast_to(m_new, m_s.shape[1:])

    ncc, ncl = lc // t, ll // t
    for i in range(ncc):
        chunks((ll + i * t, ll + (ncc - 1 - i) * t))

    def body(i, carry):
        chunks((pl.multiple_of(i * t, t), pl.multiple_of((ncl - 1 - i) * t, t)))
        return carry

    lax.fori_loop(0, ncl, body, 0)

    def epilogue(i, carry):
        s = pl.multiple_of(i * t, t)
        for j in range(hp):
            cols = slice(j * dv, (j + 1) * dv)
            hs = out_ref[0, pl.ds(s, t), cols].astype(F32) + hb[pl.ds(s, t), cols].astype(F32)
            y = _rms(hs, hg_ref[:, cols])
            out_ref[0, pl.ds(s, t), cols] = (y * _sigmoid(o_ref[0, pl.ds(s, t), cols].astype(F32))).astype(BF16)
        return carry

    lax.fori_loop(0, ltot // t, epilogue, 0, unroll=2)


def _mlstm_scan(qk, v, o, gates, conv_w, conv_b, gate_b, head_g, ll, lc):
    bsz, ltot, _ = qk.shape
    nh = M_HEADS
    hp = 2
    dk = qk.shape[2] // (2 * nh)
    dv = v.shape[2] // nh
    t = SCAN_CHUNK
    g4 = gates.reshape(bsz, ltot, 4, nh).transpose(0, 3, 1, 2)
    g4t = g4.transpose(0, 1, 3, 2)
    gb = gate_b.reshape(4, nh).T
    return pl.pallas_call(
        functools.partial(_mlstm_kernel, t, ll, lc, hp),
        grid=(bsz, nh // hp),
        in_specs=[
            pl.BlockSpec((1, ltot, hp * dk), lambda b, h: (b, 0, h)),
            pl.BlockSpec((1, ltot, hp * dk), lambda b, h: (b, 0, nh // hp + h)),
            pl.BlockSpec((1, ltot, hp * dv), lambda b, h: (b, 0, h)),
            pl.BlockSpec((1, ltot, hp * dv), lambda b, h: (b, 0, h)),
            pl.BlockSpec((1, hp, ltot, 4), lambda b, h: (b, h, 0, 0)),
            pl.BlockSpec((1, hp, 4, ltot), lambda b, h: (b, h, 0, 0)),
            pl.BlockSpec((3, hp * dk), lambda b, h: (0, h)),
            pl.BlockSpec((3, hp * dk), lambda b, h: (0, nh // hp + h)),
            pl.BlockSpec((1, hp * dk), lambda b, h: (0, h)),
            pl.BlockSpec((1, hp * dk), lambda b, h: (0, nh // hp + h)),
            pl.BlockSpec((hp, 1, 4), lambda b, h: (h, 0, 0)),
            pl.BlockSpec((hp, 4, 1), lambda b, h: (h, 0, 0)),
            pl.BlockSpec((1, hp * dv), lambda b, h: (0, h)),
        ],
        out_specs=pl.BlockSpec((1, ltot, hp * dv), lambda b, h: (b, 0, h)),
        out_shape=jax.ShapeDtypeStruct((bsz, ltot, nh * dv), BF16),
        scratch_shapes=[
            pltpu.VMEM((ltot, hp * dk), BF16), pltpu.VMEM((ltot, hp * dk), BF16),
            pltpu.VMEM((ltot, hp * dv), BF16),
            pltpu.VMEM((2 * hp, dk, dv + LANES), F32), pltpu.VMEM((2 * hp, 8, LANES), F32),
        ],
        compiler_params=_cparams(("arbitrary", "arbitrary")),
        name="mlstm_scan",
    )(qk, qk, v, o, g4, g4t, conv_w, conv_w, conv_b.reshape(1, -1), conv_b.reshape(1, -1),
      gb.reshape(nh, 1, 4), gb.reshape(nh, 4, 1), head_g.reshape(1, -1))


def _router_logits_t(h2, rhi_ref, rlo_ref):
    h_hi, h_lo = _split_bf16(h2)
    lg = _dot(h_hi, rhi_ref[...]) + _dot(h_lo, rhi_ref[...]) + _dot(h_hi, rlo_ref[...])
    return lg.T[:N_EXPERTS, :]


def _out0_kernel(n_lat_tiles, hg_ref, xl_ref, xc_ref, g1_ref, sc_ref, sh_ref, ng_ref, wo_ref, rhi_ref, rlo_ref,
                 s_ref, h2_ref, lg_ref):
    j = pl.program_id(1)
    x = jnp.where(j < n_lat_tiles, xl_ref[0], xc_ref[0])
    s = x + g1_ref[0] * _dot(hg_ref[0], wo_ref[...])
    s_ref[0] = s
    h2 = _rms(s, ng_ref[...]) * (1.0 + sc_ref[0]) + sh_ref[0]
    h2_ref[:, :h2.shape[1]] = h2
    h2_ref[:, h2.shape[1]:] = jnp.zeros((h2.shape[0], LANES), F32)
    lg_ref[...] = _router_logits_t(h2, rhi_ref, rlo_ref)


def _out0(hg, x, ctx, modv, rows, norm_g, w_out, r_hi, r_lo, tm):
    bsz, ll, d = x.shape
    lc = ctx.shape[1]
    nl, nc = ll // tm, lc // tm
    nt = nl + nc
    ltot = ll + lc
    return pl.pallas_call(
        functools.partial(_out0_kernel, nl),
        grid=(bsz, nt),
        in_specs=[
            pl.BlockSpec((1, tm, hg.shape[2]), lambda b, j: (b, j, 0)),
            pl.BlockSpec((1, tm, d), lambda b, j: (b, jnp.minimum(j, nl - 1), 0)),
            pl.BlockSpec((1, tm, d), lambda b, j: (b, jnp.maximum(j - nl, 0), 0)),
            _mod_spec(d, 0, 2, nl, rows), _mod_spec(d, 0, 4, nl, rows), _mod_spec(d, 0, 3, nl, rows),
            pl.BlockSpec((1, d), lambda b, j: (0, 0)),
            pl.BlockSpec(w_out.shape, lambda b, j: (0, 0)),
            pl.BlockSpec(r_hi.shape, lambda b, j: (0, 0)),
            pl.BlockSpec(r_lo.shape, lambda b, j: (0, 0)),
        ],
        out_specs=[
            pl.BlockSpec((1, tm, d), lambda b, j: (b, j, 0)),
            pl.BlockSpec((tm, d + LANES), lambda b, j: (b * nt + j, 0)),
            pl.BlockSpec((N_EXPERTS, tm), lambda b, j: (0, b * nt + j)),
        ],
        out_shape=[
            jax.ShapeDtypeStruct((bsz, ltot, d), F32),
            jax.ShapeDtypeStruct((bsz * ltot, d + LANES), F32),
            jax.ShapeDtypeStruct((N_EXPERTS, bsz * ltot), F32),
        ],
        compiler_params=_cparams(("arbitrary", "arbitrary")),
        name="mlstm_out_proj",
    )(hg, x, ctx, modv, modv, modv, norm_g, w_out, r_hi, r_lo)


def _route_kernel(lg_ref, bias_ref, h2x_hbm, cls_ref, w_ref):
    del h2x_hbm
    s = _sigmoid(lg_ref[...])
    sel = s + bias_ref[...]
    srow = [s[e:e + 1, :] for e in range(N_EXPERTS)]
    row = [sel[e:e + 1, :] for e in range(N_EXPERTS)]
    best = jnp.zeros(row[0].shape, jnp.int32)
    best_score = None
    for g in range(N_GROUPS):
        r = row[g * E_PER_GROUP:(g + 1) * E_PER_GROUP]
        score = None
        for lo, hi in zip(PAIR_LO, PAIR_HI):
            pair = r[lo] + r[hi]
            score = pair if score is None else jnp.maximum(score, pair)
        if g == 0:
            best_score = score
        else:
            better = score > best_score
            best = jnp.where(better, g, best)
            best_score = jnp.where(better, score, best_score)
    gs = [row[i] for i in range(E_PER_GROUP)]
    gw = [srow[i] for i in range(E_PER_GROUP)]
    for g in range(1, N_GROUPS):
        hit = best == g
        gs = [jnp.where(hit, row[g * E_PER_GROUP + i], gs[i]) for i in range(E_PER_GROUP)]
        gw = [jnp.where(hit, srow[g * E_PER_GROUP + i], gw[i]) for i in range(E_PER_GROUP)]
    keep = []
    for i in range(E_PER_GROUP):
        beaten = jnp.zeros(best.shape, jnp.int32)
        for j in range(E_PER_GROUP):
            if j == i:
                continue
            wins = (gs[j] > gs[i]) | ((gs[j] == gs[i]) & (j < i))
            beaten = beaten + wins.astype(jnp.int32)
        keep.append(beaten < 2)
    pair_id = jnp.zeros(best.shape, jnp.int32)
    w_lo = jnp.zeros(best.shape, F32)
    w_hi = jnp.zeros(best.shape, F32)
    for p, (lo, hi) in enumerate(zip(PAIR_LO, PAIR_HI)):
        hit = keep[lo] & keep[hi]
        pair_id = jnp.where(hit, p, pair_id)
        w_lo = jnp.where(hit, gw[lo], w_lo)
        w_hi = jnp.where(hit, gw[hi], w_hi)
    tot = w_lo + w_hi
    cls_ref[...] = best * N_PAIRS + pair_id
    lanes_t = jnp.concatenate([w_lo / tot, w_hi / tot, jnp.zeros((LANES - 2, w_lo.shape[1]), F32)], axis=0)
    w_ref[...] = lanes_t.T


def _route(logits_t, router_bias, h2x):
    n = logits_t.shape[1]
    d = h2x.shape[1] - LANES
    tn = next(cand for cand in (2048, 1024, 512, 256, 128) if n % cand == 0)
    return pl.pallas_call(
        _route_kernel,
        grid=(n // tn,),
        in_specs=[
            pl.BlockSpec((N_EXPERTS, tn), lambda i: (0, i)),
            pl.BlockSpec((N_EXPERTS, 1), lambda i: (0, 0)),
            pl.BlockSpec(memory_space=pl.ANY),
        ],
        out_specs=[pl.BlockSpec((1, tn), lambda i: (0, i)), pl.BlockSpec((tn, LANES), lambda i: (i, d // LANES))],
        out_shape=[jax.ShapeDtypeStruct((1, n), jnp.int32), jax.ShapeDtypeStruct(h2x.shape, F32)],
        input_output_aliases={2: 1},
        compiler_params=_cparams(("arbitrary",)),
        name="moe_route",
    )(logits_t, router_bias.reshape(N_EXPERTS, 1).astype(F32), h2x)


def _moe_kernel(tm, dump_base, spare_rows, nu_ref, e0_ref, e1_ref, base_ref, nv_ref, tok_ref, dst_ref, h2_hbm,
                wg0_ref, wu0_ref, wd0_ref, wg1_ref, wu1_ref, wd1_ref, f_hbm,
                xb0, xb1, yb0, yb1, xbf, gsem, ssem, zsem, idle_sem):
    del e0_ref, e1_ref
    t = pl.program_id(0)
    n_used = nu_ref[0]
    d = xbf.shape[1]
    xbuf, ybuf = (xb0, xb1), (yb0, yb1)

    def gather_row(base, r, slot):
        tok = tok_ref[base + r]
        pltpu.make_async_copy(h2_hbm.at[pl.ds(tok, 1)], xbuf[slot].at[pl.ds(r, 1)], gsem.at[slot]).start()

    def scatter_row(base, nv, r, slot):
        row = jnp.where(r < nv, dst_ref[base + r], dump_base + r)
        pltpu.make_async_copy(ybuf[slot].at[pl.ds(r, 1)], f_hbm.at[pl.ds(row, 1)], ssem.at[slot]).start(priority=1)

    def wait_gather(slot):
        pltpu.make_async_copy(h2_hbm.at[pl.ds(0, tm)], xbuf[slot], gsem.at[slot]).wait()

    def wait_scatter(slot):
        pltpu.make_async_copy(ybuf[slot], f_hbm.at[pl.ds(0, tm)], ssem.at[slot]).wait()

    @pl.when(t == 0)
    def _():
        yb1[...] = jnp.zeros(yb1.shape, F32)
        fills = [pltpu.make_async_copy(yb1, f_hbm.at[pl.ds(row, tm)], zsem) for row in spare_rows]
        for fill in fills:
            fill.start()
        for fill in fills:
            fill.wait()
        base0 = base_ref[1]

        def first(r, carry):
            gather_row(base0, r, 0)
            return carry

        lax.fori_loop(0, tm, first, 0)

    def step(cur):
        nxt = 1 - cur
        base_next = base_ref[t + 2]
        base_prev = base_ref[t]
        nv_prev = nv_ref[t]

        @pl.when(t < n_used)
        def _():
            wait_gather(cur)
            xbf[...] = xbuf[cur][:, :d].astype(BF16)
            w = xbuf[cur][:, d:]

            for r in range(tm):
                gather_row(base_next, r, nxt)
            x = xbf[...]
            a0 = (_silu(_dot(x, wg0_ref[0, 0])) * _dot(x, wu0_ref[0, 0]) * w[:, 0:1]).astype(BF16)

            z = pl.semaphore_read(idle_sem)
            zero = lax.shift_right_logical(z.astype(jnp.uint32), jnp.uint32(32)).astype(jnp.int32)
            off = pl.multiple_of(zero * 16, 16)

            for r in range(tm):
                scatter_row(base_prev, nv_prev, r, nxt)
            x = xbf[pl.ds(off, tm), :]
            a1 = (_silu(_dot(x, wg1_ref[0, 0])) * _dot(x, wu1_ref[0, 0]) * w[:, 1:2]).astype(BF16)
            y = _dot(a0, wd0_ref[0, 0]) + _dot(a1, wd1_ref[0, 0])

            @pl.when(t >= 1)
            def _():
                wait_scatter(cur)

            ybuf[cur][...] = y

        @pl.when(t == n_used)
        def _():
            wait_gather(cur)
            wait_scatter(cur)

            def last(r, carry):
                scatter_row(base_prev, nv_prev, r, nxt)
                return carry

            lax.fori_loop(0, tm, last, 0)
            wait_scatter(nxt)

    @pl.when(t % 2 == 0)
    def _():
        step(0)

    @pl.when(t % 2 == 1)
    def _():
        step(1)


def _moe(h2x, cls, dst_of_token, dump_base, spare_rows, out_rows, layer, wg, wu, wd):
    n = h2x.shape[0]
    d = h2x.shape[1] - LANES
    tm = MOE_TILE
    n_tiles = n // tm + N_CLASSES
    cls = cls.reshape(n)
    _, tok_sorted, dst_sorted = lax.sort((cls, jnp.arange(n, dtype=jnp.int32), dst_of_token), num_keys=1)
    tail = jnp.zeros((tm,), jnp.int32)
    tok_sorted = jnp.concatenate([tok_sorted, tail])
    dst_sorted = jnp.concatenate([dst_sorted, tail])
    counts = jnp.sum(cls[:, None] == jnp.arange(N_CLASSES, dtype=jnp.int32)[None, :], axis=0).astype(jnp.int32)
    tiles_per = (counts + tm - 1) // tm
    tile_end = jnp.cumsum(tiles_per)
    tile_start = tile_end - tiles_per
    first_sorted = jnp.cumsum(counts) - counts
    n_used = tile_end[-1]
    tile_id = jnp.arange(n_tiles, dtype=jnp.int32)
    tile_cls = jnp.searchsorted(tile_end, jnp.minimum(tile_id, n_used - 1), side="right").astype(jnp.int32)
    tile_cls = jnp.minimum(tile_cls, N_CLASSES - 1)
    group, pair = tile_cls // N_PAIRS, tile_cls % N_PAIRS
    e0 = group * E_PER_GROUP + jnp.asarray(PAIR_LO, jnp.int32)[pair]
    e1 = group * E_PER_GROUP + jnp.asarray(PAIR_HI, jnp.int32)[pair]
    in_class = (tile_id - tile_start[tile_cls]) * tm
    used = tile_id < n_used
    base = jnp.where(used, first_sorted[tile_cls] + in_class, 0)
    nv = jnp.where(used, jnp.clip(counts[tile_cls] - in_class, 0, tm), 0)
    guard = jnp.zeros((1,), jnp.int32)
    base = jnp.concatenate([guard, base, guard]).astype(jnp.int32)
    nv = jnp.concatenate([guard, nv, guard]).astype(jnp.int32)

    de = wg.shape[3]
    smem = pl.BlockSpec(memory_space=pltpu.SMEM)
    up0 = pl.BlockSpec((1, 1, d, de), lambda i, nu, e0, e1, base, nv: (layer, e0[i], 0, 0))
    up1 = pl.BlockSpec((1, 1, d, de), lambda i, nu, e0, e1, base, nv: (layer, e1[i], 0, 0))
    dn0 = pl.BlockSpec((1, 1, de, d), lambda i, nu, e0, e1, base, nv: (layer, e0[i], 0, 0))
    dn1 = pl.BlockSpec((1, 1, de, d), lambda i, nu, e0, e1, base, nv: (layer, e1[i], 0, 0))
    return pl.pallas_call(
        functools.partial(_moe_kernel, tm, dump_base, spare_rows),
        grid_spec=pltpu.PrefetchScalarGridSpec(
            num_scalar_prefetch=5,
            grid=(n_tiles,),
            in_specs=[
                smem, smem,
                pl.BlockSpec(memory_space=pl.ANY),
                up0, up0, dn0, up1, up1, dn1,
            ],
            out_specs=pl.BlockSpec(memory_space=pl.ANY),
            scratch_shapes=[
                pltpu.VMEM((tm, d + LANES), F32), pltpu.VMEM((tm, d + LANES), F32),
                pltpu.VMEM((tm, d), F32), pltpu.VMEM((tm, d), F32),
                pltpu.VMEM((tm, d), BF16),
                pltpu.SemaphoreType.DMA((2,)), pltpu.SemaphoreType.DMA((2,)), pltpu.SemaphoreType.DMA(()),
                pltpu.SemaphoreType.REGULAR(()),
            ],
        ),
        out_shape=jax.ShapeDtypeStruct((out_rows, d), F32),
        compiler_params=_cparams(("arbitrary",)),
        name="moe_experts",
    )(n_used.reshape(1).astype(jnp.int32), e0, e1, base, nv, tok_sorted, dst_sorted, h2x,
      wg, wu, wd, wg, wu, wd)


def _hproj_kernel(n_lat_tiles, rows, s_hbm, sc_ref_, f_ref, g2_ref, sc_ref, sh_ref, ng_ref, lb_ref,
                  wq_ref, wzf_ref, wzb_ref, wi_ref, wg_ref,
                  s2_ref, q_ref, kf_ref, kb_ref, lff_ref, lfb_ref, i_ref, g_ref, xt, sbuf, sem):
    b, j = pl.program_id(0), pl.program_id(1)
    n_lat = pl.num_programs(0) * n_lat_tiles
    g2 = g2_ref[0]

    def column_copies(tile, slot):
        bb, jj = tile // n_lat_tiles, tile % n_lat_tiles
        return [pltpu.make_async_copy(s_hbm.at[bb, pl.ds(0, rows), jj * COL_TILE + c, :],
                                      sbuf.at[slot, pl.ds(c * rows, rows), :], sem.at[slot])
                for c in range(COL_TILE)]

    @pl.when(j < n_lat_tiles)
    def _():
        tile = b * n_lat_tiles + j
        slot = tile % 2

        @pl.when(tile == 0)
        def _():
            for cp in column_copies(tile, slot):
                cp.start()

        @pl.when(tile + 1 < n_lat)
        def _():
            for cp in column_copies(tile + 1, 1 - slot):
                cp.start()

        for cp in column_copies(tile, slot):
            cp.wait()
        s = sbuf[slot] + g2 * f_ref[0]
        s2_ref[0] = s
        xt[...] = s

    @pl.when(j >= n_lat_tiles)
    def _():
        xt[...] = sc_ref_[0] + g2 * f_ref[0]

    h = (_rms(xt[...], ng_ref[...]) * (1.0 + sc_ref[0]) + sh_ref[0]).astype(BF16)
    n_chunks = 4
    width = wq_ref.shape[1] // n_chunks

    def gates(dd, k_ref, lf_ref, cols, z):
        log_lb = lb_ref[2 * dd:2 * dd + 1, cols]
        log_1mlb = lb_ref[2 * dd + 1:2 * dd + 2, cols]
        a = log_1mlb + _log_sigmoid(z)
        lf_ref[0, :, cols] = jnp.maximum(log_lb, a) + _log1p_exp_neg_abs(log_lb - a)
        k_ref[0, :, cols] = jnp.exp(a - z).astype(BF16)

    def store_q(cols, y):
        q_ref[0, :, cols] = _silu(y).astype(BF16)

    def store_plain(out_ref, cols, y):
        out_ref[0, :, cols] = y.astype(BF16)

    gates_f = functools.partial(gates, 0, kf_ref, lff_ref)
    gates_b = functools.partial(gates, 1, kb_ref, lfb_ref)
    plain_i = functools.partial(store_plain, i_ref)
    plain_g = functools.partial(store_plain, g_ref)
    work = []
    for c in range(n_chunks):
        work += [(wzf_ref, gates_f, c), (wi_ref, plain_i, c)]
    for c in range(n_chunks):
        work += [(wzb_ref, gates_b, c), (wg_ref, plain_g, c)]
    work += [(wq_ref, store_q, c) for c in range(n_chunks)]
    pending = None
    for w_ref, tail, c in work:
        cols = slice(c * width, (c + 1) * width)
        y = _dot(h, w_ref[:, cols])
        if pending is not None:
            pending()
        pending = functools.partial(tail, cols, y)
    pending()


def _hproj(s1, f0, modv, rows_mod, norm_g, lbtab, ws, ll, lc):
    bsz, ltot, d = s1.shape
    rows = ll // GRID_W
    tm = COL_TILE * rows
    nl, nc = GRID_W // COL_TILE, lc // tm
    s_grid = s1.reshape(bsz, ltot // GRID_W, GRID_W, d)
    f_tok = f0.reshape(bsz, -1, d)
    ctx = pl.BlockSpec((1, tm, d), lambda b, j: (b, ll // tm + jnp.maximum(j - nl, 0), 0))
    wspec = pl.BlockSpec((d, d), lambda b, j: (0, 0))
    tok = pl.BlockSpec((1, tm, d), lambda b, j: (b, j, 0))
    lat = pl.BlockSpec((1, tm, d), lambda b, j: (b, jnp.minimum(j, nl - 1), 0))

    def g2_map(b, j):
        row = jnp.where(j < nl, b, rows_mod - 1)
        return ((0 * rows_mod + row) * N_ADA + 5, 0, 0)

    outs = pl.pallas_call(
        functools.partial(_hproj_kernel, nl, rows),
        grid=(bsz, nl + nc),
        in_specs=[
            pl.BlockSpec(memory_space=pl.ANY), ctx, tok,
            pl.BlockSpec((1, 1, d), g2_map),
            _mod_spec(d, 1, 1, nl, rows_mod), _mod_spec(d, 1, 0, nl, rows_mod),
            pl.BlockSpec((1, d), lambda b, j: (0, 0)),
            pl.BlockSpec((4, d), lambda b, j: (0, 0)),
            wspec, wspec, wspec, wspec, wspec,
        ],
        out_specs=[lat, tok, tok, tok, tok, tok, tok, tok],
        out_shape=[jax.ShapeDtypeStruct((bsz, ll, d), F32)]
        + [jax.ShapeDtypeStruct((bsz, ltot, d), dt) for dt in (BF16, BF16, BF16, F32, F32, BF16, BF16)],
        scratch_shapes=[pltpu.VMEM((tm, d), F32), pltpu.VMEM((2, tm, d), F32), pltpu.SemaphoreType.DMA((2,))],
        compiler_params=_cparams(("arbitrary", "arbitrary")),
        name="hgrn_in_proj",
    )(s_grid, s1, f_tok, modv, modv, modv, norm_g, lbtab, *ws)
    return outs


def _hgrn_kernel(t, ll, lc, q_ref, kf_ref, kb_ref, lff_ref, lfb_ref, i_ref, g_ref, hg_ref, out_ref,
                 of, ob, inc_s, dec_s, qd_s):
    levels = []
    m = 2
    while m <= t:
        levels.append(m)
        m *= 2
    ti = lax.broadcasted_iota(jnp.int32, (t, t), 0)
    si = lax.broadcasted_iota(jnp.int32, (t, t), 1)
    xor = ti ^ si
    level = jnp.zeros((t, t), jnp.int32)
    for m in levels:
        level = level + (xor >= m // 2).astype(jnp.int32)
    feeds = ((si <= ti).astype(BF16), (si >= ti).astype(BF16))

    row = lax.broadcasted_iota(jnp.int32, (t, 1), 0)
    feeds_twice = tuple(jnp.concatenate([f, f], axis=1) for f in feeds)

    def block_ref(b, m, d):
        half = m // 2
        pos = half - 1 if d == 0 else half
        if m >= 8:
            dk = b.shape[1]
            b3 = b.reshape(t // m, m, dk)
            return jnp.broadcast_to(b3[:, pos:pos + 1, :], b3.shape).reshape(t, dk)
        r = row % m
        out = b
        for res in range(m):
            if res != pos:
                out = jnp.where(r == res, pltpu.roll(b, (res - pos) % t, 0), out)
        return out

    def neg_abs(x):
        sign = jnp.uint32(0x80000000)
        return lax.bitcast_convert_type(lax.bitcast_convert_type(x, jnp.uint32) | sign, F32)

    o_dir = (of, ob)

    def prepare(jobs, need_out):
        work = []
        for ci, d in jobs:
            rows = pl.ds(ci * t if isinstance(ci, int) else pl.multiple_of(ci * t, t), t)
            lf = (lff_ref if d == 0 else lfb_ref)[0, rows, :] * LOG2_E
            hi = lf.astype(BF16)
            lo = (lf - hi.astype(F32)).astype(BF16)
            b = _dot(feeds_twice[d], jnp.concatenate([hi, lo], axis=0))
            work.append((ci, d, rows, b))
        for ci, d, rows, b in work:
            b_tot = b[t - 1:t, :] if d == 0 else b[0:1, :]
            k = (kf_ref if d == 0 else kb_ref)[0, rows, :]
            inc_s[d, ci] = _dot_tn(i_ref[0, rows, :], k * jnp.exp2(b_tot - b).astype(BF16))
            dec_s[d, ci] = jnp.exp2(b_tot)
            if need_out:
                qd_s[d, rows, :] = q_ref[0, rows, :] * jnp.exp2(b).astype(BF16)
        if not need_out:
            return
        acc = []
        for ci, d, rows, b in work:
            acc.append(_dot_nt(q_ref[0, rows, :], (kf_ref if d == 0 else kb_ref)[0, rows, :]))
        for idx, m in enumerate(levels):
            for n, (ci, d, rows, b) in enumerate(work):
                e = jnp.exp2(neg_abs(b - block_ref(b, m, d))).astype(BF16)
                p = _dot_nt(q_ref[0, rows, :] * e, (kf_ref if d == 0 else kb_ref)[0, rows, :] * e)
                acc[n] = jnp.where(level == idx + 1, p, acc[n])
        for n, (ci, d, rows, b) in enumerate(work):
            o_dir[d][rows, :] = _dot(acc[n].astype(BF16) * feeds[d], i_ref[0, rows, :])

    ncc, ncl = lc // t, ll // t
    prepare([(ncl + i, d) for i in range(ncc) for d in range(2)], False)

    group = next(g for g in (8, 4, 2, 1) if ncl % g == 0)

    def body(i, carry):
        prepare([(group * i + j, d) for j in range(group) for d in range(2)], True)
        return carry

    lax.fori_loop(0, ncl // group, body, 0)

    for d in range(2):
        order = list(range(ncl, ncl + ncc)) + list(range(ncl))
        if d == 1:
            order = list(range(ncl + ncc - 1, ncl - 1, -1)) + list(range(ncl - 1, -1, -1))
        st = jnp.zeros(inc_s.shape[2:], F32)
        for ci in order:
            if ci < ncl:
                rows = pl.ds(ci * t, t)
                o_dir[d][rows, :] = o_dir[d][rows, :] + _dot_nt(qd_s[d, rows, :], st.astype(BF16))
            st = dec_s[d, ci] * st + inc_s[d, ci]

    def epilogue(i, carry):
        s = pl.multiple_of(i * t, t)
        y = _rms(of[pl.ds(s, t), :] + ob[pl.ds(s, t), :], hg_ref[...])
        out_ref[0, pl.ds(s, t), :] = (y * _silu(g_ref[0, pl.ds(s, t), :].astype(F32))).astype(BF16)
        return carry

    lax.fori_loop(0, ncl, epilogue, 0, unroll=4)


def _hgrn_scan(q, kf, kb, lff, lfb, iv, g, head_g, ll, lc):
    bsz, ltot, hk = q.shape
    nh = H_HEADS
    dk = hk // nh
    t = SCAN_CHUNK
    full = pl.BlockSpec((1, ltot, dk), lambda b, h: (b, 0, h))
    lat = pl.BlockSpec((1, ll, dk), lambda b, h: (b, 0, h))
    return pl.pallas_call(
        functools.partial(_hgrn_kernel, t, ll, lc),
        grid=(bsz, nh),
        in_specs=[full, full, full, full, full, full, lat, pl.BlockSpec((1, dk), lambda b, h: (0, h))],
        out_specs=lat,
        out_shape=jax.ShapeDtypeStruct((bsz, ll, hk), BF16),
        scratch_shapes=[pltpu.VMEM((ll, dk), F32), pltpu.VMEM((ll, dk), F32),
                        pltpu.VMEM((2, ltot // t, dk, dk), F32), pltpu.VMEM((2, ltot // t, 1, dk), F32),
                        pltpu.VMEM((2, ll, dk), BF16)],
        compiler_params=_cparams(("arbitrary", "arbitrary")),
        name="hgrn_scan",
    )(q, kf, kb, lff, lfb, iv, g, head_g.reshape(1, -1))


def _out1_kernel(hg_ref, s_ref, g1_ref, sc_ref, sh_ref, ng_ref, wo_ref, rhi_ref, rlo_ref,
                 s3_ref, h2_ref, lg_ref):
    s = s_ref[0] + g1_ref[0] * _dot(hg_ref[0], wo_ref[...])
    s3_ref[0] = s
    h2 = _rms(s, ng_ref[...]) * (1.0 + sc_ref[0]) + sh_ref[0]
    h2_ref[:, :h2.shape[1]] = h2
    h2_ref[:, h2.shape[1]:] = jnp.zeros((h2.shape[0], LANES), F32)
    lg_ref[...] = _router_logits_t(h2, rhi_ref, rlo_ref)


def _out1(hg, s2, modv, rows_mod, norm_g, w_out, r_hi, r_lo, tm):
    bsz, ll, hv = hg.shape
    d = s2.shape[2]
    nl = ll // tm
    cm = pl.BlockSpec((1, tm, d), lambda b, j: (b, j, 0))
    return pl.pallas_call(
        _out1_kernel,
        grid=(bsz, nl),
        in_specs=[
            pl.BlockSpec((1, tm, hv), lambda b, j: (b, j, 0)),
            cm,
            _mod_spec(d, 1, 2, nl, rows_mod), _mod_spec(d, 1, 4, nl, rows_mod), _mod_spec(d, 1, 3, nl, rows_mod),
            pl.BlockSpec((1, d), lambda b, j: (0, 0)),
            pl.BlockSpec(w_out.shape, lambda b, j: (0, 0)),
            pl.BlockSpec(r_hi.shape, lambda b, j: (0, 0)),
            pl.BlockSpec(r_lo.shape, lambda b, j: (0, 0)),
        ],
        out_specs=[
            cm,
            pl.BlockSpec((tm, d + LANES), lambda b, j: (b * nl + j, 0)),
            pl.BlockSpec((N_EXPERTS, tm), lambda b, j: (0, b * nl + j)),
        ],
        out_shape=[
            jax.ShapeDtypeStruct(s2.shape, F32),
            jax.ShapeDtypeStruct((bsz * ll, d + LANES), F32),
            jax.ShapeDtypeStruct((N_EXPERTS, bsz * ll), F32),
        ],
        compiler_params=_cparams(("arbitrary", "arbitrary")),
        name="hgrn_out_proj",
    )(hg, s2, modv, modv, modv, norm_g, w_out, r_hi, r_lo)


def _final_kernel(rows, s_ref, f_ref, g2_ref, fg_ref, o_hbm, obuf, sem):
    b, j = pl.program_id(0), pl.program_id(1)
    nl = pl.num_programs(1)
    tile = b * nl + j
    n_tiles = pl.num_programs(0) * nl
    slot = tile % 2

    def column_copies(t, sl):
        bb, jj = t // nl, t % nl
        return [pltpu.make_async_copy(obuf.at[sl, pl.ds(c * rows, rows), :],
                                      o_hbm.at[bb, pl.ds(0, rows), jj * COL_TILE + c, :], sem.at[sl])
                for c in range(COL_TILE)]

    @pl.when(tile >= 2)
    def _():
        for cp in column_copies(tile - 2, slot):
            cp.wait()

    obuf[slot] = _rms(s_ref[0] + g2_ref[0] * f_ref[0], fg_ref[...])
    for cp in column_copies(tile, slot):
        cp.start()

    @pl.when(tile == n_tiles - 1)
    def _():
        @pl.when(tile >= 1)
        def _():
            for cp in column_copies(tile - 1, 1 - slot):
                cp.wait()
        for cp in column_copies(tile, slot):
            cp.wait()


def _final(s3, f1, modv, rows_mod, final_g, tm):
    bsz, ll, d = s3.shape
    nl = ll // tm
    rows = ll // GRID_W
    tok = pl.BlockSpec((1, tm, d), lambda b, j: (b, j, 0))
    out = pl.pallas_call(
        functools.partial(_final_kernel, rows),
        grid=(bsz, nl),
        in_specs=[tok, tok, _mod_spec(d, 1, 5, nl, rows_mod), pl.BlockSpec((1, d), lambda b, j: (0, 0))],
        out_specs=pl.BlockSpec(memory_space=pl.ANY),
        out_shape=jax.ShapeDtypeStruct((bsz, rows, GRID_W, d), F32),
        scratch_shapes=[pltpu.VMEM((2, tm, d), F32), pltpu.SemaphoreType.DMA((2,))],
        compiler_params=_cparams(("arbitrary", "arbitrary")),
        name="final_norm",
    )(s3, f1.reshape(bsz, -1, d), modv, final_g)
    return out.reshape(bsz, ll, d)


def kernel(x, c, ctx, c_ctx, ada_w, ada_b, norm_mix_g, norm_ffn_g, final_g, m_w_in, m_conv_w, m_conv_b, m_gate_b,
           m_head_g, m_w_out, h_w_in, h_lower_bounds, h_head_g, h_w_out, router_w, router_bias, e_w_gate,
           e_w_up, e_w_down):
    bsz, ll, d = x.shape
    lc = ctx.shape[1]
    ltot = ll + lc
    depth = ada_w.shape[0]
    assert depth == 2 and ll % GRID_W == 0 and lc % SCAN_CHUNK == 0 and ll % SCAN_CHUNK == 0
    rows = ll // GRID_W
    tm = COL_TILE * rows
    assert lc % tm == 0 and ltot % GRID_W == 0 and (bsz * ltot) % MOE_TILE == 0 and (bsz * ll) % MOE_TILE == 0

    rows_mod = 8 * ((bsz + 1 + 7) // 8)
    cc = jnp.zeros((rows_mod, d), F32).at[:bsz].set(c).at[rows_mod - 1].set(c_ctx)
    modv = _ada(cc, ada_w, ada_b).reshape(depth * rows_mod * N_ADA, 1, d)

    r_pad = jnp.zeros((d, LANES), F32).at[:, :N_EXPERTS].set(router_w)
    r_hi = r_pad.astype(BF16)
    r_lo = (r_pad - r_hi.astype(F32)).astype(BF16)
    wg_all, wu_all, wd_all = e_w_gate.astype(BF16), e_w_up.astype(BF16), e_w_down.astype(BF16)

    m_qk, m_v = m_conv_w.shape[2], m_head_g.shape[1]
    w_in = m_w_in[0].astype(BF16)
    w_gates = jnp.zeros((d, LANES), BF16).at[:, :4 * M_HEADS].set(w_in[:, m_qk + 2 * m_v:])
    qk, v, o, gates = _mproj(x, ctx, modv, rows_mod, norm_mix_g[0:1], w_in[:, :m_qk], w_in[:, m_qk:m_qk + m_v],
                             w_in[:, m_qk + m_v:m_qk + 2 * m_v], w_gates, tm)
    hg = _mlstm_scan(qk, v, o, gates, m_conv_w[0], m_conv_b[0], m_gate_b[0], m_head_g[0], ll, lc)
    s1, h2, lg = _out0(hg, x, ctx, modv, rows_mod, norm_ffn_g[0:1], m_w_out[0].astype(BF16), r_hi, r_lo, tm)
    cls, h2 = _route(lg, router_bias, h2)
    tok0 = jnp.arange(bsz * ltot, dtype=jnp.int32)
    pos0 = tok0 % ltot
    pos0 = jnp.where(pos0 < ll, (pos0 % GRID_W) * rows + pos0 // GRID_W, pos0)
    dst0 = (tok0 // ltot) * (ltot + MOE_TILE) + pos0
    spare0 = tuple(b * (ltot + MOE_TILE) + ltot for b in range(1, bsz))
    f0 = _moe(h2, cls, dst0, ltot, spare0, bsz * (ltot + MOE_TILE), 0, wg_all, wu_all, wd_all)

    lbs = jnp.cumsum(jax.nn.softmax(h_lower_bounds.astype(F32), axis=0), axis=0)
    lb = (lbs - lbs[0])[1].reshape(2, -1)
    lbtab = jnp.stack([jnp.log(lb[0]), jnp.log1p(-lb[0]), jnp.log(lb[1]), jnp.log1p(-lb[1])])
    hw = h_w_in[0].astype(BF16)
    hk = lb.shape[1]
    ws = (hw[:, :hk], hw[:, hk:2 * hk], hw[:, 2 * hk:3 * hk], hw[:, 3 * hk:3 * hk + d], hw[:, 3 * hk + d:])
    s2, q, kf, kb, lff, lfb, iv, g = _hproj(s1, f0, modv, rows_mod, norm_mix_g[1:2], lbtab, ws, ll, lc)
    hg1 = _hgrn_scan(q, kf, kb, lff, lfb, iv, g, h_head_g[0], ll, lc)
    s3, h2b, lgb = _out1(hg1, s2, modv, rows_mod, norm_ffn_g[1:2], h_w_out[0].astype(BF16), r_hi, r_lo, tm)
    clsb, h2b = _route(lgb, router_bias, h2b)
    tok1 = jnp.arange(bsz * ll, dtype=jnp.int32)
    dst1 = (tok1 // ll) * (ll + MOE_TILE) + tok1 % ll
    spare1 = tuple(b * (ll + MOE_TILE) + ll for b in range(1, bsz))
    f1 = _moe(h2b, clsb, dst1, ll, spare1, bsz * (ll + MOE_TILE), 1, wg_all, wu_all, wd_all)

    return _final(s3, f1, modv, rows_mod, final_g.reshape(1, d), tm)
```

```python
import functools

import numpy as np
import jax
import jax.numpy as jnp
from jax import lax
from jax.experimental import pallas as pl
from jax.experimental.pallas import tpu as pltpu

F32 = jnp.float32
BF16 = jnp.bfloat16

EPS = 1e-6
LOG2_E = 1.4426950408889634
N_ADA = 6
GRID_W = 64
M_HEADS = 4
H_HEADS = 8
N_EXPERTS = 16
N_GROUPS = 4
E_PER_GROUP = N_EXPERTS // N_GROUPS
N_PAIRS = 6
N_CLASSES = N_GROUPS * N_PAIRS
PAIR_LO = (0, 0, 0, 1, 1, 2)
PAIR_HI = (1, 2, 3, 2, 3, 3)

LANES = 128
COL_TILE = 8
SCAN_CHUNK = 128
MOE_TILE = 256
VMEM_LIMIT = 56 * 1024 * 1024


def _cparams(sem):
    return pltpu.CompilerParams(dimension_semantics=sem, vmem_limit_bytes=VMEM_LIMIT)


def _dot(a, b):
    return jnp.dot(a, b, preferred_element_type=F32)


def _dot_nt(a, b):
    return lax.dot_general(a, b, (((1,), (1,)), ((), ())), preferred_element_type=F32)


def _dot_tn(a, b):
    return lax.dot_general(a, b, (((0,), (0,)), ((), ())), preferred_element_type=F32)


def _split_bf16(a):
    hi = a.astype(BF16)
    lo = (a - hi.astype(F32)).astype(BF16)
    return hi, lo


def _sigmoid(x):
    return 1.0 / (1.0 + jnp.exp(-x))


def _silu(x):
    return x * _sigmoid(x)


def _log1p_exp_neg_abs(x):
    return jnp.log(1.0 + jnp.exp(-jnp.abs(x)))


def _log_sigmoid(x):
    return jnp.minimum(x, 0.0) - _log1p_exp_neg_abs(x)


def _rms(x, g):
    return x * lax.rsqrt(jnp.mean(x * x, axis=-1, keepdims=True) + EPS) * g


def _ada_kernel(c_ref, w_ref, b_ref, o_ref):
    a = _silu(c_ref[...])
    a_hi, a_lo = _split_bf16(a)
    w_hi, w_lo = _split_bf16(w_ref[0])
    acc = _dot(a_hi, w_hi) + _dot(a_lo, w_hi) + _dot(a_hi, w_lo)
    o_ref[0] = acc + b_ref[0]


def _ada(cc, ada_w, ada_b):
    depth, d, n = ada_w.shape
    tn = 1024
    rows = cc.shape[0]
    return pl.pallas_call(
        _ada_kernel,
        grid=(depth, n // tn),
        in_specs=[
            pl.BlockSpec((rows, d), lambda l, j: (0, 0)),
            pl.BlockSpec((1, d, tn), lambda l, j: (l, 0, j)),
            pl.BlockSpec((1, 1, tn), lambda l, j: (l, 0, j)),
        ],
        out_specs=pl.BlockSpec((1, rows, tn), lambda l, j: (l, 0, j)),
        out_shape=jax.ShapeDtypeStruct((depth, rows, n), F32),
        compiler_params=_cparams(("arbitrary", "arbitrary")),
        name="ada_mod",
    )(cc, ada_w, ada_b.reshape(depth, 1, n))


def _mproj_kernel(n_lat_tiles, xl_ref, xc_ref, sc_ref, sh_ref, g_ref, wqk_ref, wv_ref, wo_ref, wg_ref,
                  qk_ref, v_ref, o_ref, gt_ref):
    j = pl.program_id(1)
    x = jnp.where(j < n_lat_tiles, xl_ref[0], xc_ref[0])
    h = (_rms(x, g_ref[...]) * (1.0 + sc_ref[0]) + sh_ref[0]).astype(BF16)
    qk_ref[0] = _dot(h, wqk_ref[...]).astype(BF16)
    v_ref[0] = _dot(h, wv_ref[...]).astype(BF16)
    o_ref[0] = _dot(h, wo_ref[...]).astype(BF16)
    gt_ref[0] = _dot(h, wg_ref[...])[:, :4 * M_HEADS]


def _mod_spec(d, layer, k, n_lat_tiles, rows):
    def imap(b, j):
        row = jnp.where(j < n_lat_tiles, b, rows - 1)
        return ((layer * rows + row) * N_ADA + k, 0, 0)
    return pl.BlockSpec((1, 1, d), imap)


def _mproj(x, ctx, modv, rows, norm_g, w_qk, w_v, w_o, w_g, tm):
    bsz, ll, d = x.shape
    lc = ctx.shape[1]
    nl, nc = ll // tm, lc // tm
    ltot = ll + lc
    wspec = lambda n: pl.BlockSpec((d, n), lambda b, j: (0, 0))
    tok = lambda n: pl.BlockSpec((1, tm, n), lambda b, j: (b, j, 0))
    return pl.pallas_call(
        functools.partial(_mproj_kernel, nl),
        grid=(bsz, nl + nc),
        in_specs=[
            pl.BlockSpec((1, tm, d), lambda b, j: (b, jnp.minimum(j, nl - 1), 0)),
            pl.BlockSpec((1, tm, d), lambda b, j: (b, jnp.maximum(j - nl, 0), 0)),
            _mod_spec(d, 0, 1, nl, rows), _mod_spec(d, 0, 0, nl, rows),
            pl.BlockSpec((1, d), lambda b, j: (0, 0)),
            wspec(w_qk.shape[1]), wspec(w_v.shape[1]), wspec(w_o.shape[1]), wspec(w_g.shape[1]),
        ],
        out_specs=[tok(w_qk.shape[1]), tok(w_v.shape[1]), tok(w_o.shape[1]), tok(4 * M_HEADS)],
        out_shape=[
            jax.ShapeDtypeStruct((bsz, ltot, w_qk.shape[1]), BF16),
            jax.ShapeDtypeStruct((bsz, ltot, w_v.shape[1]), BF16),
            jax.ShapeDtypeStruct((bsz, ltot, w_o.shape[1]), BF16),
            jax.ShapeDtypeStruct((bsz, ltot, 4 * M_HEADS), F32),
        ],
        compiler_params=_cparams(("arbitrary", "arbitrary")),
        name="mlstm_in_proj",
    )(x, ctx, modv, modv, norm_g, w_qk, w_v, w_o, w_g)


def _mlstm_kernel(t, ll, lc, hp, q_ref, k_ref, v_ref, o_ref, gc_ref, gr_ref, cwq_ref, cwk_ref, cbq_ref, cbk_ref,
                  gbc_ref, gbr_ref, hg_ref, out_ref, qs, ks, hb, c_s, m_s):
    ltot = ll + lc
    dk = q_ref.shape[2] // hp
    dv = v_ref.shape[2] // hp
    row = lax.broadcasted_iota(jnp.int32, (ltot, 1), 0)
    first = (row == 0) | (row == ll)
    last = (row == ll - 1) | (row == ltot - 1)

    def conv(x_ref, w_ref, b_ref, cols):
        x = x_ref[0, :, cols].astype(F32)
        w = w_ref[:, cols]
        xp = jnp.where(first, 0.0, pltpu.roll(x, 1, 0))
        xn = jnp.where(last, 0.0, pltpu.roll(x, ltot - 1, 0))
        return _silu(xp * w[0:1] + x * w[1:2] + xn * w[2:3] + b_ref[:, cols])

    for j in range(hp):
        cols = slice(j * dk, (j + 1) * dk)
        qs[:, cols] = (conv(q_ref, cwq_ref, cbq_ref, cols) * (dk ** -0.5)).astype(BF16)
        ks[:, cols] = conv(k_ref, cwk_ref, cbk_ref, cols).astype(BF16)

    c_s[...] = jnp.zeros(c_s.shape, F32)
    m_s[...] = jnp.zeros(m_s.shape, F32)

    ti = lax.broadcasted_iota(jnp.int32, (t, t), 0)
    si = lax.broadcasted_iota(jnp.int32, (t, t), 1)

    ones = jnp.ones((t, LANES), BF16)

    def chunks(starts):
        jobs = [(j, d) for j in range(hp) for d in range(2)]
        st = {}
        for j, d in jobs:
            rows = pl.ds(starts[d], t)
            gc = gc_ref[0, j, rows, :] + gbc_ref[j]
            gr = gr_ref[0, j, :, rows] + gbr_ref[j]
            ig_c, lf_c = gc[:, 2 * d:2 * d + 1], _log_sigmoid(gc[:, 2 * d + 1:2 * d + 2])
            ig_r, lf_r = gr[2 * d:2 * d + 1, :], _log_sigmoid(gr[2 * d + 1:2 * d + 2, :])
            seen = (si <= ti) if d == 0 else (si >= ti)
            seen_t = (ti <= si) if d == 0 else (ti >= si)
            b_c = jnp.sum(jnp.where(seen, lf_r, 0.0), axis=1, keepdims=True)
            b_r = jnp.sum(jnp.where(seen_t, lf_c, 0.0), axis=0, keepdims=True)
            total = jnp.sum(lf_r, axis=1, keepdims=True)
            st[j, d] = dict(rows=rows, ig_c=ig_c, ig_r=ig_r, b_c=b_c, b_r=b_r, total=total, seen=seen,
                            m_prev=m_s[2 * j + d][0:1, 0:1])
        for j, d in jobs:
            s = st[j, d]
            s["q"] = qs[s["rows"], j * dk:(j + 1) * dk]
            s["k"] = ks[s["rows"], j * dk:(j + 1) * dk]
            s["v"] = jnp.concatenate([v_ref[0, s["rows"], j * dv:(j + 1) * dv], ones], axis=1)
            s["qk"] = _dot_nt(s["q"], s["k"])
        for j, d in jobs:
            s = st[j, d]
            log_d = jnp.where(s["seen"], s["b_c"] - s["b_r"] + s["ig_r"], -jnp.inf)
            log_inter = s["b_c"] + s["m_prev"]
            m_t = jnp.maximum(log_inter, jnp.max(log_d, axis=1, keepdims=True))
            s["w_ts"] = jnp.exp(log_d - m_t) * s["qk"]
            s["inter"] = jnp.exp(log_inter - m_t)
            s["m_t"] = m_t
        for j, d in jobs:
            s = st[j, d]
            lhs = jnp.concatenate([s["w_ts"].astype(BF16), s["q"] * s["inter"].astype(BF16)], axis=1)
            rhs = jnp.concatenate([s["v"], c_s[2 * j + d].astype(BF16)], axis=0)
            s["num"] = _dot(lhs, rhs)
        for j, d in jobs:
            s = st[j, d]
            den = s["num"][:, dv:]
            scale = 1.0 / jnp.maximum(jnp.abs(den), jnp.exp(-s["m_t"]))
            h = s["num"][:, :dv] * jnp.concatenate([scale] * (dv // LANES), axis=1)
            if d == 0:
                out_ref[0, s["rows"], j * dv:(j + 1) * dv] = h.astype(BF16)
            else:
                hb[s["rows"], j * dv:(j + 1) * dv] = h.astype(BF16)
        for j, d in jobs:
            s = st[j, d]
            g_c = s["total"] - s["b_c"] + s["ig_c"]
            g_r = s["total"] - s["b_r"] + s["ig_r"]
            m_new = jnp.maximum(s["total"] + s["m_prev"], jnp.max(g_r, axis=1, keepdims=True))
            w_c = jnp.exp(g_c - m_new)
            decay = jnp.exp(s["total"] + s["m_prev"] - m_new)
            wv = s["v"] * w_c.astype(BF16)
            c_s[2 * j + d] = decay * c_s[2 * j + d] + _dot_tn(s["k"], wv)
            m_s[2 * j + d] = jnp.broadcast_to(m_new, m_s.shape[1:])

    ncc, ncl = lc // t, ll // t
    for i in range(ncc):
        chunks((ll + i * t, ll + (ncc - 1 - i) * t))

    def body(i, carry):
        chunks((pl.multiple_of(i * t, t), pl.multiple_of((ncl - 1 - i) * t, t)))
        return carry

    lax.fori_loop(0, ncl, body, 0)

    def epilogue(i, carry):
        s = pl.multiple_of(i * t, t)
        for j in range(hp):
            cols = slice(j * dv, (j + 1) * dv)
            hs = out_ref[0, pl.ds(s, t), cols].astype(F32) + hb[pl.ds(s, t), cols].astype(F32)
            y = _rms(hs, hg_ref[:, cols])
            out_ref[0, pl.ds(s, t), cols] = (y * _sigmoid(o_ref[0, pl.ds(s, t), cols].astype(F32))).astype(BF16)
        return carry

    lax.fori_loop(0, ltot // t, epilogue, 0, unroll=2)


def _mlstm_scan(qk, v, o, gates, conv_w, conv_b, gate_b, head_g, ll, lc):
    bsz, ltot, _ = qk.shape
    nh = M_HEADS
    hp = 2
    dk = qk.shape[2] // (2 * nh)
    dv = v.shape[2] // nh
    t = SCAN_CHUNK
    g4 = gates.reshape(bsz, ltot, 4, nh).transpose(0, 3, 1, 2)
    g4t = g4.transpose(0, 1, 3, 2)
    gb = gate_b.reshape(4, nh).T
    return pl.pallas_call(
        functools.partial(_mlstm_kernel, t, ll, lc, hp),
        grid=(bsz, nh // hp),
        in_specs=[
            pl.BlockSpec((1, ltot, hp * dk), lambda b, h: (b, 0, h)),
            pl.BlockSpec((1, ltot, hp * dk), lambda b, h: (b, 0, nh // hp + h)),
            pl.BlockSpec((1, ltot, hp * dv), lambda b, h: (b, 0, h)),
            pl.BlockSpec((1, ltot, hp * dv), lambda b, h: (b, 0, h)),
            pl.BlockSpec((1, hp, ltot, 4), lambda b, h: (b, h, 0, 0)),
            pl.BlockSpec((1, hp, 4, ltot), lambda b, h: (b, h, 0, 0)),
            pl.BlockSpec((3, hp * dk), lambda b, h: (0, h)),
            pl.BlockSpec((3, hp * dk), lambda b, h: (0, nh // hp + h)),
            pl.BlockSpec((1, hp * dk), lambda b, h: (0, h)),
            pl.BlockSpec((1, hp * dk), lambda b, h: (0, nh // hp + h)),
            pl.BlockSpec((hp, 1, 4), lambda b, h: (h, 0, 0)),
            pl.BlockSpec((hp, 4, 1), lambda b, h: (h, 0, 0)),
            pl.BlockSpec((1, hp * dv), lambda b, h: (0, h)),
        ],
        out_specs=pl.BlockSpec((1, ltot, hp * dv), lambda b, h: (b, 0, h)),
        out_shape=jax.ShapeDtypeStruct((bsz, ltot, nh * dv), BF16),
        scratch_shapes=[
            pltpu.VMEM((ltot, hp * dk), BF16), pltpu.VMEM((ltot, hp * dk), BF16),
            pltpu.VMEM((ltot, hp * dv), BF16),
            pltpu.VMEM((2 * hp, dk, dv + LANES), F32), pltpu.VMEM((2 * hp, 8, LANES), F32),
        ],
        compiler_params=_cparams(("arbitrary", "arbitrary")),
        name="mlstm_scan",
    )(qk, qk, v, o, g4, g4t, conv_w, conv_w, conv_b.reshape(1, -1), conv_b.reshape(1, -1),
      gb.reshape(nh, 1, 4), gb.reshape(nh, 4, 1), head_g.reshape(1, -1))


def _router_logits_t(h2, rhi_ref, rlo_ref):
    h_hi, h_lo = _split_bf16(h2)
    lg = _dot(h_hi, rhi_ref[...]) + _dot(h_lo, rhi_ref[...]) + _dot(h_hi, rlo_ref[...])
    return lg.T[:N_EXPERTS, :]


def _out0_kernel(n_lat_tiles, hg_ref, xl_ref, xc_ref, g1_ref, sc_ref, sh_ref, ng_ref, wo_ref, rhi_ref, rlo_ref,
                 s_ref, h2_ref, lg_ref):
    j = pl.program_id(1)
    half = hg_ref.shape[1] // 2
    d = s_ref.shape[2]
    parts = [slice(p * half, (p + 1) * half) for p in range(2)]
    ys = [_dot(hg_ref[0, rows, :], wo_ref[...]) for rows in parts]
    ss = []
    for rows, y in zip(parts, ys):
        s = jnp.where(j < n_lat_tiles, xl_ref[0, rows, :], xc_ref[0, rows, :]) + g1_ref[0] * y
        s_ref[0, rows, :] = s
        ss.append(s)
    h2s = []
    for rows, s in zip(parts, ss):
        h2 = _rms(s, ng_ref[...]) * (1.0 + sc_ref[0]) + sh_ref[0]
        h2_ref[rows, :d] = h2
        h2_ref[rows, d:] = jnp.zeros((half, LANES), F32)
        h2s.append(h2)
    for rows, h2 in zip(parts, h2s):
        lg_ref[:, rows] = _router_logits_t(h2, rhi_ref, rlo_ref)


def _out0(hg, x, ctx, modv, rows, norm_g, w_out, r_hi, r_lo, tm):
    bsz, ll, d = x.shape
    lc = ctx.shape[1]
    nl, nc = ll // tm, lc // tm
    nt = nl + nc
    ltot = ll + lc
    return pl.pallas_call(
        functools.partial(_out0_kernel, nl),
        grid=(bsz, nt),
        in_specs=[
            pl.BlockSpec((1, tm, hg.shape[2]), lambda b, j: (b, j, 0)),
            pl.BlockSpec((1, tm, d), lambda b, j: (b, jnp.minimum(j, nl - 1), 0)),
            pl.BlockSpec((1, tm, d), lambda b, j: (b, jnp.maximum(j - nl, 0), 0)),
            _mod_spec(d, 0, 2, nl, rows), _mod_spec(d, 0, 4, nl, rows), _mod_spec(d, 0, 3, nl, rows),
            pl.BlockSpec((1, d), lambda b, j: (0, 0)),
            pl.BlockSpec(w_out.shape, lambda b, j: (0, 0)),
            pl.BlockSpec(r_hi.shape, lambda b, j: (0, 0)),
            pl.BlockSpec(r_lo.shape, lambda b, j: (0, 0)),
        ],
        out_specs=[
            pl.BlockSpec((1, tm, d), lambda b, j: (b, j, 0)),
            pl.BlockSpec((tm, d + LANES), lambda b, j: (b * nt + j, 0)),
            pl.BlockSpec((N_EXPERTS, tm), lambda b, j: (0, b * nt + j)),
        ],
        out_shape=[
            jax.ShapeDtypeStruct((bsz, ltot, d), F32),
            jax.ShapeDtypeStruct((bsz * ltot, d + LANES), F32),
            jax.ShapeDtypeStruct((N_EXPERTS, bsz * ltot), F32),
        ],
        compiler_params=_cparams(("arbitrary", "arbitrary")),
        name="mlstm_out_proj",
    )(hg, x, ctx, modv, modv, modv, norm_g, w_out, r_hi, r_lo)


def _route_kernel(lg_ref, bias_ref, h2x_hbm, cls_ref, w_ref):
    del h2x_hbm
    s = _sigmoid(lg_ref[...])
    sel = s + bias_ref[...]
    srow = [s[e:e + 1, :] for e in range(N_EXPERTS)]
    row = [sel[e:e + 1, :] for e in range(N_EXPERTS)]
    best = jnp.zeros(row[0].shape, jnp.int32)
    best_score = None
    for g in range(N_GROUPS):
        r = row[g * E_PER_GROUP:(g + 1) * E_PER_GROUP]
        score = None
        for lo, hi in zip(PAIR_LO, PAIR_HI):
            pair = r[lo] + r[hi]
            score = pair if score is None else jnp.maximum(score, pair)
        if g == 0:
            best_score = score
        else:
            better = score > best_score
            best = jnp.where(better, g, best)
            best_score = jnp.where(better, score, best_score)
    gs = [row[i] for i in range(E_PER_GROUP)]
    gw = [srow[i] for i in range(E_PER_GROUP)]
    for g in range(1, N_GROUPS):
        hit = best == g
        gs = [jnp.where(hit, row[g * E_PER_GROUP + i], gs[i]) for i in range(E_PER_GROUP)]
        gw = [jnp.where(hit, srow[g * E_PER_GROUP + i], gw[i]) for i in range(E_PER_GROUP)]
    keep = []
    for i in range(E_PER_GROUP):
        beaten = jnp.zeros(best.shape, jnp.int32)
        for j in range(E_PER_GROUP):
            if j == i:
                continue
            wins = (gs[j] > gs[i]) | ((gs[j] == gs[i]) & (j < i))
            beaten = beaten + wins.astype(jnp.int32)
        keep.append(beaten < 2)
    pair_id = jnp.zeros(best.shape, jnp.int32)
    w_lo = jnp.zeros(best.shape, F32)
    w_hi = jnp.zeros(best.shape, F32)
    for p, (lo, hi) in enumerate(zip(PAIR_LO, PAIR_HI)):
        hit = keep[lo] & keep[hi]
        pair_id = jnp.where(hit, p, pair_id)
        w_lo = jnp.where(hit, gw[lo], w_lo)
        w_hi = jnp.where(hit, gw[hi], w_hi)
    tot = w_lo + w_hi
    cls_ref[...] = best * N_PAIRS + pair_id
    lanes_t = jnp.concatenate([w_lo / tot, w_hi / tot, jnp.zeros((LANES - 2, w_lo.shape[1]), F32)], axis=0)
    w_ref[...] = lanes_t.T


def _route(logits_t, router_bias, h2x):
    n = logits_t.shape[1]
    d = h2x.shape[1] - LANES
    tn = next(cand for cand in (2048, 1024, 512, 256, 128) if n % cand == 0)
    return pl.pallas_call(
        _route_kernel,
        grid=(n // tn,),
        in_specs=[
            pl.BlockSpec((N_EXPERTS, tn), lambda i: (0, i)),
            pl.BlockSpec((N_EXPERTS, 1), lambda i: (0, 0)),
            pl.BlockSpec(memory_space=pl.ANY),
        ],
        out_specs=[pl.BlockSpec((1, tn), lambda i: (0, i)), pl.BlockSpec((tn, LANES), lambda i: (i, d // LANES))],
        out_shape=[jax.ShapeDtypeStruct((1, n), jnp.int32), jax.ShapeDtypeStruct(h2x.shape, F32)],
        input_output_aliases={2: 1},
        compiler_params=_cparams(("arbitrary",)),
        name="moe_route",
    )(logits_t, router_bias.reshape(N_EXPERTS, 1).astype(F32), h2x)


def _moe_kernel(tm, dump_base, spare_rows, nu_ref, e0_ref, e1_ref, base_ref, nv_ref, tok_ref, dst_ref, h2_hbm,
                wg0_ref, wu0_ref, wd0_ref, wg1_ref, wu1_ref, wd1_ref, f_hbm,
                xb0, xb1, yb0, yb1, xbf, gsem, ssem, zsem, idle_sem):
    del e0_ref, e1_ref
    t = pl.program_id(0)
    n_used = nu_ref[0]
    d = xbf.shape[1]
    xbuf, ybuf = (xb0, xb1), (yb0, yb1)

    def gather_row(base, r, slot):
        tok = tok_ref[base + r]
        pltpu.make_async_copy(h2_hbm.at[pl.ds(tok, 1)], xbuf[slot].at[pl.ds(r, 1)], gsem.at[slot]).start()

    def scatter_row(base, nv, r, slot):
        row = jnp.where(r < nv, dst_ref[base + r], dump_base + r)
        pltpu.make_async_copy(ybuf[slot].at[pl.ds(r, 1)], f_hbm.at[pl.ds(row, 1)], ssem.at[slot]).start(priority=1)

    def wait_gather(slot):
        pltpu.make_async_copy(h2_hbm.at[pl.ds(0, tm)], xbuf[slot], gsem.at[slot]).wait()

    def wait_scatter(slot):
        pltpu.make_async_copy(ybuf[slot], f_hbm.at[pl.ds(0, tm)], ssem.at[slot]).wait()

    @pl.when(t == 0)
    def _():
        yb1[...] = jnp.zeros(yb1.shape, F32)
        fills = [pltpu.make_async_copy(yb1, f_hbm.at[pl.ds(row, tm)], zsem) for row in spare_rows]
        for fill in fills:
            fill.start()
        for fill in fills:
            fill.wait()
        base0 = base_ref[1]

        def first(r, carry):
            gather_row(base0, r, 0)
            return carry

        lax.fori_loop(0, tm, first, 0)

    def step(cur):
        nxt = 1 - cur
        base_next = base_ref[t + 2]
        base_prev = base_ref[t]
        nv_prev = nv_ref[t]

        @pl.when(t < n_used)
        def _():
            wait_gather(cur)
            xbf[...] = xbuf[cur][:, :d].astype(BF16)
            w = xbuf[cur][:, d:]

            for r in range(tm):
                gather_row(base_next, r, nxt)
            x = xbf[...]
            a0 = (_silu(_dot(x, wg0_ref[0, 0])) * _dot(x, wu0_ref[0, 0]) * w[:, 0:1]).astype(BF16)

            z = pl.semaphore_read(idle_sem)
            zero = lax.shift_right_logical(z.astype(jnp.uint32), jnp.uint32(32)).astype(jnp.int32)
            off = pl.multiple_of(zero * 16, 16)

            for r in range(tm):
                scatter_row(base_prev, nv_prev, r, nxt)
            x = xbf[pl.ds(off, tm), :]
            a1 = (_silu(_dot(x, wg1_ref[0, 0])) * _dot(x, wu1_ref[0, 0]) * w[:, 1:2]).astype(BF16)
            y = _dot(a0, wd0_ref[0, 0]) + _dot(a1, wd1_ref[0, 0])

            @pl.when(t >= 1)
            def _():
                wait_scatter(cur)

            ybuf[cur][...] = y

        @pl.when(t == n_used)
        def _():
            wait_gather(cur)
            wait_scatter(cur)

            def last(r, carry):
                scatter_row(base_prev, nv_prev, r, nxt)
                return carry

            lax.fori_loop(0, tm, last, 0)
            wait_scatter(nxt)

    @pl.when(t % 2 == 0)
    def _():
        step(0)

    @pl.when(t % 2 == 1)
    def _():
        step(1)


def _moe(h2x, cls, dst_of_token, dump_base, spare_rows, out_rows, layer, wg, wu, wd):
    n = h2x.shape[0]
    d = h2x.shape[1] - LANES
    tm = MOE_TILE
    n_tiles = n // tm + N_CLASSES
    cls = cls.reshape(n)
    _, tok_sorted, dst_sorted = lax.sort((cls, jnp.arange(n, dtype=jnp.int32), dst_of_token), num_keys=1)
    tail = jnp.zeros((tm,), jnp.int32)
    tok_sorted = jnp.concatenate([tok_sorted, tail])
    dst_sorted = jnp.concatenate([dst_sorted, tail])
    counts = jnp.sum(cls[:, None] == jnp.arange(N_CLASSES, dtype=jnp.int32)[None, :], axis=0).astype(jnp.int32)
    tiles_per = (counts + tm - 1) // tm
    tile_end = jnp.cumsum(tiles_per)
    tile_start = tile_end - tiles_per
    first_sorted = jnp.cumsum(counts) - counts
    n_used = tile_end[-1]
    tile_id = jnp.arange(n_tiles, dtype=jnp.int32)
    tile_cls = jnp.searchsorted(tile_end, jnp.minimum(tile_id, n_used - 1), side="right").astype(jnp.int32)
    tile_cls = jnp.minimum(tile_cls, N_CLASSES - 1)
    group, pair = tile_cls // N_PAIRS, tile_cls % N_PAIRS
    e0 = group * E_PER_GROUP + jnp.asarray(PAIR_LO, jnp.int32)[pair]
    e1 = group * E_PER_GROUP + jnp.asarray(PAIR_HI, jnp.int32)[pair]
    in_class = (tile_id - tile_start[tile_cls]) * tm
    used = tile_id < n_used
    base = jnp.where(used, first_sorted[tile_cls] + in_class, 0)
    nv = jnp.where(used, jnp.clip(counts[tile_cls] - in_class, 0, tm), 0)
    guard = jnp.zeros((1,), jnp.int32)
    base = jnp.concatenate([guard, base, guard]).astype(jnp.int32)
    nv = jnp.concatenate([guard, nv, guard]).astype(jnp.int32)

    de = wg.shape[3]
    smem = pl.BlockSpec(memory_space=pltpu.SMEM)
    up0 = pl.BlockSpec((1, 1, d, de), lambda i, nu, e0, e1, base, nv: (layer, e0[i], 0, 0))
    up1 = pl.BlockSpec((1, 1, d, de), lambda i, nu, e0, e1, base, nv: (layer, e1[i], 0, 0))
    dn0 = pl.BlockSpec((1, 1, de, d), lambda i, nu, e0, e1, base, nv: (layer, e0[i], 0, 0))
    dn1 = pl.BlockSpec((1, 1, de, d), lambda i, nu, e0, e1, base, nv: (layer, e1[i], 0, 0))
    return pl.pallas_call(
        functools.partial(_moe_kernel, tm, dump_base, spare_rows),
        grid_spec=pltpu.PrefetchScalarGridSpec(
            num_scalar_prefetch=5,
            grid=(n_tiles,),
            in_specs=[
                smem, smem,
                pl.BlockSpec(memory_space=pl.ANY),
                up0, up0, dn0, up1, up1, dn1,
            ],
            out_specs=pl.BlockSpec(memory_space=pl.ANY),
            scratch_shapes=[
                pltpu.VMEM((tm, d + LANES), F32), pltpu.VMEM((tm, d + LANES), F32),
                pltpu.VMEM((tm, d), F32), pltpu.VMEM((tm, d), F32),
                pltpu.VMEM((tm, d), BF16),
                pltpu.SemaphoreType.DMA((2,)), pltpu.SemaphoreType.DMA((2,)), pltpu.SemaphoreType.DMA(()),
                pltpu.SemaphoreType.REGULAR(()),
            ],
        ),
        out_shape=jax.ShapeDtypeStruct((out_rows, d), F32),
        compiler_params=_cparams(("arbitrary",)),
        name="moe_experts",
    )(n_used.reshape(1).astype(jnp.int32), e0, e1, base, nv, tok_sorted, dst_sorted, h2x,
      wg, wu, wd, wg, wu, wd)


def _hproj_kernel(n_lat_tiles, rows, s_hbm, sc_ref_, f_ref, g2_ref, sc_ref, sh_ref, ng_ref, lb_ref,
                  wq_ref, wzf_ref, wzb_ref, wi_ref, wg_ref,
                  s2_ref, q_ref, kf_ref, kb_ref, lff_ref, lfb_ref, i_ref, g_ref, xt, sbuf, sem):
    b, j = pl.program_id(0), pl.program_id(1)
    n_lat = pl.num_programs(0) * n_lat_tiles
    g2 = g2_ref[0]

    def column_copies(tile, slot):
        bb, jj = tile // n_lat_tiles, tile % n_lat_tiles
        return [pltpu.make_async_copy(s_hbm.at[bb, pl.ds(0, rows), jj * COL_TILE + c, :],
                                      sbuf.at[slot, pl.ds(c * rows, rows), :], sem.at[slot])
                for c in range(COL_TILE)]

    @pl.when(j < n_lat_tiles)
    def _():
        tile = b * n_lat_tiles + j
        slot = tile % 2

        @pl.when(tile == 0)
        def _():
            for cp in column_copies(tile, slot):
                cp.start()

        @pl.when(tile + 1 < n_lat)
        def _():
            for cp in column_copies(tile + 1, 1 - slot):
                cp.start()

        for cp in column_copies(tile, slot):
            cp.wait()
        s = sbuf[slot] + g2 * f_ref[0]
        s2_ref[0] = s
        xt[...] = s

    @pl.when(j >= n_lat_tiles)
    def _():
        xt[...] = sc_ref_[0] + g2 * f_ref[0]

    h = (_rms(xt[...], ng_ref[...]) * (1.0 + sc_ref[0]) + sh_ref[0]).astype(BF16)
    n_chunks = 4
    width = wq_ref.shape[1] // n_chunks

    def gates(dd, k_ref, lf_ref, cols, z):
        log_lb = lb_ref[2 * dd:2 * dd + 1, cols]
        log_1mlb = lb_ref[2 * dd + 1:2 * dd + 2, cols]
        a = log_1mlb + _log_sigmoid(z)
        lf_ref[0, :, cols] = jnp.maximum(log_lb, a) + _log1p_exp_neg_abs(log_lb - a)
        k_ref[0, :, cols] = jnp.exp(a - z).astype(BF16)

    def store_q(cols, y):
        q_ref[0, :, cols] = _silu(y).astype(BF16)

    def store_plain(out_ref, cols, y):
        out_ref[0, :, cols] = y.astype(BF16)

    gates_f = functools.partial(gates, 0, kf_ref, lff_ref)
    gates_b = functools.partial(gates, 1, kb_ref, lfb_ref)
    plain_i = functools.partial(store_plain, i_ref)
    plain_g = functools.partial(store_plain, g_ref)
    work = []
    for c in range(n_chunks):
        work += [(wzf_ref, gates_f, c), (wi_ref, plain_i, c)]
    for c in range(n_chunks):
        work += [(wzb_ref, gates_b, c), (wg_ref, plain_g, c)]
    work += [(wq_ref, store_q, c) for c in range(n_chunks)]
    pending = None
    for w_ref, tail, c in work:
        cols = slice(c * width, (c + 1) * width)
        y = _dot(h, w_ref[:, cols])
        if pending is not None:
            pending()
        pending = functools.partial(tail, cols, y)
    pending()


def _hproj(s1, f0, modv, rows_mod, norm_g, lbtab, ws, ll, lc):
    bsz, ltot, d = s1.shape
    rows = ll // GRID_W
    tm = COL_TILE * rows
    nl, nc = GRID_W // COL_TILE, lc // tm
    s_grid = s1.reshape(bsz, ltot // GRID_W, GRID_W, d)
    f_tok = f0.reshape(bsz, -1, d)
    ctx = pl.BlockSpec((1, tm, d), lambda b, j: (b, ll // tm + jnp.maximum(j - nl, 0), 0))
    wspec = pl.BlockSpec((d, d), lambda b, j: (0, 0))
    tok = pl.BlockSpec((1, tm, d), lambda b, j: (b, j, 0))
    lat = pl.BlockSpec((1, tm, d), lambda b, j: (b, jnp.minimum(j, nl - 1), 0))

    def g2_map(b, j):
        row = jnp.where(j < nl, b, rows_mod - 1)
        return ((0 * rows_mod + row) * N_ADA + 5, 0, 0)

    outs = pl.pallas_call(
        functools.partial(_hproj_kernel, nl, rows),
        grid=(bsz, nl + nc),
        in_specs=[
            pl.BlockSpec(memory_space=pl.ANY), ctx, tok,
            pl.BlockSpec((1, 1, d), g2_map),
            _mod_spec(d, 1, 1, nl, rows_mod), _mod_spec(d, 1, 0, nl, rows_mod),
            pl.BlockSpec((1, d), lambda b, j: (0, 0)),
            pl.BlockSpec((4, d), lambda b, j: (0, 0)),
            wspec, wspec, wspec, wspec, wspec,
        ],
        out_specs=[lat, tok, tok, tok, tok, tok, tok, tok],
        out_shape=[jax.ShapeDtypeStruct((bsz, ll, d), F32)]
        + [jax.ShapeDtypeStruct((bsz, ltot, d), dt) for dt in (BF16, BF16, BF16, F32, F32, BF16, BF16)],
        scratch_shapes=[pltpu.VMEM((tm, d), F32), pltpu.VMEM((2, tm, d), F32), pltpu.SemaphoreType.DMA((2,))],
        compiler_params=_cparams(("arbitrary", "arbitrary")),
        name="hgrn_in_proj",
    )(s_grid, s1, f_tok, modv, modv, modv, norm_g, lbtab, *ws)
    return outs


def _hgrn_kernel(t, ll, lc, q_ref, kf_ref, kb_ref, lff_ref, lfb_ref, i_ref, g_ref, hg_ref, out_ref,
                 of, ob, inc_s, dec_s, qd_s):
    levels = []
    m = 2
    while m <= t:
        levels.append(m)
        m *= 2
    ti = lax.broadcasted_iota(jnp.int32, (t, t), 0)
    si = lax.broadcasted_iota(jnp.int32, (t, t), 1)
    xor = ti ^ si
    level = jnp.zeros((t, t), jnp.int32)
    for m in levels:
        level = level + (xor >= m // 2).astype(jnp.int32)
    feeds = ((si <= ti).astype(BF16), (si >= ti).astype(BF16))

    row = lax.broadcasted_iota(jnp.int32, (t, 1), 0)
    feeds_twice = tuple(jnp.concatenate([f, f], axis=1) for f in feeds)

    def block_ref(b, m, d):
        half = m // 2
        pos = half - 1 if d == 0 else half
        if m >= 8:
            dk = b.shape[1]
            b3 = b.reshape(t // m, m, dk)
            return jnp.broadcast_to(b3[:, pos:pos + 1, :], b3.shape).reshape(t, dk)
        r = row % m
        out = b
        for res in range(m):
            if res != pos:
                out = jnp.where(r == res, pltpu.roll(b, (res - pos) % t, 0), out)
        return out

    def neg_abs(x):
        sign = jnp.uint32(0x80000000)
        return lax.bitcast_convert_type(lax.bitcast_convert_type(x, jnp.uint32) | sign, F32)

    o_dir = (of, ob)

    def prepare(jobs, need_out):
        work = []
        for ci, d in jobs:
            rows = pl.ds(ci * t if isinstance(ci, int) else pl.multiple_of(ci * t, t), t)
            lf = (lff_ref if d == 0 else lfb_ref)[0, rows, :] * LOG2_E
            hi = lf.astype(BF16)
            lo = (lf - hi.astype(F32)).astype(BF16)
            b = _dot(feeds_twice[d], jnp.concatenate([hi, lo], axis=0))
            work.append((ci, d, rows, b))
        for ci, d, rows, b in work:
            b_tot = b[t - 1:t, :] if d == 0 else b[0:1, :]
            k = (kf_ref if d == 0 else kb_ref)[0, rows, :]
            inc_s[d, ci] = _dot_tn(i_ref[0, rows, :], k * jnp.exp2(b_tot - b).astype(BF16))
            dec_s[d, ci] = jnp.exp2(b_tot)
            if need_out:
                qd_s[d, rows, :] = q_ref[0, rows, :] * jnp.exp2(b).astype(BF16)
        if not need_out:
            return
        acc = []
        for ci, d, rows, b in work:
            acc.append(_dot_nt(q_ref[0, rows, :], (kf_ref if d == 0 else kb_ref)[0, rows, :]))
        for idx, m in enumerate(levels):
            for n, (ci, d, rows, b) in enumerate(work):
                e = jnp.exp2(neg_abs(b - block_ref(b, m, d))).astype(BF16)
                p = _dot_nt(q_ref[0, rows, :] * e, (kf_ref if d == 0 else kb_ref)[0, rows, :] * e)
                acc[n] = jnp.where(level == idx + 1, p, acc[n])
        for n, (ci, d, rows, b) in enumerate(work):
            o_dir[d][rows, :] = _dot(acc[n].astype(BF16) * feeds[d], i_ref[0, rows, :])

    ncc, ncl = lc // t, ll // t
    prepare([(ncl + i, d) for i in range(ncc) for d in range(2)], False)

    group = next(g for g in (8, 4, 2, 1) if ncl % g == 0)

    def body(i, carry):
        prepare([(group * i + j, d) for j in range(group) for d in range(2)], True)
        return carry

    lax.fori_loop(0, ncl // group, body, 0)

    for d in range(2):
        order = list(range(ncl, ncl + ncc)) + list(range(ncl))
        if d == 1:
            order = list(range(ncl + ncc - 1, ncl - 1, -1)) + list(range(ncl - 1, -1, -1))
        st = jnp.zeros(inc_s.shape[2:], F32)
        for ci in order:
            if ci < ncl:
                rows = pl.ds(ci * t, t)
                o_dir[d][rows, :] = o_dir[d][rows, :] + _dot_nt(qd_s[d, rows, :], st.astype(BF16))
            st = dec_s[d, ci] * st + inc_s[d, ci]

    def epilogue(i, carry):
        s = pl.multiple_of(i * t, t)
        y = _rms(of[pl.ds(s, t), :] + ob[pl.ds(s, t), :], hg_ref[...])
        out_ref[0, pl.ds(s, t), :] = (y * _silu(g_ref[0, pl.ds(s, t), :].astype(F32))).astype(BF16)
        return carry

    lax.fori_loop(0, ncl, epilogue, 0, unroll=4)


def _hgrn_scan(q, kf, kb, lff, lfb, iv, g, head_g, ll, lc):
    bsz, ltot, hk = q.shape
    nh = H_HEADS
    dk = hk // nh
    t = SCAN_CHUNK
    full = pl.BlockSpec((1, ltot, dk), lambda b, h: (b, 0, h))
    lat = pl.BlockSpec((1, ll, dk), lambda b, h: (b, 0, h))
    return pl.pallas_call(
        functools.partial(_hgrn_kernel, t, ll, lc),
        grid=(bsz, nh),
        in_specs=[full, full, full, full, full, full, lat, pl.BlockSpec((1, dk), lambda b, h: (0, h))],
        out_specs=lat,
        out_shape=jax.ShapeDtypeStruct((bsz, ll, hk), BF16),
        scratch_shapes=[pltpu.VMEM((ll, dk), F32), pltpu.VMEM((ll, dk), F32),
                        pltpu.VMEM((2, ltot // t, dk, dk), F32), pltpu.VMEM((2, ltot // t, 1, dk), F32),
                        pltpu.VMEM((2, ll, dk), BF16)],
        compiler_params=_cparams(("arbitrary", "arbitrary")),
        name="hgrn_scan",
    )(q, kf, kb, lff, lfb, iv, g, head_g.reshape(1, -1))


def _out1_kernel(hg_ref, s_ref, g1_ref, sc_ref, sh_ref, ng_ref, wo_ref, rhi_ref, rlo_ref,
                 s3_ref, h2_ref, lg_ref):
    half = hg_ref.shape[1] // 2
    d = s_ref.shape[2]
    parts = [slice(p * half, (p + 1) * half) for p in range(2)]
    ys = [_dot(hg_ref[0, rows, :], wo_ref[...]) for rows in parts]
    ss = []
    for rows, y in zip(parts, ys):
        s = s_ref[0, rows, :] + g1_ref[0] * y
        s3_ref[0, rows, :] = s
        ss.append(s)
    h2s = []
    for rows, s in zip(parts, ss):
        h2 = _rms(s, ng_ref[...]) * (1.0 + sc_ref[0]) + sh_ref[0]
        h2_ref[rows, :d] = h2
        h2_ref[rows, d:] = jnp.zeros((half, LANES), F32)
        h2s.append(h2)
    for rows, h2 in zip(parts, h2s):
        lg_ref[:, rows] = _router_logits_t(h2, rhi_ref, rlo_ref)


def _out1(hg, s2, modv, rows_mod, norm_g, w_out, r_hi, r_lo, tm):
    bsz, ll, hv = hg.shape
    d = s2.shape[2]
    tm = 2 * tm if ll % (2 * tm) == 0 else tm
    nl = ll // tm
    cm = pl.BlockSpec((1, tm, d), lambda b, j: (b, j, 0))
    return pl.pallas_call(
        _out1_kernel,
        grid=(bsz, nl),
        in_specs=[
            pl.BlockSpec((1, tm, hv), lambda b, j: (b, j, 0)),
            cm,
            _mod_spec(d, 1, 2, nl, rows_mod), _mod_spec(d, 1, 4, nl, rows_mod), _mod_spec(d, 1, 3, nl, rows_mod),
            pl.BlockSpec((1, d), lambda b, j: (0, 0)),
            pl.BlockSpec(w_out.shape, lambda b, j: (0, 0)),
            pl.BlockSpec(r_hi.shape, lambda b, j: (0, 0)),
            pl.BlockSpec(r_lo.shape, lambda b, j: (0, 0)),
        ],
        out_specs=[
            cm,
            pl.BlockSpec((tm, d + LANES), lambda b, j: (b * nl + j, 0)),
            pl.BlockSpec((N_EXPERTS, tm), lambda b, j: (0, b * nl + j)),
        ],
        out_shape=[
            jax.ShapeDtypeStruct(s2.shape, F32),
            jax.ShapeDtypeStruct((bsz * ll, d + LANES), F32),
            jax.ShapeDtypeStruct((N_EXPERTS, bsz * ll), F32),
        ],
        compiler_params=_cparams(("arbitrary", "arbitrary")),
        name="hgrn_out_proj",
    )(hg, s2, modv, modv, modv, norm_g, w_out, r_hi, r_lo)


def _final_kernel(rows, s_ref, f_ref, g2_ref, fg_ref, o_hbm, obuf, sem):
    b, j = pl.program_id(0), pl.program_id(1)
    nl = pl.num_programs(1)
    tile = b * nl + j
    n_tiles = pl.num_programs(0) * nl
    slot = tile % 2

    def column_copies(t, sl):
        bb, jj = t // nl, t % nl
        return [pltpu.make_async_copy(obuf.at[sl, pl.ds(c * rows, rows), :],
                                      o_hbm.at[bb, pl.ds(0, rows), jj * COL_TILE + c, :], sem.at[sl])
                for c in range(COL_TILE)]

    @pl.when(tile >= 2)
    def _():
        for cp in column_copies(tile - 2, slot):
            cp.wait()

    obuf[slot] = _rms(s_ref[0] + g2_ref[0] * f_ref[0], fg_ref[...])
    for cp in column_copies(tile, slot):
        cp.start()

    @pl.when(tile == n_tiles - 1)
    def _():
        @pl.when(tile >= 1)
        def _():
            for cp in column_copies(tile - 1, 1 - slot):
                cp.wait()
        for cp in column_copies(tile, slot):
            cp.wait()


def _final(s3, f1, modv, rows_mod, final_g, tm):
    bsz, ll, d = s3.shape
    nl = ll // tm
    rows = ll // GRID_W
    tok = pl.BlockSpec((1, tm, d), lambda b, j: (b, j, 0))
    out = pl.pallas_call(
        functools.partial(_final_kernel, rows),
        grid=(bsz, nl),
        in_specs=[tok, tok, _mod_spec(d, 1, 5, nl, rows_mod), pl.BlockSpec((1, d), lambda b, j: (0, 0))],
        out_specs=pl.BlockSpec(memory_space=pl.ANY),
        out_shape=jax.ShapeDtypeStruct((bsz, rows, GRID_W, d), F32),
        scratch_shapes=[pltpu.VMEM((2, tm, d), F32), pltpu.SemaphoreType.DMA((2,))],
        compiler_params=_cparams(("arbitrary", "arbitrary")),
        name="final_norm",
    )(s3, f1.reshape(bsz, -1, d), modv, final_g)
    return out.reshape(bsz, ll, d)


def kernel(x, c, ctx, c_ctx, ada_w, ada_b, norm_mix_g, norm_ffn_g, final_g, m_w_in, m_conv_w, m_conv_b, m_gate_b,
           m_head_g, m_w_out, h_w_in, h_lower_bounds, h_head_g, h_w_out, router_w, router_bias, e_w_gate,
           e_w_up, e_w_down):
    bsz, ll, d = x.shape
    lc = ctx.shape[1]
    ltot = ll + lc
    depth = ada_w.shape[0]
    assert depth == 2 and ll % GRID_W == 0 and lc % SCAN_CHUNK == 0 and ll % SCAN_CHUNK == 0
    rows = ll // GRID_W
    tm = COL_TILE * rows
    assert lc % tm == 0 and ltot % GRID_W == 0 and (bsz * ltot) % MOE_TILE == 0 and (bsz * ll) % MOE_TILE == 0

    rows_mod = 8 * ((bsz + 1 + 7) // 8)
    cc = jnp.zeros((rows_mod, d), F32).at[:bsz].set(c).at[rows_mod - 1].set(c_ctx)
    modv = _ada(cc, ada_w, ada_b).reshape(depth * rows_mod * N_ADA, 1, d)

    r_pad = jnp.zeros((d, LANES), F32).at[:, :N_EXPERTS].set(router_w)
    r_hi = r_pad.astype(BF16)
    r_lo = (r_pad - r_hi.astype(F32)).astype(BF16)
    wg_all, wu_all, wd_all = e_w_gate.astype(BF16), e_w_up.astype(BF16), e_w_down.astype(BF16)

    m_qk, m_v = m_conv_w.shape[2], m_head_g.shape[1]
    w_in = m_w_in[0].astype(BF16)
    w_gates = jnp.zeros((d, LANES), BF16).at[:, :4 * M_HEADS].set(w_in[:, m_qk + 2 * m_v:])
    qk, v, o, gates = _mproj(x, ctx, modv, rows_mod, norm_mix_g[0:1], w_in[:, :m_qk], w_in[:, m_qk:m_qk + m_v],
                             w_in[:, m_qk + m_v:m_qk + 2 * m_v], w_gates, tm)
    hg = _mlstm_scan(qk, v, o, gates, m_conv_w[0], m_conv_b[0], m_gate_b[0], m_head_g[0], ll, lc)
    s1, h2, lg = _out0(hg, x, ctx, modv, rows_mod, norm_ffn_g[0:1], m_w_out[0].astype(BF16), r_hi, r_lo, tm)
    cls, h2 = _route(lg, router_bias, h2)
    tok0 = jnp.arange(bsz * ltot, dtype=jnp.int32)
    pos0 = tok0 % ltot
    pos0 = jnp.where(pos0 < ll, (pos0 % GRID_W) * rows + pos0 // GRID_W, pos0)
    dst0 = (tok0 // ltot) * (ltot + MOE_TILE) + pos0
    spare0 = tuple(b * (ltot + MOE_TILE) + ltot for b in range(1, bsz))
    f0 = _moe(h2, cls, dst0, ltot, spare0, bsz * (ltot + MOE_TILE), 0, wg_all, wu_all, wd_all)

    lbs = jnp.cumsum(jax.nn.softmax(h_lower_bounds.astype(F32), axis=0), axis=0)
    lb = (lbs - lbs[0])[1].reshape(2, -1)
    lbtab = jnp.stack([jnp.log(lb[0]), jnp.log1p(-lb[0]), jnp.log(lb[1]), jnp.log1p(-lb[1])])
    hw = h_w_in[0].astype(BF16)
    hk = lb.shape[1]
    ws = (hw[:, :hk], hw[:, hk:2 * hk], hw[:, 2 * hk:3 * hk], hw[:, 3 * hk:3 * hk + d], hw[:, 3 * hk + d:])
    s2, q, kf, kb, lff, lfb, iv, g = _hproj(s1, f0, modv, rows_mod, norm_mix_g[1:2], lbtab, ws, ll, lc)
    hg1 = _hgrn_scan(q, kf, kb, lff, lfb, iv, g, h_head_g[0], ll, lc)
    s3, h2b, lgb = _out1(hg1, s2, modv, rows_mod, norm_ffn_g[1:2], h_w_out[0].astype(BF16), r_hi, r_lo, tm)
    clsb, h2b = _route(lgb, router_bias, h2b)
    tok1 = jnp.arange(bsz * ll, dtype=jnp.int32)
    dst1 = (tok1 // ll) * (ll + MOE_TILE) + tok1 % ll
    spare1 = tuple(b * (ll + MOE_TILE) + ll for b in range(1, bsz))
    f1 = _moe(h2b, clsb, dst1, ll, spare1, bsz * (ll + MOE_TILE), 1, wg_all, wu_all, wd_all)

    return _final(s3, f1, modv, rows_mod, final_g.reshape(1, d), tm)
```

```python
import functools

import numpy as np
import jax
import jax.numpy as jnp
from jax import lax
from jax.experimental import pallas as pl
from jax.experimental.pallas import tpu as pltpu

F32 = jnp.float32
BF16 = jnp.bfloat16

EPS = 1e-6
LOG2_E = 1.4426950408889634
N_ADA = 6
GRID_W = 64
M_HEADS = 4
H_HEADS = 8
N_EXPERTS = 16
N_GROUPS = 4
E_PER_GROUP = N_EXPERTS // N_GROUPS
N_PAIRS = 6
N_CLASSES = N_GROUPS * N_PAIRS
PAIR_LO = (0, 0, 0, 1, 1, 2)
PAIR_HI = (1, 2, 3, 2, 3, 3)

LANES = 128
COL_TILE = 8
SCAN_CHUNK = 128
MOE_TILE = 256
VMEM_LIMIT = 56 * 1024 * 1024


def _cparams(sem):
    return pltpu.CompilerParams(dimension_semantics=sem, vmem_limit_bytes=VMEM_LIMIT)


def _dot(a, b):
    return jnp.dot(a, b, preferred_element_type=F32)


def _dot_nt(a, b):
    return lax.dot_general(a, b, (((1,), (1,)), ((), ())), preferred_element_type=F32)


def _dot_tn(a, b):
    return lax.dot_general(a, b, (((0,), (0,)), ((), ())), preferred_element_type=F32)


def _split_bf16(a):
    hi = a.astype(BF16)
    lo = (a - hi.astype(F32)).astype(BF16)
    return hi, lo


def _sigmoid(x):
    return 1.0 / (1.0 + jnp.exp(-x))


def _silu(x):
    return x * _sigmoid(x)


def _log1p_exp_neg_abs(x):
    return jnp.log(1.0 + jnp.exp(-jnp.abs(x)))


def _log_sigmoid(x):
    return jnp.minimum(x, 0.0) - _log1p_exp_neg_abs(x)


def _rms(x, g):
    return x * lax.rsqrt(jnp.mean(x * x, axis=-1, keepdims=True) + EPS) * g


def _ada_kernel(c_ref, w_ref, b_ref, o_ref):
    a = _silu(c_ref[...])
    a_hi, a_lo = _split_bf16(a)
    w_hi, w_lo = _split_bf16(w_ref[0])
    acc = _dot(a_hi, w_hi) + _dot(a_lo, w_hi) + _dot(a_hi, w_lo)
    o_ref[0] = acc + b_ref[0]


def _ada(cc, ada_w, ada_b):
    depth, d, n = ada_w.shape
    tn = 1024
    rows = cc.shape[0]
    return pl.pallas_call(
        _ada_kernel,
        grid=(depth, n // tn),
        in_specs=[
            pl.BlockSpec((rows, d), lambda l, j: (0, 0)),
            pl.BlockSpec((1, d, tn), lambda l, j: (l, 0, j)),
            pl.BlockSpec((1, 1, tn), lambda l, j: (l, 0, j)),
        ],
        out_specs=pl.BlockSpec((1, rows, tn), lambda l, j: (l, 0, j)),
        out_shape=jax.ShapeDtypeStruct((depth, rows, n), F32),
        compiler_params=_cparams(("arbitrary", "arbitrary")),
        name="ada_mod",
    )(cc, ada_w, ada_b.reshape(depth, 1, n))


def _mproj_kernel(n_lat_tiles, xl_ref, xc_ref, sc_ref, sh_ref, g_ref, wqk_ref, wv_ref, wo_ref, wg_ref,
                  qk_ref, v_ref, o_ref, gt_ref):
    j = pl.program_id(1)
    x = jnp.where(j < n_lat_tiles, xl_ref[0], xc_ref[0])
    h = (_rms(x, g_ref[...]) * (1.0 + sc_ref[0]) + sh_ref[0]).astype(BF16)
    qk_ref[0] = _dot(h, wqk_ref[...]).astype(BF16)
    v_ref[0] = _dot(h, wv_ref[...]).astype(BF16)
    o_ref[0] = _dot(h, wo_ref[...]).astype(BF16)
    gt_ref[0] = _dot(h, wg_ref[...])[:, :4 * M_HEADS]


def _mod_spec(d, layer, k, n_lat_tiles, rows):
    def imap(b, j):
        row = jnp.where(j < n_lat_tiles, b, rows - 1)
        return ((layer * rows + row) * N_ADA + k, 0, 0)
    return pl.BlockSpec((1, 1, d), imap)


def _mproj(x, ctx, modv, rows, norm_g, w_qk, w_v, w_o, w_g, tm):
    bsz, ll, d = x.shape
    lc = ctx.shape[1]
    nl, nc = ll // tm, lc // tm
    ltot = ll + lc
    wspec = lambda n: pl.BlockSpec((d, n), lambda b, j: (0, 0))
    tok = lambda n: pl.BlockSpec((1, tm, n), lambda b, j: (b, j, 0))
    return pl.pallas_call(
        functools.partial(_mproj_kernel, nl),
        grid=(bsz, nl + nc),
        in_specs=[
            pl.BlockSpec((1, tm, d), lambda b, j: (b, jnp.minimum(j, nl - 1), 0)),
            pl.BlockSpec((1, tm, d), lambda b, j: (b, jnp.maximum(j - nl, 0), 0)),
            _mod_spec(d, 0, 1, nl, rows), _mod_spec(d, 0, 0, nl, rows),
            pl.BlockSpec((1, d), lambda b, j: (0, 0)),
            wspec(w_qk.shape[1]), wspec(w_v.shape[1]), wspec(w_o.shape[1]), wspec(w_g.shape[1]),
        ],
        out_specs=[tok(w_qk.shape[1]), tok(w_v.shape[1]), tok(w_o.shape[1]), tok(4 * M_HEADS)],
        out_shape=[
            jax.ShapeDtypeStruct((bsz, ltot, w_qk.shape[1]), BF16),
            jax.ShapeDtypeStruct((bsz, ltot, w_v.shape[1]), BF16),
            jax.ShapeDtypeStruct((bsz, ltot, w_o.shape[1]), BF16),
            jax.ShapeDtypeStruct((bsz, ltot, 4 * M_HEADS), F32),
        ],
        compiler_params=_cparams(("arbitrary", "arbitrary")),
        name="mlstm_in_proj",
    )(x, ctx, modv, modv, norm_g, w_qk, w_v, w_o, w_g)


def _mlstm_kernel(t, ll, lc, hp, q_ref, k_ref, v_ref, o_ref, gc_ref, gr_ref, cwq_ref, cwk_ref, cbq_ref, cbk_ref,
                  gbc_ref, gbr_ref, hg_ref, out_ref, qs, ks, hb, c_s, m_s):
    ltot = ll + lc
    dk = q_ref.shape[2] // hp
    dv = v_ref.shape[2] // hp
    row = lax.broadcasted_iota(jnp.int32, (ltot, 1), 0)
    first = (row == 0) | (row == ll)
    last = (row == ll - 1) | (row == ltot - 1)

    def conv(x_ref, w_ref, b_ref, cols):
        x = x_ref[0, :, cols].astype(F32)
        w = w_ref[:, cols]
        xp = jnp.where(first, 0.0, pltpu.roll(x, 1, 0))
        xn = jnp.where(last, 0.0, pltpu.roll(x, ltot - 1, 0))
        return _silu(xp * w[0:1] + x * w[1:2] + xn * w[2:3] + b_ref[:, cols])

    for j in range(hp):
        cols = slice(j * dk, (j + 1) * dk)
        qs[:, cols] = (conv(q_ref, cwq_ref, cbq_ref, cols) * (dk ** -0.5)).astype(BF16)
        ks[:, cols] = conv(k_ref, cwk_ref, cbk_ref, cols).astype(BF16)

    c_s[...] = jnp.zeros(c_s.shape, F32)
    m_s[...] = jnp.zeros(m_s.shape, F32)

    ti = lax.broadcasted_iota(jnp.int32, (t, t), 0)
    si = lax.broadcasted_iota(jnp.int32, (t, t), 1)

    ones = jnp.ones((t, LANES), BF16)

    def chunks(starts):
        jobs = [(j, d) for j in range(hp) for d in range(2)]
        st = {}
        for j, d in jobs:
            rows = pl.ds(starts[d], t)
            gc = gc_ref[0, j, rows, :] + gbc_ref[j]
            gr = gr_ref[0, j, :, rows] + gbr_ref[j]
            ig_c, lf_c = gc[:, 2 * d:2 * d + 1], _log_sigmoid(gc[:, 2 * d + 1:2 * d + 2])
            ig_r, lf_r = gr[2 * d:2 * d + 1, :], _log_sigmoid(gr[2 * d + 1:2 * d + 2, :])
            seen = (si <= ti) if d == 0 else (si >= ti)
            seen_t = (ti <= si) if d == 0 else (ti >= si)
            b_c = jnp.sum(jnp.where(seen, lf_r, 0.0), axis=1, keepdims=True)
            b_r = jnp.sum(jnp.where(seen_t, lf_c, 0.0), axis=0, keepdims=True)
            total = jnp.sum(lf_r, axis=1, keepdims=True)
            st[j, d] = dict(rows=rows, ig_c=ig_c, ig_r=ig_r, b_c=b_c, b_r=b_r, total=total, seen=seen,
                            m_prev=m_s[2 * j + d][0:1, 0:1])
        for j, d in jobs:
            s = st[j, d]
            s["q"] = qs[s["rows"], j * dk:(j + 1) * dk]
            s["k"] = ks[s["rows"], j * dk:(j + 1) * dk]
            s["v"] = jnp.concatenate([v_ref[0, s["rows"], j * dv:(j + 1) * dv], ones], axis=1)
            s["qk"] = _dot_nt(s["q"], s["k"])
        for j, d in jobs:
            s = st[j, d]
            log_d = jnp.where(s["seen"], s["b_c"] - s["b_r"] + s["ig_r"], -jnp.inf)
            log_inter = s["b_c"] + s["m_prev"]
            m_t = jnp.maximum(log_inter, jnp.max(log_d, axis=1, keepdims=True))
            s["w_ts"] = jnp.exp(log_d - m_t) * s["qk"]
            s["inter"] = jnp.exp(log_inter - m_t)
            s["m_t"] = m_t
        for j, d in jobs:
            s = st[j, d]
            lhs = jnp.concatenate([s["w_ts"].astype(BF16), s["q"] * s["inter"].astype(BF16)], axis=1)
            rhs = jnp.concatenate([s["v"], c_s[2 * j + d].astype(BF16)], axis=0)
            s["num"] = _dot(lhs, rhs)
        for j, d in jobs:
            s = st[j, d]
            den = s["num"][:, dv:]
            scale = 1.0 / jnp.maximum(jnp.abs(den), jnp.exp(-s["m_t"]))
            h = s["num"][:, :dv] * jnp.concatenate([scale] * (dv // LANES), axis=1)
            if d == 0:
                out_ref[0, s["rows"], j * dv:(j + 1) * dv] = h.astype(BF16)
            else:
                hb[s["rows"], j * dv:(j + 1) * dv] = h.astype(BF16)
        for j, d in jobs:
            s = st[j, d]
            g_c = s["total"] - s["b_c"] + s["ig_c"]
            g_r = s["total"] - s["b_r"] + s["ig_r"]
            m_new = jnp.maximum(s["total"] + s["m_prev"], jnp.max(g_r, axis=1, keepdims=True))
            w_c = jnp.exp(g_c - m_new)
            decay = jnp.exp(s["total"] + s["m_prev"] - m_new)
            wv = s["v"] * w_c.astype(BF16)
            c_s[2 * j + d] = decay * c_s[2 * j + d] + _dot_tn(s["k"], wv)
            m_s[2 * j + d] = jnp.broadcast_to(m_new, m_s.shape[1:])

    ncc, ncl = lc // t, ll // t
    for i in range(ncc):
        chunks((ll + i * t, ll + (ncc - 1 - i) * t))

    def body(i, carry):
        chunks((pl.multiple_of(i * t, t), pl.multiple_of((ncl - 1 - i) * t, t)))
        return carry

    lax.fori_loop(0, ncl, body, 0)

    def epilogue(i, carry):
        s = pl.multiple_of(i * t, t)
        for j in range(hp):
            cols = slice(j * dv, (j + 1) * dv)
            hs = out_ref[0, pl.ds(s, t), cols].astype(F32) + hb[pl.ds(s, t), cols].astype(F32)
            y = _rms(hs, hg_ref[:, cols])
            out_ref[0, pl.ds(s, t), cols] = (y * _sigmoid(o_ref[0, pl.ds(s, t), cols].astype(F32))).astype(BF16)
        return carry

    lax.fori_loop(0, ltot // t, epilogue, 0, unroll=2)


def _mlstm_scan(qk, v, o, gates, conv_w, conv_b, gate_b, head_g, ll, lc):
    bsz, ltot, _ = qk.shape
    nh = M_HEADS
    hp = 2
    dk = qk.shape[2] // (2 * nh)
    dv = v.shape[2] // nh
    t = SCAN_CHUNK
    g4 = gates.reshape(bsz, ltot, 4, nh).transpose(0, 3, 1, 2)
    g4t = g4.transpose(0, 1, 3, 2)
    gb = gate_b.reshape(4, nh).T
    return pl.pallas_call(
        functools.partial(_mlstm_kernel, t, ll, lc, hp),
        grid=(bsz, nh // hp),
        in_specs=[
            pl.BlockSpec((1, ltot, hp * dk), lambda b, h: (b, 0, h)),
            pl.BlockSpec((1, ltot, hp * dk), lambda b, h: (b, 0, nh // hp + h)),
            pl.BlockSpec((1, ltot, hp * dv), lambda b, h: (b, 0, h)),
            pl.BlockSpec((1, ltot, hp * dv), lambda b, h: (b, 0, h)),
            pl.BlockSpec((1, hp, ltot, 4), lambda b, h: (b, h, 0, 0)),
            pl.BlockSpec((1, hp, 4, ltot), lambda b, h: (b, h, 0, 0)),
            pl.BlockSpec((3, hp * dk), lambda b, h: (0, h)),
            pl.BlockSpec((3, hp * dk), lambda b, h: (0, nh // hp + h)),
            pl.BlockSpec((1, hp * dk), lambda b, h: (0, h)),
            pl.BlockSpec((1, hp * dk), lambda b, h: (0, nh // hp + h)),
            pl.BlockSpec((hp, 1, 4), lambda b, h: (h, 0, 0)),
            pl.BlockSpec((hp, 4, 1), lambda b, h: (h, 0, 0)),
            pl.BlockSpec((1, hp * dv), lambda b, h: (0, h)),
        ],
        out_specs=pl.BlockSpec((1, ltot, hp * dv), lambda b, h: (b, 0, h)),
        out_shape=jax.ShapeDtypeStruct((bsz, ltot, nh * dv), BF16),
        scratch_shapes=[
            pltpu.VMEM((ltot, hp * dk), BF16), pltpu.VMEM((ltot, hp * dk), BF16),
            pltpu.VMEM((ltot, hp * dv), BF16),
            pltpu.VMEM((2 * hp, dk, dv + LANES), F32), pltpu.VMEM((2 * hp, 8, LANES), F32),
        ],
        compiler_params=_cparams(("arbitrary", "arbitrary")),
        name="mlstm_scan",
    )(qk, qk, v, o, g4, g4t, conv_w, conv_w, conv_b.reshape(1, -1), conv_b.reshape(1, -1),
      gb.reshape(nh, 1, 4), gb.reshape(nh, 4, 1), head_g.reshape(1, -1))


def _router_logits_t(h2, rhi_ref, rlo_ref):
    h_hi, h_lo = _split_bf16(h2)
    lg = _dot(h_hi, rhi_ref[...]) + _dot(h_lo, rhi_ref[...]) + _dot(h_hi, rlo_ref[...])
    return lg.T[:N_EXPERTS, :]


def _out0_kernel(n_lat_tiles, hg_ref, xl_ref, xc_ref, g1_ref, sc_ref, sh_ref, ng_ref, wo_ref, rhi_ref, rlo_ref,
                 s_ref, h2_ref, lg_ref):
    j = pl.program_id(1)
    half = hg_ref.shape[1] // 2
    d = s_ref.shape[2]
    parts = [slice(p * half, (p + 1) * half) for p in range(2)]
    ys = [_dot(hg_ref[0, rows, :], wo_ref[...]) for rows in parts]
    ss = []
    for rows, y in zip(parts, ys):
        s = jnp.where(j < n_lat_tiles, xl_ref[0, rows, :], xc_ref[0, rows, :]) + g1_ref[0] * y
        s_ref[0, rows, :] = s
        ss.append(s)
    h2s = []
    for rows, s in zip(parts, ss):
        h2 = _rms(s, ng_ref[...]) * (1.0 + sc_ref[0]) + sh_ref[0]
        h2_ref[rows, :d] = h2
        h2_ref[rows, d:] = jnp.zeros((half, LANES), F32)
        h2s.append(h2)
    for rows, h2 in zip(parts, h2s):
        lg_ref[:, rows] = _router_logits_t(h2, rhi_ref, rlo_ref)


def _out0(hg, x, ctx, modv, rows, norm_g, w_out, r_hi, r_lo, tm):
    bsz, ll, d = x.shape
    lc = ctx.shape[1]
    nl, nc = ll // tm, lc // tm
    nt = nl + nc
    ltot = ll + lc
    return pl.pallas_call(
        functools.partial(_out0_kernel, nl),
        grid=(bsz, nt),
        in_specs=[
            pl.BlockSpec((1, tm, hg.shape[2]), lambda b, j: (b, j, 0)),
            pl.BlockSpec((1, tm, d), lambda b, j: (b, jnp.minimum(j, nl - 1), 0)),
            pl.BlockSpec((1, tm, d), lambda b, j: (b, jnp.maximum(j - nl, 0), 0)),
            _mod_spec(d, 0, 2, nl, rows), _mod_spec(d, 0, 4, nl, rows), _mod_spec(d, 0, 3, nl, rows),
            pl.BlockSpec((1, d), lambda b, j: (0, 0)),
            pl.BlockSpec(w_out.shape, lambda b, j: (0, 0)),
            pl.BlockSpec(r_hi.shape, lambda b, j: (0, 0)),
            pl.BlockSpec(r_lo.shape, lambda b, j: (0, 0)),
        ],
        out_specs=[
            pl.BlockSpec((1, tm, d), lambda b, j: (b, j, 0)),
            pl.BlockSpec((tm, d + LANES), lambda b, j: (b * nt + j, 0)),
            pl.BlockSpec((N_EXPERTS, tm), lambda b, j: (0, b * nt + j)),
        ],
        out_shape=[
            jax.ShapeDtypeStruct((bsz, ltot, d), F32),
            jax.ShapeDtypeStruct((bsz * ltot, d + LANES), F32),
            jax.ShapeDtypeStruct((N_EXPERTS, bsz * ltot), F32),
        ],
        compiler_params=_cparams(("arbitrary", "arbitrary")),
        name="mlstm_out_proj",
    )(hg, x, ctx, modv, modv, modv, norm_g, w_out, r_hi, r_lo)


def _route_kernel(lg_ref, bias_ref, h2x_hbm, cls_ref, w_ref):
    del h2x_hbm
    s = _sigmoid(lg_ref[...])
    sel = s + bias_ref[...]
    srow = [s[e:e + 1, :] for e in range(N_EXPERTS)]
    row = [sel[e:e + 1, :] for e in range(N_EXPERTS)]
    best = jnp.zeros(row[0].shape, jnp.int32)
    best_score = None
    for g in range(N_GROUPS):
        r = row[g * E_PER_GROUP:(g + 1) * E_PER_GROUP]
        score = None
        for lo, hi in zip(PAIR_LO, PAIR_HI):
            pair = r[lo] + r[hi]
            score = pair if score is None else jnp.maximum(score, pair)
        if g == 0:
            best_score = score
        else:
            better = score > best_score
            best = jnp.where(better, g, best)
            best_score = jnp.where(better, score, best_score)
    gs = [row[i] for i in range(E_PER_GROUP)]
    gw = [srow[i] for i in range(E_PER_GROUP)]
    for g in range(1, N_GROUPS):
        hit = best == g
        gs = [jnp.where(hit, row[g * E_PER_GROUP + i], gs[i]) for i in range(E_PER_GROUP)]
        gw = [jnp.where(hit, srow[g * E_PER_GROUP + i], gw[i]) for i in range(E_PER_GROUP)]
    keep = []
    for i in range(E_PER_GROUP):
        beaten = jnp.zeros(best.shape, jnp.int32)
        for j in range(E_PER_GROUP):
            if j == i:
                continue
            wins = (gs[j] > gs[i]) | ((gs[j] == gs[i]) & (j < i))
            beaten = beaten + wins.astype(jnp.int32)
        keep.append(beaten < 2)
    pair_id = jnp.zeros(best.shape, jnp.int32)
    w_lo = jnp.zeros(best.shape, F32)
    w_hi = jnp.zeros(best.shape, F32)
    for p, (lo, hi) in enumerate(zip(PAIR_LO, PAIR_HI)):
        hit = keep[lo] & keep[hi]
        pair_id = jnp.where(hit, p, pair_id)
        w_lo = jnp.where(hit, gw[lo], w_lo)
        w_hi = jnp.where(hit, gw[hi], w_hi)
    tot = w_lo + w_hi
    cls_ref[...] = best * N_PAIRS + pair_id
    lanes_t = jnp.concatenate([w_lo / tot, w_hi / tot, jnp.zeros((LANES - 2, w_lo.shape[1]), F32)], axis=0)
    w_ref[...] = lanes_t.T


def _route(logits_t, router_bias, h2x):
    n = logits_t.shape[1]
    d = h2x.shape[1] - LANES
    tn = next(cand for cand in (2048, 1024, 512, 256, 128) if n % cand == 0)
    return pl.pallas_call(
        _route_kernel,
        grid=(n // tn,),
        in_specs=[
            pl.BlockSpec((N_EXPERTS, tn), lambda i: (0, i)),
            pl.BlockSpec((N_EXPERTS, 1), lambda i: (0, 0)),
            pl.BlockSpec(memory_space=pl.ANY),
        ],
        out_specs=[pl.BlockSpec((1, tn), lambda i: (0, i)), pl.BlockSpec((tn, LANES), lambda i: (i, d // LANES))],
        out_shape=[jax.ShapeDtypeStruct((1, n), jnp.int32), jax.ShapeDtypeStruct(h2x.shape, F32)],
        input_output_aliases={2: 1},
        compiler_params=_cparams(("arbitrary",)),
        name="moe_route",
    )(logits_t, router_bias.reshape(N_EXPERTS, 1).astype(F32), h2x)


def _moe_kernel(tm, dump_base, spare_rows, nu_ref, e0_ref, e1_ref, base_ref, nv_ref, tok_ref, dst_ref, h2_hbm,
                wg0_ref, wu0_ref, wd0_ref, wg1_ref, wu1_ref, wd1_ref, f_hbm,
                xb0, xb1, yb0, yb1, xbf, gsem, ssem, zsem, idle_sem):
    del e0_ref, e1_ref
    t = pl.program_id(0)
    n_used = nu_ref[0]
    d = xbf.shape[1]
    xbuf, ybuf = (xb0, xb1), (yb0, yb1)

    def gather_row(base, r, slot):
        tok = tok_ref[base + r]
        pltpu.make_async_copy(h2_hbm.at[pl.ds(tok, 1)], xbuf[slot].at[pl.ds(r, 1)], gsem.at[slot]).start()

    def scatter_row(base, nv, r, slot):
        row = jnp.where(r < nv, dst_ref[base + r], dump_base + r)
        pltpu.make_async_copy(ybuf[slot].at[pl.ds(r, 1)], f_hbm.at[pl.ds(row, 1)], ssem.at[slot]).start(priority=1)

    def wait_gather(slot):
        pltpu.make_async_copy(h2_hbm.at[pl.ds(0, tm)], xbuf[slot], gsem.at[slot]).wait()

    def wait_scatter(slot):
        pltpu.make_async_copy(ybuf[slot], f_hbm.at[pl.ds(0, tm)], ssem.at[slot]).wait()

    @pl.when(t == 0)
    def _():
        yb1[...] = jnp.zeros(yb1.shape, F32)
        fills = [pltpu.make_async_copy(yb1, f_hbm.at[pl.ds(row, tm)], zsem) for row in spare_rows]
        for fill in fills:
            fill.start()
        for fill in fills:
            fill.wait()
        base0 = base_ref[1]

        def first(r, carry):
            gather_row(base0, r, 0)
            return carry

        lax.fori_loop(0, tm, first, 0)

    def step(cur):
        nxt = 1 - cur
        base_next = base_ref[t + 2]
        base_prev = base_ref[t]
        nv_prev = nv_ref[t]

        @pl.when(t < n_used)
        def _():
            wait_gather(cur)
            xbf[...] = xbuf[cur][:, :d].astype(BF16)
            w = xbuf[cur][:, d:]

            for r in range(tm):
                gather_row(base_next, r, nxt)
            x = xbf[...]
            a0 = (_silu(_dot(x, wg0_ref[0, 0])) * _dot(x, wu0_ref[0, 0]) * w[:, 0:1]).astype(BF16)

            z = pl.semaphore_read(idle_sem)
            zero = lax.shift_right_logical(z.astype(jnp.uint32), jnp.uint32(32)).astype(jnp.int32)
            off = pl.multiple_of(zero * 16, 16)

            for r in range(tm):
                scatter_row(base_prev, nv_prev, r, nxt)
            x = xbf[pl.ds(off, tm), :]
            a1 = (_silu(_dot(x, wg1_ref[0, 0])) * _dot(x, wu1_ref[0, 0]) * w[:, 1:2]).astype(BF16)
            y = _dot(a0, wd0_ref[0, 0]) + _dot(a1, wd1_ref[0, 0])

            @pl.when(t >= 1)
            def _():
                wait_scatter(cur)

            ybuf[cur][...] = y

        @pl.when(t == n_used)
        def _():
            wait_gather(cur)
            wait_scatter(cur)

            def last(r, carry):
                scatter_row(base_prev, nv_prev, r, nxt)
                return carry

            lax.fori_loop(0, tm, last, 0)
            wait_scatter(nxt)

    @pl.when(t % 2 == 0)
    def _():
        step(0)

    @pl.when(t % 2 == 1)
    def _():
        step(1)


def _moe(h2x, cls, dst_of_token, dump_base, spare_rows, out_rows, layer, wg, wu, wd):
    n = h2x.shape[0]
    d = h2x.shape[1] - LANES
    tm = MOE_TILE
    n_tiles = n // tm + N_CLASSES
    cls = cls.reshape(n)
    _, tok_sorted, dst_sorted = lax.sort((cls, jnp.arange(n, dtype=jnp.int32), dst_of_token), num_keys=1)
    tail = jnp.zeros((tm,), jnp.int32)
    tok_sorted = jnp.concatenate([tok_sorted, tail])
    dst_sorted = jnp.concatenate([dst_sorted, tail])
    counts = jnp.sum(cls[:, None] == jnp.arange(N_CLASSES, dtype=jnp.int32)[None, :], axis=0).astype(jnp.int32)
    tiles_per = (counts + tm - 1) // tm
    tile_end = jnp.cumsum(tiles_per)
    tile_start = tile_end - tiles_per
    first_sorted = jnp.cumsum(counts) - counts
    n_used = tile_end[-1]
    tile_id = jnp.arange(n_tiles, dtype=jnp.int32)
    tile_cls = jnp.searchsorted(tile_end, jnp.minimum(tile_id, n_used - 1), side="right").astype(jnp.int32)
    tile_cls = jnp.minimum(tile_cls, N_CLASSES - 1)
    group, pair = tile_cls // N_PAIRS, tile_cls % N_PAIRS
    e0 = group * E_PER_GROUP + jnp.asarray(PAIR_LO, jnp.int32)[pair]
    e1 = group * E_PER_GROUP + jnp.asarray(PAIR_HI, jnp.int32)[pair]
    in_class = (tile_id - tile_start[tile_cls]) * tm
    used = tile_id < n_used
    base = jnp.where(used, first_sorted[tile_cls] + in_class, 0)
    nv = jnp.where(used, jnp.clip(counts[tile_cls] - in_class, 0, tm), 0)
    guard = jnp.zeros((1,), jnp.int32)
    base = jnp.concatenate([guard, base, guard]).astype(jnp.int32)
    nv = jnp.concatenate([guard, nv, guard]).astype(jnp.int32)

    de = wg.shape[3]
    smem = pl.BlockSpec(memory_space=pltpu.SMEM)
    up0 = pl.BlockSpec((1, 1, d, de), lambda i, nu, e0, e1, base, nv: (layer, e0[i], 0, 0))
    up1 = pl.BlockSpec((1, 1, d, de), lambda i, nu, e0, e1, base, nv: (layer, e1[i], 0, 0))
    dn0 = pl.BlockSpec((1, 1, de, d), lambda i, nu, e0, e1, base, nv: (layer, e0[i], 0, 0))
    dn1 = pl.BlockSpec((1, 1, de, d), lambda i, nu, e0, e1, base, nv: (layer, e1[i], 0, 0))
    return pl.pallas_call(
        functools.partial(_moe_kernel, tm, dump_base, spare_rows),
        grid_spec=pltpu.PrefetchScalarGridSpec(
            num_scalar_prefetch=5,
            grid=(n_tiles,),
            in_specs=[
                smem, smem,
                pl.BlockSpec(memory_space=pl.ANY),
                up0, up0, dn0, up1, up1, dn1,
            ],
            out_specs=pl.BlockSpec(memory_space=pl.ANY),
            scratch_shapes=[
                pltpu.VMEM((tm, d + LANES), F32), pltpu.VMEM((tm, d + LANES), F32),
                pltpu.VMEM((tm, d), F32), pltpu.VMEM((tm, d), F32),
                pltpu.VMEM((tm, d), BF16),
                pltpu.SemaphoreType.DMA((2,)), pltpu.SemaphoreType.DMA((2,)), pltpu.SemaphoreType.DMA(()),
                pltpu.SemaphoreType.REGULAR(()),
            ],
        ),
        out_shape=jax.ShapeDtypeStruct((out_rows, d), F32),
        compiler_params=_cparams(("arbitrary",)),
        name="moe_experts",
    )(n_used.reshape(1).astype(jnp.int32), e0, e1, base, nv, tok_sorted, dst_sorted, h2x,
      wg, wu, wd, wg, wu, wd)


def _hproj_kernel(n_lat_tiles, rows, s_hbm, sc_ref_, f_ref, g2_ref, sc_ref, sh_ref, ng_ref, lb_ref,
                  wq_ref, wzf_ref, wzb_ref, wi_ref, wg_ref,
                  s2_ref, q_ref, kf_ref, kb_ref, lff_ref, lfb_ref, i_ref, g_ref, xt, sbuf, sem):
    b, j = pl.program_id(0), pl.program_id(1)
    n_lat = pl.num_programs(0) * n_lat_tiles
    g2 = g2_ref[0]

    def column_copies(tile, slot):
        bb, jj = tile // n_lat_tiles, tile % n_lat_tiles
        return [pltpu.make_async_copy(s_hbm.at[bb, pl.ds(0, rows), jj * COL_TILE + c, :],
                                      sbuf.at[slot, pl.ds(c * rows, rows), :], sem.at[slot])
                for c in range(COL_TILE)]

    @pl.when(j < n_lat_tiles)
    def _():
        tile = b * n_lat_tiles + j
        slot = tile % 2

        @pl.when(tile == 0)
        def _():
            for cp in column_copies(tile, slot):
                cp.start()

        @pl.when(tile + 1 < n_lat)
        def _():
            for cp in column_copies(tile + 1, 1 - slot):
                cp.start()

        for cp in column_copies(tile, slot):
            cp.wait()
        s = sbuf[slot] + g2 * f_ref[0]
        s2_ref[0] = s
        xt[...] = s

    @pl.when(j >= n_lat_tiles)
    def _():
        xt[...] = sc_ref_[0] + g2 * f_ref[0]

    h = (_rms(xt[...], ng_ref[...]) * (1.0 + sc_ref[0]) + sh_ref[0]).astype(BF16)
    n_chunks = 4
    width = wq_ref.shape[1] // n_chunks

    def gates(dd, k_ref, lf_ref, cols, z):
        log_lb = lb_ref[2 * dd:2 * dd + 1, cols]
        log_1mlb = lb_ref[2 * dd + 1:2 * dd + 2, cols]
        a = log_1mlb + _log_sigmoid(z)
        lf_ref[0, :, cols] = jnp.maximum(log_lb, a) + _log1p_exp_neg_abs(log_lb - a)
        k_ref[0, :, cols] = jnp.exp(a - z).astype(BF16)

    def store_q(cols, y):
        q_ref[0, :, cols] = _silu(y).astype(BF16)

    def store_plain(out_ref, cols, y):
        out_ref[0, :, cols] = y.astype(BF16)

    gates_f = functools.partial(gates, 0, kf_ref, lff_ref)
    gates_b = functools.partial(gates, 1, kb_ref, lfb_ref)
    plain_i = functools.partial(store_plain, i_ref)
    plain_g = functools.partial(store_plain, g_ref)
    work = []
    for c in range(n_chunks):
        work += [(wzf_ref, gates_f, c), (wi_ref, plain_i, c)]
    for c in range(n_chunks):
        work += [(wzb_ref, gates_b, c), (wg_ref, plain_g, c)]
    work += [(wq_ref, store_q, c) for c in range(n_chunks)]
    pending = None
    for w_ref, tail, c in work:
        cols = slice(c * width, (c + 1) * width)
        y = _dot(h, w_ref[:, cols])
        if pending is not None:
            pending()
        pending = functools.partial(tail, cols, y)
    pending()


def _hproj(s1, f0, modv, rows_mod, norm_g, lbtab, ws, ll, lc):
    bsz, ltot, d = s1.shape
    rows = ll // GRID_W
    tm = COL_TILE * rows
    nl, nc = GRID_W // COL_TILE, lc // tm
    s_grid = s1.reshape(bsz, ltot // GRID_W, GRID_W, d)
    f_tok = f0.reshape(bsz, -1, d)
    ctx = pl.BlockSpec((1, tm, d), lambda b, j: (b, ll // tm + jnp.maximum(j - nl, 0), 0))
    wspec = pl.BlockSpec((d, d), lambda b, j: (0, 0))
    tok = pl.BlockSpec((1, tm, d), lambda b, j: (b, j, 0))
    lat = pl.BlockSpec((1, tm, d), lambda b, j: (b, jnp.minimum(j, nl - 1), 0))

    def g2_map(b, j):
        row = jnp.where(j < nl, b, rows_mod - 1)
        return ((0 * rows_mod + row) * N_ADA + 5, 0, 0)

    outs = pl.pallas_call(
        functools.partial(_hproj_kernel, nl, rows),
        grid=(bsz, nl + nc),
        in_specs=[
            pl.BlockSpec(memory_space=pl.ANY), ctx, tok,
            pl.BlockSpec((1, 1, d), g2_map),
            _mod_spec(d, 1, 1, nl, rows_mod), _mod_spec(d, 1, 0, nl, rows_mod),
            pl.BlockSpec((1, d), lambda b, j: (0, 0)),
            pl.BlockSpec((4, d), lambda b, j: (0, 0)),
            wspec, wspec, wspec, wspec, wspec,
        ],
        out_specs=[lat, tok, tok, tok, tok, tok, tok, tok],
        out_shape=[jax.ShapeDtypeStruct((bsz, ll, d), F32)]
        + [jax.ShapeDtypeStruct((bsz, ltot, d), dt) for dt in (BF16, BF16, BF16, F32, F32, BF16, BF16)],
        scratch_shapes=[pltpu.VMEM((tm, d), F32), pltpu.VMEM((2, tm, d), F32), pltpu.SemaphoreType.DMA((2,))],
        compiler_params=_cparams(("arbitrary", "arbitrary")),
        name="hgrn_in_proj",
    )(s_grid, s1, f_tok, modv, modv, modv, norm_g, lbtab, *ws)
    return outs


def _hgrn_kernel(t, ll, lc, q_ref, kf_ref, kb_ref, lff_ref, lfb_ref, i_ref, g_ref, hg_ref, out_ref,
                 of, ob, inc_s, dec_s, qd_s):
    levels = []
    m = 2
    while m <= t:
        levels.append(m)
        m *= 2
    ti = lax.broadcasted_iota(jnp.int32, (t, t), 0)
    si = lax.broadcasted_iota(jnp.int32, (t, t), 1)
    xor = ti ^ si
    level = jnp.zeros((t, t), jnp.int32)
    for m in levels:
        level = level + (xor >= m // 2).astype(jnp.int32)
    feeds = ((si <= ti).astype(BF16), (si >= ti).astype(BF16))

    row = lax.broadcasted_iota(jnp.int32, (t, 1), 0)
    feeds_twice = tuple(jnp.concatenate([f, f], axis=1) for f in feeds)

    def block_ref(b, m, d):
        half = m // 2
        pos = half - 1 if d == 0 else half
        if m >= 8:
            dk = b.shape[1]
            b3 = b.reshape(t // m, m, dk)
            return jnp.broadcast_to(b3[:, pos:pos + 1, :], b3.shape).reshape(t, dk)
        r = row % m
        out = b
        for res in range(m):
            if res != pos:
                out = jnp.where(r == res, pltpu.roll(b, (res - pos) % t, 0), out)
        return out

    def neg_abs(x):
        sign = jnp.uint32(0x80000000)
        return lax.bitcast_convert_type(lax.bitcast_convert_type(x, jnp.uint32) | sign, F32)

    o_dir = (of, ob)

    def prepare(jobs, need_out):
        work = []
        for ci, d in jobs:
            rows = pl.ds(ci * t if isinstance(ci, int) else pl.multiple_of(ci * t, t), t)
            lf = (lff_ref if d == 0 else lfb_ref)[0, rows, :] * LOG2_E
            hi = lf.astype(BF16)
            lo = (lf - hi.astype(F32)).astype(BF16)
            b = _dot(feeds_twice[d], jnp.concatenate([hi, lo], axis=0))
            work.append((ci, d, rows, b))
        for ci, d, rows, b in work:
            b_tot = b[t - 1:t, :] if d == 0 else b[0:1, :]
            k = (kf_ref if d == 0 else kb_ref)[0, rows, :]
            inc_s[d, ci] = _dot_tn(i_ref[0, rows, :], k * jnp.exp2(b_tot - b).astype(BF16))
            dec_s[d, ci] = jnp.exp2(b_tot)
            if need_out:
                qd_s[d, rows, :] = q_ref[0, rows, :] * jnp.exp2(b).astype(BF16)
        if not need_out:
            return
        acc, qf, kf = [], [], []
        for ci, d, rows, b in work:
            q, k = q_ref[0, rows, :], (kf_ref if d == 0 else kb_ref)[0, rows, :]
            acc.append(_dot_nt(q, k))
            qf.append(q.astype(F32))
            kf.append(k.astype(F32))
        for idx, m in enumerate(levels):
            for n, (ci, d, rows, b) in enumerate(work):
                e = jnp.exp2(neg_abs(b - block_ref(b, m, d)))
                p = _dot_nt((qf[n] * e).astype(BF16), (kf[n] * e).astype(BF16))
                acc[n] = jnp.where(level == idx + 1, p, acc[n])
        for n, (ci, d, rows, b) in enumerate(work):
            o_dir[d][rows, :] = _dot(acc[n].astype(BF16) * feeds[d], i_ref[0, rows, :])

    ncc, ncl = lc // t, ll // t
    prepare([(ncl + i, d) for i in range(ncc) for d in range(2)], False)

    group = next(g for g in (8, 4, 2, 1) if ncl % g == 0)

    def body(i, carry):
        prepare([(group * i + j, d) for j in range(group) for d in range(2)], True)
        return carry

    lax.fori_loop(0, ncl // group, body, 0)

    for d in range(2):
        order = list(range(ncl, ncl + ncc)) + list(range(ncl))
        if d == 1:
            order = list(range(ncl + ncc - 1, ncl - 1, -1)) + list(range(ncl - 1, -1, -1))
        st = jnp.zeros(inc_s.shape[2:], F32)
        for ci in order:
            if ci < ncl:
                rows = pl.ds(ci * t, t)
                o_dir[d][rows, :] = o_dir[d][rows, :] + _dot_nt(qd_s[d, rows, :], st.astype(BF16))
            st = dec_s[d, ci] * st + inc_s[d, ci]

    def epilogue(i, carry):
        s = pl.multiple_of(i * t, t)
        y = _rms(of[pl.ds(s, t), :] + ob[pl.ds(s, t), :], hg_ref[...])
        out_ref[0, pl.ds(s, t), :] = (y * _silu(g_ref[0, pl.ds(s, t), :].astype(F32))).astype(BF16)
        return carry

    lax.fori_loop(0, ncl, epilogue, 0, unroll=4)


def _hgrn_scan(q, kf, kb, lff, lfb, iv, g, head_g, ll, lc):
    bsz, ltot, hk = q.shape
    nh = H_HEADS
    dk = hk // nh
    t = SCAN_CHUNK
    full = pl.BlockSpec((1, ltot, dk), lambda b, h: (b, 0, h))
    lat = pl.BlockSpec((1, ll, dk), lambda b, h: (b, 0, h))
    return pl.pallas_call(
        functools.partial(_hgrn_kernel, t, ll, lc),
        grid=(bsz, nh),
        in_specs=[full, full, full, full, full, full, lat, pl.BlockSpec((1, dk), lambda b, h: (0, h))],
        out_specs=lat,
        out_shape=jax.ShapeDtypeStruct((bsz, ll, hk), BF16),
        scratch_shapes=[pltpu.VMEM((ll, dk), F32), pltpu.VMEM((ll, dk), F32),
                        pltpu.VMEM((2, ltot // t, dk, dk), F32), pltpu.VMEM((2, ltot // t, 1, dk), F32),
                        pltpu.VMEM((2, ll, dk), BF16)],
        compiler_params=_cparams(("arbitrary", "arbitrary")),
        name="hgrn_scan",
    )(q, kf, kb, lff, lfb, iv, g, head_g.reshape(1, -1))


def _out1_kernel(hg_ref, s_ref, g1_ref, sc_ref, sh_ref, ng_ref, wo_ref, rhi_ref, rlo_ref,
                 s3_ref, h2_ref, lg_ref):
    half = hg_ref.shape[1] // 2
    d = s_ref.shape[2]
    parts = [slice(p * half, (p + 1) * half) for p in range(2)]
    ys = [_dot(hg_ref[0, rows, :], wo_ref[...]) for rows in parts]
    ss = []
    for rows, y in zip(parts, ys):
        s = s_ref[0, rows, :] + g1_ref[0] * y
        s3_ref[0, rows, :] = s
        ss.append(s)
    h2s = []
    for rows, s in zip(parts, ss):
        h2 = _rms(s, ng_ref[...]) * (1.0 + sc_ref[0]) + sh_ref[0]
        h2_ref[rows, :d] = h2
        h2_ref[rows, d:] = jnp.zeros((half, LANES), F32)
        h2s.append(h2)
    for rows, h2 in zip(parts, h2s):
        lg_ref[:, rows] = _router_logits_t(h2, rhi_ref, rlo_ref)


def _out1(hg, s2, modv, rows_mod, norm_g, w_out, r_hi, r_lo, tm):
    bsz, ll, hv = hg.shape
    d = s2.shape[2]
    tm = 2 * tm if ll % (2 * tm) == 0 else tm
    nl = ll // tm
    cm = pl.BlockSpec((1, tm, d), lambda b, j: (b, j, 0))
    return pl.pallas_call(
        _out1_kernel,
        grid=(bsz, nl),
        in_specs=[
            pl.BlockSpec((1, tm, hv), lambda b, j: (b, j, 0)),
            cm,
            _mod_spec(d, 1, 2, nl, rows_mod), _mod_spec(d, 1, 4, nl, rows_mod), _mod_spec(d, 1, 3, nl, rows_mod),
            pl.BlockSpec((1, d), lambda b, j: (0, 0)),
            pl.BlockSpec(w_out.shape, lambda b, j: (0, 0)),
            pl.BlockSpec(r_hi.shape, lambda b, j: (0, 0)),
            pl.BlockSpec(r_lo.shape, lambda b, j: (0, 0)),
        ],
        out_specs=[
            cm,
            pl.BlockSpec((tm, d + LANES), lambda b, j: (b * nl + j, 0)),
            pl.BlockSpec((N_EXPERTS, tm), lambda b, j: (0, b * nl + j)),
        ],
        out_shape=[
            jax.ShapeDtypeStruct(s2.shape, F32),
            jax.ShapeDtypeStruct((bsz * ll, d + LANES), F32),
            jax.ShapeDtypeStruct((N_EXPERTS, bsz * ll), F32),
        ],
        compiler_params=_cparams(("arbitrary", "arbitrary")),
        name="hgrn_out_proj",
    )(hg, s2, modv, modv, modv, norm_g, w_out, r_hi, r_lo)


def _final_kernel(rows, s_ref, f_ref, g2_ref, fg_ref, o_hbm, obuf, sem):
    b, j = pl.program_id(0), pl.program_id(1)
    nl = pl.num_programs(1)
    tile = b * nl + j
    n_tiles = pl.num_programs(0) * nl
    slot = tile % 2

    def column_copies(t, sl):
        bb, jj = t // nl, t % nl
        return [pltpu.make_async_copy(obuf.at[sl, pl.ds(c * rows, rows), :],
                                      o_hbm.at[bb, pl.ds(0, rows), jj * COL_TILE + c, :], sem.at[sl])
                for c in range(COL_TILE)]

    @pl.when(tile >= 2)
    def _():
        for cp in column_copies(tile - 2, slot):
            cp.wait()

    obuf[slot] = _rms(s_ref[0] + g2_ref[0] * f_ref[0], fg_ref[...])
    for cp in column_copies(tile, slot):
        cp.start()

    @pl.when(tile == n_tiles - 1)
    def _():
        @pl.when(tile >= 1)
        def _():
            for cp in column_copies(tile - 1, 1 - slot):
                cp.wait()
        for cp in column_copies(tile, slot):
            cp.wait()


def _final(s3, f1, modv, rows_mod, final_g, tm):
    bsz, ll, d = s3.shape
    nl = ll // tm
    rows = ll // GRID_W
    tok = pl.BlockSpec((1, tm, d), lambda b, j: (b, j, 0))
    out = pl.pallas_call(
        functools.partial(_final_kernel, rows),
        grid=(bsz, nl),
        in_specs=[tok, tok, _mod_spec(d, 1, 5, nl, rows_mod), pl.BlockSpec((1, d), lambda b, j: (0, 0))],
        out_specs=pl.BlockSpec(memory_space=pl.ANY),
        out_shape=jax.ShapeDtypeStruct((bsz, rows, GRID_W, d), F32),
        scratch_shapes=[pltpu.VMEM((2, tm, d), F32), pltpu.SemaphoreType.DMA((2,))],
        compiler_params=_cparams(("arbitrary", "arbitrary")),
        name="final_norm",
    )(s3, f1.reshape(bsz, -1, d), modv, final_g)
    return out.reshape(bsz, ll, d)


def kernel(x, c, ctx, c_ctx, ada_w, ada_b, norm_mix_g, norm_ffn_g, final_g, m_w_in, m_conv_w, m_conv_b, m_gate_b,
           m_head_g, m_w_out, h_w_in, h_lower_bounds, h_head_g, h_w_out, router_w, router_bias, e_w_gate,
           e_w_up, e_w_down):
    bsz, ll, d = x.shape
    lc = ctx.shape[1]
    ltot = ll + lc
    depth = ada_w.shape[0]
    assert depth == 2 and ll % GRID_W == 0 and lc % SCAN_CHUNK == 0 and ll % SCAN_CHUNK == 0
    rows = ll // GRID_W
    tm = COL_TILE * rows
    assert lc % tm == 0 and ltot % GRID_W == 0 and (bsz * ltot) % MOE_TILE == 0 and (bsz * ll) % MOE_TILE == 0

    rows_mod = 8 * ((bsz + 1 + 7) // 8)
    cc = jnp.zeros((rows_mod, d), F32).at[:bsz].set(c).at[rows_mod - 1].set(c_ctx)
    modv = _ada(cc, ada_w, ada_b).reshape(depth * rows_mod * N_ADA, 1, d)

    r_pad = jnp.zeros((d, LANES), F32).at[:, :N_EXPERTS].set(router_w)
    r_hi = r_pad.astype(BF16)
    r_lo = (r_pad - r_hi.astype(F32)).astype(BF16)
    wg_all, wu_all, wd_all = e_w_gate.astype(BF16), e_w_up.astype(BF16), e_w_down.astype(BF16)

    m_qk, m_v = m_conv_w.shape[2], m_head_g.shape[1]
    w_in = m_w_in[0].astype(BF16)
    w_gates = jnp.zeros((d, LANES), BF16).at[:, :4 * M_HEADS].set(w_in[:, m_qk + 2 * m_v:])
    qk, v, o, gates = _mproj(x, ctx, modv, rows_mod, norm_mix_g[0:1], w_in[:, :m_qk], w_in[:, m_qk:m_qk + m_v],
                             w_in[:, m_qk + m_v:m_qk + 2 * m_v], w_gates, tm)
    hg = _mlstm_scan(qk, v, o, gates, m_conv_w[0], m_conv_b[0], m_gate_b[0], m_head_g[0], ll, lc)
    s1, h2, lg = _out0(hg, x, ctx, modv, rows_mod, norm_ffn_g[0:1], m_w_out[0].astype(BF16), r_hi, r_lo, tm)
    cls, h2 = _route(lg, router_bias, h2)
    tok0 = jnp.arange(bsz * ltot, dtype=jnp.int32)
    pos0 = tok0 % ltot
    pos0 = jnp.where(pos0 < ll, (pos0 % GRID_W) * rows + pos0 // GRID_W, pos0)
    dst0 = (tok0 // ltot) * (ltot + MOE_TILE) + pos0
    spare0 = tuple(b * (ltot + MOE_TILE) + ltot for b in range(1, bsz))
    f0 = _moe(h2, cls, dst0, ltot, spare0, bsz * (ltot + MOE_TILE), 0, wg_all, wu_all, wd_all)

    lbs = jnp.cumsum(jax.nn.softmax(h_lower_bounds.astype(F32), axis=0), axis=0)
    lb = (lbs - lbs[0])[1].reshape(2, -1)
    lbtab = jnp.stack([jnp.log(lb[0]), jnp.log1p(-lb[0]), jnp.log(lb[1]), jnp.log1p(-lb[1])])
    hw = h_w_in[0].astype(BF16)
    hk = lb.shape[1]
    ws = (hw[:, :hk], hw[:, hk:2 * hk], hw[:, 2 * hk:3 * hk], hw[:, 3 * hk:3 * hk + d], hw[:, 3 * hk + d:])
    s2, q, kf, kb, lff, lfb, iv, g = _hproj(s1, f0, modv, rows_mod, norm_mix_g[1:2], lbtab, ws, ll, lc)
    hg1 = _hgrn_scan(q, kf, kb, lff, lfb, iv, g, h_head_g[0], ll, lc)
    s3, h2b, lgb = _out1(hg1, s2, modv, rows_mod, norm_ffn_g[1:2], h_w_out[0].astype(BF16), r_hi, r_lo, tm)
    clsb, h2b = _route(lgb, router_bias, h2b)
    tok1 = jnp.arange(bsz * ll, dtype=jnp.int32)
    dst1 = (tok1 // ll) * (ll + MOE_TILE) + tok1 % ll
    spare1 = tuple(b * (ll + MOE_TILE) + ll for b in range(1, bsz))
    f1 = _moe(h2b, clsb, dst1, ll, spare1, bsz * (ll + MOE_TILE), 1, wg_all, wu_all, wd_all)

    return _final(s3, f1, modv, rows_mod, final_g.reshape(1, d), tm)
```

```python
import functools

import jax
import jax.numpy as jnp
from jax import lax
from jax.experimental import pallas as pl
from jax.experimental.pallas import tpu as pltpu

F32 = jnp.float32
BF16 = jnp.bfloat16

EPS = 1e-6
LOG2_E = 1.4426950408889634
N_ADA = 6
GRID_W = 64
M_HEADS = 4
H_HEADS = 8
N_EXPERTS = 16
N_GROUPS = 4
E_PER_GROUP = N_EXPERTS // N_GROUPS
N_PAIRS = 6
N_CLASSES = N_GROUPS * N_PAIRS
PAIR_LO = (0, 0, 0, 1, 1, 2)
PAIR_HI = (1, 2, 3, 2, 3, 3)

LANES = 128
COL_TILE = 8
SCAN_CHUNK = 128
MOE_TILE = 256
VMEM_LIMIT = 56 * 1024 * 1024


def _cparams(sem):
    return pltpu.CompilerParams(dimension_semantics=sem, vmem_limit_bytes=VMEM_LIMIT)


def _dot(a, b):
    return jnp.dot(a, b, preferred_element_type=F32)


def _dot_nt(a, b):
    return lax.dot_general(a, b, (((1,), (1,)), ((), ())), preferred_element_type=F32)


def _dot_tn(a, b):
    return lax.dot_general(a, b, (((0,), (0,)), ((), ())), preferred_element_type=F32)


def _split_bf16(a):
    hi = a.astype(BF16)
    lo = (a - hi.astype(F32)).astype(BF16)
    return hi, lo


def _sigmoid(x):
    return 1.0 / (1.0 + jnp.exp(-x))


def _silu(x):
    return x * _sigmoid(x)


def _log1p_exp_neg_abs(x):
    return jnp.log(1.0 + jnp.exp(-jnp.abs(x)))


def _log_sigmoid(x):
    return jnp.minimum(x, 0.0) - _log1p_exp_neg_abs(x)


def _rms(x, g):
    return x * lax.rsqrt(jnp.mean(x * x, axis=-1, keepdims=True) + EPS) * g


def _ada_kernel(c_ref, w_ref, b_ref, o_ref):
    a = _silu(c_ref[...])
    a_hi, a_lo = _split_bf16(a)
    w_hi, w_lo = _split_bf16(w_ref[0])
    acc = _dot(a_hi, w_hi) + _dot(a_lo, w_hi) + _dot(a_hi, w_lo)
    o_ref[0] = acc + b_ref[0]


def _ada(cc, ada_w, ada_b):
    depth, d, n = ada_w.shape
    tn = 1024
    rows = cc.shape[0]
    return pl.pallas_call(
        _ada_kernel,
        grid=(depth, n // tn),
        in_specs=[
            pl.BlockSpec((rows, d), lambda l, j: (0, 0)),
            pl.BlockSpec((1, d, tn), lambda l, j: (l, 0, j)),
            pl.BlockSpec((1, 1, tn), lambda l, j: (l, 0, j)),
        ],
        out_specs=pl.BlockSpec((1, rows, tn), lambda l, j: (l, 0, j)),
        out_shape=jax.ShapeDtypeStruct((depth, rows, n), F32),
        compiler_params=_cparams(("arbitrary", "arbitrary")),
        name="ada_mod",
    )(cc, ada_w, ada_b.reshape(depth, 1, n))


def _mproj_kernel(n_lat_tiles, xl_ref, xc_ref, sc_ref, sh_ref, g_ref, wqk_ref, wv_ref, wo_ref, wg_ref,
                  qk_ref, v_ref, o_ref, gt_ref):
    j = pl.program_id(1)
    x = jnp.where(j < n_lat_tiles, xl_ref[0], xc_ref[0])
    h = (_rms(x, g_ref[...]) * (1.0 + sc_ref[0]) + sh_ref[0]).astype(BF16)
    qk_ref[0] = _dot(h, wqk_ref[...]).astype(BF16)
    v_ref[0] = _dot(h, wv_ref[...]).astype(BF16)
    o_ref[0] = _dot(h, wo_ref[...]).astype(BF16)
    gt_ref[0] = _dot(h, wg_ref[...])[:, :4 * M_HEADS]


def _mod_spec(d, layer, k, n_lat_tiles, rows):
    def imap(b, j):
        row = jnp.where(j < n_lat_tiles, b, rows - 1)
        return ((layer * rows + row) * N_ADA + k, 0, 0)
    return pl.BlockSpec((1, 1, d), imap)


def _mproj(x, ctx, modv, rows, norm_g, w_qk, w_v, w_o, w_g, tm):
    bsz, ll, d = x.shape
    lc = ctx.shape[1]
    nl, nc = ll // tm, lc // tm
    ltot = ll + lc
    wspec = lambda n: pl.BlockSpec((d, n), lambda b, j: (0, 0))
    tok = lambda n: pl.BlockSpec((1, tm, n), lambda b, j: (b, j, 0))
    return pl.pallas_call(
        functools.partial(_mproj_kernel, nl),
        grid=(bsz, nl + nc),
        in_specs=[
            pl.BlockSpec((1, tm, d), lambda b, j: (b, jnp.minimum(j, nl - 1), 0)),
            pl.BlockSpec((1, tm, d), lambda b, j: (b, jnp.maximum(j - nl, 0), 0)),
            _mod_spec(d, 0, 1, nl, rows), _mod_spec(d, 0, 0, nl, rows),
            pl.BlockSpec((1, d), lambda b, j: (0, 0)),
            wspec(w_qk.shape[1]), wspec(w_v.shape[1]), wspec(w_o.shape[1]), wspec(w_g.shape[1]),
        ],
        out_specs=[tok(w_qk.shape[1]), tok(w_v.shape[1]), tok(w_o.shape[1]), tok(4 * M_HEADS)],
        out_shape=[
            jax.ShapeDtypeStruct((bsz, ltot, w_qk.shape[1]), BF16),
            jax.ShapeDtypeStruct((bsz, ltot, w_v.shape[1]), BF16),
            jax.ShapeDtypeStruct((bsz, ltot, w_o.shape[1]), BF16),
            jax.ShapeDtypeStruct((bsz, ltot, 4 * M_HEADS), F32),
        ],
        compiler_params=_cparams(("arbitrary", "arbitrary")),
        name="mlstm_in_proj",
    )(x, ctx, modv, modv, norm_g, w_qk, w_v, w_o, w_g)


def _mlstm_kernel(t, ll, lc, hp, q_ref, k_ref, v_ref, o_ref, gc_ref, gr_ref, cwq_ref, cwk_ref, cbq_ref, cbk_ref,
                  gbc_ref, gbr_ref, hg_ref, out_ref, qs, ks, hb, c_s, m_s):
    ltot = ll + lc
    dk = q_ref.shape[2] // hp
    dv = v_ref.shape[2] // hp
    row = lax.broadcasted_iota(jnp.int32, (ltot, 1), 0)
    first = (row == 0) | (row == ll)
    last = (row == ll - 1) | (row == ltot - 1)

    def conv(x_ref, w_ref, b_ref, cols):
        x = x_ref[0, :, cols].astype(F32)
        w = w_ref[:, cols]
        xp = jnp.where(first, 0.0, pltpu.roll(x, 1, 0))
        xn = jnp.where(last, 0.0, pltpu.roll(x, ltot - 1, 0))
        return _silu(xp * w[0:1] + x * w[1:2] + xn * w[2:3] + b_ref[:, cols])

    for j in range(hp):
        cols = slice(j * dk, (j + 1) * dk)
        qs[:, cols] = (conv(q_ref, cwq_ref, cbq_ref, cols) * (dk ** -0.5)).astype(BF16)
        ks[:, cols] = conv(k_ref, cwk_ref, cbk_ref, cols).astype(BF16)

    c_s[...] = jnp.zeros(c_s.shape, F32)
    m_s[...] = jnp.zeros(m_s.shape, F32)

    ti = lax.broadcasted_iota(jnp.int32, (t, t), 0)
    si = lax.broadcasted_iota(jnp.int32, (t, t), 1)

    ones = jnp.ones((t, LANES), BF16)

    def chunks(starts):
        jobs = [(j, d) for j in range(hp) for d in range(2)]
        st = {}
        for j, d in jobs:
            rows = pl.ds(starts[d], t)
            gc = gc_ref[0, j, rows, :] + gbc_ref[j]
            gr = gr_ref[0, j, :, rows] + gbr_ref[j]
            ig_c, lf_c = gc[:, 2 * d:2 * d + 1], _log_sigmoid(gc[:, 2 * d + 1:2 * d + 2])
            ig_r, lf_r = gr[2 * d:2 * d + 1, :], _log_sigmoid(gr[2 * d + 1:2 * d + 2, :])
            seen = (si <= ti) if d == 0 else (si >= ti)
            seen_t = (ti <= si) if d == 0 else (ti >= si)
            b_c = jnp.sum(jnp.where(seen, lf_r, 0.0), axis=1, keepdims=True)
            b_r = jnp.sum(jnp.where(seen_t, lf_c, 0.0), axis=0, keepdims=True)
            total = jnp.sum(lf_r, axis=1, keepdims=True)
            st[j, d] = dict(rows=rows, ig_c=ig_c, ig_r=ig_r, b_c=b_c, b_r=b_r, total=total, seen=seen,
                            m_prev=m_s[2 * j + d][0:1, 0:1])
        for j, d in jobs:
            s = st[j, d]
            s["q"] = qs[s["rows"], j * dk:(j + 1) * dk]
            s["k"] = ks[s["rows"], j * dk:(j + 1) * dk]
            s["v"] = jnp.concatenate([v_ref[0, s["rows"], j * dv:(j + 1) * dv], ones], axis=1)
            s["qk"] = _dot_nt(s["q"], s["k"])
        for j, d in jobs:
            s = st[j, d]
            log_d = jnp.where(s["seen"], s["b_c"] - s["b_r"] + s["ig_r"], -jnp.inf)
            log_inter = s["b_c"] + s["m_prev"]
            m_t = jnp.maximum(log_inter, jnp.max(log_d, axis=1, keepdims=True))
            s["w_ts"] = jnp.exp(log_d - m_t) * s["qk"]
            s["inter"] = jnp.exp(log_inter - m_t)
            s["m_t"] = m_t
        for j, d in jobs:
            s = st[j, d]
            lhs = jnp.concatenate([s["w_ts"].astype(BF16), s["q"] * s["inter"].astype(BF16)], axis=1)
            rhs = jnp.concatenate([s["v"], c_s[2 * j + d].astype(BF16)], axis=0)
            s["num"] = _dot(lhs, rhs)
        for j, d in jobs:
            s = st[j, d]
            den = s["num"][:, dv:]
            scale = 1.0 / jnp.maximum(jnp.abs(den), jnp.exp(-s["m_t"]))
            h = s["num"][:, :dv] * jnp.concatenate([scale] * (dv // LANES), axis=1)
            if d == 0:
                out_ref[0, s["rows"], j * dv:(j + 1) * dv] = h.astype(BF16)
            else:
                hb[s["rows"], j * dv:(j + 1) * dv] = h.astype(BF16)
        for j, d in jobs:
            s = st[j, d]
            g_c = s["total"] - s["b_c"] + s["ig_c"]
            g_r = s["total"] - s["b_r"] + s["ig_r"]
            m_new = jnp.maximum(s["total"] + s["m_prev"], jnp.max(g_r, axis=1, keepdims=True))
            w_c = jnp.exp(g_c - m_new)
            decay = jnp.exp(s["total"] + s["m_prev"] - m_new)
            wv = s["v"] * w_c.astype(BF16)
            c_s[2 * j + d] = decay * c_s[2 * j + d] + _dot_tn(s["k"], wv)
            m_s[2 * j + d] = jnp.broadcast_to(m_new, m_s.shape[1:])

    ncc, ncl = lc // t, ll // t
    for i in range(ncc):
        chunks((ll + i * t, ll + (ncc - 1 - i) * t))

    def body(i, carry):
        chunks((pl.multiple_of(i * t, t), pl.multiple_of((ncl - 1 - i) * t, t)))
        return carry

    lax.fori_loop(0, ncl, body, 0)

    def epilogue(i, carry):
        s = pl.multiple_of(i * t, t)
        for j in range(hp):
            cols = slice(j * dv, (j + 1) * dv)
            hs = out_ref[0, pl.ds(s, t), cols].astype(F32) + hb[pl.ds(s, t), cols].astype(F32)
            y = _rms(hs, hg_ref[:, cols])
            out_ref[0, pl.ds(s, t), cols] = (y * _sigmoid(o_ref[0, pl.ds(s, t), cols].astype(F32))).astype(BF16)
        return carry

    lax.fori_loop(0, ltot // t, epilogue, 0, unroll=2)


def _mlstm_scan(qk, v, o, gates, conv_w, conv_b, gate_b, head_g, ll, lc):
    bsz, ltot, _ = qk.shape
    nh = M_HEADS
    hp = 2
    dk = qk.shape[2] // (2 * nh)
    dv = v.shape[2] // nh
    t = SCAN_CHUNK
    g4 = gates.reshape(bsz, ltot, 4, nh).transpose(0, 3, 1, 2)
    g4t = g4.transpose(0, 1, 3, 2)
    gb = gate_b.reshape(4, nh).T
    return pl.pallas_call(
        functools.partial(_mlstm_kernel, t, ll, lc, hp),
        grid=(bsz, nh // hp),
        in_specs=[
            pl.BlockSpec((1, ltot, hp * dk), lambda b, h: (b, 0, h)),
            pl.BlockSpec((1, ltot, hp * dk), lambda b, h: (b, 0, nh // hp + h)),
            pl.BlockSpec((1, ltot, hp * dv), lambda b, h: (b, 0, h)),
            pl.BlockSpec((1, ltot, hp * dv), lambda b, h: (b, 0, h)),
            pl.BlockSpec((1, hp, ltot, 4), lambda b, h: (b, h, 0, 0)),
            pl.BlockSpec((1, hp, 4, ltot), lambda b, h: (b, h, 0, 0)),
            pl.BlockSpec((3, hp * dk), lambda b, h: (0, h)),
            pl.BlockSpec((3, hp * dk), lambda b, h: (0, nh // hp + h)),
            pl.BlockSpec((1, hp * dk), lambda b, h: (0, h)),
            pl.BlockSpec((1, hp * dk), lambda b, h: (0, nh // hp + h)),
            pl.BlockSpec((hp, 1, 4), lambda b, h: (h, 0, 0)),
            pl.BlockSpec((hp, 4, 1), lambda b, h: (h, 0, 0)),
            pl.BlockSpec((1, hp * dv), lambda b, h: (0, h)),
        ],
        out_specs=pl.BlockSpec((1, ltot, hp * dv), lambda b, h: (b, 0, h)),
        out_shape=jax.ShapeDtypeStruct((bsz, ltot, nh * dv), BF16),
        scratch_shapes=[
            pltpu.VMEM((ltot, hp * dk), BF16), pltpu.VMEM((ltot, hp * dk), BF16),
            pltpu.VMEM((ltot, hp * dv), BF16),
            pltpu.VMEM((2 * hp, dk, dv + LANES), F32), pltpu.VMEM((2 * hp, 8, LANES), F32),
        ],
        compiler_params=_cparams(("arbitrary", "arbitrary")),
        name="mlstm_scan",
    )(qk, qk, v, o, g4, g4t, conv_w, conv_w, conv_b.reshape(1, -1), conv_b.reshape(1, -1),
      gb.reshape(nh, 1, 4), gb.reshape(nh, 4, 1), head_g.reshape(1, -1))


def _router_logits_t(h2, rhi_ref, rlo_ref):
    h_hi, h_lo = _split_bf16(h2)
    lg = _dot(h_hi, rhi_ref[...]) + _dot(h_lo, rhi_ref[...]) + _dot(h_hi, rlo_ref[...])
    return lg.T[:N_EXPERTS, :]


def _out0_kernel(n_lat_tiles, hg_ref, xl_ref, xc_ref, g1_ref, sc_ref, sh_ref, ng_ref, wo_ref, rhi_ref, rlo_ref,
                 s_ref, h2_ref, lg_ref):
    j = pl.program_id(1)
    half = hg_ref.shape[1] // 2
    d = s_ref.shape[2]
    parts = [slice(p * half, (p + 1) * half) for p in range(2)]
    ys = [_dot(hg_ref[0, rows, :], wo_ref[...]) for rows in parts]
    ss = []
    for rows, y in zip(parts, ys):
        s = jnp.where(j < n_lat_tiles, xl_ref[0, rows, :], xc_ref[0, rows, :]) + g1_ref[0] * y
        s_ref[0, rows, :] = s
        ss.append(s)
    h2s = []
    for rows, s in zip(parts, ss):
        h2 = _rms(s, ng_ref[...]) * (1.0 + sc_ref[0]) + sh_ref[0]
        h2_ref[rows, :d] = h2
        h2_ref[rows, d:] = jnp.zeros((half, LANES), F32)
        h2s.append(h2)
    for rows, h2 in zip(parts, h2s):
        lg_ref[:, rows] = _router_logits_t(h2, rhi_ref, rlo_ref)


def _out0(hg, x, ctx, modv, rows, norm_g, w_out, r_hi, r_lo, tm):
    bsz, ll, d = x.shape
    lc = ctx.shape[1]
    nl, nc = ll // tm, lc // tm
    nt = nl + nc
    ltot = ll + lc
    return pl.pallas_call(
        functools.partial(_out0_kernel, nl),
        grid=(bsz, nt),
        in_specs=[
            pl.BlockSpec((1, tm, hg.shape[2]), lambda b, j: (b, j, 0)),
            pl.BlockSpec((1, tm, d), lambda b, j: (b, jnp.minimum(j, nl - 1), 0)),
            pl.BlockSpec((1, tm, d), lambda b, j: (b, jnp.maximum(j - nl, 0), 0)),
            _mod_spec(d, 0, 2, nl, rows), _mod_spec(d, 0, 4, nl, rows), _mod_spec(d, 0, 3, nl, rows),
            pl.BlockSpec((1, d), lambda b, j: (0, 0)),
            pl.BlockSpec(w_out.shape, lambda b, j: (0, 0)),
            pl.BlockSpec(r_hi.shape, lambda b, j: (0, 0)),
            pl.BlockSpec(r_lo.shape, lambda b, j: (0, 0)),
        ],
        out_specs=[
            pl.BlockSpec((1, tm, d), lambda b, j: (b, j, 0)),
            pl.BlockSpec((tm, d + LANES), lambda b, j: (b * nt + j, 0)),
            pl.BlockSpec((N_EXPERTS, tm), lambda b, j: (0, b * nt + j)),
        ],
        out_shape=[
            jax.ShapeDtypeStruct((bsz, ltot, d), F32),
            jax.ShapeDtypeStruct((bsz * ltot, d + LANES), F32),
            jax.ShapeDtypeStruct((N_EXPERTS, bsz * ltot), F32),
        ],
        compiler_params=_cparams(("arbitrary", "arbitrary")),
        name="mlstm_out_proj",
    )(hg, x, ctx, modv, modv, modv, norm_g, w_out, r_hi, r_lo)


def _route_kernel(lg_ref, bias_ref, h2x_hbm, cls_ref, w_ref):
    del h2x_hbm
    s = _sigmoid(lg_ref[...])
    sel = s + bias_ref[...]
    srow = [s[e:e + 1, :] for e in range(N_EXPERTS)]
    row = [sel[e:e + 1, :] for e in range(N_EXPERTS)]
    best = jnp.zeros(row[0].shape, jnp.int32)
    best_score = None
    for g in range(N_GROUPS):
        r = row[g * E_PER_GROUP:(g + 1) * E_PER_GROUP]
        score = None
        for lo, hi in zip(PAIR_LO, PAIR_HI):
            pair = r[lo] + r[hi]
            score = pair if score is None else jnp.maximum(score, pair)
        if g == 0:
            best_score = score
        else:
            better = score > best_score
            best = jnp.where(better, g, best)
            best_score = jnp.where(better, score, best_score)
    gs = [row[i] for i in range(E_PER_GROUP)]
    gw = [srow[i] for i in range(E_PER_GROUP)]
    for g in range(1, N_GROUPS):
        hit = best == g
        gs = [jnp.where(hit, row[g * E_PER_GROUP + i], gs[i]) for i in range(E_PER_GROUP)]
        gw = [jnp.where(hit, srow[g * E_PER_GROUP + i], gw[i]) for i in range(E_PER_GROUP)]
    keep = []
    for i in range(E_PER_GROUP):
        beaten = jnp.zeros(best.shape, jnp.int32)
        for j in range(E_PER_GROUP):
            if j == i:
                continue
            wins = (gs[j] > gs[i]) | ((gs[j] == gs[i]) & (j < i))
            beaten = beaten + wins.astype(jnp.int32)
        keep.append(beaten < 2)
    pair_id = jnp.zeros(best.shape, jnp.int32)
    w_lo = jnp.zeros(best.shape, F32)
    w_hi = jnp.zeros(best.shape, F32)
    for p, (lo, hi) in enumerate(zip(PAIR_LO, PAIR_HI)):
        hit = keep[lo] & keep[hi]
        pair_id = jnp.where(hit, p, pair_id)
        w_lo = jnp.where(hit, gw[lo], w_lo)
        w_hi = jnp.where(hit, gw[hi], w_hi)
    tot = w_lo + w_hi
    cls_ref[...] = best * N_PAIRS + pair_id
    lanes_t = jnp.concatenate([w_lo / tot, w_hi / tot, jnp.zeros((LANES - 2, w_lo.shape[1]), F32)], axis=0)
    w_ref[...] = lanes_t.T


def _route(logits_t, router_bias, h2x):
    n = logits_t.shape[1]
    d = h2x.shape[1] - LANES
    tn = next(cand for cand in (2048, 1024, 512, 256, 128) if n % cand == 0)
    return pl.pallas_call(
        _route_kernel,
        grid=(n // tn,),
        in_specs=[
            pl.BlockSpec((N_EXPERTS, tn), lambda i: (0, i)),
            pl.BlockSpec((N_EXPERTS, 1), lambda i: (0, 0)),
            pl.BlockSpec(memory_space=pl.ANY),
        ],
        out_specs=[pl.BlockSpec((1, tn), lambda i: (0, i)), pl.BlockSpec((tn, LANES), lambda i: (i, d // LANES))],
        out_shape=[jax.ShapeDtypeStruct((1, n), jnp.int32), jax.ShapeDtypeStruct(h2x.shape, F32)],
        input_output_aliases={2: 1},
        compiler_params=_cparams(("arbitrary",)),
        name="moe_route",
    )(logits_t, router_bias.reshape(N_EXPERTS, 1).astype(F32), h2x)


def _moe_kernel(tm, dump_base, spare_rows, nu_ref, e0_ref, e1_ref, base_ref, nv_ref, tok_ref, dst_ref, h2_hbm,
                wg0_ref, wu0_ref, wd0_ref, wg1_ref, wu1_ref, wd1_ref, f_hbm,
                xb0, xb1, yb0, yb1, xbf, gsem, ssem, zsem, idle_sem):
    del e0_ref, e1_ref
    t = pl.program_id(0)
    n_used = nu_ref[0]
    d = xbf.shape[1]
    xbuf, ybuf = (xb0, xb1), (yb0, yb1)

    def gather_row(base, r, slot):
        tok = tok_ref[base + r]
        pltpu.make_async_copy(h2_hbm.at[pl.ds(tok, 1)], xbuf[slot].at[pl.ds(r, 1)], gsem.at[slot]).start()

    def scatter_row(base, nv, r, slot):
        row = jnp.where(r < nv, dst_ref[base + r], dump_base + r)
        pltpu.make_async_copy(ybuf[slot].at[pl.ds(r, 1)], f_hbm.at[pl.ds(row, 1)], ssem.at[slot]).start(priority=1)

    def wait_gather(slot):
        pltpu.make_async_copy(h2_hbm.at[pl.ds(0, tm)], xbuf[slot], gsem.at[slot]).wait()

    def wait_scatter(slot):
        pltpu.make_async_copy(ybuf[slot], f_hbm.at[pl.ds(0, tm)], ssem.at[slot]).wait()

    @pl.when(t == 0)
    def _():
        yb1[...] = jnp.zeros(yb1.shape, F32)
        fills = [pltpu.make_async_copy(yb1, f_hbm.at[pl.ds(row, tm)], zsem) for row in spare_rows]
        for fill in fills:
            fill.start()
        for fill in fills:
            fill.wait()
        base0 = base_ref[1]

        def first(r, carry):
            gather_row(base0, r, 0)
            return carry

        lax.fori_loop(0, tm, first, 0)

    def step(cur):
        nxt = 1 - cur
        base_next = base_ref[t + 2]
        base_prev = base_ref[t]
        nv_prev = nv_ref[t]

        @pl.when(t < n_used)
        def _():
            wait_gather(cur)
            xbf[...] = xbuf[cur][:, :d].astype(BF16)
            w = xbuf[cur][:, d:]

            for r in range(tm):
                gather_row(base_next, r, nxt)
            x = xbf[...]
            a0 = (_silu(_dot(x, wg0_ref[0, 0])) * _dot(x, wu0_ref[0, 0]) * w[:, 0:1]).astype(BF16)

            z = pl.semaphore_read(idle_sem)
            zero = lax.shift_right_logical(z.astype(jnp.uint32), jnp.uint32(32)).astype(jnp.int32)
            off = pl.multiple_of(zero * 16, 16)

            for r in range(tm):
                scatter_row(base_prev, nv_prev, r, nxt)
            x = xbf[pl.ds(off, tm), :]
            a1 = (_silu(_dot(x, wg1_ref[0, 0])) * _dot(x, wu1_ref[0, 0]) * w[:, 1:2]).astype(BF16)
            y = _dot(a0, wd0_ref[0, 0]) + _dot(a1, wd1_ref[0, 0])

            @pl.when(t >= 1)
            def _():
                wait_scatter(cur)

            ybuf[cur][...] = y

        @pl.when(t == n_used)
        def _():
            wait_gather(cur)
            wait_scatter(cur)

            def last(r, carry):
                scatter_row(base_prev, nv_prev, r, nxt)
                return carry

            lax.fori_loop(0, tm, last, 0)
            wait_scatter(nxt)

    @pl.when(t % 2 == 0)
    def _():
        step(0)

    @pl.when(t % 2 == 1)
    def _():
        step(1)


def _moe(h2x, cls, dst_of_token, dump_base, spare_rows, out_rows, layer, wg, wu, wd):
    n = h2x.shape[0]
    d = h2x.shape[1] - LANES
    tm = MOE_TILE
    n_tiles = n // tm + N_CLASSES
    cls = cls.reshape(n)
    _, tok_sorted, dst_sorted = lax.sort((cls, jnp.arange(n, dtype=jnp.int32), dst_of_token), num_keys=1)
    tail = jnp.zeros((tm,), jnp.int32)
    tok_sorted = jnp.concatenate([tok_sorted, tail])
    dst_sorted = jnp.concatenate([dst_sorted, tail])
    counts = jnp.sum(cls[:, None] == jnp.arange(N_CLASSES, dtype=jnp.int32)[None, :], axis=0).astype(jnp.int32)
    tiles_per = (counts + tm - 1) // tm
    tile_end = jnp.cumsum(tiles_per)
    tile_start = tile_end - tiles_per
    first_sorted = jnp.cumsum(counts) - counts
    n_used = tile_end[-1]
    tile_id = jnp.arange(n_tiles, dtype=jnp.int32)
    tile_cls = jnp.searchsorted(tile_end, jnp.minimum(tile_id, n_used - 1), side="right").astype(jnp.int32)
    tile_cls = jnp.minimum(tile_cls, N_CLASSES - 1)
    group, pair = tile_cls // N_PAIRS, tile_cls % N_PAIRS
    e0 = group * E_PER_GROUP + jnp.asarray(PAIR_LO, jnp.int32)[pair]
    e1 = group * E_PER_GROUP + jnp.asarray(PAIR_HI, jnp.int32)[pair]
    in_class = (tile_id - tile_start[tile_cls]) * tm
    used = tile_id < n_used
    base = jnp.where(used, first_sorted[tile_cls] + in_class, 0)
    nv = jnp.where(used, jnp.clip(counts[tile_cls] - in_class, 0, tm), 0)
    guard = jnp.zeros((1,), jnp.int32)
    base = jnp.concatenate([guard, base, guard]).astype(jnp.int32)
    nv = jnp.concatenate([guard, nv, guard]).astype(jnp.int32)

    de = wg.shape[3]
    smem = pl.BlockSpec(memory_space=pltpu.SMEM)
    up0 = pl.BlockSpec((1, 1, d, de), lambda i, nu, e0, e1, base, nv: (layer, e0[i], 0, 0))
    up1 = pl.BlockSpec((1, 1, d, de), lambda i, nu, e0, e1, base, nv: (layer, e1[i], 0, 0))
    dn0 = pl.BlockSpec((1, 1, de, d), lambda i, nu, e0, e1, base, nv: (layer, e0[i], 0, 0))
    dn1 = pl.BlockSpec((1, 1, de, d), lambda i, nu, e0, e1, base, nv: (layer, e1[i], 0, 0))
    return pl.pallas_call(
        functools.partial(_moe_kernel, tm, dump_base, spare_rows),
        grid_spec=pltpu.PrefetchScalarGridSpec(
            num_scalar_prefetch=5,
            grid=(n_tiles,),
            in_specs=[
                smem, smem,
                pl.BlockSpec(memory_space=pl.ANY),
                up0, up0, dn0, up1, up1, dn1,
            ],
            out_specs=pl.BlockSpec(memory_space=pl.ANY),
            scratch_shapes=[
                pltpu.VMEM((tm, d + LANES), F32), pltpu.VMEM((tm, d + LANES), F32),
                pltpu.VMEM((tm, d), F32), pltpu.VMEM((tm, d), F32),
                pltpu.VMEM((tm, d), BF16),
                pltpu.SemaphoreType.DMA((2,)), pltpu.SemaphoreType.DMA((2,)), pltpu.SemaphoreType.DMA(()),
                pltpu.SemaphoreType.REGULAR(()),
            ],
        ),
        out_shape=jax.ShapeDtypeStruct((out_rows, d), F32),
        compiler_params=_cparams(("arbitrary",)),
        name="moe_experts",
    )(n_used.reshape(1).astype(jnp.int32), e0, e1, base, nv, tok_sorted, dst_sorted, h2x,
      wg, wu, wd, wg, wu, wd)


def _hproj_kernel(n_lat_tiles, rows, s_hbm, sc_ref_, f_ref, g2_ref, sc_ref, sh_ref, ng_ref, lb_ref,
                  wq_ref, wzf_ref, wzb_ref, wi_ref, wg_ref,
                  s2_ref, q_ref, kf_ref, kb_ref, lff_ref, lfb_ref, i_ref, g_ref, xt, sbuf, sem):
    b, j = pl.program_id(0), pl.program_id(1)
    n_lat = pl.num_programs(0) * n_lat_tiles
    g2 = g2_ref[0]

    def column_copies(tile, slot):
        bb, jj = tile // n_lat_tiles, tile % n_lat_tiles
        return [pltpu.make_async_copy(s_hbm.at[bb, pl.ds(0, rows), jj * COL_TILE + c, :],
                                      sbuf.at[slot, pl.ds(c * rows, rows), :], sem.at[slot])
                for c in range(COL_TILE)]

    @pl.when(j < n_lat_tiles)
    def _():
        tile = b * n_lat_tiles + j
        slot = tile % 2

        @pl.when(tile == 0)
        def _():
            for cp in column_copies(tile, slot):
                cp.start()

        @pl.when(tile + 1 < n_lat)
        def _():
            for cp in column_copies(tile + 1, 1 - slot):
                cp.start()

        for cp in column_copies(tile, slot):
            cp.wait()
        s = sbuf[slot] + g2 * f_ref[0]
        s2_ref[0] = s
        xt[...] = s

    @pl.when(j >= n_lat_tiles)
    def _():
        xt[...] = sc_ref_[0] + g2 * f_ref[0]

    h = (_rms(xt[...], ng_ref[...]) * (1.0 + sc_ref[0]) + sh_ref[0]).astype(BF16)
    n_chunks = 4
    width = wq_ref.shape[1] // n_chunks

    def gates(dd, k_ref, lf_ref, cols, z):
        log_lb = lb_ref[2 * dd:2 * dd + 1, cols]
        log_1mlb = lb_ref[2 * dd + 1:2 * dd + 2, cols]
        a = log_1mlb + _log_sigmoid(z)
        lf_ref[0, :, cols] = jnp.maximum(log_lb, a) + _log1p_exp_neg_abs(log_lb - a)
        k_ref[0, :, cols] = jnp.exp(a - z).astype(BF16)

    def store_q(cols, y):
        q_ref[0, :, cols] = _silu(y).astype(BF16)

    def store_plain(out_ref, cols, y):
        out_ref[0, :, cols] = y.astype(BF16)

    gates_f = functools.partial(gates, 0, kf_ref, lff_ref)
    gates_b = functools.partial(gates, 1, kb_ref, lfb_ref)
    plain_i = functools.partial(store_plain, i_ref)
    plain_g = functools.partial(store_plain, g_ref)
    work = []
    for c in range(n_chunks):
        work += [(wzf_ref, gates_f, c), (wi_ref, plain_i, c)]
    for c in range(n_chunks):
        work += [(wzb_ref, gates_b, c), (wg_ref, plain_g, c)]
    work += [(wq_ref, store_q, c) for c in range(n_chunks)]
    pending = None
    for w_ref, tail, c in work:
        cols = slice(c * width, (c + 1) * width)
        y = _dot(h, w_ref[:, cols])
        if pending is not None:
            pending()
        pending = functools.partial(tail, cols, y)
    pending()


def _hproj(s1, f0, modv, rows_mod, norm_g, lbtab, ws, ll, lc):
    bsz, ltot, d = s1.shape
    rows = ll // GRID_W
    tm = COL_TILE * rows
    nl, nc = GRID_W // COL_TILE, lc // tm
    s_grid = s1.reshape(bsz, ltot // GRID_W, GRID_W, d)
    f_tok = f0.reshape(bsz, -1, d)
    ctx = pl.BlockSpec((1, tm, d), lambda b, j: (b, ll // tm + jnp.maximum(j - nl, 0), 0))
    wspec = pl.BlockSpec((d, d), lambda b, j: (0, 0))
    tok = pl.BlockSpec((1, tm, d), lambda b, j: (b, j, 0))
    lat = pl.BlockSpec((1, tm, d), lambda b, j: (b, jnp.minimum(j, nl - 1), 0))

    def g2_map(b, j):
        row = jnp.where(j < nl, b, rows_mod - 1)
        return ((0 * rows_mod + row) * N_ADA + 5, 0, 0)

    outs = pl.pallas_call(
        functools.partial(_hproj_kernel, nl, rows),
        grid=(bsz, nl + nc),
        in_specs=[
            pl.BlockSpec(memory_space=pl.ANY), ctx, tok,
            pl.BlockSpec((1, 1, d), g2_map),
            _mod_spec(d, 1, 1, nl, rows_mod), _mod_spec(d, 1, 0, nl, rows_mod),
            pl.BlockSpec((1, d), lambda b, j: (0, 0)),
            pl.BlockSpec((4, d), lambda b, j: (0, 0)),
            wspec, wspec, wspec, wspec, wspec,
        ],
        out_specs=[lat, tok, tok, tok, tok, tok, tok, tok],
        out_shape=[jax.ShapeDtypeStruct((bsz, ll, d), F32)]
        + [jax.ShapeDtypeStruct((bsz, ltot, d), dt) for dt in (BF16, BF16, BF16, F32, F32, BF16, BF16)],
        scratch_shapes=[pltpu.VMEM((tm, d), F32), pltpu.VMEM((2, tm, d), F32), pltpu.SemaphoreType.DMA((2,))],
        compiler_params=_cparams(("arbitrary", "arbitrary")),
        name="hgrn_in_proj",
    )(s_grid, s1, f_tok, modv, modv, modv, norm_g, lbtab, *ws)
    return outs


def _hgrn_kernel(t, ll, lc, q_ref, kf_ref, kb_ref, lff_ref, lfb_ref, i_ref, g_ref, hg_ref, out_ref,
                 of, ob, inc_s, dec_s, qd_s):
    levels = []
    m = 2
    while m <= t:
        levels.append(m)
        m *= 2
    ti = lax.broadcasted_iota(jnp.int32, (t, t), 0)
    si = lax.broadcasted_iota(jnp.int32, (t, t), 1)
    xor = ti ^ si
    level = jnp.zeros((t, t), jnp.int32)
    for m in levels:
        level = level + (xor >= m // 2).astype(jnp.int32)
    feeds = ((si <= ti).astype(BF16), (si >= ti).astype(BF16))

    row = lax.broadcasted_iota(jnp.int32, (t, 1), 0)
    feeds_twice = tuple(jnp.concatenate([f, f], axis=1) for f in feeds)

    def block_ref(b, m, d):
        half = m // 2
        pos = half - 1 if d == 0 else half
        if m >= 8:
            dk = b.shape[1]
            b3 = b.reshape(t // m, m, dk)
            return jnp.broadcast_to(b3[:, pos:pos + 1, :], b3.shape).reshape(t, dk)
        r = row % m
        out = b
        for res in range(m):
            if res != pos:
                out = jnp.where(r == res, pltpu.roll(b, (res - pos) % t, 0), out)
        return out

    def neg_abs(x):
        sign = jnp.uint32(0x80000000)
        return lax.bitcast_convert_type(lax.bitcast_convert_type(x, jnp.uint32) | sign, F32)

    o_dir = (of, ob)

    def prepare(jobs, need_out):
        work = []
        for ci, d in jobs:
            rows = pl.ds(ci * t if isinstance(ci, int) else pl.multiple_of(ci * t, t), t)
            lf = (lff_ref if d == 0 else lfb_ref)[0, rows, :] * LOG2_E
            hi = lf.astype(BF16)
            lo = (lf - hi.astype(F32)).astype(BF16)
            b = _dot(feeds_twice[d], jnp.concatenate([hi, lo], axis=0))
            work.append((ci, d, rows, b))
        for ci, d, rows, b in work:
            b_tot = b[t - 1:t, :] if d == 0 else b[0:1, :]
            k = (kf_ref if d == 0 else kb_ref)[0, rows, :]
            inc_s[d, ci] = _dot_tn(i_ref[0, rows, :], k * jnp.exp2(b_tot - b).astype(BF16))
            dec_s[d, ci] = jnp.exp2(b_tot)
            if need_out:
                qd_s[d, rows, :] = q_ref[0, rows, :] * jnp.exp2(b).astype(BF16)
        if not need_out:
            return
        acc, qf, kf = [], [], []
        for ci, d, rows, b in work:
            q, k = q_ref[0, rows, :], (kf_ref if d == 0 else kb_ref)[0, rows, :]
            acc.append(_dot_nt(q, k))
            qf.append(q.astype(F32))
            kf.append(k.astype(F32))
        for idx, m in enumerate(levels):
            for n, (ci, d, rows, b) in enumerate(work):
                e = jnp.exp2(neg_abs(b - block_ref(b, m, d)))
                p = _dot_nt((qf[n] * e).astype(BF16), (kf[n] * e).astype(BF16))
                acc[n] = jnp.where(level == idx + 1, p, acc[n])
        for n, (ci, d, rows, b) in enumerate(work):
            o_dir[d][rows, :] = _dot(acc[n].astype(BF16) * feeds[d], i_ref[0, rows, :])

    ncc, ncl = lc // t, ll // t
    prepare([(ncl + i, d) for i in range(ncc) for d in range(2)], False)

    group = next(g for g in (16, 8, 4, 2, 1) if ncl % g == 0)

    def body(i, carry):
        prepare([(group * i + j, d) for j in range(group) for d in range(2)], True)
        return carry

    lax.fori_loop(0, ncl // group, body, 0)

    for d in range(2):
        order = list(range(ncl, ncl + ncc)) + list(range(ncl))
        if d == 1:
            order = list(range(ncl + ncc - 1, ncl - 1, -1)) + list(range(ncl - 1, -1, -1))
        st = jnp.zeros(inc_s.shape[2:], F32)
        for ci in order:
            if ci < ncl:
                rows = pl.ds(ci * t, t)
                o_dir[d][rows, :] = o_dir[d][rows, :] + _dot_nt(qd_s[d, rows, :], st.astype(BF16))
            st = dec_s[d, ci] * st + inc_s[d, ci]

    def epilogue(i, carry):
        s = pl.multiple_of(i * t, t)
        y = _rms(of[pl.ds(s, t), :] + ob[pl.ds(s, t), :], hg_ref[...])
        out_ref[0, pl.ds(s, t), :] = (y * _silu(g_ref[0, pl.ds(s, t), :].astype(F32))).astype(BF16)
        return carry

    lax.fori_loop(0, ncl, epilogue, 0, unroll=4)


def _hgrn_scan(q, kf, kb, lff, lfb, iv, g, head_g, ll, lc):
    bsz, ltot, hk = q.shape
    nh = H_HEADS
    dk = hk // nh
    t = SCAN_CHUNK
    full = pl.BlockSpec((1, ltot, dk), lambda b, h: (b, 0, h))
    lat = pl.BlockSpec((1, ll, dk), lambda b, h: (b, 0, h))
    return pl.pallas_call(
        functools.partial(_hgrn_kernel, t, ll, lc),
        grid=(bsz, nh),
        in_specs=[full, full, full, full, full, full, lat, pl.BlockSpec((1, dk), lambda b, h: (0, h))],
        out_specs=lat,
        out_shape=jax.ShapeDtypeStruct((bsz, ll, hk), BF16),
        scratch_shapes=[pltpu.VMEM((ll, dk), F32), pltpu.VMEM((ll, dk), F32),
                        pltpu.VMEM((2, ltot // t, dk, dk), F32), pltpu.VMEM((2, ltot // t, 1, dk), F32),
                        pltpu.VMEM((2, ll, dk), BF16)],
        compiler_params=_cparams(("arbitrary", "arbitrary")),
        name="hgrn_scan",
    )(q, kf, kb, lff, lfb, iv, g, head_g.reshape(1, -1))


def _out1_kernel(hg_ref, s_ref, g1_ref, sc_ref, sh_ref, ng_ref, wo_ref, rhi_ref, rlo_ref,
                 s3_ref, h2_ref, lg_ref):
    half = hg_ref.shape[1] // 2
    d = s_ref.shape[2]
    parts = [slice(p * half, (p + 1) * half) for p in range(2)]
    ys = [_dot(hg_ref[0, rows, :], wo_ref[...]) for rows in parts]
    ss = []
    for rows, y in zip(parts, ys):
        s = s_ref[0, rows, :] + g1_ref[0] * y
        s3_ref[0, rows, :] = s
        ss.append(s)
    h2s = []
    for rows, s in zip(parts, ss):
        h2 = _rms(s, ng_ref[...]) * (1.0 + sc_ref[0]) + sh_ref[0]
        h2_ref[rows, :d] = h2
        h2_ref[rows, d:] = jnp.zeros((half, LANES), F32)
        h2s.append(h2)
    for rows, h2 in zip(parts, h2s):
        lg_ref[:, rows] = _router_logits_t(h2, rhi_ref, rlo_ref)


def _out1(hg, s2, modv, rows_mod, norm_g, w_out, r_hi, r_lo, tm):
    bsz, ll, hv = hg.shape
    d = s2.shape[2]
    tm = 2 * tm if ll % (2 * tm) == 0 else tm
    nl = ll // tm
    cm = pl.BlockSpec((1, tm, d), lambda b, j: (b, j, 0))
    return pl.pallas_call(
        _out1_kernel,
        grid=(bsz, nl),
        in_specs=[
            pl.BlockSpec((1, tm, hv), lambda b, j: (b, j, 0)),
            cm,
            _mod_spec(d, 1, 2, nl, rows_mod), _mod_spec(d, 1, 4, nl, rows_mod), _mod_spec(d, 1, 3, nl, rows_mod),
            pl.BlockSpec((1, d), lambda b, j: (0, 0)),
            pl.BlockSpec(w_out.shape, lambda b, j: (0, 0)),
            pl.BlockSpec(r_hi.shape, lambda b, j: (0, 0)),
            pl.BlockSpec(r_lo.shape, lambda b, j: (0, 0)),
        ],
        out_specs=[
            cm,
            pl.BlockSpec((tm, d + LANES), lambda b, j: (b * nl + j, 0)),
            pl.BlockSpec((N_EXPERTS, tm), lambda b, j: (0, b * nl + j)),
        ],
        out_shape=[
            jax.ShapeDtypeStruct(s2.shape, F32),
            jax.ShapeDtypeStruct((bsz * ll, d + LANES), F32),
            jax.ShapeDtypeStruct((N_EXPERTS, bsz * ll), F32),
        ],
        compiler_params=_cparams(("arbitrary", "arbitrary")),
        name="hgrn_out_proj",
    )(hg, s2, modv, modv, modv, norm_g, w_out, r_hi, r_lo)


def _final_kernel(rows, s_ref, f_ref, g2_ref, fg_ref, o_hbm, obuf, sem):
    b, j = pl.program_id(0), pl.program_id(1)
    nl = pl.num_programs(1)
    tile = b * nl + j
    n_tiles = pl.num_programs(0) * nl
    slot = tile % 2

    def column_copies(t, sl):
        bb, jj = t // nl, t % nl
        return [pltpu.make_async_copy(obuf.at[sl, pl.ds(c * rows, rows), :],
                                      o_hbm.at[bb, pl.ds(0, rows), jj * COL_TILE + c, :], sem.at[sl])
                for c in range(COL_TILE)]

    @pl.when(tile >= 2)
    def _():
        for cp in column_copies(tile - 2, slot):
            cp.wait()

    obuf[slot] = _rms(s_ref[0] + g2_ref[0] * f_ref[0], fg_ref[...])
    for cp in column_copies(tile, slot):
        cp.start()

    @pl.when(tile == n_tiles - 1)
    def _():
        @pl.when(tile >= 1)
        def _():
            for cp in column_copies(tile - 1, 1 - slot):
                cp.wait()
        for cp in column_copies(tile, slot):
            cp.wait()


def _final(s3, f1, modv, rows_mod, final_g, tm):
    bsz, ll, d = s3.shape
    nl = ll // tm
    rows = ll // GRID_W
    tok = pl.BlockSpec((1, tm, d), lambda b, j: (b, j, 0))
    out = pl.pallas_call(
        functools.partial(_final_kernel, rows),
        grid=(bsz, nl),
        in_specs=[tok, tok, _mod_spec(d, 1, 5, nl, rows_mod), pl.BlockSpec((1, d), lambda b, j: (0, 0))],
        out_specs=pl.BlockSpec(memory_space=pl.ANY),
        out_shape=jax.ShapeDtypeStruct((bsz, rows, GRID_W, d), F32),
        scratch_shapes=[pltpu.VMEM((2, tm, d), F32), pltpu.SemaphoreType.DMA((2,))],
        compiler_params=_cparams(("arbitrary", "arbitrary")),
        name="final_norm",
    )(s3, f1.reshape(bsz, -1, d), modv, final_g)
    return out.reshape(bsz, ll, d)


def kernel(x, c, ctx, c_ctx, ada_w, ada_b, norm_mix_g, norm_ffn_g, final_g, m_w_in, m_conv_w, m_conv_b, m_gate_b,
           m_head_g, m_w_out, h_w_in, h_lower_bounds, h_head_g, h_w_out, router_w, router_bias, e_w_gate,
           e_w_up, e_w_down):
    bsz, ll, d = x.shape
    lc = ctx.shape[1]
    ltot = ll + lc
    depth = ada_w.shape[0]
    assert depth == 2 and ll % GRID_W == 0 and lc % SCAN_CHUNK == 0 and ll % SCAN_CHUNK == 0
    rows = ll // GRID_W
    tm = COL_TILE * rows
    assert lc % tm == 0 and ltot % GRID_W == 0 and (bsz * ltot) % MOE_TILE == 0 and (bsz * ll) % MOE_TILE == 0

    rows_mod = 8 * ((bsz + 1 + 7) // 8)
    cc = jnp.zeros((rows_mod, d), F32).at[:bsz].set(c).at[rows_mod - 1].set(c_ctx)
    modv = _ada(cc, ada_w, ada_b).reshape(depth * rows_mod * N_ADA, 1, d)

    r_pad = jnp.zeros((d, LANES), F32).at[:, :N_EXPERTS].set(router_w)
    r_hi = r_pad.astype(BF16)
    r_lo = (r_pad - r_hi.astype(F32)).astype(BF16)
    wg_all, wu_all, wd_all = e_w_gate.astype(BF16), e_w_up.astype(BF16), e_w_down.astype(BF16)

    m_qk, m_v = m_conv_w.shape[2], m_head_g.shape[1]
    w_in = m_w_in[0].astype(BF16)
    w_gates = jnp.zeros((d, LANES), BF16).at[:, :4 * M_HEADS].set(w_in[:, m_qk + 2 * m_v:])
    qk, v, o, gates = _mproj(x, ctx, modv, rows_mod, norm_mix_g[0:1], w_in[:, :m_qk], w_in[:, m_qk:m_qk + m_v],
                             w_in[:, m_qk + m_v:m_qk + 2 * m_v], w_gates, tm)
    hg = _mlstm_scan(qk, v, o, gates, m_conv_w[0], m_conv_b[0], m_gate_b[0], m_head_g[0], ll, lc)
    s1, h2, lg = _out0(hg, x, ctx, modv, rows_mod, norm_ffn_g[0:1], m_w_out[0].astype(BF16), r_hi, r_lo, tm)
    cls, h2 = _route(lg, router_bias, h2)
    tok0 = jnp.arange(bsz * ltot, dtype=jnp.int32)
    pos0 = tok0 % ltot
    pos0 = jnp.where(pos0 < ll, (pos0 % GRID_W) * rows + pos0 // GRID_W, pos0)
    dst0 = (tok0 // ltot) * (ltot + MOE_TILE) + pos0
    spare0 = tuple(b * (ltot + MOE_TILE) + ltot for b in range(1, bsz))
    f0 = _moe(h2, cls, dst0, ltot, spare0, bsz * (ltot + MOE_TILE), 0, wg_all, wu_all, wd_all)

    lbs = jnp.cumsum(jax.nn.softmax(h_lower_bounds.astype(F32), axis=0), axis=0)
    lb = (lbs - lbs[0])[1].reshape(2, -1)
    lbtab = jnp.stack([jnp.log(lb[0]), jnp.log1p(-lb[0]), jnp.log(lb[1]), jnp.log1p(-lb[1])])
    hw = h_w_in[0].astype(BF16)
    hk = lb.shape[1]
    ws = (hw[:, :hk], hw[:, hk:2 * hk], hw[:, 2 * hk:3 * hk], hw[:, 3 * hk:3 * hk + d], hw[:, 3 * hk + d:])
    s2, q, kf, kb, lff, lfb, iv, g = _hproj(s1, f0, modv, rows_mod, norm_mix_g[1:2], lbtab, ws, ll, lc)
    hg1 = _hgrn_scan(q, kf, kb, lff, lfb, iv, g, h_head_g[0], ll, lc)
    s3, h2b, lgb = _out1(hg1, s2, modv, rows_mod, norm_ffn_g[1:2], h_w_out[0].astype(BF16), r_hi, r_lo, tm)
    clsb, h2b = _route(lgb, router_bias, h2b)
    tok1 = jnp.arange(bsz * ll, dtype=jnp.int32)
    dst1 = (tok1 // ll) * (ll + MOE_TILE) + tok1 % ll
    spare1 = tuple(b * (ll + MOE_TILE) + ll for b in range(1, bsz))
    f1 = _moe(h2b, clsb, dst1, ll, spare1, bsz * (ll + MOE_TILE), 1, wg_all, wu_all, wd_all)

    return _final(s3, f1, modv, rows_mod, final_g.reshape(1, d), tm)
```

```python
import functools

import jax
import jax.numpy as jnp
from jax import lax
from jax.experimental import pallas as pl
from jax.experimental.pallas import tpu as pltpu

F32 = jnp.float32
BF16 = jnp.bfloat16

EPS = 1e-6
LOG2_E = 1.4426950408889634
N_ADA = 6
GRID_W = 64
M_HEADS = 4
H_HEADS = 8
N_EXPERTS = 16
N_GROUPS = 4
E_PER_GROUP = N_EXPERTS // N_GROUPS
N_PAIRS = 6
N_CLASSES = N_GROUPS * N_PAIRS
PAIR_LO = (0, 0, 0, 1, 1, 2)
PAIR_HI = (1, 2, 3, 2, 3, 3)

LANES = 128
COL_TILE = 8
SCAN_CHUNK = 128
MOE_TILE = 256
VMEM_LIMIT = 56 * 1024 * 1024


def _cparams(sem):
    return pltpu.CompilerParams(dimension_semantics=sem, vmem_limit_bytes=VMEM_LIMIT)


def _dot(a, b):
    return jnp.dot(a, b, preferred_element_type=F32)


def _dot_nt(a, b):
    return lax.dot_general(a, b, (((1,), (1,)), ((), ())), preferred_element_type=F32)


def _dot_tn(a, b):
    return lax.dot_general(a, b, (((0,), (0,)), ((), ())), preferred_element_type=F32)


def _split_bf16(a):
    hi = a.astype(BF16)
    lo = (a - hi.astype(F32)).astype(BF16)
    return hi, lo


def _sigmoid(x):
    return 1.0 / (1.0 + jnp.exp(-x))


def _silu(x):
    return x * _sigmoid(x)


def _log1p_exp_neg_abs(x):
    return jnp.log(1.0 + jnp.exp(-jnp.abs(x)))


def _log_sigmoid(x):
    return jnp.minimum(x, 0.0) - _log1p_exp_neg_abs(x)


def _rms(x, g):
    return x * lax.rsqrt(jnp.mean(x * x, axis=-1, keepdims=True) + EPS) * g


def _ada_kernel(c_ref, w_ref, b_ref, o_ref):
    a = _silu(c_ref[...])
    a_hi, a_lo = _split_bf16(a)
    w_hi, w_lo = _split_bf16(w_ref[0])
    acc = _dot(a_hi, w_hi) + _dot(a_lo, w_hi) + _dot(a_hi, w_lo)
    o_ref[0] = acc + b_ref[0]


def _ada(cc, ada_w, ada_b):
    depth, d, n = ada_w.shape
    tn = 1024
    rows = cc.shape[0]
    return pl.pallas_call(
        _ada_kernel,
        grid=(depth, n // tn),
        in_specs=[
            pl.BlockSpec((rows, d), lambda l, j: (0, 0)),
            pl.BlockSpec((1, d, tn), lambda l, j: (l, 0, j)),
            pl.BlockSpec((1, 1, tn), lambda l, j: (l, 0, j)),
        ],
        out_specs=pl.BlockSpec((1, rows, tn), lambda l, j: (l, 0, j)),
        out_shape=jax.ShapeDtypeStruct((depth, rows, n), F32),
        compiler_params=_cparams(("arbitrary", "arbitrary")),
        name="ada_mod",
    )(cc, ada_w, ada_b.reshape(depth, 1, n))


def _mproj_kernel(n_lat_tiles, xl_ref, xc_ref, sc_ref, sh_ref, g_ref, wqk_ref, wv_ref, wo_ref, wg_ref,
                  qk_ref, v_ref, o_ref, gt_ref, gtt_ref):
    j = pl.program_id(1)
    x = jnp.where(j < n_lat_tiles, xl_ref[0], xc_ref[0])
    h = (_rms(x, g_ref[...]) * (1.0 + sc_ref[0]) + sh_ref[0]).astype(BF16)
    gt = _dot(h, wg_ref[...])
    qk_ref[0] = _dot(h, wqk_ref[...]).astype(BF16)
    gt_t = gt.T
    for hd in range(M_HEADS):
        gt_ref[0, hd] = gt[:, 4 * hd:4 * hd + 4]
        gtt_ref[0, hd] = gt_t[4 * hd:4 * hd + 4, :]
    v_ref[0] = _dot(h, wv_ref[...]).astype(BF16)
    o_ref[0] = _dot(h, wo_ref[...]).astype(BF16)


def _mod_spec(d, layer, k, n_lat_tiles, rows):
    def imap(b, j):
        row = jnp.where(j < n_lat_tiles, b, rows - 1)
        return ((layer * rows + row) * N_ADA + k, 0, 0)
    return pl.BlockSpec((1, 1, d), imap)


def _mproj(x, ctx, modv, rows, norm_g, w_qk, w_v, w_o, w_g, tm):
    bsz, ll, d = x.shape
    lc = ctx.shape[1]
    nl, nc = ll // tm, lc // tm
    ltot = ll + lc
    wspec = lambda n: pl.BlockSpec((d, n), lambda b, j: (0, 0))
    tok = lambda n: pl.BlockSpec((1, tm, n), lambda b, j: (b, j, 0))
    return pl.pallas_call(
        functools.partial(_mproj_kernel, nl),
        grid=(bsz, nl + nc),
        in_specs=[
            pl.BlockSpec((1, tm, d), lambda b, j: (b, jnp.minimum(j, nl - 1), 0)),
            pl.BlockSpec((1, tm, d), lambda b, j: (b, jnp.maximum(j - nl, 0), 0)),
            _mod_spec(d, 0, 1, nl, rows), _mod_spec(d, 0, 0, nl, rows),
            pl.BlockSpec((1, d), lambda b, j: (0, 0)),
            wspec(w_qk.shape[1]), wspec(w_v.shape[1]), wspec(w_o.shape[1]), wspec(w_g.shape[1]),
        ],
        out_specs=[tok(w_qk.shape[1]), tok(w_v.shape[1]), tok(w_o.shape[1]),
                   pl.BlockSpec((1, M_HEADS, tm, 4), lambda b, j: (b, 0, j, 0)),
                   pl.BlockSpec((1, M_HEADS, 4, tm), lambda b, j: (b, 0, 0, j))],
        out_shape=[
            jax.ShapeDtypeStruct((bsz, ltot, w_qk.shape[1]), BF16),
            jax.ShapeDtypeStruct((bsz, ltot, w_v.shape[1]), BF16),
            jax.ShapeDtypeStruct((bsz, ltot, w_o.shape[1]), BF16),
            jax.ShapeDtypeStruct((bsz, M_HEADS, ltot, 4), F32),
            jax.ShapeDtypeStruct((bsz, M_HEADS, 4, ltot), F32),
        ],
        compiler_params=_cparams(("arbitrary", "arbitrary")),
        name="mlstm_in_proj",
    )(x, ctx, modv, modv, norm_g, w_qk, w_v, w_o, w_g)


def _mlstm_kernel(t, ll, lc, hp, q_ref, k_ref, v_ref, o_ref, gc_ref, gr_ref, cwq_ref, cwk_ref, cbq_ref, cbk_ref,
                  gbc_ref, gbr_ref, hg_ref, out_ref, qs, ks, hb, c_s, m_s):
    ltot = ll + lc
    dk = q_ref.shape[2] // hp
    dv = v_ref.shape[2] // hp
    row = lax.broadcasted_iota(jnp.int32, (ltot, 1), 0)
    first = (row == 0) | (row == ll)
    last = (row == ll - 1) | (row == ltot - 1)

    def conv(x_ref, w_ref, b_ref, cols):
        x = x_ref[0, :, cols].astype(F32)
        w = w_ref[:, cols]
        xp = jnp.where(first, 0.0, pltpu.roll(x, 1, 0))
        xn = jnp.where(last, 0.0, pltpu.roll(x, ltot - 1, 0))
        return _silu(xp * w[0:1] + x * w[1:2] + xn * w[2:3] + b_ref[:, cols])

    for j in range(hp):
        cols = slice(j * dk, (j + 1) * dk)
        qs[:, cols] = (conv(q_ref, cwq_ref, cbq_ref, cols) * (dk ** -0.5)).astype(BF16)
        ks[:, cols] = conv(k_ref, cwk_ref, cbk_ref, cols).astype(BF16)

    c_s[...] = jnp.zeros(c_s.shape, F32)
    m_s[...] = jnp.zeros(m_s.shape, F32)

    ti = lax.broadcasted_iota(jnp.int32, (t, t), 0)
    si = lax.broadcasted_iota(jnp.int32, (t, t), 1)

    ones = jnp.ones((t, LANES), BF16)

    def chunks(starts):
        jobs = [(j, d) for j in range(hp) for d in range(2)]
        st = {}
        for j, d in jobs:
            rows = pl.ds(starts[d], t)
            gc = gc_ref[0, j, rows, :] + gbc_ref[j]
            gr = gr_ref[0, j, :, rows] + gbr_ref[j]
            ig_c, lf_c = gc[:, 2 * d:2 * d + 1], _log_sigmoid(gc[:, 2 * d + 1:2 * d + 2])
            ig_r, lf_r = gr[2 * d:2 * d + 1, :], _log_sigmoid(gr[2 * d + 1:2 * d + 2, :])
            seen = (si <= ti) if d == 0 else (si >= ti)
            seen_t = (ti <= si) if d == 0 else (ti >= si)
            b_c = jnp.sum(jnp.where(seen, lf_r, 0.0), axis=1, keepdims=True)
            b_r = jnp.sum(jnp.where(seen_t, lf_c, 0.0), axis=0, keepdims=True)
            total = jnp.sum(lf_r, axis=1, keepdims=True)
            st[j, d] = dict(rows=rows, ig_c=ig_c, ig_r=ig_r, b_c=b_c, b_r=b_r, total=total, seen=seen,
                            m_prev=m_s[2 * j + d][0:1, 0:1])
        for j, d in jobs:
            s = st[j, d]
            s["q"] = qs[s["rows"], j * dk:(j + 1) * dk]
            s["k"] = ks[s["rows"], j * dk:(j + 1) * dk]
            s["v"] = jnp.concatenate([v_ref[0, s["rows"], j * dv:(j + 1) * dv], ones], axis=1)
            s["qk"] = _dot_nt(s["q"], s["k"])
        for j, d in jobs:
            s = st[j, d]
            log_d = jnp.where(s["seen"], s["b_c"] - s["b_r"] + s["ig_r"], -jnp.inf)
            log_inter = s["b_c"] + s["m_prev"]
            m_t = jnp.maximum(log_inter, jnp.max(log_d, axis=1, keepdims=True))
            s["w_ts"] = jnp.exp(log_d - m_t) * s["qk"]
            s["inter"] = jnp.exp(log_inter - m_t)
            s["m_t"] = m_t
        for j, d in jobs:
            s = st[j, d]
            lhs = jnp.concatenate([s["w_ts"].astype(BF16), s["q"] * s["inter"].astype(BF16)], axis=1)
            rhs = jnp.concatenate([s["v"], c_s[2 * j + d].astype(BF16)], axis=0)
            s["num"] = _dot(lhs, rhs)
        for j, d in jobs:
            s = st[j, d]
            den = s["num"][:, dv:]
            scale = 1.0 / jnp.maximum(jnp.abs(den), jnp.exp(-s["m_t"]))
            h = s["num"][:, :dv] * jnp.concatenate([scale] * (dv // LANES), axis=1)
            if d == 0:
                out_ref[0, s["rows"], j * dv:(j + 1) * dv] = h.astype(BF16)
            else:
                hb[s["rows"], j * dv:(j + 1) * dv] = h.astype(BF16)
        for j, d in jobs:
            s = st[j, d]
            g_c = s["total"] - s["b_c"] + s["ig_c"]
            g_r = s["total"] - s["b_r"] + s["ig_r"]
            m_new = jnp.maximum(s["total"] + s["m_prev"], jnp.max(g_r, axis=1, keepdims=True))
            w_c = jnp.exp(g_c - m_new)
            decay = jnp.exp(s["total"] + s["m_prev"] - m_new)
            wv = s["v"] * w_c.astype(BF16)
            c_s[2 * j + d] = decay * c_s[2 * j + d] + _dot_tn(s["k"], wv)
            m_s[2 * j + d] = jnp.broadcast_to(m_new, m_s.shape[1:])

    ncc, ncl = lc // t, ll // t
    for i in range(ncc):
        chunks((ll + i * t, ll + (ncc - 1 - i) * t))

    def body(i, carry):
        chunks((pl.multiple_of(i * t, t), pl.multiple_of((ncl - 1 - i) * t, t)))
        return carry

    lax.fori_loop(0, ncl, body, 0)

    def epilogue(i, carry):
        s = pl.multiple_of(i * t, t)
        for j in range(hp):
            cols = slice(j * dv, (j + 1) * dv)
            hs = out_ref[0, pl.ds(s, t), cols].astype(F32) + hb[pl.ds(s, t), cols].astype(F32)
            y = _rms(hs, hg_ref[:, cols])
            out_ref[0, pl.ds(s, t), cols] = (y * _sigmoid(o_ref[0, pl.ds(s, t), cols].astype(F32))).astype(BF16)
        return carry

    lax.fori_loop(0, ltot // t, epilogue, 0, unroll=2)


def _mlstm_scan(qk, v, o, g4, g4t, conv_w, conv_b, gate_b, head_g, ll, lc):
    bsz, ltot, _ = qk.shape
    nh = M_HEADS
    hp = 2
    dk = qk.shape[2] // (2 * nh)
    dv = v.shape[2] // nh
    t = SCAN_CHUNK
    gb = gate_b.reshape(4, nh).T
    return pl.pallas_call(
        functools.partial(_mlstm_kernel, t, ll, lc, hp),
        grid=(bsz, nh // hp),
        in_specs=[
            pl.BlockSpec((1, ltot, hp * dk), lambda b, h: (b, 0, h)),
            pl.BlockSpec((1, ltot, hp * dk), lambda b, h: (b, 0, nh // hp + h)),
            pl.BlockSpec((1, ltot, hp * dv), lambda b, h: (b, 0, h)),
            pl.BlockSpec((1, ltot, hp * dv), lambda b, h: (b, 0, h)),
            pl.BlockSpec((1, hp, ltot, 4), lambda b, h: (b, h, 0, 0)),
            pl.BlockSpec((1, hp, 4, ltot), lambda b, h: (b, h, 0, 0)),
            pl.BlockSpec((3, hp * dk), lambda b, h: (0, h)),
            pl.BlockSpec((3, hp * dk), lambda b, h: (0, nh // hp + h)),
            pl.BlockSpec((1, hp * dk), lambda b, h: (0, h)),
            pl.BlockSpec((1, hp * dk), lambda b, h: (0, nh // hp + h)),
            pl.BlockSpec((hp, 1, 4), lambda b, h: (h, 0, 0)),
            pl.BlockSpec((hp, 4, 1), lambda b, h: (h, 0, 0)),
            pl.BlockSpec((1, hp * dv), lambda b, h: (0, h)),
        ],
        out_specs=pl.BlockSpec((1, ltot, hp * dv), lambda b, h: (b, 0, h)),
        out_shape=jax.ShapeDtypeStruct((bsz, ltot, nh * dv), BF16),
        scratch_shapes=[
            pltpu.VMEM((ltot, hp * dk), BF16), pltpu.VMEM((ltot, hp * dk), BF16),
            pltpu.VMEM((ltot, hp * dv), BF16),
            pltpu.VMEM((2 * hp, dk, dv + LANES), F32), pltpu.VMEM((2 * hp, 8, LANES), F32),
        ],
        compiler_params=_cparams(("arbitrary", "arbitrary")),
        name="mlstm_scan",
    )(qk, qk, v, o, g4, g4t, conv_w, conv_w, conv_b.reshape(1, -1), conv_b.reshape(1, -1),
      gb.reshape(nh, 1, 4), gb.reshape(nh, 4, 1), head_g.reshape(1, -1))


def _router_logits_t(h2, rhi_ref, rlo_ref):
    h_hi, h_lo = _split_bf16(h2)
    lg = _dot(h_hi, rhi_ref[...]) + _dot(h_lo, rhi_ref[...]) + _dot(h_hi, rlo_ref[...])
    return lg.T[:N_EXPERTS, :]


def _out0_kernel(n_lat_tiles, hg_ref, xl_ref, xc_ref, g1_ref, sc_ref, sh_ref, ng_ref, wo_ref, rhi_ref, rlo_ref,
                 s_ref, h2_ref, lg_ref):
    j = pl.program_id(1)
    half = hg_ref.shape[1] // 2
    d = s_ref.shape[2]
    parts = [slice(p * half, (p + 1) * half) for p in range(2)]
    ys = [_dot(hg_ref[0, rows, :], wo_ref[...]) for rows in parts]
    ss = []
    for rows, y in zip(parts, ys):
        s = jnp.where(j < n_lat_tiles, xl_ref[0, rows, :], xc_ref[0, rows, :]) + g1_ref[0] * y
        s_ref[0, rows, :] = s
        ss.append(s)
    h2s = []
    for rows, s in zip(parts, ss):
        h2 = _rms(s, ng_ref[...]) * (1.0 + sc_ref[0]) + sh_ref[0]
        h2_ref[rows, :d] = h2
        h2_ref[rows, d:] = jnp.zeros((half, LANES), F32)
        h2s.append(h2)
    for rows, h2 in zip(parts, h2s):
        lg_ref[:, rows] = _router_logits_t(h2, rhi_ref, rlo_ref)


def _out0(hg, x, ctx, modv, rows, norm_g, w_out, r_hi, r_lo, tm):
    bsz, ll, d = x.shape
    lc = ctx.shape[1]
    nl, nc = ll // tm, lc // tm
    nt = nl + nc
    ltot = ll + lc
    return pl.pallas_call(
        functools.partial(_out0_kernel, nl),
        grid=(bsz, nt),
        in_specs=[
            pl.BlockSpec((1, tm, hg.shape[2]), lambda b, j: (b, j, 0)),
            pl.BlockSpec((1, tm, d), lambda b, j: (b, jnp.minimum(j, nl - 1), 0)),
            pl.BlockSpec((1, tm, d), lambda b, j: (b, jnp.maximum(j - nl, 0), 0)),
            _mod_spec(d, 0, 2, nl, rows), _mod_spec(d, 0, 4, nl, rows), _mod_spec(d, 0, 3, nl, rows),
            pl.BlockSpec((1, d), lambda b, j: (0, 0)),
            pl.BlockSpec(w_out.shape, lambda b, j: (0, 0)),
            pl.BlockSpec(r_hi.shape, lambda b, j: (0, 0)),
            pl.BlockSpec(r_lo.shape, lambda b, j: (0, 0)),
        ],
        out_specs=[
            pl.BlockSpec((1, tm, d), lambda b, j: (b, j, 0)),
            pl.BlockSpec((tm, d + LANES), lambda b, j: (b * nt + j, 0)),
            pl.BlockSpec((N_EXPERTS, tm), lambda b, j: (0, b * nt + j)),
        ],
        out_shape=[
            jax.ShapeDtypeStruct((bsz, ltot, d), F32),
            jax.ShapeDtypeStruct((bsz * ltot, d + LANES), F32),
            jax.ShapeDtypeStruct((N_EXPERTS, bsz * ltot), F32),
        ],
        compiler_params=_cparams(("arbitrary", "arbitrary")),
        name="mlstm_out_proj",
    )(hg, x, ctx, modv, modv, modv, norm_g, w_out, r_hi, r_lo)


def _route_kernel(lg_ref, bias_ref, h2x_hbm, cls_ref, w_ref):
    del h2x_hbm
    s = _sigmoid(lg_ref[...])
    sel = s + bias_ref[...]
    srow = [s[e:e + 1, :] for e in range(N_EXPERTS)]
    row = [sel[e:e + 1, :] for e in range(N_EXPERTS)]
    best = jnp.zeros(row[0].shape, jnp.int32)
    best_score = None
    for g in range(N_GROUPS):
        r = row[g * E_PER_GROUP:(g + 1) * E_PER_GROUP]
        score = None
        for lo, hi in zip(PAIR_LO, PAIR_HI):
            pair = r[lo] + r[hi]
            score = pair if score is None else jnp.maximum(score, pair)
        if g == 0:
            best_score = score
        else:
            better = score > best_score
            best = jnp.where(better, g, best)
            best_score = jnp.where(better, score, best_score)
    gs = [row[i] for i in range(E_PER_GROUP)]
    gw = [srow[i] for i in range(E_PER_GROUP)]
    for g in range(1, N_GROUPS):
        hit = best == g
        gs = [jnp.where(hit, row[g * E_PER_GROUP + i], gs[i]) for i in range(E_PER_GROUP)]
        gw = [jnp.where(hit, srow[g * E_PER_GROUP + i], gw[i]) for i in range(E_PER_GROUP)]
    keep = []
    for i in range(E_PER_GROUP):
        beaten = jnp.zeros(best.shape, jnp.int32)
        for j in range(E_PER_GROUP):
            if j == i:
                continue
            wins = (gs[j] > gs[i]) | ((gs[j] == gs[i]) & (j < i))
            beaten = beaten + wins.astype(jnp.int32)
        keep.append(beaten < 2)
    pair_id = jnp.zeros(best.shape, jnp.int32)
    w_lo = jnp.zeros(best.shape, F32)
    w_hi = jnp.zeros(best.shape, F32)
    for p, (lo, hi) in enumerate(zip(PAIR_LO, PAIR_HI)):
        hit = keep[lo] & keep[hi]
        pair_id = jnp.where(hit, p, pair_id)
        w_lo = jnp.where(hit, gw[lo], w_lo)
        w_hi = jnp.where(hit, gw[hi], w_hi)
    tot = w_lo + w_hi
    cls_ref[...] = best * N_PAIRS + pair_id
    lanes_t = jnp.concatenate([w_lo / tot, w_hi / tot, jnp.zeros((LANES - 2, w_lo.shape[1]), F32)], axis=0)
    w_ref[...] = lanes_t.T


def _route(logits_t, router_bias, h2x):
    n = logits_t.shape[1]
    d = h2x.shape[1] - LANES
    tn = next(cand for cand in (2048, 1024, 512, 256, 128) if n % cand == 0)
    return pl.pallas_call(
        _route_kernel,
        grid=(n // tn,),
        in_specs=[
            pl.BlockSpec((N_EXPERTS, tn), lambda i: (0, i)),
            pl.BlockSpec((N_EXPERTS, 1), lambda i: (0, 0)),
            pl.BlockSpec(memory_space=pl.ANY),
        ],
        out_specs=[pl.BlockSpec((1, tn), lambda i: (0, i)), pl.BlockSpec((tn, LANES), lambda i: (i, d // LANES))],
        out_shape=[jax.ShapeDtypeStruct((1, n), jnp.int32), jax.ShapeDtypeStruct(h2x.shape, F32)],
        input_output_aliases={2: 1},
        compiler_params=_cparams(("arbitrary",)),
        name="moe_route",
    )(logits_t, router_bias.reshape(N_EXPERTS, 1).astype(F32), h2x)


def _moe_kernel(tm, dump_base, spare_rows, nu_ref, e0_ref, e1_ref, base_ref, nv_ref, tok_ref, dst_ref, h2_hbm,
                wg0_ref, wu0_ref, wd0_ref, wg1_ref, wu1_ref, wd1_ref, f_hbm,
                xb0, xb1, yb0, yb1, xbf, gsem, ssem, zsem, idle_sem):
    del e0_ref, e1_ref
    t = pl.program_id(0)
    n_used = nu_ref[0]
    d = xbf.shape[1]
    xbuf, ybuf = (xb0, xb1), (yb0, yb1)

    def gather_row(base, r, slot):
        tok = tok_ref[base + r]
        pltpu.make_async_copy(h2_hbm.at[pl.ds(tok, 1)], xbuf[slot].at[pl.ds(r, 1)], gsem.at[slot]).start()

    def scatter_row(base, nv, r, slot):
        row = jnp.where(r < nv, dst_ref[base + r], dump_base + r)
        pltpu.make_async_copy(ybuf[slot].at[pl.ds(r, 1)], f_hbm.at[pl.ds(row, 1)], ssem.at[slot]).start(priority=1)

    def wait_gather(slot):
        pltpu.make_async_copy(h2_hbm.at[pl.ds(0, tm)], xbuf[slot], gsem.at[slot]).wait()

    def wait_scatter(slot):
        pltpu.make_async_copy(ybuf[slot], f_hbm.at[pl.ds(0, tm)], ssem.at[slot]).wait()

    @pl.when(t == 0)
    def _():
        yb1[...] = jnp.zeros(yb1.shape, F32)
        fills = [pltpu.make_async_copy(yb1, f_hbm.at[pl.ds(row, tm)], zsem) for row in spare_rows]
        for fill in fills:
            fill.start()
        for fill in fills:
            fill.wait()
        base0 = base_ref[1]

        def first(r, carry):
            gather_row(base0, r, 0)
            return carry

        lax.fori_loop(0, tm, first, 0)

    def step(cur):
        nxt = 1 - cur
        base_next = base_ref[t + 2]
        base_prev = base_ref[t]
        nv_prev = nv_ref[t]

        @pl.when(t < n_used)
        def _():
            wait_gather(cur)
            xbf[...] = xbuf[cur][:, :d].astype(BF16)
            w = xbuf[cur][:, d:]

            for r in range(tm):
                gather_row(base_next, r, nxt)
            x = xbf[...]
            a0 = (_silu(_dot(x, wg0_ref[0, 0])) * _dot(x, wu0_ref[0, 0]) * w[:, 0:1]).astype(BF16)

            z = pl.semaphore_read(idle_sem)
            zero = lax.shift_right_logical(z.astype(jnp.uint32), jnp.uint32(32)).astype(jnp.int32)
            off = pl.multiple_of(zero * 16, 16)

            for r in range(tm):
                scatter_row(base_prev, nv_prev, r, nxt)
            x = xbf[pl.ds(off, tm), :]
            a1 = (_silu(_dot(x, wg1_ref[0, 0])) * _dot(x, wu1_ref[0, 0]) * w[:, 1:2]).astype(BF16)
            y = _dot(a0, wd0_ref[0, 0]) + _dot(a1, wd1_ref[0, 0])

            @pl.when(t >= 1)
            def _():
                wait_scatter(cur)

            ybuf[cur][...] = y

        @pl.when(t == n_used)
        def _():
            wait_gather(cur)
            wait_scatter(cur)

            def last(r, carry):
                scatter_row(base_prev, nv_prev, r, nxt)
                return carry

            lax.fori_loop(0, tm, last, 0)
            wait_scatter(nxt)

    @pl.when(t % 2 == 0)
    def _():
        step(0)

    @pl.when(t % 2 == 1)
    def _():
        step(1)


def _moe(h2x, cls, dst_of_token, dump_base, spare_rows, out_rows, layer, wg, wu, wd):
    n = h2x.shape[0]
    d = h2x.shape[1] - LANES
    tm = MOE_TILE
    n_tiles = n // tm + N_CLASSES
    cls = cls.reshape(n)
    _, tok_sorted, dst_sorted = lax.sort((cls, jnp.arange(n, dtype=jnp.int32), dst_of_token), num_keys=1)
    tail = jnp.zeros((tm,), jnp.int32)
    tok_sorted = jnp.concatenate([tok_sorted, tail])
    dst_sorted = jnp.concatenate([dst_sorted, tail])
    counts = jnp.sum(cls[:, None] == jnp.arange(N_CLASSES, dtype=jnp.int32)[None, :], axis=0).astype(jnp.int32)
    tiles_per = (counts + tm - 1) // tm
    tile_end = jnp.cumsum(tiles_per)
    tile_start = tile_end - tiles_per
    first_sorted = jnp.cumsum(counts) - counts
    n_used = tile_end[-1]
    tile_id = jnp.arange(n_tiles, dtype=jnp.int32)
    tile_cls = jnp.searchsorted(tile_end, jnp.minimum(tile_id, n_used - 1), side="right").astype(jnp.int32)
    tile_cls = jnp.minimum(tile_cls, N_CLASSES - 1)
    group, pair = tile_cls // N_PAIRS, tile_cls % N_PAIRS
    e0 = group * E_PER_GROUP + jnp.asarray(PAIR_LO, jnp.int32)[pair]
    e1 = group * E_PER_GROUP + jnp.asarray(PAIR_HI, jnp.int32)[pair]
    in_class = (tile_id - tile_start[tile_cls]) * tm
    used = tile_id < n_used
    base = jnp.where(used, first_sorted[tile_cls] + in_class, 0)
    nv = jnp.where(used, jnp.clip(counts[tile_cls] - in_class, 0, tm), 0)
    guard = jnp.zeros((1,), jnp.int32)
    base = jnp.concatenate([guard, base, guard]).astype(jnp.int32)
    nv = jnp.concatenate([guard, nv, guard]).astype(jnp.int32)

    de = wg.shape[3]
    smem = pl.BlockSpec(memory_space=pltpu.SMEM)
    up0 = pl.BlockSpec((1, 1, d, de), lambda i, nu, e0, e1, base, nv: (layer, e0[i], 0, 0))
    up1 = pl.BlockSpec((1, 1, d, de), lambda i, nu, e0, e1, base, nv: (layer, e1[i], 0, 0))
    dn0 = pl.BlockSpec((1, 1, de, d), lambda i, nu, e0, e1, base, nv: (layer, e0[i], 0, 0))
    dn1 = pl.BlockSpec((1, 1, de, d), lambda i, nu, e0, e1, base, nv: (layer, e1[i], 0, 0))
    return pl.pallas_call(
        functools.partial(_moe_kernel, tm, dump_base, spare_rows),
        grid_spec=pltpu.PrefetchScalarGridSpec(
            num_scalar_prefetch=5,
            grid=(n_tiles,),
            in_specs=[
                smem, smem,
                pl.BlockSpec(memory_space=pl.ANY),
                up0, up0, dn0, up1, up1, dn1,
            ],
            out_specs=pl.BlockSpec(memory_space=pl.ANY),
            scratch_shapes=[
                pltpu.VMEM((tm, d + LANES), F32), pltpu.VMEM((tm, d + LANES), F32),
                pltpu.VMEM((tm, d), F32), pltpu.VMEM((tm, d), F32),
                pltpu.VMEM((tm, d), BF16),
                pltpu.SemaphoreType.DMA((2,)), pltpu.SemaphoreType.DMA((2,)), pltpu.SemaphoreType.DMA(()),
                pltpu.SemaphoreType.REGULAR(()),
            ],
        ),
        out_shape=jax.ShapeDtypeStruct((out_rows, d), F32),
        compiler_params=_cparams(("arbitrary",)),
        name="moe_experts",
    )(n_used.reshape(1).astype(jnp.int32), e0, e1, base, nv, tok_sorted, dst_sorted, h2x,
      wg, wu, wd, wg, wu, wd)


def _hproj_kernel(n_lat_tiles, rows, s_hbm, sc_ref_, f_ref, g2_ref, sc_ref, sh_ref, ng_ref, lb_ref,
                  wq_ref, wzf_ref, wzb_ref, wi_ref, wg_ref,
                  s2_ref, q_ref, kf_ref, kb_ref, lff_ref, lfb_ref, i_ref, g_ref, xt, sbuf, sem):
    b, j = pl.program_id(0), pl.program_id(1)
    n_lat = pl.num_programs(0) * n_lat_tiles
    g2 = g2_ref[0]

    def column_copies(tile, slot):
        bb, jj = tile // n_lat_tiles, tile % n_lat_tiles
        return [pltpu.make_async_copy(s_hbm.at[bb, pl.ds(0, rows), jj * COL_TILE + c, :],
                                      sbuf.at[slot, pl.ds(c * rows, rows), :], sem.at[slot])
                for c in range(COL_TILE)]

    @pl.when(j < n_lat_tiles)
    def _():
        tile = b * n_lat_tiles + j
        slot = tile % 2

        @pl.when(tile == 0)
        def _():
            for cp in column_copies(tile, slot):
                cp.start()

        @pl.when(tile + 1 < n_lat)
        def _():
            for cp in column_copies(tile + 1, 1 - slot):
                cp.start()

        for cp in column_copies(tile, slot):
            cp.wait()
        s = sbuf[slot] + g2 * f_ref[0]
        s2_ref[0] = s
        xt[...] = s

    @pl.when(j >= n_lat_tiles)
    def _():
        xt[...] = sc_ref_[0] + g2 * f_ref[0]

    h = (_rms(xt[...], ng_ref[...]) * (1.0 + sc_ref[0]) + sh_ref[0]).astype(BF16)
    n_chunks = 4
    width = wq_ref.shape[1] // n_chunks

    def gates(dd, k_ref, lf_ref, cols, z):
        log_lb = lb_ref[2 * dd:2 * dd + 1, cols]
        log_1mlb = lb_ref[2 * dd + 1:2 * dd + 2, cols]
        a = log_1mlb + _log_sigmoid(z)
        lf_ref[0, :, cols] = jnp.maximum(log_lb, a) + _log1p_exp_neg_abs(log_lb - a)
        k_ref[0, :, cols] = jnp.exp(a - z).astype(BF16)

    def store_q(cols, y):
        q_ref[0, :, cols] = _silu(y).astype(BF16)

    def store_plain(out_ref, cols, y):
        out_ref[0, :, cols] = y.astype(BF16)

    gates_f = functools.partial(gates, 0, kf_ref, lff_ref)
    gates_b = functools.partial(gates, 1, kb_ref, lfb_ref)
    plain_i = functools.partial(store_plain, i_ref)
    plain_g = functools.partial(store_plain, g_ref)
    work = []
    for c in range(n_chunks):
        work += [(wzf_ref, gates_f, c), (wi_ref, plain_i, c)]
    for c in range(n_chunks):
        work += [(wzb_ref, gates_b, c), (wg_ref, plain_g, c)]
    work += [(wq_ref, store_q, c) for c in range(n_chunks)]
    pending = None
    for w_ref, tail, c in work:
        cols = slice(c * width, (c + 1) * width)
        y = _dot(h, w_ref[:, cols])
        if pending is not None:
            pending()
        pending = functools.partial(tail, cols, y)
    pending()


def _hproj(s1, f0, modv, rows_mod, norm_g, lbtab, ws, ll, lc):
    bsz, ltot, d = s1.shape
    rows = ll // GRID_W
    tm = COL_TILE * rows
    nl, nc = GRID_W // COL_TILE, lc // tm
    s_grid = s1.reshape(bsz, ltot // GRID_W, GRID_W, d)
    f_tok = f0.reshape(bsz, -1, d)
    ctx = pl.BlockSpec((1, tm, d), lambda b, j: (b, ll // tm + jnp.maximum(j - nl, 0), 0))
    wspec = pl.BlockSpec((d, d), lambda b, j: (0, 0))
    tok = pl.BlockSpec((1, tm, d), lambda b, j: (b, j, 0))
    lat = pl.BlockSpec((1, tm, d), lambda b, j: (b, jnp.minimum(j, nl - 1), 0))

    def g2_map(b, j):
        row = jnp.where(j < nl, b, rows_mod - 1)
        return ((0 * rows_mod + row) * N_ADA + 5, 0, 0)

    outs = pl.pallas_call(
        functools.partial(_hproj_kernel, nl, rows),
        grid=(bsz, nl + nc),
        in_specs=[
            pl.BlockSpec(memory_space=pl.ANY), ctx, tok,
            pl.BlockSpec((1, 1, d), g2_map),
            _mod_spec(d, 1, 1, nl, rows_mod), _mod_spec(d, 1, 0, nl, rows_mod),
            pl.BlockSpec((1, d), lambda b, j: (0, 0)),
            pl.BlockSpec((4, d), lambda b, j: (0, 0)),
            wspec, wspec, wspec, wspec, wspec,
        ],
        out_specs=[lat, tok, tok, tok, tok, tok, tok, tok],
        out_shape=[jax.ShapeDtypeStruct((bsz, ll, d), F32)]
        + [jax.ShapeDtypeStruct((bsz, ltot, d), dt) for dt in (BF16, BF16, BF16, F32, F32, BF16, BF16)],
        scratch_shapes=[pltpu.VMEM((tm, d), F32), pltpu.VMEM((2, tm, d), F32), pltpu.SemaphoreType.DMA((2,))],
        compiler_params=_cparams(("arbitrary", "arbitrary")),
        name="hgrn_in_proj",
    )(s_grid, s1, f_tok, modv, modv, modv, norm_g, lbtab, *ws)
    return outs


def _hgrn_kernel(t, ll, lc, q_ref, kf_ref, kb_ref, lff_ref, lfb_ref, i_ref, g_ref, hg_ref, out_ref,
                 of, ob, inc_s, dec_s, qd_s):
    levels = []
    m = 2
    while m <= t:
        levels.append(m)
        m *= 2
    ti = lax.broadcasted_iota(jnp.int32, (t, t), 0)
    si = lax.broadcasted_iota(jnp.int32, (t, t), 1)
    xor = ti ^ si
    level = jnp.zeros((t, t), jnp.int32)
    for m in levels:
        level = level + (xor >= m // 2).astype(jnp.int32)
    feeds = ((si <= ti).astype(BF16), (si >= ti).astype(BF16))

    row = lax.broadcasted_iota(jnp.int32, (t, 1), 0)
    feeds_twice = tuple(jnp.concatenate([f, f], axis=1) for f in feeds)

    def block_ref(b, m, d):
        half = m // 2
        pos = half - 1 if d == 0 else half
        if m >= 8:
            dk = b.shape[1]
            b3 = b.reshape(t // m, m, dk)
            return jnp.broadcast_to(b3[:, pos:pos + 1, :], b3.shape).reshape(t, dk)
        r = row % m
        out = b
        for res in range(m):
            if res != pos:
                out = jnp.where(r == res, pltpu.roll(b, (res - pos) % t, 0), out)
        return out

    def neg_abs(x):
        sign = jnp.uint32(0x80000000)
        return lax.bitcast_convert_type(lax.bitcast_convert_type(x, jnp.uint32) | sign, F32)

    o_dir = (of, ob)

    def prepare(jobs, need_out):
        work = []
        for ci, d in jobs:
            rows = pl.ds(ci * t if isinstance(ci, int) else pl.multiple_of(ci * t, t), t)
            lf = (lff_ref if d == 0 else lfb_ref)[0, rows, :] * LOG2_E
            hi = lf.astype(BF16)
            lo = (lf - hi.astype(F32)).astype(BF16)
            b = _dot(feeds_twice[d], jnp.concatenate([hi, lo], axis=0))
            work.append((ci, d, rows, b))
        for ci, d, rows, b in work:
            b_tot = b[t - 1:t, :] if d == 0 else b[0:1, :]
            k = (kf_ref if d == 0 else kb_ref)[0, rows, :]
            inc_s[d, ci] = _dot_tn(i_ref[0, rows, :], k * jnp.exp2(b_tot - b).astype(BF16))
            dec_s[d, ci] = jnp.exp2(b_tot)
            if need_out:
                qd_s[d, rows, :] = q_ref[0, rows, :] * jnp.exp2(b).astype(BF16)
        if not need_out:
            return
        acc, qf, kf = [], [], []
        for ci, d, rows, b in work:
            q, k = q_ref[0, rows, :], (kf_ref if d == 0 else kb_ref)[0, rows, :]
            acc.append(_dot_nt(q, k))
            qf.append(q.astype(F32))
            kf.append(k.astype(F32))
        for idx, m in enumerate(levels):
            for n, (ci, d, rows, b) in enumerate(work):
                e = jnp.exp2(neg_abs(b - block_ref(b, m, d)))
                p = _dot_nt((qf[n] * e).astype(BF16), (kf[n] * e).astype(BF16))
                acc[n] = jnp.where(level == idx + 1, p, acc[n])
        for n, (ci, d, rows, b) in enumerate(work):
            o_dir[d][rows, :] = _dot(acc[n].astype(BF16) * feeds[d], i_ref[0, rows, :])

    ncc, ncl = lc // t, ll // t
    prepare([(ncl + i, d) for i in range(ncc) for d in range(2)], False)

    group = next(g for g in (16, 8, 4, 2, 1) if ncl % g == 0)

    def body(i, carry):
        prepare([(group * i + j, d) for j in range(group) for d in range(2)], True)
        return carry

    lax.fori_loop(0, ncl // group, body, 0)

    for d in range(2):
        order = list(range(ncl, ncl + ncc)) + list(range(ncl))
        if d == 1:
            order = list(range(ncl + ncc - 1, ncl - 1, -1)) + list(range(ncl - 1, -1, -1))
        st = jnp.zeros(inc_s.shape[2:], F32)
        for ci in order:
            if ci < ncl:
                rows = pl.ds(ci * t, t)
                o_dir[d][rows, :] = o_dir[d][rows, :] + _dot_nt(qd_s[d, rows, :], st.astype(BF16))
            st = dec_s[d, ci] * st + inc_s[d, ci]

    def epilogue(i, carry):
        s = pl.multiple_of(i * t, t)
        y = _rms(of[pl.ds(s, t), :] + ob[pl.ds(s, t), :], hg_ref[...])
        out_ref[0, pl.ds(s, t), :] = (y * _silu(g_ref[0, pl.ds(s, t), :].astype(F32))).astype(BF16)
        return carry

    lax.fori_loop(0, ncl, epilogue, 0, unroll=4)


def _hgrn_scan(q, kf, kb, lff, lfb, iv, g, head_g, ll, lc):
    bsz, ltot, hk = q.shape
    nh = H_HEADS
    dk = hk // nh
    t = SCAN_CHUNK
    full = pl.BlockSpec((1, ltot, dk), lambda b, h: (b, 0, h))
    lat = pl.BlockSpec((1, ll, dk), lambda b, h: (b, 0, h))
    return pl.pallas_call(
        functools.partial(_hgrn_kernel, t, ll, lc),
        grid=(bsz, nh),
        in_specs=[full, full, full, full, full, full, lat, pl.BlockSpec((1, dk), lambda b, h: (0, h))],
        out_specs=lat,
        out_shape=jax.ShapeDtypeStruct((bsz, ll, hk), BF16),
        scratch_shapes=[pltpu.VMEM((ll, dk), F32), pltpu.VMEM((ll, dk), F32),
                        pltpu.VMEM((2, ltot // t, dk, dk), F32), pltpu.VMEM((2, ltot // t, 1, dk), F32),
                        pltpu.VMEM((2, ll, dk), BF16)],
        compiler_params=_cparams(("arbitrary", "arbitrary")),
        name="hgrn_scan",
    )(q, kf, kb, lff, lfb, iv, g, head_g.reshape(1, -1))


def _out1_kernel(hg_ref, s_ref, g1_ref, sc_ref, sh_ref, ng_ref, wo_ref, rhi_ref, rlo_ref,
                 s3_ref, h2_ref, lg_ref):
    half = hg_ref.shape[1] // 2
    d = s_ref.shape[2]
    parts = [slice(p * half, (p + 1) * half) for p in range(2)]
    ys = [_dot(hg_ref[0, rows, :], wo_ref[...]) for rows in parts]
    ss = []
    for rows, y in zip(parts, ys):
        s = s_ref[0, rows, :] + g1_ref[0] * y
        s3_ref[0, rows, :] = s
        ss.append(s)
    h2s = []
    for rows, s in zip(parts, ss):
        h2 = _rms(s, ng_ref[...]) * (1.0 + sc_ref[0]) + sh_ref[0]
        h2_ref[rows, :d] = h2
        h2_ref[rows, d:] = jnp.zeros((half, LANES), F32)
        h2s.append(h2)
    for rows, h2 in zip(parts, h2s):
        lg_ref[:, rows] = _router_logits_t(h2, rhi_ref, rlo_ref)


def _out1(hg, s2, modv, rows_mod, norm_g, w_out, r_hi, r_lo, tm):
    bsz, ll, hv = hg.shape
    d = s2.shape[2]
    tm = 2 * tm if ll % (2 * tm) == 0 else tm
    nl = ll // tm
    cm = pl.BlockSpec((1, tm, d), lambda b, j: (b, j, 0))
    return pl.pallas_call(
        _out1_kernel,
        grid=(bsz, nl),
        in_specs=[
            pl.BlockSpec((1, tm, hv), lambda b, j: (b, j, 0)),
            cm,
            _mod_spec(d, 1, 2, nl, rows_mod), _mod_spec(d, 1, 4, nl, rows_mod), _mod_spec(d, 1, 3, nl, rows_mod),
            pl.BlockSpec((1, d), lambda b, j: (0, 0)),
            pl.BlockSpec(w_out.shape, lambda b, j: (0, 0)),
            pl.BlockSpec(r_hi.shape, lambda b, j: (0, 0)),
            pl.BlockSpec(r_lo.shape, lambda b, j: (0, 0)),
        ],
        out_specs=[
            cm,
            pl.BlockSpec((tm, d + LANES), lambda b, j: (b * nl + j, 0)),
            pl.BlockSpec((N_EXPERTS, tm), lambda b, j: (0, b * nl + j)),
        ],
        out_shape=[
            jax.ShapeDtypeStruct(s2.shape, F32),
            jax.ShapeDtypeStruct((bsz * ll, d + LANES), F32),
            jax.ShapeDtypeStruct((N_EXPERTS, bsz * ll), F32),
        ],
        compiler_params=_cparams(("arbitrary", "arbitrary")),
        name="hgrn_out_proj",
    )(hg, s2, modv, modv, modv, norm_g, w_out, r_hi, r_lo)


def _final_kernel(rows, s_ref, f_ref, g2_ref, fg_ref, o_hbm, obuf, sem):
    b, j = pl.program_id(0), pl.program_id(1)
    nl = pl.num_programs(1)
    tile = b * nl + j
    n_tiles = pl.num_programs(0) * nl
    slot = tile % 2

    def column_copies(t, sl):
        bb, jj = t // nl, t % nl
        return [pltpu.make_async_copy(obuf.at[sl, pl.ds(c * rows, rows), :],
                                      o_hbm.at[bb, pl.ds(0, rows), jj * COL_TILE + c, :], sem.at[sl])
                for c in range(COL_TILE)]

    @pl.when(tile >= 2)
    def _():
        for cp in column_copies(tile - 2, slot):
            cp.wait()

    obuf[slot] = _rms(s_ref[0] + g2_ref[0] * f_ref[0], fg_ref[...])
    for cp in column_copies(tile, slot):
        cp.start()

    @pl.when(tile == n_tiles - 1)
    def _():
        @pl.when(tile >= 1)
        def _():
            for cp in column_copies(tile - 1, 1 - slot):
                cp.wait()
        for cp in column_copies(tile, slot):
            cp.wait()


def _final(s3, f1, modv, rows_mod, final_g, tm):
    bsz, ll, d = s3.shape
    nl = ll // tm
    rows = ll // GRID_W
    tok = pl.BlockSpec((1, tm, d), lambda b, j: (b, j, 0))
    out = pl.pallas_call(
        functools.partial(_final_kernel, rows),
        grid=(bsz, nl),
        in_specs=[tok, tok, _mod_spec(d, 1, 5, nl, rows_mod), pl.BlockSpec((1, d), lambda b, j: (0, 0))],
        out_specs=pl.BlockSpec(memory_space=pl.ANY),
        out_shape=jax.ShapeDtypeStruct((bsz, rows, GRID_W, d), F32),
        scratch_shapes=[pltpu.VMEM((2, tm, d), F32), pltpu.SemaphoreType.DMA((2,))],
        compiler_params=_cparams(("arbitrary", "arbitrary")),
        name="final_norm",
    )(s3, f1.reshape(bsz, -1, d), modv, final_g)
    return out.reshape(bsz, ll, d)


def kernel(x, c, ctx, c_ctx, ada_w, ada_b, norm_mix_g, norm_ffn_g, final_g, m_w_in, m_conv_w, m_conv_b, m_gate_b,
           m_head_g, m_w_out, h_w_in, h_lower_bounds, h_head_g, h_w_out, router_w, router_bias, e_w_gate,
           e_w_up, e_w_down):
    bsz, ll, d = x.shape
    lc = ctx.shape[1]
    ltot = ll + lc
    depth = ada_w.shape[0]
    assert depth == 2 and ll % GRID_W == 0 and lc % SCAN_CHUNK == 0 and ll % SCAN_CHUNK == 0
    rows = ll // GRID_W
    tm = COL_TILE * rows
    assert lc % tm == 0 and ltot % GRID_W == 0 and (bsz * ltot) % MOE_TILE == 0 and (bsz * ll) % MOE_TILE == 0

    rows_mod = 8 * ((bsz + 1 + 7) // 8)
    cc = jnp.zeros((rows_mod, d), F32).at[:bsz].set(c).at[rows_mod - 1].set(c_ctx)
    modv = _ada(cc, ada_w, ada_b).reshape(depth * rows_mod * N_ADA, 1, d)

    r_pad = jnp.zeros((d, LANES), F32).at[:, :N_EXPERTS].set(router_w)
    r_hi = r_pad.astype(BF16)
    r_lo = (r_pad - r_hi.astype(F32)).astype(BF16)
    wg_all, wu_all, wd_all = e_w_gate.astype(BF16), e_w_up.astype(BF16), e_w_down.astype(BF16)

    m_qk, m_v = m_conv_w.shape[2], m_head_g.shape[1]
    w_in = m_w_in[0].astype(BF16)
    w_gate_cols = w_in[:, m_qk + 2 * m_v:].reshape(d, 4, M_HEADS).transpose(0, 2, 1).reshape(d, 4 * M_HEADS)
    w_gates = jnp.zeros((d, LANES), BF16).at[:, :4 * M_HEADS].set(w_gate_cols)
    qk, v, o, g4, g4t = _mproj(x, ctx, modv, rows_mod, norm_mix_g[0:1], w_in[:, :m_qk], w_in[:, m_qk:m_qk + m_v],
                             w_in[:, m_qk + m_v:m_qk + 2 * m_v], w_gates, tm)
    hg = _mlstm_scan(qk, v, o, g4, g4t, m_conv_w[0], m_conv_b[0], m_gate_b[0], m_head_g[0], ll, lc)
    s1, h2, lg = _out0(hg, x, ctx, modv, rows_mod, norm_ffn_g[0:1], m_w_out[0].astype(BF16), r_hi, r_lo, tm)
    cls, h2 = _route(lg, router_bias, h2)
    tok0 = jnp.arange(bsz * ltot, dtype=jnp.int32)
    pos0 = tok0 % ltot
    pos0 = jnp.where(pos0 < ll, (pos0 % GRID_W) * rows + pos0 // GRID_W, pos0)
    dst0 = (tok0 // ltot) * (ltot + MOE_TILE) + pos0
    spare0 = tuple(b * (ltot + MOE_TILE) + ltot for b in range(1, bsz))
    f0 = _moe(h2, cls, dst0, ltot, spare0, bsz * (ltot + MOE_TILE), 0, wg_all, wu_all, wd_all)

    lbs = jnp.cumsum(jax.nn.softmax(h_lower_bounds.astype(F32), axis=0), axis=0)
    lb = (lbs - lbs[0])[1].reshape(2, -1)
    lbtab = jnp.stack([jnp.log(lb[0]), jnp.log1p(-lb[0]), jnp.log(lb[1]), jnp.log1p(-lb[1])])
    hw = h_w_in[0].astype(BF16)
    hk = lb.shape[1]
    ws = (hw[:, :hk], hw[:, hk:2 * hk], hw[:, 2 * hk:3 * hk], hw[:, 3 * hk:3 * hk + d], hw[:, 3 * hk + d:])
    s2, q, kf, kb, lff, lfb, iv, g = _hproj(s1, f0, modv, rows_mod, norm_mix_g[1:2], lbtab, ws, ll, lc)
    hg1 = _hgrn_scan(q, kf, kb, lff, lfb, iv, g, h_head_g[0], ll, lc)
    s3, h2b, lgb = _out1(hg1, s2, modv, rows_mod, norm_ffn_g[1:2], h_w_out[0].astype(BF16), r_hi, r_lo, tm)
    clsb, h2b = _route(lgb, router_bias, h2b)
    tok1 = jnp.arange(bsz * ll, dtype=jnp.int32)
    dst1 = (tok1 // ll) * (ll + MOE_TILE) + tok1 % ll
    spare1 = tuple(b * (ll + MOE_TILE) + ll for b in range(1, bsz))
    f1 = _moe(h2b, clsb, dst1, ll, spare1, bsz * (ll + MOE_TILE), 1, wg_all, wu_all, wd_all)

    return _final(s3, f1, modv, rows_mod, final_g.reshape(1, d), tm)
```

```python
import functools

import jax
import jax.numpy as jnp
from jax import lax
from jax.experimental import pallas as pl
from jax.experimental.pallas import tpu as pltpu

F32 = jnp.float32
BF16 = jnp.bfloat16

EPS = 1e-6
LOG2_E = 1.4426950408889634
N_ADA = 6
GRID_W = 64
M_HEADS = 4
H_HEADS = 8
N_EXPERTS = 16
N_GROUPS = 4
E_PER_GROUP = N_EXPERTS // N_GROUPS
N_PAIRS = 6
N_CLASSES = N_GROUPS * N_PAIRS
PAIR_LO = (0, 0, 0, 1, 1, 2)
PAIR_HI = (1, 2, 3, 2, 3, 3)

LANES = 128
COL_TILE = 8
SCAN_CHUNK = 128
MOE_TILE = 256
VMEM_LIMIT = 56 * 1024 * 1024


def _cparams(sem):
    return pltpu.CompilerParams(dimension_semantics=sem, vmem_limit_bytes=VMEM_LIMIT)


def _dot(a, b):
    return jnp.dot(a, b, preferred_element_type=F32)


def _dot_nt(a, b):
    return lax.dot_general(a, b, (((1,), (1,)), ((), ())), preferred_element_type=F32)


def _dot_tn(a, b):
    return lax.dot_general(a, b, (((0,), (0,)), ((), ())), preferred_element_type=F32)


def _split_bf16(a):
    hi = a.astype(BF16)
    lo = (a - hi.astype(F32)).astype(BF16)
    return hi, lo


def _sigmoid(x):
    return 1.0 / (1.0 + jnp.exp(-x))


def _silu(x):
    return x * _sigmoid(x)


def _log1p_exp_neg_abs(x):
    return jnp.log(1.0 + jnp.exp(-jnp.abs(x)))


def _log_sigmoid(x):
    return jnp.minimum(x, 0.0) - _log1p_exp_neg_abs(x)


def _rms(x, g):
    return x * lax.rsqrt(jnp.mean(x * x, axis=-1, keepdims=True) + EPS) * g


def _ada_kernel(c_ref, w_ref, b_ref, o_ref):
    a = _silu(c_ref[...])
    a_hi, a_lo = _split_bf16(a)
    w_hi, w_lo = _split_bf16(w_ref[0])
    acc = _dot(a_hi, w_hi) + _dot(a_lo, w_hi) + _dot(a_hi, w_lo)
    o_ref[0] = acc + b_ref[0]


def _ada(cc, ada_w, ada_b):
    depth, d, n = ada_w.shape
    tn = 1024
    rows = cc.shape[0]
    return pl.pallas_call(
        _ada_kernel,
        grid=(depth, n // tn),
        in_specs=[
            pl.BlockSpec((rows, d), lambda l, j: (0, 0)),
            pl.BlockSpec((1, d, tn), lambda l, j: (l, 0, j)),
            pl.BlockSpec((1, 1, tn), lambda l, j: (l, 0, j)),
        ],
        out_specs=pl.BlockSpec((1, rows, tn), lambda l, j: (l, 0, j)),
        out_shape=jax.ShapeDtypeStruct((depth, rows, n), F32),
        compiler_params=_cparams(("arbitrary", "arbitrary")),
        name="ada_mod",
    )(cc, ada_w, ada_b.reshape(depth, 1, n))


def _mproj_kernel(n_lat_tiles, xl_ref, xc_ref, sc_ref, sh_ref, g_ref, wqk_ref, wv_ref, wo_ref, wg_ref,
                  qk_ref, v_ref, o_ref, gt_ref, gtt_ref):
    j = pl.program_id(1)
    x = jnp.where(j < n_lat_tiles, xl_ref[0], xc_ref[0])
    h = (_rms(x, g_ref[...]) * (1.0 + sc_ref[0]) + sh_ref[0]).astype(BF16)
    gt = _dot(h, wg_ref[...])
    qk_ref[0] = _dot(h, wqk_ref[...]).astype(BF16)
    gt_t = gt.T
    for hd in range(M_HEADS):
        gt_ref[0, hd] = gt[:, 4 * hd:4 * hd + 4]
        gtt_ref[0, hd] = gt_t[4 * hd:4 * hd + 4, :]
    v_ref[0] = _dot(h, wv_ref[...]).astype(BF16)
    o_ref[0] = _dot(h, wo_ref[...]).astype(BF16)


def _mod_spec(d, layer, k, n_lat_tiles, rows):
    def imap(b, j):
        row = jnp.where(j < n_lat_tiles, b, rows - 1)
        return ((layer * rows + row) * N_ADA + k, 0, 0)
    return pl.BlockSpec((1, 1, d), imap)


def _mproj(x, ctx, modv, rows, norm_g, w_qk, w_v, w_o, w_g, tm):
    bsz, ll, d = x.shape
    lc = ctx.shape[1]
    nl, nc = ll // tm, lc // tm
    ltot = ll + lc
    wspec = lambda n: pl.BlockSpec((d, n), lambda b, j: (0, 0))
    tok = lambda n: pl.BlockSpec((1, tm, n), lambda b, j: (b, j, 0))
    return pl.pallas_call(
        functools.partial(_mproj_kernel, nl),
        grid=(bsz, nl + nc),
        in_specs=[
            pl.BlockSpec((1, tm, d), lambda b, j: (b, jnp.minimum(j, nl - 1), 0)),
            pl.BlockSpec((1, tm, d), lambda b, j: (b, jnp.maximum(j - nl, 0), 0)),
            _mod_spec(d, 0, 1, nl, rows), _mod_spec(d, 0, 0, nl, rows),
            pl.BlockSpec((1, d), lambda b, j: (0, 0)),
            wspec(w_qk.shape[1]), wspec(w_v.shape[1]), wspec(w_o.shape[1]), wspec(w_g.shape[1]),
        ],
        out_specs=[tok(w_qk.shape[1]), tok(w_v.shape[1]), tok(w_o.shape[1]),
                   pl.BlockSpec((1, M_HEADS, tm, 4), lambda b, j: (b, 0, j, 0)),
                   pl.BlockSpec((1, M_HEADS, 4, tm), lambda b, j: (b, 0, 0, j))],
        out_shape=[
            jax.ShapeDtypeStruct((bsz, ltot, w_qk.shape[1]), BF16),
            jax.ShapeDtypeStruct((bsz, ltot, w_v.shape[1]), BF16),
            jax.ShapeDtypeStruct((bsz, ltot, w_o.shape[1]), BF16),
            jax.ShapeDtypeStruct((bsz, M_HEADS, ltot, 4), F32),
            jax.ShapeDtypeStruct((bsz, M_HEADS, 4, ltot), F32),
        ],
        compiler_params=_cparams(("arbitrary", "arbitrary")),
        name="mlstm_in_proj",
    )(x, ctx, modv, modv, norm_g, w_qk, w_v, w_o, w_g)


def _mlstm_kernel(t, ll, lc, hp, q_ref, k_ref, v_ref, o_ref, gc_ref, gr_ref, cwq_ref, cwk_ref, cbq_ref, cbk_ref,
                  gbc_ref, gbr_ref, hg_ref, out_ref, qs, ks, hb, c_s, m_s):
    ltot = ll + lc
    dk = q_ref.shape[2] // hp
    dv = v_ref.shape[2] // hp
    row = lax.broadcasted_iota(jnp.int32, (ltot, 1), 0)
    first = (row == 0) | (row == ll)
    last = (row == ll - 1) | (row == ltot - 1)

    def conv(x_ref, w_ref, b_ref, cols):
        x = x_ref[0, :, cols].astype(F32)
        w = w_ref[:, cols]
        xp = jnp.where(first, 0.0, pltpu.roll(x, 1, 0))
        xn = jnp.where(last, 0.0, pltpu.roll(x, ltot - 1, 0))
        return _silu(xp * w[0:1] + x * w[1:2] + xn * w[2:3] + b_ref[:, cols])

    for j in range(hp):
        cols = slice(j * dk, (j + 1) * dk)
        qs[:, cols] = (conv(q_ref, cwq_ref, cbq_ref, cols) * (dk ** -0.5)).astype(BF16)
        ks[:, cols] = conv(k_ref, cwk_ref, cbk_ref, cols).astype(BF16)

    c_s[...] = jnp.zeros(c_s.shape, F32)
    m_s[...] = jnp.zeros(m_s.shape, F32)

    ti = lax.broadcasted_iota(jnp.int32, (t, t), 0)
    si = lax.broadcasted_iota(jnp.int32, (t, t), 1)

    ones = jnp.ones((t, LANES), BF16)

    def chunks(starts):
        jobs = [(j, d) for j in range(hp) for d in range(2)]
        st = {}
        for j, d in jobs:
            rows = pl.ds(starts[d], t)
            gc = gc_ref[0, j, rows, :] + gbc_ref[j]
            gr = gr_ref[0, j, :, rows] + gbr_ref[j]
            ig_c, lf_c = gc[:, 2 * d:2 * d + 1], _log_sigmoid(gc[:, 2 * d + 1:2 * d + 2])
            ig_r, lf_r = gr[2 * d:2 * d + 1, :], _log_sigmoid(gr[2 * d + 1:2 * d + 2, :])
            seen = (si <= ti) if d == 0 else (si >= ti)
            seen_t = (ti <= si) if d == 0 else (ti >= si)
            b_c = jnp.sum(jnp.where(seen, lf_r, 0.0), axis=1, keepdims=True)
            b_r = jnp.sum(jnp.where(seen_t, lf_c, 0.0), axis=0, keepdims=True)
            total = jnp.sum(lf_r, axis=1, keepdims=True)
            st[j, d] = dict(rows=rows, ig_c=ig_c, ig_r=ig_r, b_c=b_c, b_r=b_r, total=total, seen=seen,
                            m_prev=m_s[2 * j + d][0:1, 0:1])
        for j, d in jobs:
            s = st[j, d]
            s["q"] = qs[s["rows"], j * dk:(j + 1) * dk]
            s["k"] = ks[s["rows"], j * dk:(j + 1) * dk]
            s["v"] = jnp.concatenate([v_ref[0, s["rows"], j * dv:(j + 1) * dv], ones], axis=1)
            s["qk"] = _dot_nt(s["q"], s["k"])
        for j, d in jobs:
            s = st[j, d]
            log_d = jnp.where(s["seen"], s["b_c"] - s["b_r"] + s["ig_r"], -jnp.inf)
            log_inter = s["b_c"] + s["m_prev"]
            m_t = jnp.maximum(log_inter, jnp.max(log_d, axis=1, keepdims=True))
            s["w_ts"] = jnp.exp(log_d - m_t) * s["qk"]
            s["inter"] = jnp.exp(log_inter - m_t)
            s["m_t"] = m_t
        for j, d in jobs:
            s = st[j, d]
            lhs = jnp.concatenate([s["w_ts"].astype(BF16), s["q"] * s["inter"].astype(BF16)], axis=1)
            rhs = jnp.concatenate([s["v"], c_s[2 * j + d].astype(BF16)], axis=0)
            s["num"] = _dot(lhs, rhs)
        for j, d in jobs:
            s = st[j, d]
            den = s["num"][:, dv:]
            scale = 1.0 / jnp.maximum(jnp.abs(den), jnp.exp(-s["m_t"]))
            h = s["num"][:, :dv] * jnp.concatenate([scale] * (dv // LANES), axis=1)
            if d == 0:
                out_ref[0, s["rows"], j * dv:(j + 1) * dv] = h.astype(BF16)
            else:
                hb[s["rows"], j * dv:(j + 1) * dv] = h.astype(BF16)
        for j, d in jobs:
            s = st[j, d]
            g_c = s["total"] - s["b_c"] + s["ig_c"]
            g_r = s["total"] - s["b_r"] + s["ig_r"]
            m_new = jnp.maximum(s["total"] + s["m_prev"], jnp.max(g_r, axis=1, keepdims=True))
            w_c = jnp.exp(g_c - m_new)
            decay = jnp.exp(s["total"] + s["m_prev"] - m_new)
            wv = s["v"] * w_c.astype(BF16)
            c_s[2 * j + d] = decay * c_s[2 * j + d] + _dot_tn(s["k"], wv)
            m_s[2 * j + d] = jnp.broadcast_to(m_new, m_s.shape[1:])

    ncc, ncl = lc // t, ll // t
    for i in range(ncc):
        chunks((ll + i * t, ll + (ncc - 1 - i) * t))

    def body(i, carry):
        chunks((pl.multiple_of(i * t, t), pl.multiple_of((ncl - 1 - i) * t, t)))
        return carry

    lax.fori_loop(0, ncl, body, 0)

    def epilogue(i, carry):
        s = pl.multiple_of(i * t, t)
        for j in range(hp):
            cols = slice(j * dv, (j + 1) * dv)
            hs = out_ref[0, pl.ds(s, t), cols].astype(F32) + hb[pl.ds(s, t), cols].astype(F32)
            y = _rms(hs, hg_ref[:, cols])
            out_ref[0, pl.ds(s, t), cols] = (y * _sigmoid(o_ref[0, pl.ds(s, t), cols].astype(F32))).astype(BF16)
        return carry

    lax.fori_loop(0, ltot // t, epilogue, 0, unroll=2)


def _mlstm_scan(qk, v, o, g4, g4t, conv_w, conv_b, gate_b, head_g, ll, lc):
    bsz, ltot, _ = qk.shape
    nh = M_HEADS
    hp = 2
    dk = qk.shape[2] // (2 * nh)
    dv = v.shape[2] // nh
    t = 2 * SCAN_CHUNK if (ll % (2 * SCAN_CHUNK) == 0 and lc % (2 * SCAN_CHUNK) == 0) else SCAN_CHUNK
    gb = gate_b.reshape(4, nh).T
    return pl.pallas_call(
        functools.partial(_mlstm_kernel, t, ll, lc, hp),
        grid=(bsz, nh // hp),
        in_specs=[
            pl.BlockSpec((1, ltot, hp * dk), lambda b, h: (b, 0, h)),
            pl.BlockSpec((1, ltot, hp * dk), lambda b, h: (b, 0, nh // hp + h)),
            pl.BlockSpec((1, ltot, hp * dv), lambda b, h: (b, 0, h)),
            pl.BlockSpec((1, ltot, hp * dv), lambda b, h: (b, 0, h)),
            pl.BlockSpec((1, hp, ltot, 4), lambda b, h: (b, h, 0, 0)),
            pl.BlockSpec((1, hp, 4, ltot), lambda b, h: (b, h, 0, 0)),
            pl.BlockSpec((3, hp * dk), lambda b, h: (0, h)),
            pl.BlockSpec((3, hp * dk), lambda b, h: (0, nh // hp + h)),
            pl.BlockSpec((1, hp * dk), lambda b, h: (0, h)),
            pl.BlockSpec((1, hp * dk), lambda b, h: (0, nh // hp + h)),
            pl.BlockSpec((hp, 1, 4), lambda b, h: (h, 0, 0)),
            pl.BlockSpec((hp, 4, 1), lambda b, h: (h, 0, 0)),
            pl.BlockSpec((1, hp * dv), lambda b, h: (0, h)),
        ],
        out_specs=pl.BlockSpec((1, ltot, hp * dv), lambda b, h: (b, 0, h)),
        out_shape=jax.ShapeDtypeStruct((bsz, ltot, nh * dv), BF16),
        scratch_shapes=[
            pltpu.VMEM((ltot, hp * dk), BF16), pltpu.VMEM((ltot, hp * dk), BF16),
            pltpu.VMEM((ltot, hp * dv), BF16),
            pltpu.VMEM((2 * hp, dk, dv + LANES), F32), pltpu.VMEM((2 * hp, 8, LANES), F32),
        ],
        compiler_params=_cparams(("arbitrary", "arbitrary")),
        name="mlstm_scan",
    )(qk, qk, v, o, g4, g4t, conv_w, conv_w, conv_b.reshape(1, -1), conv_b.reshape(1, -1),
      gb.reshape(nh, 1, 4), gb.reshape(nh, 4, 1), head_g.reshape(1, -1))


def _router_logits_t(h2, rhi_ref, rlo_ref):
    h_hi, h_lo = _split_bf16(h2)
    lg = _dot(h_hi, rhi_ref[...]) + _dot(h_lo, rhi_ref[...]) + _dot(h_hi, rlo_ref[...])
    return lg.T[:N_EXPERTS, :]


def _out0_kernel(n_lat_tiles, hg_ref, xl_ref, xc_ref, g1_ref, sc_ref, sh_ref, ng_ref, wo_ref, rhi_ref, rlo_ref,
                 s_ref, h2_ref, lg_ref):
    j = pl.program_id(1)
    half = hg_ref.shape[1] // 2
    d = s_ref.shape[2]
    parts = [slice(p * half, (p + 1) * half) for p in range(2)]
    ys = [_dot(hg_ref[0, rows, :], wo_ref[...]) for rows in parts]
    ss = []
    for rows, y in zip(parts, ys):
        s = jnp.where(j < n_lat_tiles, xl_ref[0, rows, :], xc_ref[0, rows, :]) + g1_ref[0] * y
        s_ref[0, rows, :] = s
        ss.append(s)
    h2s = []
    for rows, s in zip(parts, ss):
        h2 = _rms(s, ng_ref[...]) * (1.0 + sc_ref[0]) + sh_ref[0]
        h2_ref[rows, :d] = h2
        h2_ref[rows, d:] = jnp.zeros((half, LANES), F32)
        h2s.append(h2)
    for rows, h2 in zip(parts, h2s):
        lg_ref[:, rows] = _router_logits_t(h2, rhi_ref, rlo_ref)


def _out0(hg, x, ctx, modv, rows, norm_g, w_out, r_hi, r_lo, tm):
    bsz, ll, d = x.shape
    lc = ctx.shape[1]
    nl, nc = ll // tm, lc // tm
    nt = nl + nc
    ltot = ll + lc
    return pl.pallas_call(
        functools.partial(_out0_kernel, nl),
        grid=(bsz, nt),
        in_specs=[
            pl.BlockSpec((1, tm, hg.shape[2]), lambda b, j: (b, j, 0)),
            pl.BlockSpec((1, tm, d), lambda b, j: (b, jnp.minimum(j, nl - 1), 0)),
            pl.BlockSpec((1, tm, d), lambda b, j: (b, jnp.maximum(j - nl, 0), 0)),
            _mod_spec(d, 0, 2, nl, rows), _mod_spec(d, 0, 4, nl, rows), _mod_spec(d, 0, 3, nl, rows),
            pl.BlockSpec((1, d), lambda b, j: (0, 0)),
            pl.BlockSpec(w_out.shape, lambda b, j: (0, 0)),
            pl.BlockSpec(r_hi.shape, lambda b, j: (0, 0)),
            pl.BlockSpec(r_lo.shape, lambda b, j: (0, 0)),
        ],
        out_specs=[
            pl.BlockSpec((1, tm, d), lambda b, j: (b, j, 0)),
            pl.BlockSpec((tm, d + LANES), lambda b, j: (b * nt + j, 0)),
            pl.BlockSpec((N_EXPERTS, tm), lambda b, j: (0, b * nt + j)),
        ],
        out_shape=[
            jax.ShapeDtypeStruct((bsz, ltot, d), F32),
            jax.ShapeDtypeStruct((bsz * ltot, d + LANES), F32),
            jax.ShapeDtypeStruct((N_EXPERTS, bsz * ltot), F32),
        ],
        compiler_params=_cparams(("arbitrary", "arbitrary")),
        name="mlstm_out_proj",
    )(hg, x, ctx, modv, modv, modv, norm_g, w_out, r_hi, r_lo)


def _route_kernel(lg_ref, bias_ref, h2x_hbm, cls_ref, w_ref):
    del h2x_hbm
    s = _sigmoid(lg_ref[...])
    sel = s + bias_ref[...]
    srow = [s[e:e + 1, :] for e in range(N_EXPERTS)]
    row = [sel[e:e + 1, :] for e in range(N_EXPERTS)]
    best = jnp.zeros(row[0].shape, jnp.int32)
    best_score = None
    for g in range(N_GROUPS):
        r = row[g * E_PER_GROUP:(g + 1) * E_PER_GROUP]
        score = None
        for lo, hi in zip(PAIR_LO, PAIR_HI):
            pair = r[lo] + r[hi]
            score = pair if score is None else jnp.maximum(score, pair)
        if g == 0:
            best_score = score
        else:
            better = score > best_score
            best = jnp.where(better, g, best)
            best_score = jnp.where(better, score, best_score)
    gs = [row[i] for i in range(E_PER_GROUP)]
    gw = [srow[i] for i in range(E_PER_GROUP)]
    for g in range(1, N_GROUPS):
        hit = best == g
        gs = [jnp.where(hit, row[g * E_PER_GROUP + i], gs[i]) for i in range(E_PER_GROUP)]
        gw = [jnp.where(hit, srow[g * E_PER_GROUP + i], gw[i]) for i in range(E_PER_GROUP)]
    keep = []
    for i in range(E_PER_GROUP):
        beaten = jnp.zeros(best.shape, jnp.int32)
        for j in range(E_PER_GROUP):
            if j == i:
                continue
            wins = (gs[j] > gs[i]) | ((gs[j] == gs[i]) & (j < i))
            beaten = beaten + wins.astype(jnp.int32)
        keep.append(beaten < 2)
    pair_id = jnp.zeros(best.shape, jnp.int32)
    w_lo = jnp.zeros(best.shape, F32)
    w_hi = jnp.zeros(best.shape, F32)
    for p, (lo, hi) in enumerate(zip(PAIR_LO, PAIR_HI)):
        hit = keep[lo] & keep[hi]
        pair_id = jnp.where(hit, p, pair_id)
        w_lo = jnp.where(hit, gw[lo], w_lo)
        w_hi = jnp.where(hit, gw[hi], w_hi)
    tot = w_lo + w_hi
    cls_ref[...] = best * N_PAIRS + pair_id
    lanes_t = jnp.concatenate([w_lo / tot, w_hi / tot, jnp.zeros((LANES - 2, w_lo.shape[1]), F32)], axis=0)
    w_ref[...] = lanes_t.T


def _route(logits_t, router_bias, h2x):
    n = logits_t.shape[1]
    d = h2x.shape[1] - LANES
    tn = next(cand for cand in (2048, 1024, 512, 256, 128) if n % cand == 0)
    return pl.pallas_call(
        _route_kernel,
        grid=(n // tn,),
        in_specs=[
            pl.BlockSpec((N_EXPERTS, tn), lambda i: (0, i)),
            pl.BlockSpec((N_EXPERTS, 1), lambda i: (0, 0)),
            pl.BlockSpec(memory_space=pl.ANY),
        ],
        out_specs=[pl.BlockSpec((1, tn), lambda i: (0, i)), pl.BlockSpec((tn, LANES), lambda i: (i, d // LANES))],
        out_shape=[jax.ShapeDtypeStruct((1, n), jnp.int32), jax.ShapeDtypeStruct(h2x.shape, F32)],
        input_output_aliases={2: 1},
        compiler_params=_cparams(("arbitrary",)),
        name="moe_route",
    )(logits_t, router_bias.reshape(N_EXPERTS, 1).astype(F32), h2x)


def _moe_kernel(tm, dump_base, spare_rows, nu_ref, e0_ref, e1_ref, base_ref, nv_ref, tok_ref, dst_ref, h2_hbm,
                wg0_ref, wu0_ref, wd0_ref, wg1_ref, wu1_ref, wd1_ref, f_hbm,
                xb0, xb1, yb0, yb1, xbf, gsem, ssem, zsem, idle_sem):
    del e0_ref, e1_ref
    t = pl.program_id(0)
    n_used = nu_ref[0]
    d = xbf.shape[1]
    xbuf, ybuf = (xb0, xb1), (yb0, yb1)

    def gather_row(base, r, slot):
        tok = tok_ref[base + r]
        pltpu.make_async_copy(h2_hbm.at[pl.ds(tok, 1)], xbuf[slot].at[pl.ds(r, 1)], gsem.at[slot]).start()

    def scatter_row(base, nv, r, slot):
        row = jnp.where(r < nv, dst_ref[base + r], dump_base + r)
        pltpu.make_async_copy(ybuf[slot].at[pl.ds(r, 1)], f_hbm.at[pl.ds(row, 1)], ssem.at[slot]).start(priority=1)

    def wait_gather(slot):
        pltpu.make_async_copy(h2_hbm.at[pl.ds(0, tm)], xbuf[slot], gsem.at[slot]).wait()

    def wait_scatter(slot):
        pltpu.make_async_copy(ybuf[slot], f_hbm.at[pl.ds(0, tm)], ssem.at[slot]).wait()

    @pl.when(t == 0)
    def _():
        yb1[...] = jnp.zeros(yb1.shape, F32)
        fills = [pltpu.make_async_copy(yb1, f_hbm.at[pl.ds(row, tm)], zsem) for row in spare_rows]
        for fill in fills:
            fill.start()
        for fill in fills:
            fill.wait()
        base0 = base_ref[1]

        def first(r, carry):
            gather_row(base0, r, 0)
            return carry

        lax.fori_loop(0, tm, first, 0)

    def step(cur):
        nxt = 1 - cur
        base_next = base_ref[t + 2]
        base_prev = base_ref[t]
        nv_prev = nv_ref[t]

        @pl.when(t < n_used)
        def _():
            wait_gather(cur)
            xbf[...] = xbuf[cur][:, :d].astype(BF16)
            w = xbuf[cur][:, d:]

            for r in range(tm):
                gather_row(base_next, r, nxt)
            x = xbf[...]
            a0 = (_silu(_dot(x, wg0_ref[0, 0])) * _dot(x, wu0_ref[0, 0]) * w[:, 0:1]).astype(BF16)

            z = pl.semaphore_read(idle_sem)
            zero = lax.shift_right_logical(z.astype(jnp.uint32), jnp.uint32(32)).astype(jnp.int32)
            off = pl.multiple_of(zero * 16, 16)

            for r in range(tm):
                scatter_row(base_prev, nv_prev, r, nxt)
            x = xbf[pl.ds(off, tm), :]
            a1 = (_silu(_dot(x, wg1_ref[0, 0])) * _dot(x, wu1_ref[0, 0]) * w[:, 1:2]).astype(BF16)
            y = _dot(a0, wd0_ref[0, 0]) + _dot(a1, wd1_ref[0, 0])

            @pl.when(t >= 1)
            def _():
                wait_scatter(cur)

            ybuf[cur][...] = y

        @pl.when(t == n_used)
        def _():
            wait_gather(cur)
            wait_scatter(cur)

            def last(r, carry):
                scatter_row(base_prev, nv_prev, r, nxt)
                return carry

            lax.fori_loop(0, tm, last, 0)
            wait_scatter(nxt)

    @pl.when(t % 2 == 0)
    def _():
        step(0)

    @pl.when(t % 2 == 1)
    def _():
        step(1)


def _moe(h2x, cls, dst_of_token, dump_base, spare_rows, out_rows, layer, wg, wu, wd):
    n = h2x.shape[0]
    d = h2x.shape[1] - LANES
    tm = MOE_TILE
    n_tiles = n // tm + N_CLASSES
    cls = cls.reshape(n)
    _, tok_sorted, dst_sorted = lax.sort((cls, jnp.arange(n, dtype=jnp.int32), dst_of_token), num_keys=1)
    tail = jnp.zeros((tm,), jnp.int32)
    tok_sorted = jnp.concatenate([tok_sorted, tail])
    dst_sorted = jnp.concatenate([dst_sorted, tail])
    counts = jnp.sum(cls[:, None] == jnp.arange(N_CLASSES, dtype=jnp.int32)[None, :], axis=0).astype(jnp.int32)
    tiles_per = (counts + tm - 1) // tm
    tile_end = jnp.cumsum(tiles_per)
    tile_start = tile_end - tiles_per
    first_sorted = jnp.cumsum(counts) - counts
    n_used = tile_end[-1]
    tile_id = jnp.arange(n_tiles, dtype=jnp.int32)
    tile_cls = jnp.searchsorted(tile_end, jnp.minimum(tile_id, n_used - 1), side="right").astype(jnp.int32)
    tile_cls = jnp.minimum(tile_cls, N_CLASSES - 1)
    group, pair = tile_cls // N_PAIRS, tile_cls % N_PAIRS
    e0 = group * E_PER_GROUP + jnp.asarray(PAIR_LO, jnp.int32)[pair]
    e1 = group * E_PER_GROUP + jnp.asarray(PAIR_HI, jnp.int32)[pair]
    in_class = (tile_id - tile_start[tile_cls]) * tm
    used = tile_id < n_used
    base = jnp.where(used, first_sorted[tile_cls] + in_class, 0)
    nv = jnp.where(used, jnp.clip(counts[tile_cls] - in_class, 0, tm), 0)
    guard = jnp.zeros((1,), jnp.int32)
    base = jnp.concatenate([guard, base, guard]).astype(jnp.int32)
    nv = jnp.concatenate([guard, nv, guard]).astype(jnp.int32)

    de = wg.shape[3]
    smem = pl.BlockSpec(memory_space=pltpu.SMEM)
    up0 = pl.BlockSpec((1, 1, d, de), lambda i, nu, e0, e1, base, nv: (layer, e0[i], 0, 0))
    up1 = pl.BlockSpec((1, 1, d, de), lambda i, nu, e0, e1, base, nv: (layer, e1[i], 0, 0))
    dn0 = pl.BlockSpec((1, 1, de, d), lambda i, nu, e0, e1, base, nv: (layer, e0[i], 0, 0))
    dn1 = pl.BlockSpec((1, 1, de, d), lambda i, nu, e0, e1, base, nv: (layer, e1[i], 0, 0))
    return pl.pallas_call(
        functools.partial(_moe_kernel, tm, dump_base, spare_rows),
        grid_spec=pltpu.PrefetchScalarGridSpec(
            num_scalar_prefetch=5,
            grid=(n_tiles,),
            in_specs=[
                smem, smem,
                pl.BlockSpec(memory_space=pl.ANY),
                up0, up0, dn0, up1, up1, dn1,
            ],
            out_specs=pl.BlockSpec(memory_space=pl.ANY),
            scratch_shapes=[
                pltpu.VMEM((tm, d + LANES), F32), pltpu.VMEM((tm, d + LANES), F32),
                pltpu.VMEM((tm, d), F32), pltpu.VMEM((tm, d), F32),
                pltpu.VMEM((tm, d), BF16),
                pltpu.SemaphoreType.DMA((2,)), pltpu.SemaphoreType.DMA((2,)), pltpu.SemaphoreType.DMA(()),
                pltpu.SemaphoreType.REGULAR(()),
            ],
        ),
        out_shape=jax.ShapeDtypeStruct((out_rows, d), F32),
        compiler_params=_cparams(("arbitrary",)),
        name="moe_experts",
    )(n_used.reshape(1).astype(jnp.int32), e0, e1, base, nv, tok_sorted, dst_sorted, h2x,
      wg, wu, wd, wg, wu, wd)


def _hproj_kernel(n_lat_tiles, rows, s_hbm, sc_ref_, f_ref, g2_ref, sc_ref, sh_ref, ng_ref, lb_ref,
                  wq_ref, wzf_ref, wzb_ref, wi_ref, wg_ref,
                  s2_ref, q_ref, kf_ref, kb_ref, lff_ref, lfb_ref, i_ref, g_ref, xt, sbuf, sem):
    b, j = pl.program_id(0), pl.program_id(1)
    n_lat = pl.num_programs(0) * n_lat_tiles
    g2 = g2_ref[0]

    def column_copies(tile, slot):
        bb, jj = tile // n_lat_tiles, tile % n_lat_tiles
        return [pltpu.make_async_copy(s_hbm.at[bb, pl.ds(0, rows), jj * COL_TILE + c, :],
                                      sbuf.at[slot, pl.ds(c * rows, rows), :], sem.at[slot])
                for c in range(COL_TILE)]

    @pl.when(j < n_lat_tiles)
    def _():
        tile = b * n_lat_tiles + j
        slot = tile % 2

        @pl.when(tile == 0)
        def _():
            for cp in column_copies(tile, slot):
                cp.start()

        @pl.when(tile + 1 < n_lat)
        def _():
            for cp in column_copies(tile + 1, 1 - slot):
                cp.start()

        for cp in column_copies(tile, slot):
            cp.wait()
        s = sbuf[slot] + g2 * f_ref[0]
        s2_ref[0] = s
        xt[...] = s

    @pl.when(j >= n_lat_tiles)
    def _():
        xt[...] = sc_ref_[0] + g2 * f_ref[0]

    h = (_rms(xt[...], ng_ref[...]) * (1.0 + sc_ref[0]) + sh_ref[0]).astype(BF16)
    n_chunks = 4
    width = wq_ref.shape[1] // n_chunks

    def gates(dd, k_ref, lf_ref, cols, z):
        log_lb = lb_ref[2 * dd:2 * dd + 1, cols]
        log_1mlb = lb_ref[2 * dd + 1:2 * dd + 2, cols]
        a = log_1mlb + _log_sigmoid(z)
        lf_ref[0, :, cols] = jnp.maximum(log_lb, a) + _log1p_exp_neg_abs(log_lb - a)
        k_ref[0, :, cols] = jnp.exp(a - z).astype(BF16)

    def store_q(cols, y):
        q_ref[0, :, cols] = _silu(y).astype(BF16)

    def store_plain(out_ref, cols, y):
        out_ref[0, :, cols] = y.astype(BF16)

    gates_f = functools.partial(gates, 0, kf_ref, lff_ref)
    gates_b = functools.partial(gates, 1, kb_ref, lfb_ref)
    plain_i = functools.partial(store_plain, i_ref)
    plain_g = functools.partial(store_plain, g_ref)
    work = []
    for c in range(n_chunks):
        work += [(wzf_ref, gates_f, c), (wi_ref, plain_i, c)]
    for c in range(n_chunks):
        work += [(wzb_ref, gates_b, c), (wg_ref, plain_g, c)]
    work += [(wq_ref, store_q, c) for c in range(n_chunks)]
    pending = None
    for w_ref, tail, c in work:
        cols = slice(c * width, (c + 1) * width)
        y = _dot(h, w_ref[:, cols])
        if pending is not None:
            pending()
        pending = functools.partial(tail, cols, y)
    pending()


def _hproj(s1, f0, modv, rows_mod, norm_g, lbtab, ws, ll, lc):
    bsz, ltot, d = s1.shape
    rows = ll // GRID_W
    tm = COL_TILE * rows
    nl, nc = GRID_W // COL_TILE, lc // tm
    s_grid = s1.reshape(bsz, ltot // GRID_W, GRID_W, d)
    f_tok = f0.reshape(bsz, -1, d)
    ctx = pl.BlockSpec((1, tm, d), lambda b, j: (b, ll // tm + jnp.maximum(j - nl, 0), 0))
    wspec = pl.BlockSpec((d, d), lambda b, j: (0, 0))
    tok = pl.BlockSpec((1, tm, d), lambda b, j: (b, j, 0))
    lat = pl.BlockSpec((1, tm, d), lambda b, j: (b, jnp.minimum(j, nl - 1), 0))

    def g2_map(b, j):
        row = jnp.where(j < nl, b, rows_mod - 1)
        return ((0 * rows_mod + row) * N_ADA + 5, 0, 0)

    outs = pl.pallas_call(
        functools.partial(_hproj_kernel, nl, rows),
        grid=(bsz, nl + nc),
        in_specs=[
            pl.BlockSpec(memory_space=pl.ANY), ctx, tok,
            pl.BlockSpec((1, 1, d), g2_map),
            _mod_spec(d, 1, 1, nl, rows_mod), _mod_spec(d, 1, 0, nl, rows_mod),
            pl.BlockSpec((1, d), lambda b, j: (0, 0)),
            pl.BlockSpec((4, d), lambda b, j: (0, 0)),
            wspec, wspec, wspec, wspec, wspec,
        ],
        out_specs=[lat, tok, tok, tok, tok, tok, tok, tok],
        out_shape=[jax.ShapeDtypeStruct((bsz, ll, d), F32)]
        + [jax.ShapeDtypeStruct((bsz, ltot, d), dt) for dt in (BF16, BF16, BF16, F32, F32, BF16, BF16)],
        scratch_shapes=[pltpu.VMEM((tm, d), F32), pltpu.VMEM((2, tm, d), F32), pltpu.SemaphoreType.DMA((2,))],
        compiler_params=_cparams(("arbitrary", "arbitrary")),
        name="hgrn_in_proj",
    )(s_grid, s1, f_tok, modv, modv, modv, norm_g, lbtab, *ws)
    return outs


def _hgrn_kernel(t, ll, lc, q_ref, kf_ref, kb_ref, lff_ref, lfb_ref, i_ref, g_ref, hg_ref, out_ref,
                 of, ob, inc_s, dec_s, qd_s):
    levels = []
    m = 2
    while m <= t:
        levels.append(m)
        m *= 2
    ti = lax.broadcasted_iota(jnp.int32, (t, t), 0)
    si = lax.broadcasted_iota(jnp.int32, (t, t), 1)
    xor = ti ^ si
    level = jnp.zeros((t, t), jnp.int32)
    for m in levels:
        level = level + (xor >= m // 2).astype(jnp.int32)
    feeds = ((si <= ti).astype(BF16), (si >= ti).astype(BF16))

    row = lax.broadcasted_iota(jnp.int32, (t, 1), 0)
    feeds_twice = tuple(jnp.concatenate([f, f], axis=1) for f in feeds)

    def block_ref(b, m, d):
        half = m // 2
        pos = half - 1 if d == 0 else half
        if m >= 8:
            dk = b.shape[1]
            b3 = b.reshape(t // m, m, dk)
            return jnp.broadcast_to(b3[:, pos:pos + 1, :], b3.shape).reshape(t, dk)
        r = row % m
        out = b
        for res in range(m):
            if res != pos:
                out = jnp.where(r == res, pltpu.roll(b, (res - pos) % t, 0), out)
        return out

    def neg_abs(x):
        sign = jnp.uint32(0x80000000)
        return lax.bitcast_convert_type(lax.bitcast_convert_type(x, jnp.uint32) | sign, F32)

    o_dir = (of, ob)

    def prepare(jobs, need_out):
        work = []
        for ci, d in jobs:
            rows = pl.ds(ci * t if isinstance(ci, int) else pl.multiple_of(ci * t, t), t)
            lf = (lff_ref if d == 0 else lfb_ref)[0, rows, :] * LOG2_E
            hi = lf.astype(BF16)
            lo = (lf - hi.astype(F32)).astype(BF16)
            b = _dot(feeds_twice[d], jnp.concatenate([hi, lo], axis=0))
            work.append((ci, d, rows, b))
        for ci, d, rows, b in work:
            b_tot = b[t - 1:t, :] if d == 0 else b[0:1, :]
            k = (kf_ref if d == 0 else kb_ref)[0, rows, :]
            inc_s[d, ci] = _dot_tn(i_ref[0, rows, :], k * jnp.exp2(b_tot - b).astype(BF16))
            dec_s[d, ci] = jnp.exp2(b_tot)
            if need_out:
                qd_s[d, rows, :] = q_ref[0, rows, :] * jnp.exp2(b).astype(BF16)
        if not need_out:
            return
        acc, qf, kf = [], [], []
        for ci, d, rows, b in work:
            q, k = q_ref[0, rows, :], (kf_ref if d == 0 else kb_ref)[0, rows, :]
            acc.append(_dot_nt(q, k))
            qf.append(q.astype(F32))
            kf.append(k.astype(F32))
        for idx, m in enumerate(levels):
            for n, (ci, d, rows, b) in enumerate(work):
                e = jnp.exp2(neg_abs(b - block_ref(b, m, d)))
                p = _dot_nt((qf[n] * e).astype(BF16), (kf[n] * e).astype(BF16))
                acc[n] = jnp.where(level == idx + 1, p, acc[n])
        for n, (ci, d, rows, b) in enumerate(work):
            o_dir[d][rows, :] = _dot(acc[n].astype(BF16) * feeds[d], i_ref[0, rows, :])

    ncc, ncl = lc // t, ll // t
    prepare([(ncl + i, d) for i in range(ncc) for d in range(2)], False)

    group = next(g for g in (16, 8, 4, 2, 1) if ncl % g == 0)

    def body(i, carry):
        prepare([(group * i + j, d) for j in range(group) for d in range(2)], True)
        return carry

    lax.fori_loop(0, ncl // group, body, 0)

    for d in range(2):
        order = list(range(ncl, ncl + ncc)) + list(range(ncl))
        if d == 1:
            order = list(range(ncl + ncc - 1, ncl - 1, -1)) + list(range(ncl - 1, -1, -1))
        st = jnp.zeros(inc_s.shape[2:], F32)
        for ci in order:
            if ci < ncl:
                rows = pl.ds(ci * t, t)
                o_dir[d][rows, :] = o_dir[d][rows, :] + _dot_nt(qd_s[d, rows, :], st.astype(BF16))
            st = dec_s[d, ci] * st + inc_s[d, ci]

    def epilogue(i, carry):
        s = pl.multiple_of(i * t, t)
        y = _rms(of[pl.ds(s, t), :] + ob[pl.ds(s, t), :], hg_ref[...])
        out_ref[0, pl.ds(s, t), :] = (y * _silu(g_ref[0, pl.ds(s, t), :].astype(F32))).astype(BF16)
        return carry

    lax.fori_loop(0, ncl, epilogue, 0, unroll=4)


def _hgrn_scan(q, kf, kb, lff, lfb, iv, g, head_g, ll, lc):
    bsz, ltot, hk = q.shape
    nh = H_HEADS
    dk = hk // nh
    t = SCAN_CHUNK
    full = pl.BlockSpec((1, ltot, dk), lambda b, h: (b, 0, h))
    lat = pl.BlockSpec((1, ll, dk), lambda b, h: (b, 0, h))
    return pl.pallas_call(
        functools.partial(_hgrn_kernel, t, ll, lc),
        grid=(bsz, nh),
        in_specs=[full, full, full, full, full, full, lat, pl.BlockSpec((1, dk), lambda b, h: (0, h))],
        out_specs=lat,
        out_shape=jax.ShapeDtypeStruct((bsz, ll, hk), BF16),
        scratch_shapes=[pltpu.VMEM((ll, dk), F32), pltpu.VMEM((ll, dk), F32),
                        pltpu.VMEM((2, ltot // t, dk, dk), F32), pltpu.VMEM((2, ltot // t, 1, dk), F32),
                        pltpu.VMEM((2, ll, dk), BF16)],
        compiler_params=_cparams(("arbitrary", "arbitrary")),
        name="hgrn_scan",
    )(q, kf, kb, lff, lfb, iv, g, head_g.reshape(1, -1))


def _out1_kernel(hg_ref, s_ref, g1_ref, sc_ref, sh_ref, ng_ref, wo_ref, rhi_ref, rlo_ref,
                 s3_ref, h2_ref, lg_ref):
    half = hg_ref.shape[1] // 2
    d = s_ref.shape[2]
    parts = [slice(p * half, (p + 1) * half) for p in range(2)]
    ys = [_dot(hg_ref[0, rows, :], wo_ref[...]) for rows in parts]
    ss = []
    for rows, y in zip(parts, ys):
        s = s_ref[0, rows, :] + g1_ref[0] * y
        s3_ref[0, rows, :] = s
        ss.append(s)
    h2s = []
    for rows, s in zip(parts, ss):
        h2 = _rms(s, ng_ref[...]) * (1.0 + sc_ref[0]) + sh_ref[0]
        h2_ref[rows, :d] = h2
        h2_ref[rows, d:] = jnp.zeros((half, LANES), F32)
        h2s.append(h2)
    for rows, h2 in zip(parts, h2s):
        lg_ref[:, rows] = _router_logits_t(h2, rhi_ref, rlo_ref)


def _out1(hg, s2, modv, rows_mod, norm_g, w_out, r_hi, r_lo, tm):
    bsz, ll, hv = hg.shape
    d = s2.shape[2]
    tm = 2 * tm if ll % (2 * tm) == 0 else tm
    nl = ll // tm
    cm = pl.BlockSpec((1, tm, d), lambda b, j: (b, j, 0))
    return pl.pallas_call(
        _out1_kernel,
        grid=(bsz, nl),
        in_specs=[
            pl.BlockSpec((1, tm, hv), lambda b, j: (b, j, 0)),
            cm,
            _mod_spec(d, 1, 2, nl, rows_mod), _mod_spec(d, 1, 4, nl, rows_mod), _mod_spec(d, 1, 3, nl, rows_mod),
            pl.BlockSpec((1, d), lambda b, j: (0, 0)),
            pl.BlockSpec(w_out.shape, lambda b, j: (0, 0)),
            pl.BlockSpec(r_hi.shape, lambda b, j: (0, 0)),
            pl.BlockSpec(r_lo.shape, lambda b, j: (0, 0)),
        ],
        out_specs=[
            cm,
            pl.BlockSpec((tm, d + LANES), lambda b, j: (b * nl + j, 0)),
            pl.BlockSpec((N_EXPERTS, tm), lambda b, j: (0, b * nl + j)),
        ],
        out_shape=[
            jax.ShapeDtypeStruct(s2.shape, F32),
            jax.ShapeDtypeStruct((bsz * ll, d + LANES), F32),
            jax.ShapeDtypeStruct((N_EXPERTS, bsz * ll), F32),
        ],
        compiler_params=_cparams(("arbitrary", "arbitrary")),
        name="hgrn_out_proj",
    )(hg, s2, modv, modv, modv, norm_g, w_out, r_hi, r_lo)


def _final_kernel(rows, s_ref, f_ref, g2_ref, fg_ref, o_hbm, obuf, sem):
    b, j = pl.program_id(0), pl.program_id(1)
    nl = pl.num_programs(1)
    tile = b * nl + j
    n_tiles = pl.num_programs(0) * nl
    slot = tile % 2

    def column_copies(t, sl):
        bb, jj = t // nl, t % nl
        return [pltpu.make_async_copy(obuf.at[sl, pl.ds(c * rows, rows), :],
                                      o_hbm.at[bb, pl.ds(0, rows), jj * COL_TILE + c, :], sem.at[sl])
                for c in range(COL_TILE)]

    @pl.when(tile >= 2)
    def _():
        for cp in column_copies(tile - 2, slot):
            cp.wait()

    obuf[slot] = _rms(s_ref[0] + g2_ref[0] * f_ref[0], fg_ref[...])
    for cp in column_copies(tile, slot):
        cp.start()

    @pl.when(tile == n_tiles - 1)
    def _():
        @pl.when(tile >= 1)
        def _():
            for cp in column_copies(tile - 1, 1 - slot):
                cp.wait()
        for cp in column_copies(tile, slot):
            cp.wait()


def _final(s3, f1, modv, rows_mod, final_g, tm):
    bsz, ll, d = s3.shape
    nl = ll // tm
    rows = ll // GRID_W
    tok = pl.BlockSpec((1, tm, d), lambda b, j: (b, j, 0))
    out = pl.pallas_call(
        functools.partial(_final_kernel, rows),
        grid=(bsz, nl),
        in_specs=[tok, tok, _mod_spec(d, 1, 5, nl, rows_mod), pl.BlockSpec((1, d), lambda b, j: (0, 0))],
        out_specs=pl.BlockSpec(memory_space=pl.ANY),
        out_shape=jax.ShapeDtypeStruct((bsz, rows, GRID_W, d), F32),
        scratch_shapes=[pltpu.VMEM((2, tm, d), F32), pltpu.SemaphoreType.DMA((2,))],
        compiler_params=_cparams(("arbitrary", "arbitrary")),
        name="final_norm",
    )(s3, f1.reshape(bsz, -1, d), modv, final_g)
    return out.reshape(bsz, ll, d)


def kernel(x, c, ctx, c_ctx, ada_w, ada_b, norm_mix_g, norm_ffn_g, final_g, m_w_in, m_conv_w, m_conv_b, m_gate_b,
           m_head_g, m_w_out, h_w_in, h_lower_bounds, h_head_g, h_w_out, router_w, router_bias, e_w_gate,
           e_w_up, e_w_down):
    bsz, ll, d = x.shape
    lc = ctx.shape[1]
    ltot = ll + lc
    depth = ada_w.shape[0]
    assert depth == 2 and ll % GRID_W == 0 and lc % SCAN_CHUNK == 0 and ll % SCAN_CHUNK == 0
    rows = ll // GRID_W
    tm = COL_TILE * rows
    assert lc % tm == 0 and ltot % GRID_W == 0 and (bsz * ltot) % MOE_TILE == 0 and (bsz * ll) % MOE_TILE == 0

    rows_mod = 8 * ((bsz + 1 + 7) // 8)
    cc = jnp.zeros((rows_mod, d), F32).at[:bsz].set(c).at[rows_mod - 1].set(c_ctx)
    modv = _ada(cc, ada_w, ada_b).reshape(depth * rows_mod * N_ADA, 1, d)

    r_pad = jnp.zeros((d, LANES), F32).at[:, :N_EXPERTS].set(router_w)
    r_hi = r_pad.astype(BF16)
    r_lo = (r_pad - r_hi.astype(F32)).astype(BF16)
    wg_all, wu_all, wd_all = e_w_gate.astype(BF16), e_w_up.astype(BF16), e_w_down.astype(BF16)

    m_qk, m_v = m_conv_w.shape[2], m_head_g.shape[1]
    w_in = m_w_in[0].astype(BF16)
    w_gate_cols = w_in[:, m_qk + 2 * m_v:].reshape(d, 4, M_HEADS).transpose(0, 2, 1).reshape(d, 4 * M_HEADS)
    w_gates = jnp.zeros((d, LANES), BF16).at[:, :4 * M_HEADS].set(w_gate_cols)
    qk, v, o, g4, g4t = _mproj(x, ctx, modv, rows_mod, norm_mix_g[0:1], w_in[:, :m_qk], w_in[:, m_qk:m_qk + m_v],
                             w_in[:, m_qk + m_v:m_qk + 2 * m_v], w_gates, tm)
    hg = _mlstm_scan(qk, v, o, g4, g4t, m_conv_w[0], m_conv_b[0], m_gate_b[0], m_head_g[0], ll, lc)
    s1, h2, lg = _out0(hg, x, ctx, modv, rows_mod, norm_ffn_g[0:1], m_w_out[0].astype(BF16), r_hi, r_lo, tm)
    cls, h2 = _route(lg, router_bias, h2)
    tok0 = jnp.arange(bsz * ltot, dtype=jnp.int32)
    pos0 = tok0 % ltot
    pos0 = jnp.where(pos0 < ll, (pos0 % GRID_W) * rows + pos0 // GRID_W, pos0)
    dst0 = (tok0 // ltot) * (ltot + MOE_TILE) + pos0
    spare0 = tuple(b * (ltot + MOE_TILE) + ltot for b in range(1, bsz))
    f0 = _moe(h2, cls, dst0, ltot, spare0, bsz * (ltot + MOE_TILE), 0, wg_all, wu_all, wd_all)

    lbs = jnp.cumsum(jax.nn.softmax(h_lower_bounds.astype(F32), axis=0), axis=0)
    lb = (lbs - lbs[0])[1].reshape(2, -1)
    lbtab = jnp.stack([jnp.log(lb[0]), jnp.log1p(-lb[0]), jnp.log(lb[1]), jnp.log1p(-lb[1])])
    hw = h_w_in[0].astype(BF16)
    hk = lb.shape[1]
    ws = (hw[:, :hk], hw[:, hk:2 * hk], hw[:, 2 * hk:3 * hk], hw[:, 3 * hk:3 * hk + d], hw[:, 3 * hk + d:])
    s2, q, kf, kb, lff, lfb, iv, g = _hproj(s1, f0, modv, rows_mod, norm_mix_g[1:2], lbtab, ws, ll, lc)
    hg1 = _hgrn_scan(q, kf, kb, lff, lfb, iv, g, h_head_g[0], ll, lc)
    s3, h2b, lgb = _out1(hg1, s2, modv, rows_mod, norm_ffn_g[1:2], h_w_out[0].astype(BF16), r_hi, r_lo, tm)
    clsb, h2b = _route(lgb, router_bias, h2b)
    tok1 = jnp.arange(bsz * ll, dtype=jnp.int32)
    dst1 = (tok1 // ll) * (ll + MOE_TILE) + tok1 % ll
    spare1 = tuple(b * (ll + MOE_TILE) + ll for b in range(1, bsz))
    f1 = _moe(h2b, clsb, dst1, ll, spare1, bsz * (ll + MOE_TILE), 1, wg_all, wu_all, wd_all)

    return _final(s3, f1, modv, rows_mod, final_g.reshape(1, d), tm)
```

```python
import functools

import jax
import jax.numpy as jnp
from jax import lax
from jax.experimental import pallas as pl
from jax.experimental.pallas import tpu as pltpu

F32 = jnp.float32
BF16 = jnp.bfloat16

EPS = 1e-6
LOG2_E = 1.4426950408889634
N_ADA = 6
GRID_W = 64
M_HEADS = 4
H_HEADS = 8
N_EXPERTS = 16
N_GROUPS = 4
E_PER_GROUP = N_EXPERTS // N_GROUPS
N_PAIRS = 6
N_CLASSES = N_GROUPS * N_PAIRS
PAIR_LO = (0, 0, 0, 1, 1, 2)
PAIR_HI = (1, 2, 3, 2, 3, 3)

LANES = 128
COL_TILE = 8
SCAN_CHUNK = 128
MOE_TILE = 256
VMEM_LIMIT = 56 * 1024 * 1024


def _cparams(sem):
    return pltpu.CompilerParams(dimension_semantics=sem, vmem_limit_bytes=VMEM_LIMIT)


def _dot(a, b):
    return jnp.dot(a, b, preferred_element_type=F32)


def _dot_nt(a, b):
    return lax.dot_general(a, b, (((1,), (1,)), ((), ())), preferred_element_type=F32)


def _dot_tn(a, b):
    return lax.dot_general(a, b, (((0,), (0,)), ((), ())), preferred_element_type=F32)


def _split_bf16(a):
    hi = a.astype(BF16)
    lo = (a - hi.astype(F32)).astype(BF16)
    return hi, lo


def _sigmoid(x):
    return 1.0 / (1.0 + jnp.exp(-x))


def _silu(x):
    return x * _sigmoid(x)


def _log1p_exp_neg_abs(x):
    return jnp.log(1.0 + jnp.exp(-jnp.abs(x)))


def _log_sigmoid(x):
    return jnp.minimum(x, 0.0) - _log1p_exp_neg_abs(x)


def _rms(x, g):
    return x * lax.rsqrt(jnp.mean(x * x, axis=-1, keepdims=True) + EPS) * g


def _ada_kernel(c_ref, w_ref, b_ref, o_ref):
    a = _silu(c_ref[...])
    a_hi, a_lo = _split_bf16(a)
    w_hi, w_lo = _split_bf16(w_ref[0])
    acc = _dot(a_hi, w_hi) + _dot(a_lo, w_hi) + _dot(a_hi, w_lo)
    o_ref[0] = acc + b_ref[0]


def _ada(cc, ada_w, ada_b):
    depth, d, n = ada_w.shape
    tn = 1024
    rows = cc.shape[0]
    return pl.pallas_call(
        _ada_kernel,
        grid=(depth, n // tn),
        in_specs=[
            pl.BlockSpec((rows, d), lambda l, j: (0, 0)),
            pl.BlockSpec((1, d, tn), lambda l, j: (l, 0, j)),
            pl.BlockSpec((1, 1, tn), lambda l, j: (l, 0, j)),
        ],
        out_specs=pl.BlockSpec((1, rows, tn), lambda l, j: (l, 0, j)),
        out_shape=jax.ShapeDtypeStruct((depth, rows, n), F32),
        compiler_params=_cparams(("arbitrary", "arbitrary")),
        name="ada_mod",
    )(cc, ada_w, ada_b.reshape(depth, 1, n))


def _mproj_kernel(n_lat_tiles, xl_ref, xc_ref, sc_ref, sh_ref, g_ref, wqk_ref, wv_ref, wo_ref, wg_ref,
                  qk_ref, v_ref, o_ref, gt_ref, gtt_ref):
    j = pl.program_id(1)
    x = jnp.where(j < n_lat_tiles, xl_ref[0], xc_ref[0])
    h = (_rms(x, g_ref[...]) * (1.0 + sc_ref[0]) + sh_ref[0]).astype(BF16)
    gt = _dot(h, wg_ref[...])
    qk_ref[0] = _dot(h, wqk_ref[...]).astype(BF16)
    gt_t = gt.T
    for hd in range(M_HEADS):
        gt_ref[0, hd] = gt[:, 4 * hd:4 * hd + 4]
        gtt_ref[0, hd] = gt_t[4 * hd:4 * hd + 4, :]
    v_ref[0] = _dot(h, wv_ref[...]).astype(BF16)
    o_ref[0] = _dot(h, wo_ref[...]).astype(BF16)


def _mod_spec(d, layer, k, n_lat_tiles, rows):
    def imap(b, j):
        row = jnp.where(j < n_lat_tiles, b, rows - 1)
        return ((layer * rows + row) * N_ADA + k, 0, 0)
    return pl.BlockSpec((1, 1, d), imap)


def _mproj(x, ctx, modv, rows, norm_g, w_qk, w_v, w_o, w_g, tm):
    bsz, ll, d = x.shape
    lc = ctx.shape[1]
    nl, nc = ll // tm, lc // tm
    ltot = ll + lc
    wspec = lambda n: pl.BlockSpec((d, n), lambda b, j: (0, 0))
    tok = lambda n: pl.BlockSpec((1, tm, n), lambda b, j: (b, j, 0))
    return pl.pallas_call(
        functools.partial(_mproj_kernel, nl),
        grid=(bsz, nl + nc),
        in_specs=[
            pl.BlockSpec((1, tm, d), lambda b, j: (b, jnp.minimum(j, nl - 1), 0)),
            pl.BlockSpec((1, tm, d), lambda b, j: (b, jnp.maximum(j - nl, 0), 0)),
            _mod_spec(d, 0, 1, nl, rows), _mod_spec(d, 0, 0, nl, rows),
            pl.BlockSpec((1, d), lambda b, j: (0, 0)),
            wspec(w_qk.shape[1]), wspec(w_v.shape[1]), wspec(w_o.shape[1]), wspec(w_g.shape[1]),
        ],
        out_specs=[tok(w_qk.shape[1]), tok(w_v.shape[1]), tok(w_o.shape[1]),
                   pl.BlockSpec((1, M_HEADS, tm, 4), lambda b, j: (b, 0, j, 0)),
                   pl.BlockSpec((1, M_HEADS, 4, tm), lambda b, j: (b, 0, 0, j))],
        out_shape=[
            jax.ShapeDtypeStruct((bsz, ltot, w_qk.shape[1]), BF16),
            jax.ShapeDtypeStruct((bsz, ltot, w_v.shape[1]), BF16),
            jax.ShapeDtypeStruct((bsz, ltot, w_o.shape[1]), BF16),
            jax.ShapeDtypeStruct((bsz, M_HEADS, ltot, 4), F32),
            jax.ShapeDtypeStruct((bsz, M_HEADS, 4, ltot), F32),
        ],
        compiler_params=_cparams(("arbitrary", "arbitrary")),
        name="mlstm_in_proj",
    )(x, ctx, modv, modv, norm_g, w_qk, w_v, w_o, w_g)


def _mlstm_kernel(t, ll, lc, hp, q_ref, k_ref, v_ref, o_ref, gc_ref, gr_ref, cwq_ref, cwk_ref, cbq_ref, cbk_ref,
                  gbc_ref, gbr_ref, hg_ref, out_ref, qs, ks, hb, c_s, m_s):
    ltot = ll + lc
    dk = q_ref.shape[2] // hp
    dv = v_ref.shape[2] // hp
    row = lax.broadcasted_iota(jnp.int32, (ltot, 1), 0)
    first = (row == 0) | (row == ll)
    last = (row == ll - 1) | (row == ltot - 1)

    def conv(x_ref, w_ref, b_ref, cols):
        x = x_ref[0, :, cols].astype(F32)
        w = w_ref[:, cols]
        xp = jnp.where(first, 0.0, pltpu.roll(x, 1, 0))
        xn = jnp.where(last, 0.0, pltpu.roll(x, ltot - 1, 0))
        return _silu(xp * w[0:1] + x * w[1:2] + xn * w[2:3] + b_ref[:, cols])

    for j in range(hp):
        cols = slice(j * dk, (j + 1) * dk)
        qs[:, cols] = (conv(q_ref, cwq_ref, cbq_ref, cols) * (dk ** -0.5)).astype(BF16)
        ks[:, cols] = conv(k_ref, cwk_ref, cbk_ref, cols).astype(BF16)

    c_s[...] = jnp.zeros(c_s.shape, F32)
    m_s[...] = jnp.zeros(m_s.shape, F32)

    ti = lax.broadcasted_iota(jnp.int32, (t, t), 0)
    si = lax.broadcasted_iota(jnp.int32, (t, t), 1)

    ones = jnp.ones((t, LANES), BF16)

    def chunks(starts):
        jobs = [(j, d) for j in range(hp) for d in range(2)]
        st = {}
        for j, d in jobs:
            rows = pl.ds(starts[d], t)
            gc = gc_ref[0, j, rows, :] + gbc_ref[j]
            gr = gr_ref[0, j, :, rows] + gbr_ref[j]
            ig_c, lf_c = gc[:, 2 * d:2 * d + 1], _log_sigmoid(gc[:, 2 * d + 1:2 * d + 2])
            ig_r, lf_r = gr[2 * d:2 * d + 1, :], _log_sigmoid(gr[2 * d + 1:2 * d + 2, :])
            seen = (si <= ti) if d == 0 else (si >= ti)
            seen_t = (ti <= si) if d == 0 else (ti >= si)
            b_c = jnp.sum(jnp.where(seen, lf_r, 0.0), axis=1, keepdims=True)
            b_r = jnp.sum(jnp.where(seen_t, lf_c, 0.0), axis=0, keepdims=True)
            total = jnp.sum(lf_r, axis=1, keepdims=True)
            st[j, d] = dict(rows=rows, ig_c=ig_c, ig_r=ig_r, b_c=b_c, b_r=b_r, total=total, seen=seen,
                            m_prev=m_s[2 * j + d][0:1, 0:1])
        for j, d in jobs:
            s = st[j, d]
            s["q"] = qs[s["rows"], j * dk:(j + 1) * dk]
            s["k"] = ks[s["rows"], j * dk:(j + 1) * dk]
            s["v"] = jnp.concatenate([v_ref[0, s["rows"], j * dv:(j + 1) * dv], ones], axis=1)
            s["qk"] = _dot_nt(s["q"], s["k"])
        for j, d in jobs:
            s = st[j, d]
            log_d = jnp.where(s["seen"], s["b_c"] - s["b_r"] + s["ig_r"], -jnp.inf)
            log_inter = s["b_c"] + s["m_prev"]
            m_t = jnp.maximum(log_inter, jnp.max(log_d, axis=1, keepdims=True))
            s["w_ts"] = jnp.exp(log_d - m_t) * s["qk"]
            s["inter"] = jnp.exp(log_inter - m_t)
            s["m_t"] = m_t
        for j, d in jobs:
            s = st[j, d]
            lhs = jnp.concatenate([s["w_ts"].astype(BF16), s["q"] * s["inter"].astype(BF16)], axis=1)
            rhs = jnp.concatenate([s["v"], c_s[2 * j + d].astype(BF16)], axis=0)
            s["num"] = _dot(lhs, rhs)
        for j, d in jobs:
            s = st[j, d]
            den = s["num"][:, dv:]
            scale = 1.0 / jnp.maximum(jnp.abs(den), jnp.exp(-s["m_t"]))
            h = s["num"][:, :dv] * jnp.concatenate([scale] * (dv // LANES), axis=1)
            if d == 0:
                out_ref[0, s["rows"], j * dv:(j + 1) * dv] = h.astype(BF16)
            else:
                hb[s["rows"], j * dv:(j + 1) * dv] = h.astype(BF16)
        for j, d in jobs:
            s = st[j, d]
            g_c = s["total"] - s["b_c"] + s["ig_c"]
            g_r = s["total"] - s["b_r"] + s["ig_r"]
            m_new = jnp.maximum(s["total"] + s["m_prev"], jnp.max(g_r, axis=1, keepdims=True))
            w_c = jnp.exp(g_c - m_new)
            decay = jnp.exp(s["total"] + s["m_prev"] - m_new)
            wv = s["v"] * w_c.astype(BF16)
            c_s[2 * j + d] = decay * c_s[2 * j + d] + _dot_tn(s["k"], wv)
            m_s[2 * j + d] = jnp.broadcast_to(m_new, m_s.shape[1:])

    ncc, ncl = lc // t, ll // t
    for i in range(ncc):
        chunks((ll + i * t, ll + (ncc - 1 - i) * t))

    def body(i, carry):
        chunks((pl.multiple_of(i * t, t), pl.multiple_of((ncl - 1 - i) * t, t)))
        return carry

    lax.fori_loop(0, ncl, body, 0)

    def epilogue(i, carry):
        s = pl.multiple_of(i * t, t)
        for j in range(hp):
            cols = slice(j * dv, (j + 1) * dv)
            hs = out_ref[0, pl.ds(s, t), cols].astype(F32) + hb[pl.ds(s, t), cols].astype(F32)
            y = _rms(hs, hg_ref[:, cols])
            out_ref[0, pl.ds(s, t), cols] = (y * _sigmoid(o_ref[0, pl.ds(s, t), cols].astype(F32))).astype(BF16)
        return carry

    lax.fori_loop(0, ltot // t, epilogue, 0, unroll=2)


def _mlstm_scan(qk, v, o, g4, g4t, conv_w, conv_b, gate_b, head_g, ll, lc):
    bsz, ltot, _ = qk.shape
    nh = M_HEADS
    hp = 2
    dk = qk.shape[2] // (2 * nh)
    dv = v.shape[2] // nh
    t = 2 * SCAN_CHUNK if (ll % (2 * SCAN_CHUNK) == 0 and lc % (2 * SCAN_CHUNK) == 0) else SCAN_CHUNK
    gb = gate_b.reshape(4, nh).T
    return pl.pallas_call(
        functools.partial(_mlstm_kernel, t, ll, lc, hp),
        grid=(bsz, nh // hp),
        in_specs=[
            pl.BlockSpec((1, ltot, hp * dk), lambda b, h: (b, 0, h)),
            pl.BlockSpec((1, ltot, hp * dk), lambda b, h: (b, 0, nh // hp + h)),
            pl.BlockSpec((1, ltot, hp * dv), lambda b, h: (b, 0, h)),
            pl.BlockSpec((1, ltot, hp * dv), lambda b, h: (b, 0, h)),
            pl.BlockSpec((1, hp, ltot, 4), lambda b, h: (b, h, 0, 0)),
            pl.BlockSpec((1, hp, 4, ltot), lambda b, h: (b, h, 0, 0)),
            pl.BlockSpec((3, hp * dk), lambda b, h: (0, h)),
            pl.BlockSpec((3, hp * dk), lambda b, h: (0, nh // hp + h)),
            pl.BlockSpec((1, hp * dk), lambda b, h: (0, h)),
            pl.BlockSpec((1, hp * dk), lambda b, h: (0, nh // hp + h)),
            pl.BlockSpec((hp, 1, 4), lambda b, h: (h, 0, 0)),
            pl.BlockSpec((hp, 4, 1), lambda b, h: (h, 0, 0)),
            pl.BlockSpec((1, hp * dv), lambda b, h: (0, h)),
        ],
        out_specs=pl.BlockSpec((1, ltot, hp * dv), lambda b, h: (b, 0, h)),
        out_shape=jax.ShapeDtypeStruct((bsz, ltot, nh * dv), BF16),
        scratch_shapes=[
            pltpu.VMEM((ltot, hp * dk), BF16), pltpu.VMEM((ltot, hp * dk), BF16),
            pltpu.VMEM((ltot, hp * dv), BF16),
            pltpu.VMEM((2 * hp, dk, dv + LANES), F32), pltpu.VMEM((2 * hp, 8, LANES), F32),
        ],
        compiler_params=_cparams(("arbitrary", "arbitrary")),
        name="mlstm_scan",
    )(qk, qk, v, o, g4, g4t, conv_w, conv_w, conv_b.reshape(1, -1), conv_b.reshape(1, -1),
      gb.reshape(nh, 1, 4), gb.reshape(nh, 4, 1), head_g.reshape(1, -1))


def _router_logits_t(h2, rhi_ref, rlo_ref):
    h_hi, h_lo = _split_bf16(h2)
    lg = _dot(h_hi, rhi_ref[...]) + _dot(h_lo, rhi_ref[...]) + _dot(h_hi, rlo_ref[...])
    return lg.T[:N_EXPERTS, :]


def _out0_kernel(n_lat_tiles, hg_ref, xl_ref, xc_ref, g1_ref, sc_ref, sh_ref, ng_ref, wo_ref, rhi_ref, rlo_ref,
                 s_ref, h2_ref, lg_ref):
    j = pl.program_id(1)
    half = hg_ref.shape[1] // 2
    d = s_ref.shape[2]
    parts = [slice(p * half, (p + 1) * half) for p in range(2)]
    ys = [_dot(hg_ref[0, rows, :], wo_ref[...]) for rows in parts]
    ss = []
    for rows, y in zip(parts, ys):
        s = jnp.where(j < n_lat_tiles, xl_ref[0, rows, :], xc_ref[0, rows, :]) + g1_ref[0] * y
        s_ref[0, rows, :] = s
        ss.append(s)
    h2s = []
    for rows, s in zip(parts, ss):
        h2 = _rms(s, ng_ref[...]) * (1.0 + sc_ref[0]) + sh_ref[0]
        h2_ref[rows, :d] = h2
        h2_ref[rows, d:] = jnp.zeros((half, LANES), F32)
        h2s.append(h2)
    for rows, h2 in zip(parts, h2s):
        lg_ref[:, rows] = _router_logits_t(h2, rhi_ref, rlo_ref)


def _out0(hg, x, ctx, modv, rows, norm_g, w_out, r_hi, r_lo, tm):
    bsz, ll, d = x.shape
    lc = ctx.shape[1]
    nl, nc = ll // tm, lc // tm
    nt = nl + nc
    ltot = ll + lc
    return pl.pallas_call(
        functools.partial(_out0_kernel, nl),
        grid=(bsz, nt),
        in_specs=[
            pl.BlockSpec((1, tm, hg.shape[2]), lambda b, j: (b, j, 0)),
            pl.BlockSpec((1, tm, d), lambda b, j: (b, jnp.minimum(j, nl - 1), 0)),
            pl.BlockSpec((1, tm, d), lambda b, j: (b, jnp.maximum(j - nl, 0), 0)),
            _mod_spec(d, 0, 2, nl, rows), _mod_spec(d, 0, 4, nl, rows), _mod_spec(d, 0, 3, nl, rows),
            pl.BlockSpec((1, d), lambda b, j: (0, 0)),
            pl.BlockSpec(w_out.shape, lambda b, j: (0, 0)),
            pl.BlockSpec(r_hi.shape, lambda b, j: (0, 0)),
            pl.BlockSpec(r_lo.shape, lambda b, j: (0, 0)),
        ],
        out_specs=[
            pl.BlockSpec((1, tm, d), lambda b, j: (b, j, 0)),
            pl.BlockSpec((tm, d + LANES), lambda b, j: (b * nt + j, 0)),
            pl.BlockSpec((N_EXPERTS, tm), lambda b, j: (0, b * nt + j)),
        ],
        out_shape=[
            jax.ShapeDtypeStruct((bsz, ltot, d), F32),
            jax.ShapeDtypeStruct((bsz * ltot, d + LANES), F32),
            jax.ShapeDtypeStruct((N_EXPERTS, bsz * ltot), F32),
        ],
        compiler_params=_cparams(("arbitrary", "arbitrary")),
        name="mlstm_out_proj",
    )(hg, x, ctx, modv, modv, modv, norm_g, w_out, r_hi, r_lo)


def _route_kernel(lg_ref, bias_ref, h2x_hbm, cls_ref, w_ref):
    del h2x_hbm
    s = _sigmoid(lg_ref[...])
    sel = s + bias_ref[...]
    srow = [s[e:e + 1, :] for e in range(N_EXPERTS)]
    row = [sel[e:e + 1, :] for e in range(N_EXPERTS)]
    best = jnp.zeros(row[0].shape, jnp.int32)
    best_score = None
    for g in range(N_GROUPS):
        r = row[g * E_PER_GROUP:(g + 1) * E_PER_GROUP]
        score = None
        for lo, hi in zip(PAIR_LO, PAIR_HI):
            pair = r[lo] + r[hi]
            score = pair if score is None else jnp.maximum(score, pair)
        if g == 0:
            best_score = score
        else:
            better = score > best_score
            best = jnp.where(better, g, best)
            best_score = jnp.where(better, score, best_score)
    gs = [row[i] for i in range(E_PER_GROUP)]
    gw = [srow[i] for i in range(E_PER_GROUP)]
    for g in range(1, N_GROUPS):
        hit = best == g
        gs = [jnp.where(hit, row[g * E_PER_GROUP + i], gs[i]) for i in range(E_PER_GROUP)]
        gw = [jnp.where(hit, srow[g * E_PER_GROUP + i], gw[i]) for i in range(E_PER_GROUP)]
    keep = []
    for i in range(E_PER_GROUP):
        beaten = jnp.zeros(best.shape, jnp.int32)
        for j in range(E_PER_GROUP):
            if j == i:
                continue
            wins = (gs[j] > gs[i]) | ((gs[j] == gs[i]) & (j < i))
            beaten = beaten + wins.astype(jnp.int32)
        keep.append(beaten < 2)
    pair_id = jnp.zeros(best.shape, jnp.int32)
    w_lo = jnp.zeros(best.shape, F32)
    w_hi = jnp.zeros(best.shape, F32)
    for p, (lo, hi) in enumerate(zip(PAIR_LO, PAIR_HI)):
        hit = keep[lo] & keep[hi]
        pair_id = jnp.where(hit, p, pair_id)
        w_lo = jnp.where(hit, gw[lo], w_lo)
        w_hi = jnp.where(hit, gw[hi], w_hi)
    tot = w_lo + w_hi
    cls_ref[...] = best * N_PAIRS + pair_id
    lanes_t = jnp.concatenate([w_lo / tot, w_hi / tot, jnp.zeros((LANES - 2, w_lo.shape[1]), F32)], axis=0)
    w_ref[...] = lanes_t.T


def _route(logits_t, router_bias, h2x):
    n = logits_t.shape[1]
    d = h2x.shape[1] - LANES
    tn = next(cand for cand in (2048, 1024, 512, 256, 128) if n % cand == 0)
    return pl.pallas_call(
        _route_kernel,
        grid=(n // tn,),
        in_specs=[
            pl.BlockSpec((N_EXPERTS, tn), lambda i: (0, i)),
            pl.BlockSpec((N_EXPERTS, 1), lambda i: (0, 0)),
            pl.BlockSpec(memory_space=pl.ANY),
        ],
        out_specs=[pl.BlockSpec((1, tn), lambda i: (0, i)), pl.BlockSpec((tn, LANES), lambda i: (i, d // LANES))],
        out_shape=[jax.ShapeDtypeStruct((1, n), jnp.int32), jax.ShapeDtypeStruct(h2x.shape, F32)],
        input_output_aliases={2: 1},
        compiler_params=_cparams(("arbitrary",)),
        name="moe_route",
    )(logits_t, router_bias.reshape(N_EXPERTS, 1).astype(F32), h2x)


def _moe_kernel(tm, dump_base, spare_rows, nu_ref, e0_ref, e1_ref, base_ref, nv_ref, tok_ref, dst_ref, h2_hbm,
                wg0_ref, wu0_ref, wd0_ref, wg1_ref, wu1_ref, wd1_ref, f_hbm,
                xb0, xb1, yb0, yb1, xbf, gsem, ssem, zsem, idle_sem):
    del e0_ref, e1_ref
    t = pl.program_id(0)
    n_used = nu_ref[0]
    d = xbf.shape[1]
    xbuf, ybuf = (xb0, xb1), (yb0, yb1)

    def gather_row(base, r, slot):
        tok = tok_ref[base + r]
        queue = r % 2 if isinstance(r, int) else 0
        pltpu.make_async_copy(h2_hbm.at[pl.ds(tok, 1)], xbuf[slot].at[pl.ds(r, 1)], gsem.at[slot]).start(priority=queue)

    def scatter_row(base, nv, r, slot):
        row = jnp.where(r < nv, dst_ref[base + r], dump_base + r)
        queue = (r + 1) % 2 if isinstance(r, int) else 1
        pltpu.make_async_copy(ybuf[slot].at[pl.ds(r, 1)], f_hbm.at[pl.ds(row, 1)], ssem.at[slot]).start(priority=queue)

    def wait_gather(slot):
        pltpu.make_async_copy(h2_hbm.at[pl.ds(0, tm)], xbuf[slot], gsem.at[slot]).wait()

    def wait_scatter(slot):
        pltpu.make_async_copy(ybuf[slot], f_hbm.at[pl.ds(0, tm)], ssem.at[slot]).wait()

    @pl.when(t == 0)
    def _():
        yb1[...] = jnp.zeros(yb1.shape, F32)
        fills = [pltpu.make_async_copy(yb1, f_hbm.at[pl.ds(row, tm)], zsem) for row in spare_rows]
        for fill in fills:
            fill.start()
        for fill in fills:
            fill.wait()
        base0 = base_ref[1]

        def first(r, carry):
            gather_row(base0, r, 0)
            return carry

        lax.fori_loop(0, tm, first, 0)

    def step(cur):
        nxt = 1 - cur
        base_next = base_ref[t + 2]
        base_prev = base_ref[t]
        nv_prev = nv_ref[t]

        @pl.when(t < n_used)
        def _():
            wait_gather(cur)
            xbf[...] = xbuf[cur][:, :d].astype(BF16)
            w = xbuf[cur][:, d:]

            for r in range(tm):
                gather_row(base_next, r, nxt)
            x = xbf[...]
            a0 = (_silu(_dot(x, wg0_ref[0, 0])) * _dot(x, wu0_ref[0, 0]) * w[:, 0:1]).astype(BF16)

            z = pl.semaphore_read(idle_sem)
            zero = lax.shift_right_logical(z.astype(jnp.uint32), jnp.uint32(32)).astype(jnp.int32)
            off = pl.multiple_of(zero * 16, 16)

            for r in range(tm):
                scatter_row(base_prev, nv_prev, r, nxt)
            x = xbf[pl.ds(off, tm), :]
            a1 = (_silu(_dot(x, wg1_ref[0, 0])) * _dot(x, wu1_ref[0, 0]) * w[:, 1:2]).astype(BF16)
            y = _dot(a0, wd0_ref[0, 0]) + _dot(a1, wd1_ref[0, 0])

            @pl.when(t >= 1)
            def _():
                wait_scatter(cur)

            ybuf[cur][...] = y

        @pl.when(t == n_used)
        def _():
            wait_gather(cur)
            wait_scatter(cur)

            def last(r, carry):
                scatter_row(base_prev, nv_prev, r, nxt)
                return carry

            lax.fori_loop(0, tm, last, 0)
            wait_scatter(nxt)

    @pl.when(t % 2 == 0)
    def _():
        step(0)

    @pl.when(t % 2 == 1)
    def _():
        step(1)


def _moe(h2x, cls, dst_of_token, dump_base, spare_rows, out_rows, layer, wg, wu, wd):
    n = h2x.shape[0]
    d = h2x.shape[1] - LANES
    tm = MOE_TILE
    n_tiles = n // tm + N_CLASSES
    cls = cls.reshape(n)
    _, tok_sorted, dst_sorted = lax.sort((cls, jnp.arange(n, dtype=jnp.int32), dst_of_token), num_keys=1)
    tail = jnp.zeros((tm,), jnp.int32)
    tok_sorted = jnp.concatenate([tok_sorted, tail])
    dst_sorted = jnp.concatenate([dst_sorted, tail])
    counts = jnp.sum(cls[:, None] == jnp.arange(N_CLASSES, dtype=jnp.int32)[None, :], axis=0).astype(jnp.int32)
    tiles_per = (counts + tm - 1) // tm
    tile_end = jnp.cumsum(tiles_per)
    tile_start = tile_end - tiles_per
    first_sorted = jnp.cumsum(counts) - counts
    n_used = tile_end[-1]
    tile_id = jnp.arange(n_tiles, dtype=jnp.int32)
    tile_cls = jnp.searchsorted(tile_end, jnp.minimum(tile_id, n_used - 1), side="right").astype(jnp.int32)
    tile_cls = jnp.minimum(tile_cls, N_CLASSES - 1)
    group, pair = tile_cls // N_PAIRS, tile_cls % N_PAIRS
    e0 = group * E_PER_GROUP + jnp.asarray(PAIR_LO, jnp.int32)[pair]
    e1 = group * E_PER_GROUP + jnp.asarray(PAIR_HI, jnp.int32)[pair]
    in_class = (tile_id - tile_start[tile_cls]) * tm
    used = tile_id < n_used
    base = jnp.where(used, first_sorted[tile_cls] + in_class, 0)
    nv = jnp.where(used, jnp.clip(counts[tile_cls] - in_class, 0, tm), 0)
    guard = jnp.zeros((1,), jnp.int32)
    base = jnp.concatenate([guard, base, guard]).astype(jnp.int32)
    nv = jnp.concatenate([guard, nv, guard]).astype(jnp.int32)

    de = wg.shape[3]
    smem = pl.BlockSpec(memory_space=pltpu.SMEM)
    up0 = pl.BlockSpec((1, 1, d, de), lambda i, nu, e0, e1, base, nv: (layer, e0[i], 0, 0))
    up1 = pl.BlockSpec((1, 1, d, de), lambda i, nu, e0, e1, base, nv: (layer, e1[i], 0, 0))
    dn0 = pl.BlockSpec((1, 1, de, d), lambda i, nu, e0, e1, base, nv: (layer, e0[i], 0, 0))
    dn1 = pl.BlockSpec((1, 1, de, d), lambda i, nu, e0, e1, base, nv: (layer, e1[i], 0, 0))
    return pl.pallas_call(
        functools.partial(_moe_kernel, tm, dump_base, spare_rows),
        grid_spec=pltpu.PrefetchScalarGridSpec(
            num_scalar_prefetch=5,
            grid=(n_tiles,),
            in_specs=[
                smem, smem,
                pl.BlockSpec(memory_space=pl.ANY),
                up0, up0, dn0, up1, up1, dn1,
            ],
            out_specs=pl.BlockSpec(memory_space=pl.ANY),
            scratch_shapes=[
                pltpu.VMEM((tm, d + LANES), F32), pltpu.VMEM((tm, d + LANES), F32),
                pltpu.VMEM((tm, d), F32), pltpu.VMEM((tm, d), F32),
                pltpu.VMEM((tm, d), BF16),
                pltpu.SemaphoreType.DMA((2,)), pltpu.SemaphoreType.DMA((2,)), pltpu.SemaphoreType.DMA(()),
                pltpu.SemaphoreType.REGULAR(()),
            ],
        ),
        out_shape=jax.ShapeDtypeStruct((out_rows, d), F32),
        compiler_params=_cparams(("arbitrary",)),
        name="moe_experts",
    )(n_used.reshape(1).astype(jnp.int32), e0, e1, base, nv, tok_sorted, dst_sorted, h2x,
      wg, wu, wd, wg, wu, wd)


def _hproj_kernel(n_lat_tiles, rows, s_hbm, sc_ref_, f_ref, g2_ref, sc_ref, sh_ref, ng_ref, lb_ref,
                  wq_ref, wzf_ref, wzb_ref, wi_ref, wg_ref,
                  s2_ref, q_ref, kf_ref, kb_ref, lff_ref, lfb_ref, i_ref, g_ref, xt, sbuf, sem):
    b, j = pl.program_id(0), pl.program_id(1)
    n_lat = pl.num_programs(0) * n_lat_tiles
    g2 = g2_ref[0]

    def column_copies(tile, slot):
        bb, jj = tile // n_lat_tiles, tile % n_lat_tiles
        return [pltpu.make_async_copy(s_hbm.at[bb, pl.ds(0, rows), jj * COL_TILE + c, :],
                                      sbuf.at[slot, pl.ds(c * rows, rows), :], sem.at[slot])
                for c in range(COL_TILE)]

    @pl.when(j < n_lat_tiles)
    def _():
        tile = b * n_lat_tiles + j
        slot = tile % 2

        @pl.when(tile == 0)
        def _():
            for cp in column_copies(tile, slot):
                cp.start()

        @pl.when(tile + 1 < n_lat)
        def _():
            for cp in column_copies(tile + 1, 1 - slot):
                cp.start()

        for cp in column_copies(tile, slot):
            cp.wait()
        s = sbuf[slot] + g2 * f_ref[0]
        s2_ref[0] = s
        xt[...] = s

    @pl.when(j >= n_lat_tiles)
    def _():
        xt[...] = sc_ref_[0] + g2 * f_ref[0]

    h = (_rms(xt[...], ng_ref[...]) * (1.0 + sc_ref[0]) + sh_ref[0]).astype(BF16)
    n_chunks = 4
    width = wq_ref.shape[1] // n_chunks

    def gates(dd, k_ref, lf_ref, cols, z):
        log_lb = lb_ref[2 * dd:2 * dd + 1, cols]
        log_1mlb = lb_ref[2 * dd + 1:2 * dd + 2, cols]
        a = log_1mlb + _log_sigmoid(z)
        lf_ref[0, :, cols] = jnp.maximum(log_lb, a) + _log1p_exp_neg_abs(log_lb - a)
        k_ref[0, :, cols] = jnp.exp(a - z).astype(BF16)

    def store_q(cols, y):
        q_ref[0, :, cols] = _silu(y).astype(BF16)

    def store_plain(out_ref, cols, y):
        out_ref[0, :, cols] = y.astype(BF16)

    gates_f = functools.partial(gates, 0, kf_ref, lff_ref)
    gates_b = functools.partial(gates, 1, kb_ref, lfb_ref)
    plain_i = functools.partial(store_plain, i_ref)
    plain_g = functools.partial(store_plain, g_ref)
    work = []
    for c in range(n_chunks):
        work += [(wzf_ref, gates_f, c), (wi_ref, plain_i, c)]
    for c in range(n_chunks):
        work += [(wzb_ref, gates_b, c), (wg_ref, plain_g, c)]
    work += [(wq_ref, store_q, c) for c in range(n_chunks)]
    pending = None
    for w_ref, tail, c in work:
        cols = slice(c * width, (c + 1) * width)
        y = _dot(h, w_ref[:, cols])
        if pending is not None:
            pending()
        pending = functools.partial(tail, cols, y)
    pending()


def _hproj(s1, f0, modv, rows_mod, norm_g, lbtab, ws, ll, lc):
    bsz, ltot, d = s1.shape
    rows = ll // GRID_W
    tm = COL_TILE * rows
    nl, nc = GRID_W // COL_TILE, lc // tm
    s_grid = s1.reshape(bsz, ltot // GRID_W, GRID_W, d)
    f_tok = f0.reshape(bsz, -1, d)
    ctx = pl.BlockSpec((1, tm, d), lambda b, j: (b, ll // tm + jnp.maximum(j - nl, 0), 0))
    wspec = pl.BlockSpec((d, d), lambda b, j: (0, 0))
    tok = pl.BlockSpec((1, tm, d), lambda b, j: (b, j, 0))
    lat = pl.BlockSpec((1, tm, d), lambda b, j: (b, jnp.minimum(j, nl - 1), 0))

    def g2_map(b, j):
        row = jnp.where(j < nl, b, rows_mod - 1)
        return ((0 * rows_mod + row) * N_ADA + 5, 0, 0)

    outs = pl.pallas_call(
        functools.partial(_hproj_kernel, nl, rows),
        grid=(bsz, nl + nc),
        in_specs=[
            pl.BlockSpec(memory_space=pl.ANY), ctx, tok,
            pl.BlockSpec((1, 1, d), g2_map),
            _mod_spec(d, 1, 1, nl, rows_mod), _mod_spec(d, 1, 0, nl, rows_mod),
            pl.BlockSpec((1, d), lambda b, j: (0, 0)),
            pl.BlockSpec((4, d), lambda b, j: (0, 0)),
            wspec, wspec, wspec, wspec, wspec,
        ],
        out_specs=[lat, tok, tok, tok, tok, tok, tok, tok],
        out_shape=[jax.ShapeDtypeStruct((bsz, ll, d), F32)]
        + [jax.ShapeDtypeStruct((bsz, ltot, d), dt) for dt in (BF16, BF16, BF16, F32, F32, BF16, BF16)],
        scratch_shapes=[pltpu.VMEM((tm, d), F32), pltpu.VMEM((2, tm, d), F32), pltpu.SemaphoreType.DMA((2,))],
        compiler_params=_cparams(("arbitrary", "arbitrary")),
        name="hgrn_in_proj",
    )(s_grid, s1, f_tok, modv, modv, modv, norm_g, lbtab, *ws)
    return outs


def _hgrn_kernel(t, ll, lc, q_ref, kf_ref, kb_ref, lff_ref, lfb_ref, i_ref, g_ref, hg_ref, out_ref,
                 of, ob, inc_s, dec_s, qd_s):
    levels = []
    m = 2
    while m <= t:
        levels.append(m)
        m *= 2
    ti = lax.broadcasted_iota(jnp.int32, (t, t), 0)
    si = lax.broadcasted_iota(jnp.int32, (t, t), 1)
    xor = ti ^ si
    level = jnp.zeros((t, t), jnp.int32)
    for m in levels:
        level = level + (xor >= m // 2).astype(jnp.int32)
    feeds = ((si <= ti).astype(BF16), (si >= ti).astype(BF16))

    row = lax.broadcasted_iota(jnp.int32, (t, 1), 0)
    feeds_twice = tuple(jnp.concatenate([f, f], axis=1) for f in feeds)

    def block_ref(b, m, d):
        half = m // 2
        pos = half - 1 if d == 0 else half
        if m >= 8:
            dk = b.shape[1]
            b3 = b.reshape(t // m, m, dk)
            return jnp.broadcast_to(b3[:, pos:pos + 1, :], b3.shape).reshape(t, dk)
        r = row % m
        out = b
        for res in range(m):
            if res != pos:
                out = jnp.where(r == res, pltpu.roll(b, (res - pos) % t, 0), out)
        return out

    def neg_abs(x):
        sign = jnp.uint32(0x80000000)
        return lax.bitcast_convert_type(lax.bitcast_convert_type(x, jnp.uint32) | sign, F32)

    o_dir = (of, ob)

    def prepare(jobs, need_out):
        work = []
        for ci, d in jobs:
            rows = pl.ds(ci * t if isinstance(ci, int) else pl.multiple_of(ci * t, t), t)
            lf = (lff_ref if d == 0 else lfb_ref)[0, rows, :] * LOG2_E
            hi = lf.astype(BF16)
            lo = (lf - hi.astype(F32)).astype(BF16)
            b = _dot(feeds_twice[d], jnp.concatenate([hi, lo], axis=0))
            work.append((ci, d, rows, b))
        for ci, d, rows, b in work:
            b_tot = b[t - 1:t, :] if d == 0 else b[0:1, :]
            k = (kf_ref if d == 0 else kb_ref)[0, rows, :]
            inc_s[d, ci] = _dot_tn(i_ref[0, rows, :], k * jnp.exp2(b_tot - b).astype(BF16))
            dec_s[d, ci] = jnp.exp2(b_tot)
            if need_out:
                qd_s[d, rows, :] = q_ref[0, rows, :] * jnp.exp2(b).astype(BF16)
        if not need_out:
            return
        acc, qf, kf = [], [], []
        for ci, d, rows, b in work:
            q, k = q_ref[0, rows, :], (kf_ref if d == 0 else kb_ref)[0, rows, :]
            acc.append(_dot_nt(q, k))
            qf.append(q.astype(F32))
            kf.append(k.astype(F32))
        for idx, m in enumerate(levels):
            for n, (ci, d, rows, b) in enumerate(work):
                e = jnp.exp2(neg_abs(b - block_ref(b, m, d)))
                p = _dot_nt((qf[n] * e).astype(BF16), (kf[n] * e).astype(BF16))
                acc[n] = jnp.where(level == idx + 1, p, acc[n])
        for n, (ci, d, rows, b) in enumerate(work):
            o_dir[d][rows, :] = _dot(acc[n].astype(BF16) * feeds[d], i_ref[0, rows, :])

    ncc, ncl = lc // t, ll // t
    prepare([(ncl + i, d) for i in range(ncc) for d in range(2)], False)

    group = next(g for g in (16, 8, 4, 2, 1) if ncl % g == 0)

    def body(i, carry):
        prepare([(group * i + j, d) for j in range(group) for d in range(2)], True)
        return carry

    lax.fori_loop(0, ncl // group, body, 0)

    for d in range(2):
        order = list(range(ncl, ncl + ncc)) + list(range(ncl))
        if d == 1:
            order = list(range(ncl + ncc - 1, ncl - 1, -1)) + list(range(ncl - 1, -1, -1))
        st = jnp.zeros(inc_s.shape[2:], F32)
        for ci in order:
            if ci < ncl:
                rows = pl.ds(ci * t, t)
                o_dir[d][rows, :] = o_dir[d][rows, :] + _dot_nt(qd_s[d, rows, :], st.astype(BF16))
            st = dec_s[d, ci] * st + inc_s[d, ci]

    def epilogue(i, carry):
        s = pl.multiple_of(i * t, t)
        y = _rms(of[pl.ds(s, t), :] + ob[pl.ds(s, t), :], hg_ref[...])
        out_ref[0, pl.ds(s, t), :] = (y * _silu(g_ref[0, pl.ds(s, t), :].astype(F32))).astype(BF16)
        return carry

    lax.fori_loop(0, ncl, epilogue, 0, unroll=4)


def _hgrn_scan(q, kf, kb, lff, lfb, iv, g, head_g, ll, lc):
    bsz, ltot, hk = q.shape
    nh = H_HEADS
    dk = hk // nh
    t = SCAN_CHUNK
    full = pl.BlockSpec((1, ltot, dk), lambda b, h: (b, 0, h))
    lat = pl.BlockSpec((1, ll, dk), lambda b, h: (b, 0, h))
    return pl.pallas_call(
        functools.partial(_hgrn_kernel, t, ll, lc),
        grid=(bsz, nh),
        in_specs=[full, full, full, full, full, full, lat, pl.BlockSpec((1, dk), lambda b, h: (0, h))],
        out_specs=lat,
        out_shape=jax.ShapeDtypeStruct((bsz, ll, hk), BF16),
        scratch_shapes=[pltpu.VMEM((ll, dk), F32), pltpu.VMEM((ll, dk), F32),
                        pltpu.VMEM((2, ltot // t, dk, dk), F32), pltpu.VMEM((2, ltot // t, 1, dk), F32),
                        pltpu.VMEM((2, ll, dk), BF16)],
        compiler_params=_cparams(("arbitrary", "arbitrary")),
        name="hgrn_scan",
    )(q, kf, kb, lff, lfb, iv, g, head_g.reshape(1, -1))


def _out1_kernel(hg_ref, s_ref, g1_ref, sc_ref, sh_ref, ng_ref, wo_ref, rhi_ref, rlo_ref,
                 s3_ref, h2_ref, lg_ref):
    half = hg_ref.shape[1] // 2
    d = s_ref.shape[2]
    parts = [slice(p * half, (p + 1) * half) for p in range(2)]
    ys = [_dot(hg_ref[0, rows, :], wo_ref[...]) for rows in parts]
    ss = []
    for rows, y in zip(parts, ys):
        s = s_ref[0, rows, :] + g1_ref[0] * y
        s3_ref[0, rows, :] = s
        ss.append(s)
    h2s = []
    for rows, s in zip(parts, ss):
        h2 = _rms(s, ng_ref[...]) * (1.0 + sc_ref[0]) + sh_ref[0]
        h2_ref[rows, :d] = h2
        h2_ref[rows, d:] = jnp.zeros((half, LANES), F32)
        h2s.append(h2)
    for rows, h2 in zip(parts, h2s):
        lg_ref[:, rows] = _router_logits_t(h2, rhi_ref, rlo_ref)


def _out1(hg, s2, modv, rows_mod, norm_g, w_out, r_hi, r_lo, tm):
    bsz, ll, hv = hg.shape
    d = s2.shape[2]
    tm = 2 * tm if ll % (2 * tm) == 0 else tm
    nl = ll // tm
    cm = pl.BlockSpec((1, tm, d), lambda b, j: (b, j, 0))
    return pl.pallas_call(
        _out1_kernel,
        grid=(bsz, nl),
        in_specs=[
            pl.BlockSpec((1, tm, hv), lambda b, j: (b, j, 0)),
            cm,
            _mod_spec(d, 1, 2, nl, rows_mod), _mod_spec(d, 1, 4, nl, rows_mod), _mod_spec(d, 1, 3, nl, rows_mod),
            pl.BlockSpec((1, d), lambda b, j: (0, 0)),
            pl.BlockSpec(w_out.shape, lambda b, j: (0, 0)),
            pl.BlockSpec(r_hi.shape, lambda b, j: (0, 0)),
            pl.BlockSpec(r_lo.shape, lambda b, j: (0, 0)),
        ],
        out_specs=[
            cm,
            pl.BlockSpec((tm, d + LANES), lambda b, j: (b * nl + j, 0)),
            pl.BlockSpec((N_EXPERTS, tm), lambda b, j: (0, b * nl + j)),
        ],
        out_shape=[
            jax.ShapeDtypeStruct(s2.shape, F32),
            jax.ShapeDtypeStruct((bsz * ll, d + LANES), F32),
            jax.ShapeDtypeStruct((N_EXPERTS, bsz * ll), F32),
        ],
        compiler_params=_cparams(("arbitrary", "arbitrary")),
        name="hgrn_out_proj",
    )(hg, s2, modv, modv, modv, norm_g, w_out, r_hi, r_lo)


def _final_kernel(rows, s_ref, f_ref, g2_ref, fg_ref, o_hbm, obuf, sem):
    b, j = pl.program_id(0), pl.program_id(1)
    nl = pl.num_programs(1)
    tile = b * nl + j
    n_tiles = pl.num_programs(0) * nl
    slot = tile % 2

    def column_copies(t, sl):
        bb, jj = t // nl, t % nl
        return [pltpu.make_async_copy(obuf.at[sl, pl.ds(c * rows, rows), :],
                                      o_hbm.at[bb, pl.ds(0, rows), jj * COL_TILE + c, :], sem.at[sl])
                for c in range(COL_TILE)]

    @pl.when(tile >= 2)
    def _():
        for cp in column_copies(tile - 2, slot):
            cp.wait()

    obuf[slot] = _rms(s_ref[0] + g2_ref[0] * f_ref[0], fg_ref[...])
    for cp in column_copies(tile, slot):
        cp.start()

    @pl.when(tile == n_tiles - 1)
    def _():
        @pl.when(tile >= 1)
        def _():
            for cp in column_copies(tile - 1, 1 - slot):
                cp.wait()
        for cp in column_copies(tile, slot):
            cp.wait()


def _final(s3, f1, modv, rows_mod, final_g, tm):
    bsz, ll, d = s3.shape
    nl = ll // tm
    rows = ll // GRID_W
    tok = pl.BlockSpec((1, tm, d), lambda b, j: (b, j, 0))
    out = pl.pallas_call(
        functools.partial(_final_kernel, rows),
        grid=(bsz, nl),
        in_specs=[tok, tok, _mod_spec(d, 1, 5, nl, rows_mod), pl.BlockSpec((1, d), lambda b, j: (0, 0))],
        out_specs=pl.BlockSpec(memory_space=pl.ANY),
        out_shape=jax.ShapeDtypeStruct((bsz, rows, GRID_W, d), F32),
        scratch_shapes=[pltpu.VMEM((2, tm, d), F32), pltpu.SemaphoreType.DMA((2,))],
        compiler_params=_cparams(("arbitrary", "arbitrary")),
        name="final_norm",
    )(s3, f1.reshape(bsz, -1, d), modv, final_g)
    return out.reshape(bsz, ll, d)


def kernel(x, c, ctx, c_ctx, ada_w, ada_b, norm_mix_g, norm_ffn_g, final_g, m_w_in, m_conv_w, m_conv_b, m_gate_b,
           m_head_g, m_w_out, h_w_in, h_lower_bounds, h_head_g, h_w_out, router_w, router_bias, e_w_gate,
           e_w_up, e_w_down):
    bsz, ll, d = x.shape
    lc = ctx.shape[1]
    ltot = ll + lc
    depth = ada_w.shape[0]
    assert depth == 2 and ll % GRID_W == 0 and lc % SCAN_CHUNK == 0 and ll % SCAN_CHUNK == 0
    rows = ll // GRID_W
    tm = COL_TILE * rows
    assert lc % tm == 0 and ltot % GRID_W == 0 and (bsz * ltot) % MOE_TILE == 0 and (bsz * ll) % MOE_TILE == 0

    rows_mod = 8 * ((bsz + 1 + 7) // 8)
    cc = jnp.zeros((rows_mod, d), F32).at[:bsz].set(c).at[rows_mod - 1].set(c_ctx)
    modv = _ada(cc, ada_w, ada_b).reshape(depth * rows_mod * N_ADA, 1, d)

    r_pad = jnp.zeros((d, LANES), F32).at[:, :N_EXPERTS].set(router_w)
    r_hi = r_pad.astype(BF16)
    r_lo = (r_pad - r_hi.astype(F32)).astype(BF16)
    wg_all, wu_all, wd_all = e_w_gate.astype(BF16), e_w_up.astype(BF16), e_w_down.astype(BF16)

    m_qk, m_v = m_conv_w.shape[2], m_head_g.shape[1]
    w_in = m_w_in[0].astype(BF16)
    w_gate_cols = w_in[:, m_qk + 2 * m_v:].reshape(d, 4, M_HEADS).transpose(0, 2, 1).reshape(d, 4 * M_HEADS)
    w_gates = jnp.zeros((d, LANES), BF16).at[:, :4 * M_HEADS].set(w_gate_cols)
    qk, v, o, g4, g4t = _mproj(x, ctx, modv, rows_mod, norm_mix_g[0:1], w_in[:, :m_qk], w_in[:, m_qk:m_qk + m_v],
                             w_in[:, m_qk + m_v:m_qk + 2 * m_v], w_gates, tm)
    hg = _mlstm_scan(qk, v, o, g4, g4t, m_conv_w[0], m_conv_b[0], m_gate_b[0], m_head_g[0], ll, lc)
    s1, h2, lg = _out0(hg, x, ctx, modv, rows_mod, norm_ffn_g[0:1], m_w_out[0].astype(BF16), r_hi, r_lo, tm)
    cls, h2 = _route(lg, router_bias, h2)
    tok0 = jnp.arange(bsz * ltot, dtype=jnp.int32)
    pos0 = tok0 % ltot
    pos0 = jnp.where(pos0 < ll, (pos0 % GRID_W) * rows + pos0 // GRID_W, pos0)
    dst0 = (tok0 // ltot) * (ltot + MOE_TILE) + pos0
    spare0 = tuple(b * (ltot + MOE_TILE) + ltot for b in range(1, bsz))
    f0 = _moe(h2, cls, dst0, ltot, spare0, bsz * (ltot + MOE_TILE), 0, wg_all, wu_all, wd_all)

    lbs = jnp.cumsum(jax.nn.softmax(h_lower_bounds.astype(F32), axis=0), axis=0)
    lb = (lbs - lbs[0])[1].reshape(2, -1)
    lbtab = jnp.stack([jnp.log(lb[0]), jnp.log1p(-lb[0]), jnp.log(lb[1]), jnp.log1p(-lb[1])])
    hw = h_w_in[0].astype(BF16)
    hk = lb.shape[1]
    ws = (hw[:, :hk], hw[:, hk:2 * hk], hw[:, 2 * hk:3 * hk], hw[:, 3 * hk:3 * hk + d], hw[:, 3 * hk + d:])
    s2, q, kf, kb, lff, lfb, iv, g = _hproj(s1, f0, modv, rows_mod, norm_mix_g[1:2], lbtab, ws, ll, lc)
    hg1 = _hgrn_scan(q, kf, kb, lff, lfb, iv, g, h_head_g[0], ll, lc)
    s3, h2b, lgb = _out1(hg1, s2, modv, rows_mod, norm_ffn_g[1:2], h_w_out[0].astype(BF16), r_hi, r_lo, tm)
    clsb, h2b = _route(lgb, router_bias, h2b)
    tok1 = jnp.arange(bsz * ll, dtype=jnp.int32)
    dst1 = (tok1 // ll) * (ll + MOE_TILE) + tok1 % ll
    spare1 = tuple(b * (ll + MOE_TILE) + ll for b in range(1, bsz))
    f1 = _moe(h2b, clsb, dst1, ll, spare1, bsz * (ll + MOE_TILE), 1, wg_all, wu_all, wd_all)

    return _final(s3, f1, modv, rows_mod, final_g.reshape(1, d), tm)
```
